```python
import jax, jax.numpy as jnp
from jax import lax
import numpy as np

D_MODEL = 2048
BATCH = 1
SEQ = 8192
DEPTH = 4

HEAD_DIM = 128
ROPE_THETA = 10000.0
Q_CHUNK = 128
LN_EPS = 1e-5
RMS_EPS = 1e-6

MOBA_HEADS = 4
MOBA_BLOCK = 256
MOBA_TOPK = 3

MLA_HEADS = 4
MLA_Q_RANK = 512
MLA_KV_RANK = 512
MLA_NOPE = 128
MLA_ROPE = 64
MLA_V = 128

NSA_HEADS = 4
NSA_CMP_STRIDE = 16
NSA_CMP_LEN = 2 * NSA_CMP_STRIDE
NSA_SEL_BLOCK = 64
NSA_SEL_TOPK = 16
NSA_WINDOW = 512
NSA_BRANCH_KV = 6

DSA_HEADS = 4
DSA_IDX_HEADS = 16
DSA_IDX_DIM = 64
DSA_TOPK = 256

MEM_LEN = 256
MEM_HEADS = 4

D_FF = 5632
D_MIX = (MOBA_HEADS + MLA_HEADS + NSA_HEADS + DSA_HEADS) * HEAD_DIM
DN_ALPHA = (2 * DEPTH) ** 0.25
DN_BETA = (8 * DEPTH) ** -0.25

IN_SIZES = (
    MOBA_HEADS * HEAD_DIM, MOBA_HEADS * HEAD_DIM, MOBA_HEADS * HEAD_DIM,
    MLA_Q_RANK, MLA_KV_RANK, MLA_ROPE,
    NSA_HEADS * HEAD_DIM, NSA_BRANCH_KV * HEAD_DIM, 3 * NSA_HEADS,
    DSA_HEADS * HEAD_DIM, DSA_HEADS * HEAD_DIM, DSA_HEADS * HEAD_DIM,
    DSA_IDX_HEADS * DSA_IDX_DIM, DSA_IDX_DIM, DSA_IDX_HEADS,
)
D_IN = sum(IN_SIZES)

kernel_name = 'hybrid_parallel_sparse_attention_trunk'


def layer_norm(x, g, b):
    xf = x.astype(jnp.float32)
    mu = jnp.mean(xf, -1, keepdims=True)
    var = jnp.mean(jnp.square(xf - mu), -1, keepdims=True)
    return ((xf - mu) * lax.rsqrt(var + LN_EPS) * g + b).astype(x.dtype)


def rms_norm(x, g):
    xf = x.astype(jnp.float32)
    return (xf * lax.rsqrt(jnp.mean(xf * xf, -1, keepdims=True) + RMS_EPS) * g).astype(x.dtype)


def rope(x, pos):
    d = x.shape[-1]
    inv = ROPE_THETA ** (-jnp.arange(0, d, 2, dtype=jnp.float32) / d)
    ang = pos.astype(jnp.float32)[..., None] * inv
    cos, sin = jnp.cos(ang)[:, :, None, :], jnp.sin(ang)[:, :, None, :]
    x1, x2 = jnp.split(x.astype(jnp.float32), 2, axis=-1)
    return jnp.concatenate([x1 * cos - x2 * sin, x1 * sin + x2 * cos], -1).astype(x.dtype)


def masked_softmax(logits, mask):
    lf = jnp.where(mask, logits.astype(jnp.float32), -jnp.inf)
    m = jnp.max(lf, -1, keepdims=True)
    m = jnp.where(jnp.isfinite(m), m, 0.0)
    e = jnp.where(mask, jnp.exp(lf - m), 0.0)
    s = jnp.sum(e, -1, keepdims=True)
    return e / jnp.where(s > 0, s, 1.0)


def to_chunks(t):
    b, s = t.shape[:2]
    return jnp.moveaxis(t.reshape((b, s // Q_CHUNK, Q_CHUNK) + t.shape[2:]), 1, 0)


def from_chunks(t):
    t = jnp.moveaxis(t, 0, 1)
    return t.reshape((t.shape[0], -1) + t.shape[3:])


def swiglu(x, w_gu, w_down):
    g, u = jnp.split(x @ w_gu, 2, axis=-1)
    return (jax.nn.silu(g) * u) @ w_down


def moba_attention(q, k, v, pos):
    B, S, H, D = q.shape
    q, k = rope(q, pos), rope(k, pos)
    nb = -(-S // MOBA_BLOCK)
    pad = ((0, 0), (0, nb * MOBA_BLOCK - S), (0, 0), (0, 0))
    kb = jnp.pad(k, pad).reshape(B, nb, MOBA_BLOCK, H, D).transpose(0, 3, 1, 2, 4)
    vb = jnp.pad(v, pad).reshape(B, nb, MOBA_BLOCK, H, D).transpose(0, 3, 1, 2, 4)
    k_mean = jnp.mean(kb.astype(jnp.float32), axis=3)
    n_sel = min(MOBA_TOPK, nb - 1)
    scale = D ** -0.5
    b_ix = jnp.arange(B)[:, None, None, None]
    h_ix = jnp.arange(H)[None, :, None, None]

    def block(args):
        qc, c = args
        qc = qc.transpose(0, 2, 1, 3)
        qpos = c * Q_CHUNK + jnp.arange(Q_CHUNK)
        cur = qpos[0] // MOBA_BLOCK
        k_own = lax.dynamic_index_in_dim(kb, cur, axis=2, keepdims=False)
        v_own = lax.dynamic_index_in_dim(vb, cur, axis=2, keepdims=False)
        own_pos = cur * MOBA_BLOCK + jnp.arange(MOBA_BLOCK)
        logits = [jnp.einsum('bhqd,bhkd->bhqk', qc, k_own)]
        masks = [jnp.broadcast_to(own_pos[None, :] <= qpos[:, None], (B, H, Q_CHUNK, MOBA_BLOCK))]
        if n_sel > 0:
            gate = jnp.einsum('bhqd,bhnd->bhqn', qc.astype(jnp.float32), k_mean)
            gate = jnp.where(jnp.arange(nb) < cur, gate, -jnp.inf)
            _, idx = lax.top_k(gate, n_sel)
            k_sel = kb[b_ix, h_ix, idx]
            v_sel = vb[b_ix, h_ix, idx]
            logits.append(jnp.einsum('bhqd,bhqnkd->bhqnk', qc, k_sel).reshape(B, H, Q_CHUNK, n_sel * MOBA_BLOCK))
            sel_ok = jnp.broadcast_to((idx < cur)[..., None], idx.shape + (MOBA_BLOCK,))
            masks.append(sel_ok.reshape(B, H, Q_CHUNK, n_sel * MOBA_BLOCK))
        p = masked_softmax(jnp.concatenate(logits, -1) * scale, jnp.concatenate(masks, -1)).astype(v.dtype)
        out = jnp.einsum('bhqk,bhkd->bqhd', p[..., :MOBA_BLOCK], v_own)
        if n_sel > 0:
            p_sel = p[..., MOBA_BLOCK:].reshape(B, H, Q_CHUNK, n_sel, MOBA_BLOCK)
            out = out + jnp.einsum('bhqnk,bhqnkd->bqhd', p_sel, v_sel)
        return out

    out = lax.map(block, (to_chunks(q), jnp.arange(S // Q_CHUNK)))
    return from_chunks(out).reshape(B, S, H * D)


def mla_attention(c_q, c_kv, k_rope, g_cq, g_ckv, w_uq, w_ukv, pos):
    B, S, _ = c_q.shape
    q = (rms_norm(c_q, g_cq) @ w_uq).reshape(B, S, MLA_HEADS, MLA_NOPE + MLA_ROPE)
    q_nope, q_rope = q[..., :MLA_NOPE], rope(q[..., MLA_NOPE:], pos)
    kv = (rms_norm(c_kv, g_ckv) @ w_ukv).reshape(B, S, MLA_HEADS, MLA_NOPE + MLA_V)
    k_nope, v = kv[..., :MLA_NOPE], kv[..., MLA_NOPE:]
    k_r = rope(k_rope[:, :, None, :], pos)[:, :, 0]
    scale = (MLA_NOPE + MLA_ROPE) ** -0.5
    kpos = jnp.arange(S)

    def block(args):
        qn, qr, c = args
        qpos = c * Q_CHUNK + jnp.arange(Q_CHUNK)
        logits = (jnp.einsum('bqhd,bkhd->bhqk', qn, k_nope) + jnp.einsum('bqhd,bkd->bhqk', qr, k_r)) * scale
        p = masked_softmax(logits, kpos[None, :] <= qpos[:, None]).astype(v.dtype)
        return jnp.einsum('bhqk,bkhd->bqhd', p, v)

    out = lax.map(block, (to_chunks(q_nope), to_chunks(q_rope), jnp.arange(S // Q_CHUNK)))
    return from_chunks(out).reshape(B, S, MLA_HEADS * MLA_V)


def nsa_attention(q, kv, gate_logits, cmp_pe, cmp_w1, cmp_w2, pos):
    B, S, H, D = q.shape
    k_cmp, v_cmp, k_slc, v_slc, k_win, v_win = (kv[:, :, i] for i in range(NSA_BRANCH_KV))
    scale = D ** -0.5
    t_pos = jnp.arange(S)

    n_cmp = S // NSA_CMP_STRIDE - 1

    def compress(t, i):
        tw = t.reshape(B, S // NSA_CMP_STRIDE, NSA_CMP_STRIDE, D)
        blocks = jnp.concatenate([tw[:, :-1], tw[:, 1:]], axis=2) + cmp_pe[i]
        return jax.nn.gelu(blocks.reshape(B, n_cmp, NSA_CMP_LEN * D) @ cmp_w1[i]) @ cmp_w2[i]

    kc, vc = compress(k_cmp, 0), compress(v_cmp, 1)
    cmp_end = jnp.arange(n_cmp) * NSA_CMP_STRIDE + NSA_CMP_LEN - 1
    p_cmp = masked_softmax(jnp.einsum('bshd,bnd->bhsn', q, kc) * scale, cmp_end[None, :] <= t_pos[:, None])
    o_cmp = jnp.einsum('bhsn,bnd->bshd', p_cmp.astype(vc.dtype), vc)

    ratio = NSA_SEL_BLOCK // NSA_CMP_STRIDE
    lead = NSA_CMP_LEN // NSA_CMP_STRIDE - 1
    n_blk = S // NSA_SEL_BLOCK
    pp = jnp.pad(jnp.sum(p_cmp, axis=1), ((0, 0), (0, 0), (lead, ratio * n_blk - n_cmp)))
    imp = sum(pp[..., r:r + ratio * n_blk:ratio] for r in range(ratio + lead))
    blk = jnp.arange(n_blk)
    cur = (t_pos // NSA_SEL_BLOCK)[:, None]
    forced = (blk == 0) | (blk == cur) | (blk == cur - 1)
    imp = jnp.where(blk > cur, -jnp.inf, jnp.where(forced, jnp.inf, imp))
    k_top = min(NSA_SEL_TOPK, n_blk)
    _, sel_idx = lax.top_k(imp, k_top)
    sel_valid = sel_idx <= cur

    q_r = rope(q, pos)
    k_slc = rope(k_slc[:, :, None], pos)[:, :, 0]
    k_win = rope(k_win[:, :, None], pos)[:, :, 0]
    ksb = k_slc.reshape(B, n_blk, NSA_SEL_BLOCK, D)
    vsb = v_slc.reshape(B, n_blk, NSA_SEL_BLOCK, D)
    kwp = jnp.pad(k_win, ((0, 0), (NSA_WINDOW, 0), (0, 0)))
    vwp = jnp.pad(v_win, ((0, 0), (NSA_WINDOW, 0), (0, 0)))
    b_ix = jnp.arange(B)[:, None, None]

    def block(args):
        qc, idx, valid, c = args
        qpos = c * Q_CHUNK + jnp.arange(Q_CHUNK)
        k_sel = ksb[b_ix, idx].reshape(B, Q_CHUNK, k_top * NSA_SEL_BLOCK, D)
        v_sel = vsb[b_ix, idx].reshape(B, Q_CHUNK, k_top * NSA_SEL_BLOCK, D)
        kpos = (idx[..., None] * NSA_SEL_BLOCK + jnp.arange(NSA_SEL_BLOCK)).reshape(B, Q_CHUNK, -1)
        m_sel = (kpos <= qpos[None, :, None]) & jnp.repeat(valid, NSA_SEL_BLOCK, axis=-1)
        p = masked_softmax(jnp.einsum('bqhd,bqkd->bhqk', qc, k_sel) * scale, m_sel[:, None])
        o_slc = jnp.einsum('bhqk,bqkd->bqhd', p.astype(v_sel.dtype), v_sel)
        start = c * Q_CHUNK
        k_w = lax.dynamic_slice_in_dim(kwp, start, NSA_WINDOW + Q_CHUNK, axis=1)
        v_w = lax.dynamic_slice_in_dim(vwp, start, NSA_WINDOW + Q_CHUNK, axis=1)
        wpos = start - NSA_WINDOW + jnp.arange(NSA_WINDOW + Q_CHUNK)
        diff = qpos[:, None] - wpos[None, :]
        m_win = (wpos[None, :] >= 0) & (diff >= 0) & (diff < NSA_WINDOW)
        p = masked_softmax(jnp.einsum('bqhd,bkd->bhqk', qc, k_w) * scale, m_win)
        o_win = jnp.einsum('bhqk,bkd->bqhd', p.astype(v_w.dtype), v_w)
        return o_slc, o_win

    o_slc, o_win = lax.map(block, (to_chunks(q_r), to_chunks(sel_idx), to_chunks(sel_valid), jnp.arange(S // Q_CHUNK)))
    o_slc, o_win = from_chunks(o_slc), from_chunks(o_win)
    g = jax.nn.sigmoid(gate_logits.astype(jnp.float32)).reshape(B, S, H, 3).astype(q.dtype)
    out = g[..., 0:1] * o_cmp + g[..., 1:2] * o_slc + g[..., 2:3] * o_win
    return out.reshape(B, S, H * D)


def dsa_attention(q, k, v, iq, ik, iw, pos):
    B, S, H, D = q.shape
    q, k = rope(q, pos), rope(k, pos)
    iq = rope(iq.reshape(B, S, DSA_IDX_HEADS, DSA_IDX_DIM), pos)
    ik = rope(ik[:, :, None], pos)[:, :, 0]
    keep = min(DSA_TOPK, S // 4)
    scale = D ** -0.5
    b_ix = jnp.arange(B)[:, None, None]
    kpos = jnp.arange(S)

    def block(args):
        qc, iqc, wc, c = args
        qpos = c * Q_CHUNK + jnp.arange(Q_CHUNK)
        dots = jnp.einsum('bqhd,bkd->bqhk', iqc, ik).astype(jnp.float32) * (DSA_IDX_DIM ** -0.5)
        score = jnp.einsum('bqhk,bqh->bqk', jax.nn.relu(dots), wc.astype(jnp.float32) * (DSA_IDX_HEADS ** -0.5))
        score = jnp.where(kpos[None, :] <= qpos[:, None], score, -jnp.inf)
        _, idx = lax.top_k(score, keep)
        valid = idx <= qpos[None, :, None]
        k_sel = k[b_ix, idx]
        v_sel = v[b_ix, idx]
        p = masked_softmax(jnp.einsum('bqhd,bqkhd->bhqk', qc, k_sel) * scale, valid[:, None])
        return jnp.einsum('bhqk,bqkhd->bqhd', p.astype(v_sel.dtype), v_sel)

    out = lax.map(block, (to_chunks(q), to_chunks(iq), to_chunks(iw), jnp.arange(S // Q_CHUNK)))
    return from_chunks(out).reshape(B, S, H * D)


def hybrid_mixer(x, pos, w_in, w_out, g_cq, g_ckv, w_uq, w_ukv, cmp_pe, cmp_w1, cmp_w2):
    B, S, _ = x.shape
    h = x @ w_in
    splits = np.cumsum(IN_SIZES)[:-1].tolist()
    (a_q, a_k, a_v, b_cq, b_ckv, b_kr, c_q, c_kv, c_g,
     d_q, d_k, d_v, d_iq, d_ik, d_iw) = jnp.split(h, splits, axis=-1)
    heads = lambda t, n: t.reshape(B, S, n, -1)
    o_a = moba_attention(heads(a_q, MOBA_HEADS), heads(a_k, MOBA_HEADS), heads(a_v, MOBA_HEADS), pos)
    o_b = mla_attention(b_cq, b_ckv, b_kr, g_cq, g_ckv, w_uq, w_ukv, pos)
    o_c = nsa_attention(heads(c_q, NSA_HEADS), heads(c_kv, NSA_BRANCH_KV), c_g, cmp_pe, cmp_w1, cmp_w2, pos)
    o_d = dsa_attention(heads(d_q, DSA_HEADS), heads(d_k, DSA_HEADS), heads(d_v, DSA_HEADS), d_iq, d_ik, d_iw, pos)
    return jnp.concatenate([o_a, o_b, o_c, o_d], axis=-1) @ w_out


def memory_cross_attention(x, mem, wq, wkv, wo):
    B, S, _ = x.shape
    M = mem.shape[1]
    q = (x @ wq).reshape(B, S, MEM_HEADS, HEAD_DIM)
    k, v = jnp.split((mem @ wkv).reshape(B, M, 2 * MEM_HEADS, HEAD_DIM), 2, axis=2)
    logits = jnp.einsum('bshd,bmhd->bhsm', q, k) * (HEAD_DIM ** -0.5)
    p = jax.nn.softmax(logits.astype(jnp.float32), axis=-1).astype(v.dtype)
    return jnp.einsum('bhsm,bmhd->bshd', p, v).reshape(B, S, MEM_HEADS * HEAD_DIM) @ wo


def setup_inputs(seed: int = 0) -> dict:
    key = jax.random.key(seed)
    ks = jax.random.split(key, 20)

    def w(k, shape, fan_in, scale=1.0):
        return jax.random.normal(k, shape, jnp.float32) * (scale * fan_in ** -0.5)

    hd = HEAD_DIM
    return {
        'x': jax.random.normal(ks[0], (BATCH, SEQ, D_MODEL), jnp.float32),
        'mem': jax.random.normal(ks[1], (BATCH, MEM_LEN, D_MODEL), jnp.float32),
        'positions': jnp.broadcast_to(jnp.arange(SEQ, dtype=jnp.int32), (BATCH, SEQ)),
        'ln_g': 1.0 + 0.02 * jax.random.normal(ks[2], (DEPTH, 4, D_MODEL), jnp.float32),
        'ln_b': 0.02 * jax.random.normal(ks[3], (DEPTH, 4, D_MODEL), jnp.float32),
        'ffn_w_gu': w(ks[4], (DEPTH, 2, D_MODEL, 2 * D_FF), D_MODEL),
        'ffn_w_down': w(ks[5], (DEPTH, 2, D_FF, D_MODEL), D_FF, DN_BETA),
        'w_in': w(ks[6], (DEPTH, D_MODEL, D_IN), D_MODEL),
        'w_out': w(ks[7], (DEPTH, D_MIX, D_MODEL), D_MIX, DN_BETA),
        'mla_g_cq': 1.0 + 0.02 * jax.random.normal(ks[8], (DEPTH, MLA_Q_RANK), jnp.float32),
        'mla_g_ckv': 1.0 + 0.02 * jax.random.normal(ks[9], (DEPTH, MLA_KV_RANK), jnp.float32),
        'mla_w_uq': w(ks[10], (DEPTH, MLA_Q_RANK, MLA_HEADS * (MLA_NOPE + MLA_ROPE)), MLA_Q_RANK),
        'mla_w_ukv': w(ks[11], (DEPTH, MLA_KV_RANK, MLA_HEADS * (MLA_NOPE + MLA_V)), MLA_KV_RANK),
        'nsa_cmp_pe': 0.02 * jax.random.normal(ks[12], (DEPTH, 2, NSA_CMP_LEN, hd), jnp.float32),
        'nsa_cmp_w1': w(ks[13], (DEPTH, 2, NSA_CMP_LEN * hd, hd), NSA_CMP_LEN * hd),
        'nsa_cmp_w2': w(ks[14], (DEPTH, 2, hd, hd), hd),
        'mem_wq': w(ks[15], (DEPTH, D_MODEL, MEM_HEADS * hd), D_MODEL),
        'mem_wkv': w(ks[16], (DEPTH, D_MODEL, 2 * MEM_HEADS * hd), D_MODEL),
        'mem_wo': w(ks[17], (DEPTH, MEM_HEADS * hd, D_MODEL), MEM_HEADS * hd, DN_BETA),
    }


def reference(x, mem, positions, ln_g, ln_b, ffn_w_gu, ffn_w_down, w_in, w_out,
              mla_g_cq, mla_g_ckv, mla_w_uq, mla_w_ukv, nsa_cmp_pe, nsa_cmp_w1, nsa_cmp_w2,
              mem_wq, mem_wkv, mem_wo):
    for l in range(DEPTH):
        x = layer_norm(DN_ALPHA * x + 0.5 * swiglu(x, ffn_w_gu[l, 0], ffn_w_down[l, 0]), ln_g[l, 0], ln_b[l, 0])
        mix = hybrid_mixer(x, positions, w_in[l], w_out[l], mla_g_cq[l], mla_g_ckv[l], mla_w_uq[l], mla_w_ukv[l],
                           nsa_cmp_pe[l], nsa_cmp_w1[l], nsa_cmp_w2[l])
        x = layer_norm(DN_ALPHA * x + mix, ln_g[l, 1], ln_b[l, 1])
        x = layer_norm(DN_ALPHA * x + memory_cross_attention(x, mem, mem_wq[l], mem_wkv[l], mem_wo[l]), ln_g[l, 2], ln_b[l, 2])
        x = layer_norm(DN_ALPHA * x + 0.5 * swiglu(x, ffn_w_gu[l, 1], ffn_w_down[l, 1]), ln_g[l, 3], ln_b[l, 3])
    return x
```

```python
import functools

import numpy as np
import jax
import jax.numpy as jnp
from jax import lax
from jax.experimental import pallas as pl
from jax.experimental.pallas import tpu as pltpu

F32 = jnp.float32
BF16 = jnp.bfloat16
I32 = jnp.int32

D_MODEL = 2048
DEPTH = 4
HEAD_DIM = 128
ROPE_THETA = 10000.0
LN_EPS = 1e-5
RMS_EPS = 1e-6

MOBA_BLOCK = 256
MOBA_TOPK = 3
MLA_NOPE = 128
MLA_ROPE = 64
NSA_CMP_STRIDE = 16
NSA_SEL_BLOCK = 64
NSA_SEL_TOPK = 16
NSA_WINDOW = 512
DSA_IDX_HEADS = 16
DSA_TOPK = 256
D_FF = 5632
DN_ALPHA = (2 * DEPTH) ** 0.25

NEG = -1e30
INT_MIN = -(2 ** 31)
VMEM_LIMIT = 56 * 1024 * 1024


def _cp(*sem):
    return pltpu.CompilerParams(dimension_semantics=sem, vmem_limit_bytes=VMEM_LIMIT)


def _resident(shape, index_map):
    return pl.BlockSpec(shape, index_map, pipeline_mode=pl.Buffered(1))


def _dot(a, b):
    return jnp.dot(a, b, preferred_element_type=F32)


def _dot_nt(a, b):
    return lax.dot_general(a, b, (((1,), (1,)), ((), ())), preferred_element_type=F32)


def _layer_norm(y, g, b):
    mu = jnp.mean(y, axis=-1, keepdims=True)
    d = y - mu
    var = jnp.mean(d * d, axis=-1, keepdims=True)
    return d * lax.rsqrt(var + LN_EPS) * g + b


def _rope128(x, cos, sin_signed):
    return x * cos + pltpu.roll(x, 64, 1) * sin_signed


def _rope64(x, cos, s_up, s_dn):
    return x * cos + pltpu.roll(x, 32, 1) * s_up + pltpu.roll(x, 96, 1) * s_dn


def _online_softmax_step(s, m, l, acc, v):
    m_new = jnp.maximum(m, jnp.max(s, axis=1, keepdims=True))
    alpha = jnp.exp(m - m_new)
    p = jnp.exp(s - m_new)
    l_new = alpha * l + jnp.sum(p, axis=1, keepdims=True)
    acc_new = alpha * acc + _dot(p.astype(BF16), v)
    return m_new, l_new, acc_new


def _top_k_mask(work, lane, k):
    n = work.shape[1]
    sel = jnp.zeros(work.shape, F32)
    for _ in range(k):
        m = jnp.max(work, axis=1, keepdims=True)
        idx = jnp.min(jnp.where(work == m, lane, n), axis=1, keepdims=True)
        pick = lane == jnp.where(m > -jnp.inf, idx, n)
        sel = jnp.where(pick, 1.0, sel)
        work = jnp.where(pick, -jnp.inf, work)
    return sel


def _rope_table_kernel(pos_ref, c_ref, cos128_ref, sin128_ref, cos64_ref, sup64_ref, sdn64_ref):
    pos = pos_ref[...]
    a128 = pos * c_ref[0:1, :]
    cos128_ref[...] = jnp.cos(a128)
    sin128_ref[...] = jnp.sin(a128) * c_ref[1:2, :]
    a64 = pos * c_ref[2:3, :]
    s64 = jnp.sin(a64)
    cos64_ref[...] = jnp.cos(a64)
    sup64_ref[...] = s64 * c_ref[3:4, :]
    sdn64_ref[...] = s64 * c_ref[4:5, :]


def _rope_consts():
    lane = np.arange(128)
    inv128 = (ROPE_THETA ** (-np.arange(0, 128, 2, dtype=np.float32) / 128)).astype(np.float32)
    inv64 = (ROPE_THETA ** (-np.arange(0, 64, 2, dtype=np.float32) / 64)).astype(np.float32)
    c = np.zeros((8, 128), np.float32)
    c[0] = inv128[lane % 64]
    c[1] = np.where(lane < 64, -1.0, 1.0)
    c[2] = inv64[(lane % 64) % 32]
    c[3] = np.where(lane % 64 >= 32, 1.0, 0.0)
    c[4] = np.where(lane % 64 < 32, -1.0, 0.0)
    return jnp.asarray(c)


def _rope_tables(positions):
    s = positions.shape[0]
    tm = min(512, s)
    pos = positions.astype(F32).reshape(s, 1)
    tab = jax.ShapeDtypeStruct((s, 128), F32)
    row = pl.BlockSpec((tm, 128), lambda i: (i, 0))
    return pl.pallas_call(
        _rope_table_kernel,
        grid=(s // tm,),
        in_specs=[pl.BlockSpec((tm, 1), lambda i: (i, 0)), pl.BlockSpec((8, 128), lambda i: (0, 0))],
        out_specs=[row] * 5,
        out_shape=[tab] * 5,
        compiler_params=_cp("parallel"),
        name="rope_tables",
    )(pos, _rope_consts())


def _ffn_kernel(x_ref, wg_ref, wu_ref, wd_ref, g_ref, b_ref, o_ref, xb_ref, acc_ref):
    j = pl.program_id(1)

    @pl.when(j == 0)
    def _():
        xb_ref[...] = x_ref[...].astype(BF16)
        acc_ref[...] = jnp.zeros_like(acc_ref)

    xb = xb_ref[...]
    g = _dot(xb, wg_ref[...])
    u = _dot(xb, wu_ref[...])
    h = (g * jax.nn.sigmoid(g) * u).astype(BF16)
    acc_ref[...] += _dot(h, wd_ref[...])

    @pl.when(j == pl.num_programs(1) - 1)
    def _():
        y = DN_ALPHA * x_ref[...] + 0.5 * acc_ref[...]
        o_ref[...] = _layer_norm(y, g_ref[...], b_ref[...])


def _ffn_ln(x, w_gu, w_down, g, b):
    s, d = x.shape
    f = w_down.shape[0]
    tm = min(512, s)
    tf = 512
    nf = f // tf
    return pl.pallas_call(
        _ffn_kernel,
        grid=(s // tm, nf),
        in_specs=[
            pl.BlockSpec((tm, d), lambda i, j: (i, 0)),
            pl.BlockSpec((d, tf), lambda i, j: (0, j)),
            pl.BlockSpec((d, tf), lambda i, j: (0, j + nf)),
            pl.BlockSpec((tf, d), lambda i, j: (j, 0)),
            pl.BlockSpec((1, d), lambda i, j: (0, 0)),
            pl.BlockSpec((1, d), lambda i, j: (0, 0)),
        ],
        out_specs=pl.BlockSpec((tm, d), lambda i, j: (i, 0)),
        out_shape=jax.ShapeDtypeStruct((s, d), F32),
        scratch_shapes=[pltpu.VMEM((tm, d), BF16), pltpu.VMEM((tm, d), F32)],
        compiler_params=_cp("parallel", "arbitrary"),
        name="ffn_ln",
    )(x, w_gu, w_gu, w_down, g, b)


def _proj_a_kernel(x_ref, w_ref, cos_ref, sin_ref, q_ref, k_ref, v_ref, km_ref):
    h = _dot(x_ref[...].astype(BF16), w_ref[...])
    cos, sin = cos_ref[...], sin_ref[...]
    nblk = km_ref.shape[0]
    for hd in range(4):
        lo = hd * HEAD_DIM
        q_ref[:, lo:lo + HEAD_DIM] = _rope128(h[:, lo:lo + HEAD_DIM], cos, sin)
        k = _rope128(h[:, 512 + lo:512 + lo + HEAD_DIM], cos, sin)
        k_ref[:, lo:lo + HEAD_DIM] = k.astype(BF16)
        for bi in range(nblk):
            kb = k[bi * MOBA_BLOCK:(bi + 1) * MOBA_BLOCK]
            km_ref[bi, :, lo:lo + HEAD_DIM] = jnp.mean(kb, axis=0, keepdims=True)
    v_ref[...] = h[:, 1024:1536].astype(BF16)


def _proj_a(x, w, cos128, sin128):
    s, d = x.shape
    tm = min(512, s)
    nblk = tm // MOBA_BLOCK
    row512 = pl.BlockSpec((tm, 512), lambda i: (i, 0))
    tab = pl.BlockSpec((tm, 128), lambda i: (i, 0))
    return pl.pallas_call(
        _proj_a_kernel,
        grid=(s // tm,),
        in_specs=[pl.BlockSpec((tm, d), lambda i: (i, 0)), _resident(w.shape, lambda i: (0, 0)), tab, tab],
        out_specs=[row512, row512, row512, pl.BlockSpec((nblk, 1, 512), lambda i: (i, 0, 0))],
        out_shape=[
            jax.ShapeDtypeStruct((s, 512), F32),
            jax.ShapeDtypeStruct((s, 512), BF16),
            jax.ShapeDtypeStruct((s, 512), BF16),
            jax.ShapeDtypeStruct((s // MOBA_BLOCK, 1, 512), F32),
        ],
        compiler_params=_cp("parallel"),
        name="proj_moba",
    )(x, w, cos128, sin128)


def _rms_norm(x, g):
    return x * lax.rsqrt(jnp.mean(x * x, axis=-1, keepdims=True) + RMS_EPS) * g


def _proj_b_kernel(x_ref, w_ref, gq_ref, gkv_ref, wuq_ref, wukv_ref, cos_ref, sup_ref, sdn_ref,
                   q_ref, k_ref, v_ref):
    h = _dot(x_ref[...].astype(BF16), w_ref[...])
    cos, sup, sdn = cos_ref[...], sup_ref[...], sdn_ref[...]
    cq = _rms_norm(h[:, 0:512], gq_ref[...]).astype(BF16)
    ckv = _rms_norm(h[:, 512:1024], gkv_ref[...]).astype(BF16)
    kr = _rope64(h[:, 1024:1152], cos, sup, sdn).astype(BF16)
    q = _dot(cq, wuq_ref[...])
    kv = _dot(ckv, wukv_ref[...])
    for hd in range(4):
        lo = hd * 256
        q_ref[:, lo:lo + 128] = q[:, lo:lo + 128].astype(BF16)
        q_ref[:, lo + 128:lo + 256] = _rope64(q[:, lo + 128:lo + 256], cos, sup, sdn).astype(BF16)
        k_ref[:, lo:lo + 128] = kv[:, lo:lo + 128].astype(BF16)
        k_ref[:, lo + 128:lo + 256] = kr
        v_ref[:, hd * 128:(hd + 1) * 128] = kv[:, lo + 128:lo + 256].astype(BF16)


def _proj_b(x, w, g_cq, g_ckv, w_uq, w_ukv, cos64, sup64, sdn64):
    s, d = x.shape
    tm = min(512, s)
    tab = pl.BlockSpec((tm, 128), lambda i: (i, 0))
    full = lambda a: _resident(a.shape, lambda i: (0, 0))
    return pl.pallas_call(
        _proj_b_kernel,
        grid=(s // tm,),
        in_specs=[pl.BlockSpec((tm, d), lambda i: (i, 0)), full(w), full(g_cq), full(g_ckv), full(w_uq),
                  full(w_ukv), tab, tab, tab],
        out_specs=[pl.BlockSpec((tm, 1024), lambda i: (i, 0)), pl.BlockSpec((tm, 1024), lambda i: (i, 0)),
                   pl.BlockSpec((tm, 512), lambda i: (i, 0))],
        out_shape=[jax.ShapeDtypeStruct((s, 1024), BF16), jax.ShapeDtypeStruct((s, 1024), BF16),
                   jax.ShapeDtypeStruct((s, 512), BF16)],
        compiler_params=_cp("parallel"),
        name="proj_mla",
    )(x, w, g_cq, g_ckv, w_uq, w_ukv, cos64, sup64, sdn64)


def _proj_c_kernel(x_ref, w_ref, cos_ref, sin_ref, q_ref, qr_ref, kcmp_ref, vcmp_ref, kslc_ref, vslc_ref,
                   kwin_ref, vwin_ref, gate_ref):
    h = _dot(x_ref[...].astype(BF16), w_ref[...])
    cos, sin = cos_ref[...], sin_ref[...]
    for hd in range(4):
        lo = hd * HEAD_DIM
        qh = h[:, lo:lo + HEAD_DIM]
        q_ref[:, lo:lo + HEAD_DIM] = qh.astype(BF16)
        qr_ref[:, lo:lo + HEAD_DIM] = _rope128(qh, cos, sin).astype(BF16)
    kv = lambda i: h[:, 512 + i * 128:512 + (i + 1) * 128]
    kcmp_ref[...] = kv(0).astype(BF16)
    vcmp_ref[...] = kv(1).astype(BF16)
    kslc_ref[...] = _rope128(kv(2), cos, sin).astype(BF16)
    vslc_ref[...] = kv(3).astype(BF16)
    kwin_ref[...] = _rope128(kv(4), cos, sin).astype(BF16)
    vwin_ref[...] = kv(5).astype(BF16)
    gate_ref[...] = jax.nn.sigmoid(h[:, 1280:1408])


def _proj_c(x, w, cos128, sin128):
    s, d = x.shape
    tm = min(512, s)
    r512 = pl.BlockSpec((tm, 512), lambda i: (i, 0))
    r128 = pl.BlockSpec((tm, 128), lambda i: (i, 0))
    b512 = jax.ShapeDtypeStruct((s, 512), BF16)
    b128 = jax.ShapeDtypeStruct((s, 128), BF16)
    return pl.pallas_call(
        _proj_c_kernel,
        grid=(s // tm,),
        in_specs=[pl.BlockSpec((tm, d), lambda i: (i, 0)), _resident(w.shape, lambda i: (0, 0)), r128, r128],
        out_specs=[r512, r512] + [r128] * 7,
        out_shape=[b512, b512] + [b128] * 6 + [jax.ShapeDtypeStruct((s, 128), F32)],
        compiler_params=_cp("parallel"),
        name="proj_nsa",
    )(x, w, cos128, sin128)


def _proj_d_kernel(x_ref, w_ref, cos_ref, sin_ref, c64_ref, sup_ref, sdn_ref,
                   q_ref, k_ref, v_ref, iq_ref, ik_ref, iw_ref):
    h = _dot(x_ref[...].astype(BF16), w_ref[...])
    cos, sin = cos_ref[...], sin_ref[...]
    c64, sup, sdn = c64_ref[...], sup_ref[...], sdn_ref[...]
    for hd in range(4):
        lo = hd * HEAD_DIM
        q_ref[:, lo:lo + HEAD_DIM] = _rope128(h[:, lo:lo + HEAD_DIM], cos, sin).astype(BF16)
        k_ref[:, lo:lo + HEAD_DIM] = _rope128(h[:, 512 + lo:512 + lo + HEAD_DIM], cos, sin).astype(BF16)
    v_ref[...] = h[:, 1024:1536].astype(BF16)
    for p in range(8):
        lo = 1536 + p * 128
        iq_ref[:, p * 128:(p + 1) * 128] = _rope64(h[:, lo:lo + 128], c64, sup, sdn).astype(BF16)
    ik = _rope64(h[:, 2560:2688], c64, sup, sdn)
    ik_ref[:, 0:128] = ik.astype(BF16)
    ik_ref[:, 128:256] = pltpu.roll(ik, 64, 1).astype(BF16)
    iw_ref[...] = h[:, 2688:2816] * (1.0 / 32.0)


def _proj_d(x, w, cos128, sin128, cos64, sup64, sdn64):
    s, d = x.shape
    tm = min(512, s)
    r = lambda n: pl.BlockSpec((tm, n), lambda i: (i, 0))
    return pl.pallas_call(
        _proj_d_kernel,
        grid=(s // tm,),
        in_specs=[pl.BlockSpec((tm, d), lambda i: (i, 0)), _resident(w.shape, lambda i: (0, 0))] + [r(128)] * 5,
        out_specs=[r(512), r(512), r(512), r(1024), r(256), r(128)],
        out_shape=[jax.ShapeDtypeStruct((s, 512), BF16)] * 3 + [
            jax.ShapeDtypeStruct((s, 1024), BF16), jax.ShapeDtypeStruct((s, 256), BF16),
            jax.ShapeDtypeStruct((s, 128), F32)],
        compiler_params=_cp("parallel"),
        name="proj_dsa",
    )(x, w, cos128, sin128, cos64, sup64, sdn64)


def _moba_kernel(q_ref, k_ref, v_ref, km_ref, o_ref):
    i = pl.program_id(1)
    tq = q_ref.shape[0]
    nb = km_ref.shape[0]
    scale = HEAD_DIM ** -0.5
    q32 = q_ref[...]
    qb = q32.astype(BF16)

    gate = lax.dot_general(q32, km_ref[...], (((1,), (1,)), ((), ())),
                           precision=lax.Precision.HIGHEST, preferred_element_type=F32)
    blk = lax.broadcasted_iota(I32, (tq, nb), 1)
    gate = jnp.where(blk < i, gate, -jnp.inf)
    sel = _top_k_mask(gate, blk, min(MOBA_TOPK, nb - 1))

    def body(j, carry):
        m, l, acc = carry
        off = pl.multiple_of(j * MOBA_BLOCK, MOBA_BLOCK)
        s = _dot_nt(qb, k_ref[pl.ds(off, MOBA_BLOCK), :]) * scale
        picked = jnp.sum(jnp.where(blk == j, sel, 0.0), axis=1, keepdims=True) > 0.5
        s = jnp.where(picked, s, NEG)
        return _online_softmax_step(s, m, l, acc, v_ref[pl.ds(off, MOBA_BLOCK), :])

    init = (jnp.full((tq, 1), NEG, F32), jnp.zeros((tq, 1), F32), jnp.zeros((tq, HEAD_DIM), F32))
    m, l, acc = lax.fori_loop(0, i, body, init)

    off = pl.multiple_of(i * MOBA_BLOCK, MOBA_BLOCK)
    s = _dot_nt(qb, k_ref[pl.ds(off, MOBA_BLOCK), :]) * scale
    row = lax.broadcasted_iota(I32, (tq, MOBA_BLOCK), 0)
    col = lax.broadcasted_iota(I32, (tq, MOBA_BLOCK), 1)
    s = jnp.where(col <= row, s, NEG)
    m, l, acc = _online_softmax_step(s, m, l, acc, v_ref[pl.ds(off, MOBA_BLOCK), :])
    o_ref[...] = (acc / l).astype(o_ref.dtype)


def _moba(q32, k, v, kmean):
    s = q32.shape[0]
    nb = s // MOBA_BLOCK
    km = kmean.reshape(nb, 512)
    return pl.pallas_call(
        _moba_kernel,
        grid=(4, nb),
        in_specs=[
            pl.BlockSpec((MOBA_BLOCK, HEAD_DIM), lambda h, i: (i, h)),
            pl.BlockSpec((s, HEAD_DIM), lambda h, i: (0, h)),
            pl.BlockSpec((s, HEAD_DIM), lambda h, i: (0, h)),
            pl.BlockSpec((nb, HEAD_DIM), lambda h, i: (0, h)),
        ],
        out_specs=pl.BlockSpec((MOBA_BLOCK, HEAD_DIM), lambda h, i: (i, h)),
        out_shape=jax.ShapeDtypeStruct((s, 512), BF16),
        compiler_params=_cp("parallel", "arbitrary"),
        name="moba_attn",
    )(q32, k, v, km)


MLA_TILE = 256


def _mla_kernel(q_ref, k_ref, v_ref, o_ref):
    i = pl.program_id(1)
    t = MLA_TILE
    scale = (MLA_NOPE + MLA_ROPE) ** -0.5
    q = q_ref[...]

    def body(j, carry):
        m, l, acc = carry
        off = pl.multiple_of(j * t, t)
        s = _dot_nt(q, k_ref[pl.ds(off, t), :]) * scale
        return _online_softmax_step(s, m, l, acc, v_ref[pl.ds(off, t), :])

    init = (jnp.full((t, 1), NEG, F32), jnp.zeros((t, 1), F32), jnp.zeros((t, HEAD_DIM), F32))
    m, l, acc = lax.fori_loop(0, i, body, init)

    off = pl.multiple_of(i * t, t)
    s = _dot_nt(q, k_ref[pl.ds(off, t), :]) * scale
    row = lax.broadcasted_iota(I32, (t, t), 0)
    col = lax.broadcasted_iota(I32, (t, t), 1)
    s = jnp.where(col <= row, s, NEG)
    m, l, acc = _online_softmax_step(s, m, l, acc, v_ref[pl.ds(off, t), :])
    o_ref[...] = (acc / l).astype(o_ref.dtype)


def _mla(qcat, kcat, v):
    s = qcat.shape[0]
    t = MLA_TILE
    return pl.pallas_call(
        _mla_kernel,
        grid=(4, s // t),
        in_specs=[
            pl.BlockSpec((t, 256), lambda h, i: (i, h)),
            pl.BlockSpec((s, 256), lambda h, i: (0, h)),
            pl.BlockSpec((s, HEAD_DIM), lambda h, i: (0, h)),
        ],
        out_specs=pl.BlockSpec((t, HEAD_DIM), lambda h, i: (i, h)),
        out_shape=jax.ShapeDtypeStruct((s, 512), BF16),
        compiler_params=_cp("parallel", "arbitrary"),
        name="mla_attn",
    )(qcat, kcat, v)


def _gelu_tanh(x):
    return 0.5 * x * (1.0 + jnp.tanh(0.7978845608028654 * (x + 0.044715 * x * x * x)))


def _nsa_compress_kernel(t_ref, pe_ref, w1_ref, w2_ref, o_ref):
    t = t_ref[0]
    w1 = w1_ref[0]
    half = t.shape[1]
    n = t.shape[0]
    lo = _dot(t, w1[0:half])
    hi = _dot(t, w1[half:2 * half])
    pe = _dot(pe_ref[0].astype(BF16), w1)
    pre = lo + pltpu.roll(hi, n - 1, 0) + pe
    o_ref[0] = _dot(_gelu_tanh(pre).astype(BF16), w2_ref[0]).astype(o_ref.dtype)


def _nsa_compress(kv16, pe, w1, w2):
    n = kv16.shape[1]
    blk = lambda a: pl.BlockSpec((1,) + a.shape[1:], lambda i: (i, 0, 0))
    return pl.pallas_call(
        _nsa_compress_kernel,
        grid=(2,),
        in_specs=[blk(kv16), blk(pe), blk(w1), blk(w2)],
        out_specs=pl.BlockSpec((1, n, HEAD_DIM), lambda i: (i, 0, 0)),
        out_shape=jax.ShapeDtypeStruct((2, n, HEAD_DIM), BF16),
        compiler_params=_cp("parallel"),
        name="nsa_compress",
    )(kv16, pe, w1, w2)


def _nsa_cmp_kernel(q_ref, kvc_ref, m_ref, o_ref, sel_ref):
    i = pl.program_id(0)
    tq = q_ref.shape[0]
    n = kvc_ref.shape[1]
    nblk = m_ref.shape[1]
    scale = HEAD_DIM ** -0.5
    kc, vc = kvc_ref[0], kvc_ref[1]
    t_pos = i * tq + lax.broadcasted_iota(I32, (tq, n), 0)
    cmp_end = lax.broadcasted_iota(I32, (tq, n), 1) * NSA_CMP_STRIDE + (2 * NSA_CMP_STRIDE - 1)
    ok = cmp_end <= t_pos
    p_sum = jnp.zeros((tq, n), F32)
    for hd in range(4):
        lo = hd * HEAD_DIM
        s = jnp.where(ok, _dot_nt(q_ref[:, lo:lo + HEAD_DIM], kc) * scale, NEG)
        m = jnp.max(s, axis=1, keepdims=True)
        e = jnp.where(ok, jnp.exp(s - m), 0.0)
        l = jnp.sum(e, axis=1, keepdims=True)
        p = e / jnp.where(l > 0, l, 1.0)
        p_sum = p_sum + p
        o_ref[:, lo:lo + HEAD_DIM] = _dot(p.astype(BF16), vc)

    imp = jnp.dot(p_sum, m_ref[...], precision=lax.Precision.HIGHEST, preferred_element_type=F32)
    blk = lax.broadcasted_iota(I32, (tq, nblk), 1)
    cur = (i * tq + lax.broadcasted_iota(I32, (tq, nblk), 0)) // NSA_SEL_BLOCK
    forced = (blk == 0) | (blk == cur) | (blk == cur - 1)
    imp = jnp.where(blk > cur, -jnp.inf, jnp.where(forced, jnp.inf, imp))
    sel_ref[...] = _top_k_mask(imp, blk, min(NSA_SEL_TOPK, nblk)).astype(sel_ref.dtype)


def _nsa_cmp(q, kvc, imp_map):
    s = q.shape[0]
    n = kvc.shape[1]
    nblk = s // NSA_SEL_BLOCK
    tq = min(256, s)
    return pl.pallas_call(
        _nsa_cmp_kernel,
        grid=(s // tq,),
        in_specs=[pl.BlockSpec((tq, 512), lambda i: (i, 0)),
                  pl.BlockSpec((2, n, HEAD_DIM), lambda i: (0, 0, 0)),
                  pl.BlockSpec((n, nblk), lambda i: (0, 0))],
        out_specs=[pl.BlockSpec((tq, 512), lambda i: (i, 0)), pl.BlockSpec((tq, nblk), lambda i: (i, 0))],
        out_shape=[jax.ShapeDtypeStruct((s, 512), F32), jax.ShapeDtypeStruct((s, nblk), BF16)],
        compiler_params=_cp("parallel"),
        name="nsa_cmp_select",
    )(q, kvc, imp_map)


NSA_TILE = 256


def _nsa_slc_win_kernel(q_ref, sel_ref, e_ref, ks_ref, vs_ref, kw_ref, vw_ref, g_ref, oc_ref, o_ref,
                        m_ref, l_ref, acc_ref):
    i = pl.program_id(0)
    t = NSA_TILE
    scale = HEAD_DIM ** -0.5
    sel = sel_ref[...]
    m_ref[...] = jnp.full(m_ref.shape, NEG, F32)
    l_ref[...] = jnp.zeros(l_ref.shape, F32)
    acc_ref[...] = jnp.zeros(acc_ref.shape, F32)

    def attend(j, keep):
        off = pl.multiple_of(j * t, t)
        k = ks_ref[pl.ds(off, t), :]
        v = vs_ref[pl.ds(off, t), :]
        for hd in range(4):
            lo = hd * HEAD_DIM
            s = jnp.where(keep, _dot_nt(q_ref[:, lo:lo + HEAD_DIM], k) * scale, NEG)
            m, l, acc = _online_softmax_step(s, m_ref[hd], l_ref[hd], acc_ref[hd], v)
            m_ref[hd] = m
            l_ref[hd] = l
            acc_ref[hd] = acc

    def body(j, carry):
        attend(j, _dot(sel, e_ref[j]) > 0.5)
        return carry

    lax.fori_loop(0, i, body, 0)
    row = lax.broadcasted_iota(I32, (t, t), 0)
    col = lax.broadcasted_iota(I32, (t, t), 1)
    attend(i, jnp.where(col <= row, _dot(sel, e_ref[i]), 0.0) > 0.5)

    wlen = NSA_WINDOW + t
    start = pl.multiple_of(jnp.maximum(i * t - NSA_WINDOW, 0), t)
    kw = kw_ref[pl.ds(start, wlen), :]
    vw = vw_ref[pl.ds(start, wlen), :]
    diff = (i * t + lax.broadcasted_iota(I32, (t, wlen), 0)) - (start + lax.broadcasted_iota(I32, (t, wlen), 1))
    in_win = (diff >= 0) & (diff < NSA_WINDOW)
    g = g_ref[...]
    for hd in range(4):
        lo = hd * HEAD_DIM
        s = jnp.where(in_win, _dot_nt(q_ref[:, lo:lo + HEAD_DIM], kw) * scale, NEG)
        p = jnp.exp(s - jnp.max(s, axis=1, keepdims=True))
        o_win = _dot(p.astype(BF16), vw) / jnp.sum(p, axis=1, keepdims=True)
        o_slc = acc_ref[hd] / l_ref[hd]
        out = (g[:, 3 * hd:3 * hd + 1] * oc_ref[:, lo:lo + HEAD_DIM] + g[:, 3 * hd + 1:3 * hd + 2] * o_slc
               + g[:, 3 * hd + 2:3 * hd + 3] * o_win)
        o_ref[:, lo:lo + HEAD_DIM] = out.astype(o_ref.dtype)


def _nsa_slc_win(q_r, sel, expand, kslc, vslc, kwin, vwin, gates, o_cmp):
    s = q_r.shape[0]
    t = NSA_TILE
    nblk = s // NSA_SEL_BLOCK
    full = lambda a: _resident(a.shape, lambda i: (0,) * a.ndim)
    return pl.pallas_call(
        _nsa_slc_win_kernel,
        grid=(s // t,),
        in_specs=[pl.BlockSpec((t, 512), lambda i: (i, 0)), pl.BlockSpec((t, nblk), lambda i: (i, 0)),
                  full(expand), full(kslc), full(vslc), full(kwin), full(vwin),
                  pl.BlockSpec((t, 128), lambda i: (i, 0)), pl.BlockSpec((t, 512), lambda i: (i, 0))],
        out_specs=pl.BlockSpec((t, 512), lambda i: (i, 0)),
        out_shape=jax.ShapeDtypeStruct((s, 512), BF16),
        scratch_shapes=[pltpu.VMEM((4, t, 1), F32), pltpu.VMEM((4, t, 1), F32), pltpu.VMEM((4, t, HEAD_DIM), F32)],
        compiler_params=_cp("parallel"),
        name="nsa_slc_win",
    )(q_r, sel, expand, kslc, vslc, kwin, vwin, gates, o_cmp)


DSA_TQ = 256
DSA_TK = 512


def _dsa_kernel(q_ref, iq_ref, iw_ref, k_ref, v_ref, ik_ref, o_ref, key_ref, wb_ref, m_ref, l_ref, acc_ref):
    i = pl.program_id(0)
    tq, tk = DSA_TQ, DSA_TK
    nch = tk // 128
    scale = HEAD_DIM ** -0.5
    n_tiles = (i * tq + tq + tk - 1) // tk
    row = lax.broadcasted_iota(I32, (tq, tk), 0) + i * tq
    col = lax.broadcasted_iota(I32, (tq, tk), 1)

    w = iw_ref[...]
    for hd in range(DSA_IDX_HEADS):
        wb_ref[hd] = jnp.broadcast_to(w[:, hd:hd + 1], (tq, 128))

    def score_body(j, carry):
        off = pl.multiple_of(j * tk, tk)
        ik_even = ik_ref[pl.ds(off, tk), 0:128]
        ik_odd = ik_ref[pl.ds(off, tk), 128:256]
        sc = jnp.zeros((tq, tk), F32)
        for p in range(DSA_IDX_HEADS // 2):
            x = iq_ref[:, p * 128:(p + 1) * 128]
            we = jnp.concatenate([wb_ref[2 * p]] * nch, axis=1)
            wo = jnp.concatenate([wb_ref[2 * p + 1]] * nch, axis=1)
            sc = sc + jnp.maximum(_dot_nt(x, ik_even), 0.0) * we
            sc = sc + jnp.maximum(_dot_nt(x, ik_odd), 0.0) * wo
        bits = pltpu.bitcast(sc, I32)
        key = jnp.where(bits >= 0, bits, bits ^ 0x7FFFFFFF)
        key_ref[j] = jnp.where(col + j * tk <= row, key, INT_MIN)
        return carry

    lax.fori_loop(0, n_tiles, score_body, 0)

    def count_ge(cand):
        cand128 = jnp.broadcast_to(cand, (tq, 128))

        def body(j, acc):
            blk = key_ref[j]
            for c in range(nch):
                acc = acc + jnp.where(blk[:, c * 128:(c + 1) * 128] >= cand128, 1, 0)
            return acc

        acc = lax.fori_loop(0, n_tiles, body, jnp.zeros((tq, 128), I32))
        return jnp.sum(acc, axis=1, keepdims=True)

    zero = jnp.zeros((tq, 1), I32)
    lo = jnp.where(count_ge(zero) >= DSA_TOPK, zero, INT_MIN)

    def bit_body(b, lo):
        cand = lo | lax.shift_left(jnp.int32(1), 30 - b)
        return jnp.where(count_ge(cand) >= DSA_TOPK, cand, lo)

    lo = lax.fori_loop(0, 31, bit_body, lo)
    thr128 = jnp.broadcast_to(jnp.maximum(lo, INT_MIN + 1), (tq, 128))

    m_ref[...] = jnp.full(m_ref.shape, NEG, F32)
    l_ref[...] = jnp.zeros(l_ref.shape, F32)
    acc_ref[...] = jnp.zeros(acc_ref.shape, F32)

    def attn_body(j, carry):
        off = pl.multiple_of(j * tk, tk)
        keep = key_ref[j] >= jnp.concatenate([thr128] * nch, axis=1)
        for hd in range(4):
            lo_ = hd * HEAD_DIM
            s = _dot_nt(q_ref[:, lo_:lo_ + HEAD_DIM], k_ref[pl.ds(off, tk), lo_:lo_ + HEAD_DIM]) * scale
            s = jnp.where(keep, s, NEG)
            m, l, acc = _online_softmax_step(s, m_ref[hd], l_ref[hd], acc_ref[hd],
                                             v_ref[pl.ds(off, tk), lo_:lo_ + HEAD_DIM])
            m_ref[hd] = m
            l_ref[hd] = l
            acc_ref[hd] = acc
        return carry

    lax.fori_loop(0, n_tiles, attn_body, 0)
    for hd in range(4):
        o_ref[:, hd * HEAD_DIM:(hd + 1) * HEAD_DIM] = (acc_ref[hd] / l_ref[hd]).astype(o_ref.dtype)


def _dsa(q, k, v, iq, ik2, iw):
    s = q.shape[0]
    tq, tk = DSA_TQ, DSA_TK
    full = lambda a: _resident(a.shape, lambda i: (0, 0))
    return pl.pallas_call(
        _dsa_kernel,
        grid=(s // tq,),
        in_specs=[pl.BlockSpec((tq, 512), lambda i: (i, 0)), pl.BlockSpec((tq, 1024), lambda i: (i, 0)),
                  pl.BlockSpec((tq, 128), lambda i: (i, 0)), full(k), full(v), full(ik2)],
        out_specs=pl.BlockSpec((tq, 512), lambda i: (i, 0)),
        out_shape=jax.ShapeDtypeStruct((s, 512), BF16),
        scratch_shapes=[pltpu.VMEM((s // tk, tq, tk), I32), pltpu.VMEM((DSA_IDX_HEADS, tq, 128), F32),
                        pltpu.VMEM((4, tq, 1), F32), pltpu.VMEM((4, tq, 1), F32),
                        pltpu.VMEM((4, tq, HEAD_DIM), F32)],
        compiler_params=_cp("parallel"),
        name="dsa_attn",
    )(q, iq, iw, k, v, ik2)


def _out_proj_kernel(oa_ref, ob_ref, oc_ref, od_ref, w_ref, x_ref, g_ref, b_ref, o_ref):
    mix = _dot(oa_ref[...], w_ref[0:512, :])
    mix = mix + _dot(ob_ref[...], w_ref[512:1024, :])
    mix = mix + _dot(oc_ref[...], w_ref[1024:1536, :])
    mix = mix + _dot(od_ref[...], w_ref[1536:2048, :])
    o_ref[...] = _layer_norm(DN_ALPHA * x_ref[...] + mix, g_ref[...], b_ref[...])


def _out_proj_ln(o_a, o_b, o_c, o_d, w_out, x, g, b):
    s, d = x.shape
    tm = min(512, s)
    r512 = pl.BlockSpec((tm, 512), lambda i: (i, 0))
    vec = pl.BlockSpec((1, d), lambda i: (0, 0))
    return pl.pallas_call(
        _out_proj_kernel,
        grid=(s // tm,),
        in_specs=[r512, r512, r512, r512, _resident(w_out.shape, lambda i: (0, 0)),
                  pl.BlockSpec((tm, d), lambda i: (i, 0)), vec, vec],
        out_specs=pl.BlockSpec((tm, d), lambda i: (i, 0)),
        out_shape=jax.ShapeDtypeStruct((s, d), F32),
        compiler_params=_cp("parallel"),
        name="out_proj_ln",
    )(o_a, o_b, o_c, o_d, w_out, x, g, b)


def _mem_kv_kernel(mem_ref, w_ref, o_ref):
    o_ref[...] = _dot(mem_ref[...].astype(BF16), w_ref[...]).astype(o_ref.dtype)


def _mem_kv(mem, wkv):
    m, d = mem.shape
    n = wkv.shape[1]
    return pl.pallas_call(
        _mem_kv_kernel,
        grid=(1,),
        in_specs=[pl.BlockSpec((m, d), lambda i: (0, 0)), pl.BlockSpec((d, n), lambda i: (0, 0))],
        out_specs=pl.BlockSpec((m, n), lambda i: (0, 0)),
        out_shape=jax.ShapeDtypeStruct((m, n), BF16),
        compiler_params=_cp("arbitrary"),
        name="mem_kv",
    )(mem, wkv)


def _mem_attn_kernel(x_ref, wq_ref, kv_ref, wo_ref, g_ref, b_ref, o_ref):
    x = x_ref[...]
    scale = HEAD_DIM ** -0.5
    q = _dot(x.astype(BF16), wq_ref[...]).astype(BF16)
    out = jnp.zeros(x.shape, F32)
    for hd in range(4):
        lo = hd * HEAD_DIM
        s = _dot_nt(q[:, lo:lo + HEAD_DIM], kv_ref[:, lo:lo + HEAD_DIM]) * scale
        p = jnp.exp(s - jnp.max(s, axis=1, keepdims=True))
        o = _dot(p.astype(BF16), kv_ref[:, 512 + lo:512 + lo + HEAD_DIM]) / jnp.sum(p, axis=1, keepdims=True)
        out = out + _dot(o.astype(BF16), wo_ref[lo:lo + HEAD_DIM, :])
    o_ref[...] = _layer_norm(DN_ALPHA * x + out, g_ref[...], b_ref[...])


def _mem_attn_ln(x, wq, kv, wo, g, b):
    s, d = x.shape
    tm = min(512, s)
    full = lambda a: _resident(a.shape, lambda i: (0, 0))
    vec = pl.BlockSpec((1, d), lambda i: (0, 0))
    return pl.pallas_call(
        _mem_attn_kernel,
        grid=(s // tm,),
        in_specs=[pl.BlockSpec((tm, d), lambda i: (i, 0)), full(wq), full(kv), full(wo), vec, vec],
        out_specs=pl.BlockSpec((tm, d), lambda i: (i, 0)),
        out_shape=jax.ShapeDtypeStruct((s, d), F32),
        compiler_params=_cp("parallel"),
        name="mem_attn_ln",
    )(x, wq, kv, wo, g, b)


def _pad_cols(w, n):
    return jnp.pad(w, ((0, 0), (0, n - w.shape[1])))


def _split_w_in(w_in):
    a = w_in[:, 0:1536]
    b = _pad_cols(w_in[:, 1536:2624], 1152)
    c = _pad_cols(w_in[:, 2624:3916], 1408)
    d = jnp.concatenate([w_in[:, 3916:6476], _pad_cols(w_in[:, 6476:6540], 128),
                         _pad_cols(w_in[:, 6540:6556], 128)], axis=1)
    return tuple(t.astype(BF16) for t in (a, b, c, d))


def _pad_w_uq(w_uq):
    r = w_uq.shape[0]
    w = jnp.pad(w_uq.reshape(r, 4, MLA_NOPE + MLA_ROPE), ((0, 0), (0, 0), (0, 64)))
    return w.reshape(r, 4 * 256).astype(BF16)


def _nsa_tables(s):
    n = s // NSA_CMP_STRIDE
    nblk = s // NSA_SEL_BLOCK
    ni = np.arange(n)[:, None]
    bi = np.arange(nblk)[None, :]
    imp_map = ((ni >= 4 * bi - 1) & (ni <= 4 * bi + 3)).astype(np.float32)
    t = NSA_TILE
    key_blk = (np.arange(s) // NSA_SEL_BLOCK).reshape(s // t, 1, t)
    expand = (np.arange(nblk)[None, :, None] == key_blk).astype(np.float32)
    return jnp.asarray(imp_map), jnp.asarray(expand, dtype=BF16)


def _mixer(x, tabs, nsa_tabs, w_groups, w_out, g_cq, g_ckv, w_uq, w_ukv, cmp_pe, cmp_w1, cmp_w2, ln_g, ln_b):
    cos128, sin128, cos64, sup64, sdn64 = tabs
    imp_map, expand = nsa_tabs
    s = x.shape[0]
    w_a, w_b, w_c, w_d = w_groups

    aq, ak, av, akm = _proj_a(x, w_a, cos128, sin128)
    o_a = _moba(aq, ak, av, akm)

    bq, bk, bv = _proj_b(x, w_b, g_cq, g_ckv, w_uq, w_ukv, cos64, sup64, sdn64)
    o_b = _mla(bq, bk, bv)

    cq, cqr, kcmp, vcmp, kslc, vslc, kwin, vwin, gates = _proj_c(x, w_c, cos128, sin128)
    kv16 = jnp.stack([kcmp, vcmp]).reshape(2, s // NSA_CMP_STRIDE, NSA_CMP_STRIDE * HEAD_DIM)
    kvc = _nsa_compress(kv16, cmp_pe, cmp_w1, cmp_w2)
    o_cmp, sel = _nsa_cmp(cq, kvc, imp_map)
    o_c = _nsa_slc_win(cqr, sel, expand, kslc, vslc, kwin, vwin, gates, o_cmp)

    dq, dk, dv, diq, dik, diw = _proj_d(x, w_d, cos128, sin128, cos64, sup64, sdn64)
    o_d = _dsa(dq, dk, dv, diq, dik, diw)

    return _out_proj_ln(o_a, o_b, o_c, o_d, w_out, x, ln_g, ln_b)


def kernel(x, mem, positions, ln_g, ln_b, ffn_w_gu, ffn_w_down, w_in, w_out, mla_g_cq, mla_g_ckv, mla_w_uq,
           mla_w_ukv, nsa_cmp_pe, nsa_cmp_w1, nsa_cmp_w2, mem_wq, mem_wkv, mem_wo):
    batch, s, d = x.shape
    outs = []
    nsa_tabs = _nsa_tables(s)
    for bi in range(batch):
        xb = x[bi]
        tabs = _rope_tables(positions[bi])
        for l in range(DEPTH):
            vec = lambda a, k: a[l, k][None, :]
            xb = _ffn_ln(xb, ffn_w_gu[l, 0].astype(BF16), ffn_w_down[l, 0].astype(BF16), vec(ln_g, 0), vec(ln_b, 0))
            xb = _mixer(xb, tabs, nsa_tabs, _split_w_in(w_in[l]), w_out[l].astype(BF16),
                        mla_g_cq[l][None, :], mla_g_ckv[l][None, :], _pad_w_uq(mla_w_uq[l]),
                        mla_w_ukv[l].astype(BF16), nsa_cmp_pe[l].reshape(2, 1, -1),
                        nsa_cmp_w1[l].astype(BF16), nsa_cmp_w2[l].astype(BF16), vec(ln_g, 1), vec(ln_b, 1))
            kv = _mem_kv(mem[bi], mem_wkv[l].astype(BF16))
            xb = _mem_attn_ln(xb, mem_wq[l].astype(BF16), kv, mem_wo[l].astype(BF16), vec(ln_g, 2), vec(ln_b, 2))
            xb = _ffn_ln(xb, ffn_w_gu[l, 1].astype(BF16), ffn_w_down[l, 1].astype(BF16), vec(ln_g, 3), vec(ln_b, 3))
        outs.append(xb)
    return jnp.stack(outs)
```

```python
import functools

import numpy as np
import jax
import jax.numpy as jnp
from jax import lax
from jax.experimental import pallas as pl
from jax.experimental.pallas import tpu as pltpu

F32 = jnp.float32
BF16 = jnp.bfloat16
I32 = jnp.int32

D_MODEL = 2048
DEPTH = 4
HEAD_DIM = 128
ROPE_THETA = 10000.0
LN_EPS = 1e-5
RMS_EPS = 1e-6

MOBA_BLOCK = 256
MOBA_TOPK = 3
MLA_NOPE = 128
MLA_ROPE = 64
NSA_CMP_STRIDE = 16
NSA_SEL_BLOCK = 64
NSA_SEL_TOPK = 16
NSA_WINDOW = 512
DSA_IDX_HEADS = 16
DSA_TOPK = 256
D_FF = 5632
DN_ALPHA = (2 * DEPTH) ** 0.25

NEG = -1e30
INT_MIN = -(2 ** 31)
VMEM_LIMIT = 56 * 1024 * 1024

ROW_TILE = 512
ATT_TQ = 512
ATT_TK = 1024
DSA_TQ = 256


def _cp(*sem):
    return pltpu.CompilerParams(dimension_semantics=sem, vmem_limit_bytes=VMEM_LIMIT)


def _wspec(w, block, index_map, resident=True):
    _, lead = w
    shape = (None,) * len(lead) + tuple(block)
    imap = lambda *g: tuple(lead) + tuple(index_map(*g))
    if resident:
        return pl.BlockSpec(shape, imap, pipeline_mode=pl.Buffered(1))
    return pl.BlockSpec(shape, imap)


def _wshape(w):
    arr, lead = w
    return arr.shape[len(lead):]


def _resident(shape, index_map):
    return pl.BlockSpec(shape, index_map, pipeline_mode=pl.Buffered(1))


def _dot(a, b):
    return jnp.dot(a, b, preferred_element_type=F32)


def _dot_nt(a, b):
    return lax.dot_general(a, b, (((1,), (1,)), ((), ())), preferred_element_type=F32)


def _layer_norm(y, g, b):
    mu = jnp.mean(y, axis=-1, keepdims=True)
    d = y - mu
    var = jnp.mean(d * d, axis=-1, keepdims=True)
    return d * lax.rsqrt(var + LN_EPS) * g + b


def _rope128(x, cos, sin_signed):
    return x * cos + pltpu.roll(x, 64, 1) * sin_signed


def _rope64(x, cos, s_up, s_dn):
    return x * cos + pltpu.roll(x, 32, 1) * s_up + pltpu.roll(x, 96, 1) * s_dn


def _flash_init(m_ref, l_ref, acc_ref):
    m_ref[...] = jnp.full(m_ref.shape, NEG, F32)
    l_ref[...] = jnp.zeros(l_ref.shape, F32)
    acc_ref[...] = jnp.zeros(acc_ref.shape, F32)


def _flash_tile(q, k, v, keep, scale, m_ref, l_ref, acc_ref, hd):
    s = _dot_nt(q, k) * scale
    if keep is not None:
        s = jnp.where(keep, s, NEG)
    m_old = m_ref[hd]
    m_new = jnp.maximum(m_old, jnp.max(s, axis=1, keepdims=True))
    alpha = jnp.exp(m_old - m_new)
    p = jnp.exp(s - m_new)
    l_ref[hd] = alpha * l_ref[hd] + jnp.sum(p, axis=1, keepdims=True)
    acc_ref[hd] = alpha * acc_ref[hd] + _dot(p.astype(BF16), v)
    m_ref[hd] = m_new


def _flash_scratch(heads, tq):
    return [pltpu.VMEM((heads, tq, 1), F32), pltpu.VMEM((heads, tq, 1), F32), pltpu.VMEM((heads, tq, HEAD_DIM), F32)]


def _top_k_mask(work, lane, k):
    n = work.shape[1]
    sel = jnp.zeros(work.shape, F32)
    for _ in range(k):
        m = jnp.max(work, axis=1, keepdims=True)
        idx = jnp.min(jnp.where(work == m, lane, n), axis=1, keepdims=True)
        pick = lane == jnp.where(m > -jnp.inf, idx, n)
        sel = jnp.where(pick, 1.0, sel)
        work = jnp.where(pick, -jnp.inf, work)
    return sel


def _block_expand_table(s, tk, block, nblk):
    key_blk = (np.arange(s) // block).reshape(s // tk, 1, tk)
    return jnp.asarray(np.arange(nblk)[None, :, None] == key_blk, dtype=BF16)


def _rope_table_kernel(pos_ref, c_ref, cos128_ref, sin128_ref, cos64_ref, sup64_ref, sdn64_ref):
    pos = pos_ref[...]
    a128 = pos * c_ref[0:1, :]
    cos128_ref[...] = jnp.cos(a128)
    sin128_ref[...] = jnp.sin(a128) * c_ref[1:2, :]
    a64 = pos * c_ref[2:3, :]
    s64 = jnp.sin(a64)
    cos64_ref[...] = jnp.cos(a64)
    sup64_ref[...] = s64 * c_ref[3:4, :]
    sdn64_ref[...] = s64 * c_ref[4:5, :]


def _rope_consts():
    lane = np.arange(128)
    inv128 = (ROPE_THETA ** (-np.arange(0, 128, 2, dtype=np.float32) / 128)).astype(np.float32)
    inv64 = (ROPE_THETA ** (-np.arange(0, 64, 2, dtype=np.float32) / 64)).astype(np.float32)
    c = np.zeros((8, 128), np.float32)
    c[0] = inv128[lane % 64]
    c[1] = np.where(lane < 64, -1.0, 1.0)
    c[2] = inv64[(lane % 64) % 32]
    c[3] = np.where(lane % 64 >= 32, 1.0, 0.0)
    c[4] = np.where(lane % 64 < 32, -1.0, 0.0)
    return jnp.asarray(c)


def _rope_tables(positions):
    s = positions.shape[0]
    tm = min(ROW_TILE, s)
    pos = positions.astype(F32).reshape(s, 1)
    tab = jax.ShapeDtypeStruct((s, 128), F32)
    row = pl.BlockSpec((tm, 128), lambda i: (i, 0))
    return pl.pallas_call(
        _rope_table_kernel,
        grid=(s // tm,),
        in_specs=[pl.BlockSpec((tm, 1), lambda i: (i, 0)), pl.BlockSpec((8, 128), lambda i: (0, 0))],
        out_specs=[row] * 5,
        out_shape=[tab] * 5,
        compiler_params=_cp("parallel"),
        name="rope_tables",
    )(pos, _rope_consts())


def _ffn_kernel(x_ref, wg_ref, wu_ref, wd_ref, g_ref, b_ref, o_ref, xb_ref, acc_ref):
    j = pl.program_id(1)

    @pl.when(j == 0)
    def _():
        xb_ref[...] = x_ref[...].astype(BF16)
        acc_ref[...] = jnp.zeros_like(acc_ref)

    xb = xb_ref[...]
    g = _dot(xb, wg_ref[...])
    u = _dot(xb, wu_ref[...])
    h = (g * jax.nn.sigmoid(g) * u).astype(BF16)
    acc_ref[...] += _dot(h, wd_ref[...])

    @pl.when(j == pl.num_programs(1) - 1)
    def _():
        y = DN_ALPHA * x_ref[...] + 0.5 * acc_ref[...]
        o_ref[...] = _layer_norm(y, g_ref[...], b_ref[...])


def _ffn_ln(x, w_gu, w_down, g, b):
    s, d = x.shape
    f = _wshape(w_down)[0]
    tm = min(ROW_TILE, s)
    tf = 512
    nf = f // tf
    return pl.pallas_call(
        _ffn_kernel,
        grid=(s // tm, nf),
        in_specs=[
            pl.BlockSpec((tm, d), lambda i, j: (i, 0)),
            _wspec(w_gu, (d, tf), lambda i, j: (0, j), resident=False),
            _wspec(w_gu, (d, tf), lambda i, j: (0, j + nf), resident=False),
            _wspec(w_down, (tf, d), lambda i, j: (j, 0), resident=False),
            _wspec(g, (1, d), lambda i, j: (0, 0)),
            _wspec(b, (1, d), lambda i, j: (0, 0)),
        ],
        out_specs=pl.BlockSpec((tm, d), lambda i, j: (i, 0)),
        out_shape=jax.ShapeDtypeStruct((s, d), F32),
        scratch_shapes=[pltpu.VMEM((tm, d), BF16), pltpu.VMEM((tm, d), F32)],
        compiler_params=_cp("parallel", "arbitrary"),
        name="ffn_ln",
    )(x, w_gu[0], w_gu[0], w_down[0], g[0], b[0])


def _proj_a_kernel(x_ref, w_ref, cos_ref, sin_ref, q_ref, k_ref, v_ref, km_ref):
    h = _dot(x_ref[...].astype(BF16), w_ref[...])
    cos, sin = cos_ref[...], sin_ref[...]
    nblk = km_ref.shape[0]
    for hd in range(4):
        lo = hd * HEAD_DIM
        q_ref[:, lo:lo + HEAD_DIM] = _rope128(h[:, lo:lo + HEAD_DIM], cos, sin)
        k = _rope128(h[:, 512 + lo:512 + lo + HEAD_DIM], cos, sin)
        k_ref[:, lo:lo + HEAD_DIM] = k.astype(BF16)
        for bi in range(nblk):
            kb = k[bi * MOBA_BLOCK:(bi + 1) * MOBA_BLOCK]
            km_ref[bi, :, lo:lo + HEAD_DIM] = jnp.mean(kb, axis=0, keepdims=True)
    v_ref[...] = h[:, 1024:1536].astype(BF16)


def _proj_a(x, w, cos128, sin128):
    s, d = x.shape
    tm = min(ROW_TILE, s)
    nblk = tm // MOBA_BLOCK
    row512 = pl.BlockSpec((tm, 512), lambda i: (i, 0))
    tab = pl.BlockSpec((tm, 128), lambda i: (i, 0))
    return pl.pallas_call(
        _proj_a_kernel,
        grid=(s // tm,),
        in_specs=[pl.BlockSpec((tm, d), lambda i: (i, 0)), _wspec(w, _wshape(w), lambda i: (0, 0)), tab, tab],
        out_specs=[row512, row512, row512, pl.BlockSpec((nblk, 1, 512), lambda i: (i, 0, 0))],
        out_shape=[
            jax.ShapeDtypeStruct((s, 512), F32),
            jax.ShapeDtypeStruct((s, 512), BF16),
            jax.ShapeDtypeStruct((s, 512), BF16),
            jax.ShapeDtypeStruct((s // MOBA_BLOCK, 1, 512), F32),
        ],
        compiler_params=_cp("parallel"),
        name="proj_moba",
    )(x, w[0], cos128, sin128)


def _rms_norm(x, g):
    return x * lax.rsqrt(jnp.mean(x * x, axis=-1, keepdims=True) + RMS_EPS) * g


def _proj_b_kernel(x_ref, w_ref, gq_ref, gkv_ref, wuq_ref, wukv_ref, cos_ref, sup_ref, sdn_ref,
                   q_ref, k_ref, v_ref):
    h = _dot(x_ref[...].astype(BF16), w_ref[...])
    cos, sup, sdn = cos_ref[...], sup_ref[...], sdn_ref[...]
    cq = _rms_norm(h[:, 0:512], gq_ref[...]).astype(BF16)
    ckv = _rms_norm(h[:, 512:1024], gkv_ref[...]).astype(BF16)
    kr = _rope64(h[:, 1024:1152], cos, sup, sdn).astype(BF16)
    q = _dot(cq, wuq_ref[...])
    kv = _dot(ckv, wukv_ref[...])
    for hd in range(4):
        lo = hd * 256
        q_ref[:, lo:lo + 128] = q[:, lo:lo + 128].astype(BF16)
        q_ref[:, lo + 128:lo + 256] = _rope64(q[:, lo + 128:lo + 256], cos, sup, sdn).astype(BF16)
        k_ref[:, lo:lo + 128] = kv[:, lo:lo + 128].astype(BF16)
        k_ref[:, lo + 128:lo + 256] = kr
        v_ref[:, hd * 128:(hd + 1) * 128] = kv[:, lo + 128:lo + 256].astype(BF16)


def _proj_b(x, w, g_cq, g_ckv, w_uq, w_ukv, cos64, sup64, sdn64):
    s, d = x.shape
    tm = min(ROW_TILE, s)
    tab = pl.BlockSpec((tm, 128), lambda i: (i, 0))
    full = lambda a: _wspec(a, _wshape(a), lambda i: (0, 0))
    return pl.pallas_call(
        _proj_b_kernel,
        grid=(s // tm,),
        in_specs=[pl.BlockSpec((tm, d), lambda i: (i, 0)), full(w), full(g_cq), full(g_ckv), full(w_uq),
                  full(w_ukv), tab, tab, tab],
        out_specs=[pl.BlockSpec((tm, 1024), lambda i: (i, 0)), pl.BlockSpec((tm, 1024), lambda i: (i, 0)),
                   pl.BlockSpec((tm, 512), lambda i: (i, 0))],
        out_shape=[jax.ShapeDtypeStruct((s, 1024), BF16), jax.ShapeDtypeStruct((s, 1024), BF16),
                   jax.ShapeDtypeStruct((s, 512), BF16)],
        compiler_params=_cp("parallel"),
        name="proj_mla",
    )(x, w[0], g_cq[0], g_ckv[0], w_uq[0], w_ukv[0], cos64, sup64, sdn64)


def _proj_c_kernel(x_ref, w_ref, cos_ref, sin_ref, q_ref, qr_ref, kvcmp_ref, kslc_ref, vslc_ref,
                   kwin_ref, vwin_ref, gate_ref):
    h = _dot(x_ref[...].astype(BF16), w_ref[...])
    cos, sin = cos_ref[...], sin_ref[...]
    for hd in range(4):
        lo = hd * HEAD_DIM
        qh = h[:, lo:lo + HEAD_DIM]
        q_ref[:, lo:lo + HEAD_DIM] = qh.astype(BF16)
        qr_ref[:, lo:lo + HEAD_DIM] = _rope128(qh, cos, sin).astype(BF16)
    kv = lambda i: h[:, 512 + i * 128:512 + (i + 1) * 128]
    kvcmp_ref[0] = kv(0).astype(BF16)
    kvcmp_ref[1] = kv(1).astype(BF16)
    kslc_ref[...] = _rope128(kv(2), cos, sin).astype(BF16)
    vslc_ref[...] = kv(3).astype(BF16)
    kwin_ref[...] = _rope128(kv(4), cos, sin).astype(BF16)
    vwin_ref[...] = kv(5).astype(BF16)
    gate_ref[...] = jax.nn.sigmoid(h[:, 1280:1408])


def _proj_c(x, w, cos128, sin128):
    s, d = x.shape
    tm = min(ROW_TILE, s)
    r512 = pl.BlockSpec((tm, 512), lambda i: (i, 0))
    r128 = pl.BlockSpec((tm, 128), lambda i: (i, 0))
    b512 = jax.ShapeDtypeStruct((s, 512), BF16)
    b128 = jax.ShapeDtypeStruct((s, 128), BF16)
    return pl.pallas_call(
        _proj_c_kernel,
        grid=(s // tm,),
        in_specs=[pl.BlockSpec((tm, d), lambda i: (i, 0)), _wspec(w, _wshape(w), lambda i: (0, 0)), r128, r128],
        out_specs=[r512, r512, pl.BlockSpec((2, tm, 128), lambda i: (0, i, 0))] + [r128] * 5,
        out_shape=[b512, b512, jax.ShapeDtypeStruct((2, s, 128), BF16)] + [b128] * 4
                  + [jax.ShapeDtypeStruct((s, 128), F32)],
        compiler_params=_cp("parallel"),
        name="proj_nsa",
    )(x, w[0], cos128, sin128)


def _proj_d_kernel(x_ref, w_ref, cos_ref, sin_ref, c64_ref, sup_ref, sdn_ref,
                   q_ref, k_ref, v_ref, iq_ref, ik_ref, iw_ref):
    h = _dot(x_ref[...].astype(BF16), w_ref[...])
    cos, sin = cos_ref[...], sin_ref[...]
    c64, sup, sdn = c64_ref[...], sup_ref[...], sdn_ref[...]
    for hd in range(4):
        lo = hd * HEAD_DIM
        q_ref[:, lo:lo + HEAD_DIM] = _rope128(h[:, lo:lo + HEAD_DIM], cos, sin).astype(BF16)
        k_ref[:, lo:lo + HEAD_DIM] = _rope128(h[:, 512 + lo:512 + lo + HEAD_DIM], cos, sin).astype(BF16)
    v_ref[...] = h[:, 1024:1536].astype(BF16)
    for p in range(8):
        lo = 1536 + p * 128
        iq_ref[:, p * 128:(p + 1) * 128] = _rope64(h[:, lo:lo + 128], c64, sup, sdn).astype(BF16)
    ik = _rope64(h[:, 2560:2688], c64, sup, sdn)
    ik_ref[:, 0:128] = ik.astype(BF16)
    ik_ref[:, 128:256] = pltpu.roll(ik, 64, 1).astype(BF16)
    iw_ref[...] = h[:, 2688:2816] * (1.0 / 32.0)


def _proj_d(x, w, cos128, sin128, cos64, sup64, sdn64):
    s, d = x.shape
    tm = min(ROW_TILE, s)
    r = lambda n: pl.BlockSpec((tm, n), lambda i: (i, 0))
    return pl.pallas_call(
        _proj_d_kernel,
        grid=(s // tm,),
        in_specs=[pl.BlockSpec((tm, d), lambda i: (i, 0)), _wspec(w, _wshape(w), lambda i: (0, 0))] + [r(128)] * 5,
        out_specs=[r(512), r(512), r(512), r(1024), r(256), r(128)],
        out_shape=[jax.ShapeDtypeStruct((s, 512), BF16)] * 3 + [
            jax.ShapeDtypeStruct((s, 1024), BF16), jax.ShapeDtypeStruct((s, 256), BF16),
            jax.ShapeDtypeStruct((s, 128), F32)],
        compiler_params=_cp("parallel"),
        name="proj_dsa",
    )(x, w[0], cos128, sin128, cos64, sup64, sdn64)


def _moba_kernel(q_ref, k_ref, v_ref, km_ref, e_ref, o_ref, m_ref, l_ref, acc_ref, *, tk):
    i = pl.program_id(1)
    tq = q_ref.shape[0]
    nb = km_ref.shape[0]
    scale = HEAD_DIM ** -0.5
    q32 = q_ref[...]
    qb = q32.astype(BF16)

    gate = lax.dot_general(q32, km_ref[...], (((1,), (1,)), ((), ())),
                           precision=lax.Precision.HIGHEST, preferred_element_type=F32)
    blk = lax.broadcasted_iota(I32, (tq, nb), 1)
    cur = (i * tq + lax.broadcasted_iota(I32, (tq, nb), 0)) // MOBA_BLOCK
    gate = jnp.where(blk < cur, gate, -jnp.inf)
    sel = _top_k_mask(gate, blk, min(MOBA_TOPK, nb - 1))
    sel = jnp.where(blk == cur, 1.0, sel).astype(BF16)

    _flash_init(m_ref, l_ref, acc_ref)

    def tile(j, causal):
        off = pl.multiple_of(j * tk, tk)
        chosen = _dot(sel, e_ref[j])
        if causal:
            row = i * tq + lax.broadcasted_iota(I32, (tq, tk), 0)
            col = off + lax.broadcasted_iota(I32, (tq, tk), 1)
            chosen = jnp.where(col <= row, chosen, 0.0)
        _flash_tile(qb, k_ref[pl.ds(off, tk), :], v_ref[pl.ds(off, tk), :], chosen > 0.5, scale,
                    m_ref, l_ref, acc_ref, 0)

    n_past = (i * tq) // tk

    def body(j, carry):
        tile(j, False)
        return carry

    lax.fori_loop(0, n_past, body, 0)
    tile(n_past, True)
    o_ref[...] = (acc_ref[0] / l_ref[0]).astype(o_ref.dtype)


def _moba(q32, k, v, kmean):
    s = q32.shape[0]
    nb = s // MOBA_BLOCK
    tq = min(ATT_TQ, s)
    tk = min(ATT_TK, s)
    km = kmean.reshape(nb, 512)
    expand = _block_expand_table(s, tk, MOBA_BLOCK, nb)
    return pl.pallas_call(
        functools.partial(_moba_kernel, tk=tk),
        grid=(4, s // tq),
        in_specs=[
            pl.BlockSpec((tq, HEAD_DIM), lambda h, i: (i, h)),
            pl.BlockSpec((s, HEAD_DIM), lambda h, i: (0, h)),
            pl.BlockSpec((s, HEAD_DIM), lambda h, i: (0, h)),
            pl.BlockSpec((nb, HEAD_DIM), lambda h, i: (0, h)),
            _resident(expand.shape, lambda h, i: (0, 0, 0)),
        ],
        out_specs=pl.BlockSpec((tq, HEAD_DIM), lambda h, i: (i, h)),
        out_shape=jax.ShapeDtypeStruct((s, 512), BF16),
        scratch_shapes=_flash_scratch(1, tq),
        compiler_params=_cp("parallel", "arbitrary"),
        name="moba_attn",
    )(q32, k, v, km, expand)


def _mla_kernel(q_ref, k_ref, v_ref, o_ref, m_ref, l_ref, acc_ref, *, tk):
    i = pl.program_id(1)
    tq = q_ref.shape[0]
    scale = (MLA_NOPE + MLA_ROPE) ** -0.5
    q = q_ref[...]
    _flash_init(m_ref, l_ref, acc_ref)
    n_past = (i * tq) // tk

    def body(j, carry):
        off = pl.multiple_of(j * tk, tk)
        _flash_tile(q, k_ref[pl.ds(off, tk), :], v_ref[pl.ds(off, tk), :], None, scale, m_ref, l_ref, acc_ref, 0)
        return carry

    lax.fori_loop(0, n_past, body, 0)
    off = pl.multiple_of(n_past * tk, tk)
    row = i * tq + lax.broadcasted_iota(I32, (tq, tk), 0)
    col = off + lax.broadcasted_iota(I32, (tq, tk), 1)
    _flash_tile(q, k_ref[pl.ds(off, tk), :], v_ref[pl.ds(off, tk), :], col <= row, scale, m_ref, l_ref, acc_ref, 0)
    o_ref[...] = (acc_ref[0] / l_ref[0]).astype(o_ref.dtype)


def _mla(qcat, kcat, v):
    s = qcat.shape[0]
    tq = min(ATT_TQ, s)
    tk = min(ATT_TK, s)
    return pl.pallas_call(
        functools.partial(_mla_kernel, tk=tk),
        grid=(4, s // tq),
        in_specs=[
            pl.BlockSpec((tq, 256), lambda h, i: (i, h)),
            pl.BlockSpec((s, 256), lambda h, i: (0, h)),
            pl.BlockSpec((s, HEAD_DIM), lambda h, i: (0, h)),
        ],
        out_specs=pl.BlockSpec((tq, HEAD_DIM), lambda h, i: (i, h)),
        out_shape=jax.ShapeDtypeStruct((s, 512), BF16),
        scratch_shapes=_flash_scratch(1, tq),
        compiler_params=_cp("parallel", "arbitrary"),
        name="mla_attn",
    )(qcat, kcat, v)


def _gelu_tanh(x):
    return 0.5 * x * (1.0 + jnp.tanh(0.7978845608028654 * (x + 0.044715 * x * x * x)))


def _nsa_compress_kernel(t_ref, pe_ref, w1_ref, w2_ref, o_ref):
    t = t_ref[0]
    w1 = w1_ref[...]
    half = t.shape[1]
    n = t.shape[0]
    lo = _dot(t, w1[0:half])
    hi = _dot(t, w1[half:2 * half])
    pe = _dot(pe_ref[...].astype(BF16), w1)
    pre = lo + pltpu.roll(hi, n - 1, 0) + pe
    o_ref[0] = _dot(_gelu_tanh(pre).astype(BF16), w2_ref[...]).astype(o_ref.dtype)


def _nsa_compress(kv16, pe, w1, w2):
    n = kv16.shape[1]
    sub = lambda a: _wspec(a, (None,) + _wshape(a)[1:], lambda i: (i, 0, 0), resident=False)
    return pl.pallas_call(
        _nsa_compress_kernel,
        grid=(2,),
        in_specs=[pl.BlockSpec((1,) + kv16.shape[1:], lambda i: (i, 0, 0)), sub(pe), sub(w1), sub(w2)],
        out_specs=pl.BlockSpec((1, n, HEAD_DIM), lambda i: (i, 0, 0)),
        out_shape=jax.ShapeDtypeStruct((2, n, HEAD_DIM), BF16),
        compiler_params=_cp("parallel"),
        name="nsa_compress",
    )(kv16, pe[0], w1[0], w2[0])


def _nsa_cmp_kernel(q_ref, kvc_ref, m_ref, o_ref, sel_ref):
    i = pl.program_id(0)
    tq = q_ref.shape[0]
    n = kvc_ref.shape[1]
    nblk = m_ref.shape[1]
    scale = HEAD_DIM ** -0.5
    kc, vc = kvc_ref[0], kvc_ref[1]
    t_pos = i * tq + lax.broadcasted_iota(I32, (tq, n), 0)
    cmp_end = lax.broadcasted_iota(I32, (tq, n), 1) * NSA_CMP_STRIDE + (2 * NSA_CMP_STRIDE - 1)
    ok = cmp_end <= t_pos
    p_sum = jnp.zeros((tq, n), F32)
    for hd in range(4):
        lo = hd * HEAD_DIM
        s = jnp.where(ok, _dot_nt(q_ref[:, lo:lo + HEAD_DIM], kc) * scale, NEG)
        m = jnp.max(s, axis=1, keepdims=True)
        e = jnp.where(ok, jnp.exp(s - m), 0.0)
        l = jnp.sum(e, axis=1, keepdims=True)
        p = e / jnp.where(l > 0, l, 1.0)
        p_sum = p_sum + p
        o_ref[:, lo:lo + HEAD_DIM] = _dot(p.astype(BF16), vc)

    imp = jnp.dot(p_sum, m_ref[...], precision=lax.Precision.HIGHEST, preferred_element_type=F32)
    blk = lax.broadcasted_iota(I32, (tq, nblk), 1)
    cur = (i * tq + lax.broadcasted_iota(I32, (tq, nblk), 0)) // NSA_SEL_BLOCK
    forced = (blk == 0) | (blk == cur) | (blk == cur - 1)
    imp = jnp.where(blk > cur, -jnp.inf, jnp.where(forced, jnp.inf, imp))
    sel_ref[...] = _top_k_mask(imp, blk, min(NSA_SEL_TOPK, nblk)).astype(sel_ref.dtype)


def _nsa_cmp(q, kvc, imp_map):
    s = q.shape[0]
    n = kvc.shape[1]
    nblk = s // NSA_SEL_BLOCK
    tq = min(256, s)
    return pl.pallas_call(
        _nsa_cmp_kernel,
        grid=(s // tq,),
        in_specs=[pl.BlockSpec((tq, 512), lambda i: (i, 0)),
                  _resident((2, n, HEAD_DIM), lambda i: (0, 0, 0)),
                  _resident((n, nblk), lambda i: (0, 0))],
        out_specs=[pl.BlockSpec((tq, 512), lambda i: (i, 0)), pl.BlockSpec((tq, nblk), lambda i: (i, 0))],
        out_shape=[jax.ShapeDtypeStruct((s, 512), F32), jax.ShapeDtypeStruct((s, nblk), BF16)],
        compiler_params=_cp("parallel"),
        name="nsa_cmp_select",
    )(q, kvc, imp_map)


def _nsa_slc_win_kernel(q_ref, sel_ref, e_ref, ks_ref, vs_ref, kw_ref, vw_ref, g_ref, oc_ref, o_ref,
                        m_ref, l_ref, acc_ref, *, tk):
    i = pl.program_id(0)
    tq = q_ref.shape[0]
    scale = HEAD_DIM ** -0.5
    sel = sel_ref[...]
    _flash_init(m_ref, l_ref, acc_ref)

    def tile(j, causal):
        off = pl.multiple_of(j * tk, tk)
        chosen = _dot(sel, e_ref[j])
        if causal:
            row = i * tq + lax.broadcasted_iota(I32, (tq, tk), 0)
            col = off + lax.broadcasted_iota(I32, (tq, tk), 1)
            chosen = jnp.where(col <= row, chosen, 0.0)
        keep = chosen > 0.5
        k = ks_ref[pl.ds(off, tk), :]
        v = vs_ref[pl.ds(off, tk), :]
        for hd in range(4):
            _flash_tile(q_ref[:, hd * HEAD_DIM:(hd + 1) * HEAD_DIM], k, v, keep, scale, m_ref, l_ref, acc_ref, hd)

    n_past = (i * tq) // tk

    def body(j, carry):
        tile(j, False)
        return carry

    lax.fori_loop(0, n_past, body, 0)
    tile(n_past, True)

    wlen = NSA_WINDOW + tq
    start = pl.multiple_of(jnp.maximum(i * tq - NSA_WINDOW, 0), 256)
    kw = kw_ref[pl.ds(start, wlen), :]
    vw = vw_ref[pl.ds(start, wlen), :]
    diff = (i * tq + lax.broadcasted_iota(I32, (tq, wlen), 0)) - (start + lax.broadcasted_iota(I32, (tq, wlen), 1))
    in_win = (diff >= 0) & (diff < NSA_WINDOW)
    g = g_ref[...]
    for hd in range(4):
        lo = hd * HEAD_DIM
        s = jnp.where(in_win, _dot_nt(q_ref[:, lo:lo + HEAD_DIM], kw) * scale, NEG)
        p = jnp.exp(s - jnp.max(s, axis=1, keepdims=True))
        o_win = _dot(p.astype(BF16), vw) / jnp.sum(p, axis=1, keepdims=True)
        o_slc = acc_ref[hd] / l_ref[hd]
        out = (g[:, 3 * hd:3 * hd + 1] * oc_ref[:, lo:lo + HEAD_DIM] + g[:, 3 * hd + 1:3 * hd + 2] * o_slc
               + g[:, 3 * hd + 2:3 * hd + 3] * o_win)
        o_ref[:, lo:lo + HEAD_DIM] = out.astype(o_ref.dtype)


def _nsa_slc_win(q_r, sel, kslc, vslc, kwin, vwin, gates, o_cmp):
    s = q_r.shape[0]
    tq = min(ATT_TQ, s)
    tk = min(ATT_TK, s)
    nblk = s // NSA_SEL_BLOCK
    expand = _block_expand_table(s, tk, NSA_SEL_BLOCK, nblk)
    full = lambda a: _resident(a.shape, lambda i: (0,) * a.ndim)
    return pl.pallas_call(
        functools.partial(_nsa_slc_win_kernel, tk=tk),
        grid=(s // tq,),
        in_specs=[pl.BlockSpec((tq, 512), lambda i: (i, 0)), pl.BlockSpec((tq, nblk), lambda i: (i, 0)),
                  full(expand), full(kslc), full(vslc), full(kwin), full(vwin),
                  pl.BlockSpec((tq, 128), lambda i: (i, 0)), pl.BlockSpec((tq, 512), lambda i: (i, 0))],
        out_specs=pl.BlockSpec((tq, 512), lambda i: (i, 0)),
        out_shape=jax.ShapeDtypeStruct((s, 512), BF16),
        scratch_shapes=_flash_scratch(4, tq),
        compiler_params=_cp("parallel"),
        name="nsa_slc_win",
    )(q_r, sel, expand, kslc, vslc, kwin, vwin, gates, o_cmp)


def _dsa_kernel(q_ref, iq_ref, iw_ref, k_ref, v_ref, ik_ref, o_ref, key_ref, wb_ref, m_ref, l_ref, acc_ref, *, tk):
    i = pl.program_id(0)
    tq = q_ref.shape[0]
    half = tk // 2
    scale = HEAD_DIM ** -0.5
    n_tiles = (i * tq + tq + tk - 1) // tk
    row_h = lax.broadcasted_iota(I32, (tq, half), 0) + i * tq
    col_h = lax.broadcasted_iota(I32, (tq, half), 1)

    w = iw_ref[...]
    for hd in range(DSA_IDX_HEADS):
        wb_ref[hd] = jnp.broadcast_to(w[:, hd:hd + 1], (tq, 128))

    def score_body(j, carry):
        for hf in range(2):
            off = pl.multiple_of(j * tk + hf * half, half)
            ik_even = ik_ref[pl.ds(off, half), 0:128]
            ik_odd = ik_ref[pl.ds(off, half), 128:256]
            sc = jnp.zeros((tq, half), F32)
            for p in range(DSA_IDX_HEADS // 2):
                x = iq_ref[:, p * 128:(p + 1) * 128]
                we = jnp.concatenate([wb_ref[2 * p]] * (half // 128), axis=1)
                wo = jnp.concatenate([wb_ref[2 * p + 1]] * (half // 128), axis=1)
                sc = sc + jnp.maximum(_dot_nt(x, ik_even), 0.0) * we
                sc = sc + jnp.maximum(_dot_nt(x, ik_odd), 0.0) * wo
            bits = pltpu.bitcast(sc, I32)
            key = jnp.where(bits >= 0, bits, bits ^ 0x7FFFFFFF)
            key_ref[j, :, hf * half:(hf + 1) * half] = jnp.where(col_h + off <= row_h, key, INT_MIN)
        return carry

    lax.fori_loop(0, n_tiles, score_body, 0)

    def count_ge(cand):
        cand128 = jnp.broadcast_to(cand, (tq, 128))

        def body(j, acc):
            for c in range(tk // 128):
                acc = acc + jnp.where(key_ref[j, :, c * 128:(c + 1) * 128] >= cand128, 1, 0)
            return acc

        acc = lax.fori_loop(0, n_tiles, body, jnp.zeros((tq, 128), I32))
        return jnp.sum(acc, axis=1, keepdims=True)

    zero = jnp.zeros((tq, 1), I32)
    lo = jnp.where(count_ge(zero) >= DSA_TOPK, zero, INT_MIN)

    def bit_body(b, lo):
        cand = lo | lax.shift_left(jnp.int32(1), 30 - b)
        return jnp.where(count_ge(cand) >= DSA_TOPK, cand, lo)

    lo = lax.fori_loop(0, 31, bit_body, lo)
    thr128 = jnp.broadcast_to(jnp.maximum(lo, INT_MIN + 1), (tq, 128))

    _flash_init(m_ref, l_ref, acc_ref)

    def attn_body(j, carry):
        off = pl.multiple_of(j * tk, tk)
        keep = key_ref[j] >= jnp.concatenate([thr128] * (tk // 128), axis=1)
        for hd in range(4):
            lo_ = hd * HEAD_DIM
            _flash_tile(q_ref[:, lo_:lo_ + HEAD_DIM], k_ref[pl.ds(off, tk), lo_:lo_ + HEAD_DIM],
                        v_ref[pl.ds(off, tk), lo_:lo_ + HEAD_DIM], keep, scale, m_ref, l_ref, acc_ref, hd)
        return carry

    lax.fori_loop(0, n_tiles, attn_body, 0)
    for hd in range(4):
        o_ref[:, hd * HEAD_DIM:(hd + 1) * HEAD_DIM] = (acc_ref[hd] / l_ref[hd]).astype(o_ref.dtype)


def _dsa(q, k, v, iq, ik2, iw):
    s = q.shape[0]
    tq = min(DSA_TQ, s)
    tk = min(ATT_TK, s)
    full = lambda a: _resident(a.shape, lambda i: (0, 0))
    return pl.pallas_call(
        functools.partial(_dsa_kernel, tk=tk),
        grid=(s // tq,),
        in_specs=[pl.BlockSpec((tq, 512), lambda i: (i, 0)), pl.BlockSpec((tq, 1024), lambda i: (i, 0)),
                  pl.BlockSpec((tq, 128), lambda i: (i, 0)), full(k), full(v), full(ik2)],
        out_specs=pl.BlockSpec((tq, 512), lambda i: (i, 0)),
        out_shape=jax.ShapeDtypeStruct((s, 512), BF16),
        scratch_shapes=[pltpu.VMEM((s // tk, tq, tk), I32), pltpu.VMEM((DSA_IDX_HEADS, tq, 128), F32)]
                       + _flash_scratch(4, tq),
        compiler_params=_cp("parallel"),
        name="dsa_attn",
    )(q, iq, iw, k, v, ik2)


def _out_proj_kernel(oa_ref, ob_ref, oc_ref, od_ref, w_ref, x_ref, g_ref, b_ref, o_ref):
    mix = _dot(oa_ref[...], w_ref[0:512, :])
    mix = mix + _dot(ob_ref[...], w_ref[512:1024, :])
    mix = mix + _dot(oc_ref[...], w_ref[1024:1536, :])
    mix = mix + _dot(od_ref[...], w_ref[1536:2048, :])
    o_ref[...] = _layer_norm(DN_ALPHA * x_ref[...] + mix, g_ref[...], b_ref[...])


def _out_proj_ln(o_a, o_b, o_c, o_d, w_out, x, g, b):
    s, d = x.shape
    tm = min(ROW_TILE, s)
    r512 = pl.BlockSpec((tm, 512), lambda i: (i, 0))
    vec = lambda a: _wspec(a, (1, d), lambda i: (0, 0))
    return pl.pallas_call(
        _out_proj_kernel,
        grid=(s // tm,),
        in_specs=[r512, r512, r512, r512, _wspec(w_out, _wshape(w_out), lambda i: (0, 0)),
                  pl.BlockSpec((tm, d), lambda i: (i, 0)), vec(g), vec(b)],
        out_specs=pl.BlockSpec((tm, d), lambda i: (i, 0)),
        out_shape=jax.ShapeDtypeStruct((s, d), F32),
        compiler_params=_cp("parallel"),
        name="out_proj_ln",
    )(o_a, o_b, o_c, o_d, w_out[0], x, g[0], b[0])


def _mem_kv_kernel(mem_ref, w_ref, o_ref):
    o_ref[...] = _dot(mem_ref[...].astype(BF16), w_ref[...]).astype(o_ref.dtype)


def _mem_kv(mem, wkv):
    m, d = mem.shape
    n = _wshape(wkv)[1]
    return pl.pallas_call(
        _mem_kv_kernel,
        grid=(1,),
        in_specs=[pl.BlockSpec((m, d), lambda i: (0, 0)), _wspec(wkv, (d, n), lambda i: (0, 0))],
        out_specs=pl.BlockSpec((m, n), lambda i: (0, 0)),
        out_shape=jax.ShapeDtypeStruct((m, n), BF16),
        compiler_params=_cp("arbitrary"),
        name="mem_kv",
    )(mem, wkv[0])


def _mem_attn_kernel(x_ref, wq_ref, kv_ref, wo_ref, g_ref, b_ref, o_ref):
    x = x_ref[...]
    scale = HEAD_DIM ** -0.5
    q = _dot(x.astype(BF16), wq_ref[...]).astype(BF16)
    out = jnp.zeros(x.shape, F32)
    for hd in range(4):
        lo = hd * HEAD_DIM
        s = _dot_nt(q[:, lo:lo + HEAD_DIM], kv_ref[:, lo:lo + HEAD_DIM]) * scale
        p = jnp.exp(s - jnp.max(s, axis=1, keepdims=True))
        o = _dot(p.astype(BF16), kv_ref[:, 512 + lo:512 + lo + HEAD_DIM]) / jnp.sum(p, axis=1, keepdims=True)
        out = out + _dot(o.astype(BF16), wo_ref[lo:lo + HEAD_DIM, :])
    o_ref[...] = _layer_norm(DN_ALPHA * x + out, g_ref[...], b_ref[...])


def _mem_attn_ln(x, wq, kv, wo, g, b):
    s, d = x.shape
    tm = min(ROW_TILE, s)
    full = lambda a: _wspec(a, _wshape(a), lambda i: (0, 0))
    return pl.pallas_call(
        _mem_attn_kernel,
        grid=(s // tm,),
        in_specs=[pl.BlockSpec((tm, d), lambda i: (i, 0)), full(wq), _resident(kv.shape, lambda i: (0, 0)),
                  full(wo), full(g), full(b)],
        out_specs=pl.BlockSpec((tm, d), lambda i: (i, 0)),
        out_shape=jax.ShapeDtypeStruct((s, d), F32),
        compiler_params=_cp("parallel"),
        name="mem_attn_ln",
    )(x, wq[0], kv, wo[0], g[0], b[0])


def _pad_last(w, n):
    return jnp.pad(w, [(0, 0)] * (w.ndim - 1) + [(0, n - w.shape[-1])])


def _split_w_in(w_in):
    a = w_in[..., 0:1536]
    b = _pad_last(w_in[..., 1536:2624], 1152)
    c = _pad_last(w_in[..., 2624:3916], 1408)
    d = jnp.concatenate([w_in[..., 3916:6476], _pad_last(w_in[..., 6476:6540], 128),
                         _pad_last(w_in[..., 6540:6556], 128)], axis=-1)
    return tuple(t.astype(BF16) for t in (a, b, c, d))


def _pad_w_uq(w_uq):
    lead = w_uq.shape[:-1]
    w = _pad_last(w_uq.reshape(lead + (4, MLA_NOPE + MLA_ROPE)), 256)
    return w.reshape(lead + (4 * 256,)).astype(BF16)


def _nsa_importance_map(s):
    n = s // NSA_CMP_STRIDE
    nblk = s // NSA_SEL_BLOCK
    ni = np.arange(n)[:, None]
    bi = np.arange(nblk)[None, :]
    return jnp.asarray(((ni >= 4 * bi - 1) & (ni <= 4 * bi + 3)).astype(np.float32))


def _mixer(x, tabs, imp_map, w_groups, w_out, g_cq, g_ckv, w_uq, w_ukv, cmp_pe, cmp_w1, cmp_w2, ln_g, ln_b):
    cos128, sin128, cos64, sup64, sdn64 = tabs
    s = x.shape[0]
    w_a, w_b, w_c, w_d = w_groups

    aq, ak, av, akm = _proj_a(x, w_a, cos128, sin128)
    o_a = _moba(aq, ak, av, akm)

    bq, bk, bv = _proj_b(x, w_b, g_cq, g_ckv, w_uq, w_ukv, cos64, sup64, sdn64)
    o_b = _mla(bq, bk, bv)

    cq, cqr, kvcmp, kslc, vslc, kwin, vwin, gates = _proj_c(x, w_c, cos128, sin128)
    kv16 = kvcmp.reshape(2, s // NSA_CMP_STRIDE, NSA_CMP_STRIDE * HEAD_DIM)
    kvc = _nsa_compress(kv16, cmp_pe, cmp_w1, cmp_w2)
    o_cmp, sel = _nsa_cmp(cq, kvc, imp_map)
    o_c = _nsa_slc_win(cqr, sel, kslc, vslc, kwin, vwin, gates, o_cmp)

    dq, dk, dv, diq, dik, diw = _proj_d(x, w_d, cos128, sin128, cos64, sup64, sdn64)
    o_d = _dsa(dq, dk, dv, diq, dik, diw)

    return _out_proj_ln(o_a, o_b, o_c, o_d, w_out, x, ln_g, ln_b)


def kernel(x, mem, positions, ln_g, ln_b, ffn_w_gu, ffn_w_down, w_in, w_out, mla_g_cq, mla_g_ckv, mla_w_uq,
           mla_w_ukv, nsa_cmp_pe, nsa_cmp_w1, nsa_cmp_w2, mem_wq, mem_wkv, mem_wo):
    batch, s, d = x.shape
    w_gu_b, w_dn_b = ffn_w_gu.astype(BF16), ffn_w_down.astype(BF16)
    w_groups = _split_w_in(w_in)
    w_out_b = w_out.astype(BF16)
    w_uq_b, w_ukv_b = _pad_w_uq(mla_w_uq), mla_w_ukv.astype(BF16)
    g_cq, g_ckv = mla_g_cq[:, None, :], mla_g_ckv[:, None, :]
    pe = nsa_cmp_pe.reshape(DEPTH, 2, 1, -1)
    w1_b, w2_b = nsa_cmp_w1.astype(BF16), nsa_cmp_w2.astype(BF16)
    wq_b, wkv_b, wo_b = mem_wq.astype(BF16), mem_wkv.astype(BF16), mem_wo.astype(BF16)
    g4, b4 = ln_g[:, :, None, :], ln_b[:, :, None, :]
    imp_map = _nsa_importance_map(s)

    outs = []
    for bi in range(batch):
        xb = x[bi]
        tabs = _rope_tables(positions[bi])
        for l in range(DEPTH):
            at = lambda a, *lead: (a, (l,) + lead)
            xb = _ffn_ln(xb, at(w_gu_b, 0), at(w_dn_b, 0), at(g4, 0), at(b4, 0))
            xb = _mixer(xb, tabs, imp_map, tuple(at(w) for w in w_groups), at(w_out_b), at(g_cq), at(g_ckv),
                        at(w_uq_b), at(w_ukv_b), at(pe), at(w1_b), at(w2_b), at(g4, 1), at(b4, 1))
            kv = _mem_kv(mem[bi], at(wkv_b))
            xb = _mem_attn_ln(xb, at(wq_b), kv, at(wo_b), at(g4, 2), at(b4, 2))
            xb = _ffn_ln(xb, at(w_gu_b, 1), at(w_dn_b, 1), at(g4, 3), at(b4, 3))
        outs.append(xb)
    return jnp.stack(outs)
```

```python
import functools

import numpy as np
import jax
import jax.numpy as jnp
from jax import lax
from jax.experimental import pallas as pl
from jax.experimental.pallas import tpu as pltpu

F32 = jnp.float32
BF16 = jnp.bfloat16
I32 = jnp.int32

D_MODEL = 2048
DEPTH = 4
HEAD_DIM = 128
ROPE_THETA = 10000.0
LN_EPS = 1e-5
RMS_EPS = 1e-6

MOBA_BLOCK = 256
MOBA_TOPK = 3
MLA_NOPE = 128
MLA_ROPE = 64
NSA_CMP_STRIDE = 16
NSA_SEL_BLOCK = 64
NSA_SEL_TOPK = 16
NSA_WINDOW = 512
DSA_IDX_HEADS = 16
DSA_TOPK = 256
D_FF = 5632
DN_ALPHA = (2 * DEPTH) ** 0.25

NEG = -(2.0 ** 100)
LOG2_E = 1.4426950408889634
INT_MIN = -(2 ** 31)
VMEM_LIMIT = 56 * 1024 * 1024

ROW_TILE = 512
ATT_TQ = 512
ATT_TK = 1024
DSA_TQ = 256
DSA_COUNT_ROWS = 128


def _cp(*sem):
    return pltpu.CompilerParams(dimension_semantics=sem, vmem_limit_bytes=VMEM_LIMIT)


def _wspec(w, block, index_map, resident=True):
    _, lead = w
    shape = (None,) * len(lead) + tuple(block)
    imap = lambda *g: tuple(lead) + tuple(index_map(*g))
    if resident:
        return pl.BlockSpec(shape, imap, pipeline_mode=pl.Buffered(1))
    return pl.BlockSpec(shape, imap)


def _wshape(w):
    arr, lead = w
    return arr.shape[len(lead):]


def _resident(shape, index_map):
    return pl.BlockSpec(shape, index_map, pipeline_mode=pl.Buffered(1))


def _dot(a, b):
    return jnp.dot(a, b, preferred_element_type=F32)


def _dot_nt(a, b):
    return lax.dot_general(a, b, (((1,), (1,)), ((), ())), preferred_element_type=F32)


def _layer_norm(y, g, b):
    mu = jnp.mean(y, axis=-1, keepdims=True)
    d = y - mu
    var = jnp.mean(d * d, axis=-1, keepdims=True)
    return d * lax.rsqrt(var + LN_EPS) * g + b


def _rope128(x, cos, sin_signed):
    return x * cos + pltpu.roll(x, 64, 1) * sin_signed


def _rope64(x, cos, s_up, s_dn):
    return x * cos + pltpu.roll(x, 32, 1) * s_up + pltpu.roll(x, 96, 1) * s_dn


def _log2_scale(d):
    return float(d) ** -0.5 * LOG2_E


def _with_ones(v):
    return jnp.concatenate([v, jnp.ones_like(v)], axis=1)


def _flash_update(s, v1, bias, keep, m_ref, acc_ref, slot):
    if bias is not None:
        s = s + bias
    if keep is not None:
        s = jnp.where(keep, s, NEG)
    m_old = m_ref[slot]
    m_new = jnp.maximum(m_old, jnp.max(s, axis=1, keepdims=True))
    alpha = jnp.exp2(m_old - m_new)
    p = jnp.exp2(s - m_new)
    acc_ref[slot] = alpha * acc_ref[slot] + _dot(p.astype(BF16), v1)
    m_ref[slot] = m_new


FLASH_ROWS = 256


def _flash_rows(q, k, v1, bias, keep, m_ref, acc_ref, hd):
    n = max(1, q.shape[0] // FLASH_ROWS)
    r = q.shape[0] // n
    rows = [slice(sp * r, (sp + 1) * r) for sp in range(n)]
    logits = [_dot_nt(q[rw], k) for rw in rows]
    for sp, rw in enumerate(rows):
        _flash_update(logits[sp], v1, None if bias is None else bias[rw], None if keep is None else keep[rw],
                      m_ref, acc_ref, hd * n + sp)


def _flash_result(acc_ref, hd, heads):
    n = acc_ref.shape[0] // heads
    parts = []
    for sp in range(n):
        a = acc_ref[hd * n + sp]
        parts.append(a[:, 0:HEAD_DIM] / a[:, HEAD_DIM:2 * HEAD_DIM])
    return jnp.concatenate(parts, axis=0) if n > 1 else parts[0]


def _flash_scratch(heads, tq):
    n = max(1, tq // FLASH_ROWS)
    return [pltpu.VMEM((heads * n, tq // n, 1), F32), pltpu.VMEM((heads * n, tq // n, 2 * HEAD_DIM), F32)]


def _flash_init(m_ref, acc_ref):
    m_ref[...] = jnp.full(m_ref.shape, NEG, F32)
    acc_ref[...] = jnp.zeros(acc_ref.shape, F32)


def _top_k_mask(work, lane, k):
    n = work.shape[1]
    sel = jnp.zeros(work.shape, F32)
    for _ in range(k):
        m = jnp.max(work, axis=1, keepdims=True)
        idx = jnp.min(jnp.where(work == m, lane, n), axis=1, keepdims=True)
        pick = lane == jnp.where(m > -jnp.inf, idx, n)
        sel = jnp.where(pick, 1.0, sel)
        work = jnp.where(pick, -jnp.inf, work)
    return sel


def _key_block_table(s, block):
    return jnp.asarray((np.arange(128)[None, :] == (np.arange(s) // block)[:, None]) * -NEG, dtype=BF16)


def _rope_table_kernel(pos_ref, c_ref, cos128_ref, sin128_ref, cos64_ref, sup64_ref, sdn64_ref):
    pos = pos_ref[...]
    a128 = pos * c_ref[0:1, :]
    cos128_ref[...] = jnp.cos(a128)
    sin128_ref[...] = jnp.sin(a128) * c_ref[1:2, :]
    a64 = pos * c_ref[2:3, :]
    s64 = jnp.sin(a64)
    cos64_ref[...] = jnp.cos(a64)
    sup64_ref[...] = s64 * c_ref[3:4, :]
    sdn64_ref[...] = s64 * c_ref[4:5, :]


def _rope_consts():
    lane = np.arange(128)
    inv128 = (ROPE_THETA ** (-np.arange(0, 128, 2, dtype=np.float32) / 128)).astype(np.float32)
    inv64 = (ROPE_THETA ** (-np.arange(0, 64, 2, dtype=np.float32) / 64)).astype(np.float32)
    c = np.zeros((8, 128), np.float32)
    c[0] = inv128[lane % 64]
    c[1] = np.where(lane < 64, -1.0, 1.0)
    c[2] = inv64[(lane % 64) % 32]
    c[3] = np.where(lane % 64 >= 32, 1.0, 0.0)
    c[4] = np.where(lane % 64 < 32, -1.0, 0.0)
    return jnp.asarray(c)


def _rope_tables(positions):
    s = positions.shape[0]
    tm = min(ROW_TILE, s)
    pos = positions.astype(F32).reshape(s, 1)
    tab = jax.ShapeDtypeStruct((s, 128), F32)
    row = pl.BlockSpec((tm, 128), lambda i: (i, 0))
    return pl.pallas_call(
        _rope_table_kernel,
        grid=(s // tm,),
        in_specs=[pl.BlockSpec((tm, 1), lambda i: (i, 0)), pl.BlockSpec((8, 128), lambda i: (0, 0))],
        out_specs=[row] * 5,
        out_shape=[tab] * 5,
        compiler_params=_cp("parallel"),
        name="rope_tables",
    )(pos, _rope_consts())


def _ffn_kernel(x_ref, wg_ref, wu_ref, wd_ref, g_ref, b_ref, o_ref, xb_ref, acc_ref):
    j = pl.program_id(1)

    @pl.when(j == 0)
    def _():
        xb_ref[...] = x_ref[...].astype(BF16)
        acc_ref[...] = jnp.zeros_like(acc_ref)

    xb = xb_ref[...]
    g = _dot(xb, wg_ref[...])
    u = _dot(xb, wu_ref[...])
    h = (g * jax.nn.sigmoid(g) * u).astype(BF16)
    acc_ref[...] += _dot(h, wd_ref[...])

    @pl.when(j == pl.num_programs(1) - 1)
    def _():
        y = DN_ALPHA * x_ref[...] + 0.5 * acc_ref[...]
        o_ref[...] = _layer_norm(y, g_ref[...], b_ref[...])


def _ffn_ln(x, w_gu, w_down, g, b):
    s, d = x.shape
    f = _wshape(w_down)[0]
    tm = min(ROW_TILE, s)
    tf = 512
    nf = f // tf
    return pl.pallas_call(
        _ffn_kernel,
        grid=(s // tm, nf),
        in_specs=[
            pl.BlockSpec((tm, d), lambda i, j: (i, 0)),
            _wspec(w_gu, (d, tf), lambda i, j: (0, j), resident=False),
            _wspec(w_gu, (d, tf), lambda i, j: (0, j + nf), resident=False),
            _wspec(w_down, (tf, d), lambda i, j: (j, 0), resident=False),
            _wspec(g, (1, d), lambda i, j: (0, 0)),
            _wspec(b, (1, d), lambda i, j: (0, 0)),
        ],
        out_specs=pl.BlockSpec((tm, d), lambda i, j: (i, 0)),
        out_shape=jax.ShapeDtypeStruct((s, d), F32),
        scratch_shapes=[pltpu.VMEM((tm, d), BF16), pltpu.VMEM((tm, d), F32)],
        compiler_params=_cp("parallel", "arbitrary"),
        name="ffn_ln",
    )(x, w_gu[0], w_gu[0], w_down[0], g[0], b[0])


def _proj_a_kernel(x_ref, w_ref, cos_ref, sin_ref, kb_ref, q_ref, k_ref, v_ref, km_ref):
    h = _dot(x_ref[...].astype(BF16), w_ref[...])
    cos, sin = cos_ref[...], sin_ref[...]
    nblk = km_ref.shape[0]
    for hd in range(4):
        lo = hd * HEAD_DIM
        q_ref[:, lo:lo + HEAD_DIM] = _rope128(h[:, lo:lo + HEAD_DIM], cos, sin)
        k = _rope128(h[:, 512 + lo:512 + lo + HEAD_DIM], cos, sin)
        k_ref[:, 2 * lo:2 * lo + HEAD_DIM] = k.astype(BF16)
        k_ref[:, 2 * lo + HEAD_DIM:2 * lo + 2 * HEAD_DIM] = kb_ref[...]
        for bi in range(nblk):
            kb = k[bi * MOBA_BLOCK:(bi + 1) * MOBA_BLOCK]
            km_ref[bi, :, lo:lo + HEAD_DIM] = jnp.mean(kb, axis=0, keepdims=True)
    v_ref[...] = h[:, 1024:1536].astype(BF16)


def _proj_a(x, w, cos128, sin128):
    s, d = x.shape
    tm = min(ROW_TILE, s)
    nblk = tm // MOBA_BLOCK
    row512 = pl.BlockSpec((tm, 512), lambda i: (i, 0))
    tab = pl.BlockSpec((tm, 128), lambda i: (i, 0))
    return pl.pallas_call(
        _proj_a_kernel,
        grid=(s // tm,),
        in_specs=[pl.BlockSpec((tm, d), lambda i: (i, 0)), _wspec(w, _wshape(w), lambda i: (0, 0)), tab, tab, tab],
        out_specs=[row512, pl.BlockSpec((tm, 1024), lambda i: (i, 0)), row512,
                   pl.BlockSpec((nblk, 1, 512), lambda i: (i, 0, 0))],
        out_shape=[
            jax.ShapeDtypeStruct((s, 512), F32),
            jax.ShapeDtypeStruct((s, 1024), BF16),
            jax.ShapeDtypeStruct((s, 512), BF16),
            jax.ShapeDtypeStruct((s // MOBA_BLOCK, 1, 512), F32),
        ],
        compiler_params=_cp("parallel"),
        name="proj_moba",
    )(x, w[0], cos128, sin128, _key_block_table(s, MOBA_BLOCK))


def _rms_norm(x, g):
    return x * lax.rsqrt(jnp.mean(x * x, axis=-1, keepdims=True) + RMS_EPS) * g


def _proj_b_kernel(x_ref, w_ref, gq_ref, gkv_ref, wuq_ref, wukv_ref, cos_ref, sup_ref, sdn_ref,
                   q_ref, k_ref, v_ref):
    h = _dot(x_ref[...].astype(BF16), w_ref[...])
    cos, sup, sdn = cos_ref[...], sup_ref[...], sdn_ref[...]
    cq = _rms_norm(h[:, 0:512], gq_ref[...]).astype(BF16)
    ckv = _rms_norm(h[:, 512:1024], gkv_ref[...]).astype(BF16)
    kr = _rope64(h[:, 1024:1152], cos, sup, sdn).astype(BF16)
    q = _dot(cq, wuq_ref[...])
    kv = _dot(ckv, wukv_ref[...])
    qs = _log2_scale(MLA_NOPE + MLA_ROPE)
    for hd in range(4):
        lo = hd * 256
        q_ref[:, lo:lo + 128] = (q[:, lo:lo + 128] * qs).astype(BF16)
        q_ref[:, lo + 128:lo + 256] = (_rope64(q[:, lo + 128:lo + 256], cos, sup, sdn) * qs).astype(BF16)
        k_ref[:, lo:lo + 128] = kv[:, lo:lo + 128].astype(BF16)
        k_ref[:, lo + 128:lo + 256] = kr
        v_ref[:, hd * 128:(hd + 1) * 128] = kv[:, lo + 128:lo + 256].astype(BF16)


def _proj_b(x, w, g_cq, g_ckv, w_uq, w_ukv, cos64, sup64, sdn64):
    s, d = x.shape
    tm = min(ROW_TILE, s)
    tab = pl.BlockSpec((tm, 128), lambda i: (i, 0))
    full = lambda a: _wspec(a, _wshape(a), lambda i: (0, 0))
    return pl.pallas_call(
        _proj_b_kernel,
        grid=(s // tm,),
        in_specs=[pl.BlockSpec((tm, d), lambda i: (i, 0)), full(w), full(g_cq), full(g_ckv), full(w_uq),
                  full(w_ukv), tab, tab, tab],
        out_specs=[pl.BlockSpec((tm, 1024), lambda i: (i, 0)), pl.BlockSpec((tm, 1024), lambda i: (i, 0)),
                   pl.BlockSpec((tm, 512), lambda i: (i, 0))],
        out_shape=[jax.ShapeDtypeStruct((s, 1024), BF16), jax.ShapeDtypeStruct((s, 1024), BF16),
                   jax.ShapeDtypeStruct((s, 512), BF16)],
        compiler_params=_cp("parallel"),
        name="proj_mla",
    )(x, w[0], g_cq[0], g_ckv[0], w_uq[0], w_ukv[0], cos64, sup64, sdn64)


def _proj_c_kernel(x_ref, w_ref, cos_ref, sin_ref, kb_ref, q_ref, qr_ref, kvcmp_ref, kslc_ref, vslc_ref,
                   kwin_ref, vwin_ref, gate_ref):
    h = _dot(x_ref[...].astype(BF16), w_ref[...])
    cos, sin = cos_ref[...], sin_ref[...]
    for hd in range(4):
        lo = hd * HEAD_DIM
        qh = h[:, lo:lo + HEAD_DIM]
        q_ref[:, lo:lo + HEAD_DIM] = qh.astype(BF16)
        qr_ref[:, lo:lo + HEAD_DIM] = (_rope128(qh, cos, sin) * _log2_scale(HEAD_DIM)).astype(BF16)
    kv = lambda i: h[:, 512 + i * 128:512 + (i + 1) * 128]
    kvcmp_ref[0] = kv(0).astype(BF16)
    kvcmp_ref[1] = kv(1).astype(BF16)
    kslc_ref[:, 0:HEAD_DIM] = _rope128(kv(2), cos, sin).astype(BF16)
    kslc_ref[:, HEAD_DIM:2 * HEAD_DIM] = kb_ref[...]
    vslc_ref[...] = kv(3).astype(BF16)
    kwin_ref[...] = _rope128(kv(4), cos, sin).astype(BF16)
    vwin_ref[...] = kv(5).astype(BF16)
    gate_ref[...] = jax.nn.sigmoid(h[:, 1280:1408])


def _proj_c(x, w, cos128, sin128):
    s, d = x.shape
    tm = min(ROW_TILE, s)
    r512 = pl.BlockSpec((tm, 512), lambda i: (i, 0))
    r128 = pl.BlockSpec((tm, 128), lambda i: (i, 0))
    b512 = jax.ShapeDtypeStruct((s, 512), BF16)
    b128 = jax.ShapeDtypeStruct((s, 128), BF16)
    return pl.pallas_call(
        _proj_c_kernel,
        grid=(s // tm,),
        in_specs=[pl.BlockSpec((tm, d), lambda i: (i, 0)), _wspec(w, _wshape(w), lambda i: (0, 0)), r128, r128, r128],
        out_specs=[r512, r512, pl.BlockSpec((2, tm, 128), lambda i: (0, i, 0)),
                   pl.BlockSpec((tm, 256), lambda i: (i, 0))] + [r128] * 4,
        out_shape=[b512, b512, jax.ShapeDtypeStruct((2, s, 128), BF16), jax.ShapeDtypeStruct((s, 256), BF16)]
                  + [b128] * 3 + [jax.ShapeDtypeStruct((s, 128), F32)],
        compiler_params=_cp("parallel"),
        name="proj_nsa",
    )(x, w[0], cos128, sin128, _key_block_table(s, NSA_SEL_BLOCK))


def _proj_d_kernel(x_ref, w_ref, cos_ref, sin_ref, c64_ref, sup_ref, sdn_ref,
                   q_ref, k_ref, v_ref, iq_ref, ik_ref, iw_ref):
    h = _dot(x_ref[...].astype(BF16), w_ref[...])
    cos, sin = cos_ref[...], sin_ref[...]
    c64, sup, sdn = c64_ref[...], sup_ref[...], sdn_ref[...]
    for hd in range(4):
        lo = hd * HEAD_DIM
        q_ref[:, lo:lo + HEAD_DIM] = (_rope128(h[:, lo:lo + HEAD_DIM], cos, sin) * _log2_scale(HEAD_DIM)).astype(BF16)
        k_ref[:, lo:lo + HEAD_DIM] = _rope128(h[:, 512 + lo:512 + lo + HEAD_DIM], cos, sin).astype(BF16)
    v_ref[...] = h[:, 1024:1536].astype(BF16)
    for p in range(8):
        lo = 1536 + p * 128
        iq_ref[:, p * 128:(p + 1) * 128] = _rope64(h[:, lo:lo + 128], c64, sup, sdn).astype(BF16)
    ik = _rope64(h[:, 2560:2688], c64, sup, sdn)
    ik_ref[:, 0:128] = ik.astype(BF16)
    ik_ref[:, 128:256] = pltpu.roll(ik, 64, 1).astype(BF16)
    iw_ref[...] = h[:, 2688:2816] * (1.0 / 32.0)


def _proj_d(x, w, cos128, sin128, cos64, sup64, sdn64):
    s, d = x.shape
    tm = min(ROW_TILE, s)
    r = lambda n: pl.BlockSpec((tm, n), lambda i: (i, 0))
    return pl.pallas_call(
        _proj_d_kernel,
        grid=(s // tm,),
        in_specs=[pl.BlockSpec((tm, d), lambda i: (i, 0)), _wspec(w, _wshape(w), lambda i: (0, 0))] + [r(128)] * 5,
        out_specs=[r(512), r(512), r(512), r(1024), r(256), r(128)],
        out_shape=[jax.ShapeDtypeStruct((s, 512), BF16)] * 3 + [
            jax.ShapeDtypeStruct((s, 1024), BF16), jax.ShapeDtypeStruct((s, 256), BF16),
            jax.ShapeDtypeStruct((s, 128), F32)],
        compiler_params=_cp("parallel"),
        name="proj_dsa",
    )(x, w[0], cos128, sin128, cos64, sup64, sdn64)


def _moba_kernel(q_ref, k_ref, v_ref, km_ref, o_ref, m_ref, acc_ref, *, tk, nb):
    i = pl.program_id(1)
    tq = q_ref.shape[0]
    q32 = q_ref[...]

    gate = lax.dot_general(q32, km_ref[...], (((1,), (1,)), ((), ())),
                           precision=lax.Precision.HIGHEST, preferred_element_type=F32)
    blk = lax.broadcasted_iota(I32, (tq, 128), 1)
    cur = (i * tq + lax.broadcasted_iota(I32, (tq, 128), 0)) // MOBA_BLOCK
    gate = jnp.where(blk < cur, gate, -jnp.inf)
    sel = _top_k_mask(gate, blk, min(MOBA_TOPK, nb - 1))
    unchosen = jnp.where(blk == cur, 1.0, sel) - 1.0
    qa = jnp.concatenate([q32 * _log2_scale(HEAD_DIM), unchosen], axis=1).astype(BF16)

    _flash_init(m_ref, acc_ref)

    def tile(j, causal):
        off = pl.multiple_of(j * tk, tk)
        keep = None
        if causal:
            row = i * tq + lax.broadcasted_iota(I32, (tq, tk), 0)
            col = off + lax.broadcasted_iota(I32, (tq, tk), 1)
            keep = col <= row
        _flash_rows(qa, k_ref[pl.ds(off, tk), :], _with_ones(v_ref[pl.ds(off, tk), :]), None, keep,
                    m_ref, acc_ref, 0)

    n_past = (i * tq) // tk

    def body(j, carry):
        tile(j, False)
        return carry

    lax.fori_loop(0, n_past, body, 0)
    tile(n_past, True)
    o_ref[...] = _flash_result(acc_ref, 0, 1).astype(o_ref.dtype)


def _moba(q32, k, v, kmean):
    s = q32.shape[0]
    nb = s // MOBA_BLOCK
    tq = min(ATT_TQ, s)
    tk = min(ATT_TK, s)
    km = jnp.pad(kmean.reshape(nb, 512), ((0, 128 - nb), (0, 0)))
    return pl.pallas_call(
        functools.partial(_moba_kernel, tk=tk, nb=nb),
        grid=(4, s // tq),
        in_specs=[
            pl.BlockSpec((tq, HEAD_DIM), lambda h, i: (i, h)),
            pl.BlockSpec((s, 2 * HEAD_DIM), lambda h, i: (0, h)),
            pl.BlockSpec((s, HEAD_DIM), lambda h, i: (0, h)),
            pl.BlockSpec((128, HEAD_DIM), lambda h, i: (0, h)),
        ],
        out_specs=pl.BlockSpec((tq, HEAD_DIM), lambda h, i: (i, h)),
        out_shape=jax.ShapeDtypeStruct((s, 512), BF16),
        scratch_shapes=_flash_scratch(1, tq),
        compiler_params=_cp("parallel", "arbitrary"),
        name="moba_attn",
    )(q32, k, v, km)


def _mla_kernel(q_ref, k_ref, v_ref, o_ref, m_ref, acc_ref, *, tk):
    i = pl.program_id(1)
    tq = q_ref.shape[0]
    q = q_ref[...]
    _flash_init(m_ref, acc_ref)
    n_past = (i * tq) // tk

    def body(j, carry):
        off = pl.multiple_of(j * tk, tk)
        _flash_rows(q, k_ref[pl.ds(off, tk), :], _with_ones(v_ref[pl.ds(off, tk), :]), None, None, m_ref, acc_ref, 0)
        return carry

    lax.fori_loop(0, n_past, body, 0)
    off = pl.multiple_of(n_past * tk, tk)
    row = i * tq + lax.broadcasted_iota(I32, (tq, tk), 0)
    col = off + lax.broadcasted_iota(I32, (tq, tk), 1)
    _flash_rows(q, k_ref[pl.ds(off, tk), :], _with_ones(v_ref[pl.ds(off, tk), :]), None, col <= row, m_ref, acc_ref, 0)
    o_ref[...] = _flash_result(acc_ref, 0, 1).astype(o_ref.dtype)


def _mla(qcat, kcat, v):
    s = qcat.shape[0]
    tq = min(ATT_TQ, s)
    tk = min(ATT_TK, s)
    return pl.pallas_call(
        functools.partial(_mla_kernel, tk=tk),
        grid=(4, s // tq),
        in_specs=[
            pl.BlockSpec((tq, 256), lambda h, i: (i, h)),
            pl.BlockSpec((s, 256), lambda h, i: (0, h)),
            pl.BlockSpec((s, HEAD_DIM), lambda h, i: (0, h)),
        ],
        out_specs=pl.BlockSpec((tq, HEAD_DIM), lambda h, i: (i, h)),
        out_shape=jax.ShapeDtypeStruct((s, 512), BF16),
        scratch_shapes=_flash_scratch(1, tq),
        compiler_params=_cp("parallel", "arbitrary"),
        name="mla_attn",
    )(qcat, kcat, v)


def _gelu_tanh(x):
    return 0.5 * x * (1.0 + jnp.tanh(0.7978845608028654 * (x + 0.044715 * x * x * x)))


def _nsa_compress_kernel(t_ref, pe_ref, w1_ref, w2_ref, o_ref):
    t = t_ref[0]
    w1 = w1_ref[...]
    half = t.shape[1]
    n = t.shape[0]
    lo = _dot(t, w1[0:half])
    hi = _dot(t, w1[half:2 * half])
    pe = _dot(pe_ref[...].astype(BF16), w1)
    pre = lo + pltpu.roll(hi, n - 1, 0) + pe
    o_ref[0] = _dot(_gelu_tanh(pre).astype(BF16), w2_ref[...]).astype(o_ref.dtype)


def _nsa_compress(kv16, pe, w1, w2):
    n = kv16.shape[1]
    sub = lambda a: _wspec(a, (None,) + _wshape(a)[1:], lambda i: (i, 0, 0), resident=False)
    return pl.pallas_call(
        _nsa_compress_kernel,
        grid=(2,),
        in_specs=[pl.BlockSpec((1,) + kv16.shape[1:], lambda i: (i, 0, 0)), sub(pe), sub(w1), sub(w2)],
        out_specs=pl.BlockSpec((1, n, HEAD_DIM), lambda i: (i, 0, 0)),
        out_shape=jax.ShapeDtypeStruct((2, n, HEAD_DIM), BF16),
        compiler_params=_cp("parallel"),
        name="nsa_compress",
    )(kv16, pe[0], w1[0], w2[0])


def _nsa_cmp_kernel(q_ref, kvc_ref, m_ref, o_ref, sel_ref, *, n_sel):
    i = pl.program_id(0)
    tq = q_ref.shape[0]
    n = kvc_ref.shape[1]
    nblk = m_ref.shape[1]
    scale = HEAD_DIM ** -0.5
    kc, vc = kvc_ref[0], kvc_ref[1]
    t_pos = i * tq + lax.broadcasted_iota(I32, (tq, n), 0)
    cmp_end = lax.broadcasted_iota(I32, (tq, n), 1) * NSA_CMP_STRIDE + (2 * NSA_CMP_STRIDE - 1)
    ok = cmp_end <= t_pos
    p_sum = jnp.zeros((tq, n), F32)
    for hd in range(4):
        lo = hd * HEAD_DIM
        s = jnp.where(ok, _dot_nt(q_ref[:, lo:lo + HEAD_DIM], kc) * scale, NEG)
        m = jnp.max(s, axis=1, keepdims=True)
        e = jnp.where(ok, jnp.exp(s - m), 0.0)
        l = jnp.sum(e, axis=1, keepdims=True)
        p = e / jnp.where(l > 0, l, 1.0)
        p_sum = p_sum + p
        o_ref[:, lo:lo + HEAD_DIM] = _dot(p.astype(BF16), vc)

    imp = jnp.dot(p_sum, m_ref[...], precision=lax.Precision.HIGHEST, preferred_element_type=F32)
    blk = lax.broadcasted_iota(I32, (tq, nblk), 1)
    cur = (i * tq + lax.broadcasted_iota(I32, (tq, nblk), 0)) // NSA_SEL_BLOCK
    forced = (blk == 0) | (blk == cur) | (blk == cur - 1)
    imp = jnp.where(blk > cur, -jnp.inf, jnp.where(forced, jnp.inf, imp))
    sel_ref[...] = (_top_k_mask(imp, blk, n_sel) - 1.0).astype(sel_ref.dtype)


def _nsa_cmp(q, kvc, imp_map):
    s = q.shape[0]
    n = kvc.shape[1]
    nblk = imp_map.shape[1]
    tq = min(256, s)
    return pl.pallas_call(
        functools.partial(_nsa_cmp_kernel, n_sel=min(NSA_SEL_TOPK, s // NSA_SEL_BLOCK)),
        grid=(s // tq,),
        in_specs=[pl.BlockSpec((tq, 512), lambda i: (i, 0)),
                  _resident((2, n, HEAD_DIM), lambda i: (0, 0, 0)),
                  _resident((n, nblk), lambda i: (0, 0))],
        out_specs=[pl.BlockSpec((tq, 512), lambda i: (i, 0)), pl.BlockSpec((tq, nblk), lambda i: (i, 0))],
        out_shape=[jax.ShapeDtypeStruct((s, 512), F32), jax.ShapeDtypeStruct((s, nblk), BF16)],
        compiler_params=_cp("parallel"),
        name="nsa_cmp_select",
    )(q, kvc, imp_map)


def _nsa_slc_win_kernel(q_ref, sel_ref, ks_ref, vs_ref, kw_ref, vw_ref, g_ref, oc_ref, o_ref,
                        m_ref, acc_ref, *, tk):
    i = pl.program_id(0)
    tq = q_ref.shape[0]
    _flash_init(m_ref, acc_ref)

    def tile(j, causal):
        off = pl.multiple_of(j * tk, tk)
        keep = None
        if causal:
            row = i * tq + lax.broadcasted_iota(I32, (tq, tk), 0)
            col = off + lax.broadcasted_iota(I32, (tq, tk), 1)
            keep = col <= row
        k = ks_ref[pl.ds(off, tk), 0:HEAD_DIM]
        v1 = _with_ones(vs_ref[pl.ds(off, tk), :])
        bias = _dot_nt(sel_ref[...], ks_ref[pl.ds(off, tk), HEAD_DIM:2 * HEAD_DIM])
        for hd in range(4):
            _flash_rows(q_ref[:, hd * HEAD_DIM:(hd + 1) * HEAD_DIM], k, v1, bias, keep, m_ref, acc_ref, hd)

    n_past = (i * tq) // tk

    def body(j, carry):
        tile(j, False)
        return carry

    lax.fori_loop(0, n_past, body, 0)
    tile(n_past, True)

    wlen = NSA_WINDOW + tq
    start = pl.multiple_of(jnp.maximum(i * tq - NSA_WINDOW, 0), 256)
    kw = kw_ref[pl.ds(start, wlen), :]
    vw1 = _with_ones(vw_ref[pl.ds(start, wlen), :])
    diff = (i * tq + lax.broadcasted_iota(I32, (tq, wlen), 0)) - (start + lax.broadcasted_iota(I32, (tq, wlen), 1))
    in_win = (diff >= 0) & (diff < NSA_WINDOW)
    g = g_ref[...]
    for hd in range(4):
        lo = hd * HEAD_DIM
        s = jnp.where(in_win, _dot_nt(q_ref[:, lo:lo + HEAD_DIM], kw), NEG)
        p = jnp.exp2(s - jnp.max(s, axis=1, keepdims=True))
        pv = _dot(p.astype(BF16), vw1)
        o_win = pv[:, 0:HEAD_DIM] / pv[:, HEAD_DIM:2 * HEAD_DIM]
        o_slc = _flash_result(acc_ref, hd, 4)
        out = (g[:, 3 * hd:3 * hd + 1] * oc_ref[:, lo:lo + HEAD_DIM] + g[:, 3 * hd + 1:3 * hd + 2] * o_slc
               + g[:, 3 * hd + 2:3 * hd + 3] * o_win)
        o_ref[:, lo:lo + HEAD_DIM] = out.astype(o_ref.dtype)


def _nsa_slc_win(q_r, sel, kslc, vslc, kwin, vwin, gates, o_cmp):
    s = q_r.shape[0]
    tq = min(ATT_TQ, s)
    tk = min(ATT_TK, s)
    full = lambda a: _resident(a.shape, lambda i: (0,) * a.ndim)
    return pl.pallas_call(
        functools.partial(_nsa_slc_win_kernel, tk=tk),
        grid=(s // tq,),
        in_specs=[pl.BlockSpec((tq, 512), lambda i: (i, 0)), pl.BlockSpec((tq, 128), lambda i: (i, 0)),
                  full(kslc), full(vslc), full(kwin), full(vwin),
                  pl.BlockSpec((tq, 128), lambda i: (i, 0)), pl.BlockSpec((tq, 512), lambda i: (i, 0))],
        out_specs=pl.BlockSpec((tq, 512), lambda i: (i, 0)),
        out_shape=jax.ShapeDtypeStruct((s, 512), BF16),
        scratch_shapes=_flash_scratch(4, tq),
        compiler_params=_cp("parallel"),
        name="nsa_slc_win",
    )(q_r, sel, kslc, vslc, kwin, vwin, gates, o_cmp)


def _dsa_kernel(q_ref, iq_ref, iw_ref, k_ref, v_ref, ik_ref, o_ref, key_ref, wb_ref, m_ref, acc_ref, *, tk):
    i = pl.program_id(0)
    tq = q_ref.shape[0]
    half = tk // 2
    n_tiles = (i * tq + tq + tk - 1) // tk
    row_h = lax.broadcasted_iota(I32, (tq, half), 0) + i * tq
    col_h = lax.broadcasted_iota(I32, (tq, half), 1)

    w = iw_ref[...]
    for hd in range(DSA_IDX_HEADS):
        wb_ref[hd] = jnp.broadcast_to(w[:, hd:hd + 1], (tq, 128))

    def score_body(j, carry):
        for hf in range(2):
            off = pl.multiple_of(j * tk + hf * half, half)
            ik_even = ik_ref[pl.ds(off, half), 0:128]
            ik_odd = ik_ref[pl.ds(off, half), 128:256]
            sc = jnp.zeros((tq, half), F32)
            for p in range(DSA_IDX_HEADS // 2):
                x = iq_ref[:, p * 128:(p + 1) * 128]
                we = jnp.concatenate([wb_ref[2 * p]] * (half // 128), axis=1)
                wo = jnp.concatenate([wb_ref[2 * p + 1]] * (half // 128), axis=1)
                sc = sc + jnp.maximum(_dot_nt(x, ik_even), 0.0) * we
                sc = sc + jnp.maximum(_dot_nt(x, ik_odd), 0.0) * wo
            bits = pltpu.bitcast(sc, I32)
            key = jnp.where(bits >= 0, bits, bits ^ 0x7FFFFFFF)
            key_ref[j, :, hf * half:(hf + 1) * half] = jnp.where(col_h + off <= row_h, key, INT_MIN)
        return carry

    lax.fori_loop(0, n_tiles, score_body, 0)

    rb = min(DSA_COUNT_ROWS, tq)

    def count_ge(cand):
        parts = []
        for r0 in range(0, tq, rb):
            cand128 = jnp.broadcast_to(cand[r0:r0 + rb], (rb, 128))

            def body(j, acc):
                for c in range(tk // 128):
                    acc = acc + jnp.where(key_ref[j, r0:r0 + rb, c * 128:(c + 1) * 128] >= cand128, 1, 0)
                return acc

            acc = lax.fori_loop(0, n_tiles, body, jnp.zeros((rb, 128), I32))
            parts.append(jnp.sum(acc, axis=1, keepdims=True))
        return jnp.concatenate(parts, axis=0)

    zero = jnp.zeros((tq, 1), I32)
    lo = jnp.where(count_ge(zero) >= DSA_TOPK, zero, INT_MIN)

    def bit_body(b, lo):
        cand = lo | lax.shift_left(jnp.int32(1), 30 - b)
        return jnp.where(count_ge(cand) >= DSA_TOPK, cand, lo)

    lo = lax.fori_loop(0, 31, bit_body, lo)
    thr128 = jnp.broadcast_to(jnp.maximum(lo, INT_MIN + 1), (tq, 128))

    _flash_init(m_ref, acc_ref)

    def attn_body(j, carry):
        off = pl.multiple_of(j * tk, tk)
        keep = key_ref[j] >= jnp.concatenate([thr128] * (tk // 128), axis=1)
        for hd in range(4):
            lo_ = hd * HEAD_DIM
            _flash_rows(q_ref[:, lo_:lo_ + HEAD_DIM], k_ref[pl.ds(off, tk), lo_:lo_ + HEAD_DIM],
                        _with_ones(v_ref[pl.ds(off, tk), lo_:lo_ + HEAD_DIM]), None, keep, m_ref, acc_ref, hd)
        return carry

    lax.fori_loop(0, n_tiles, attn_body, 0)
    for hd in range(4):
        o_ref[:, hd * HEAD_DIM:(hd + 1) * HEAD_DIM] = _flash_result(acc_ref, hd, 4).astype(o_ref.dtype)


def _dsa(q, k, v, iq, ik2, iw):
    s = q.shape[0]
    tq = min(DSA_TQ, s)
    tk = min(ATT_TK, s)
    full = lambda a: _resident(a.shape, lambda i: (0, 0))
    return pl.pallas_call(
        functools.partial(_dsa_kernel, tk=tk),
        grid=(s // tq,),
        in_specs=[pl.BlockSpec((tq, 512), lambda i: (i, 0)), pl.BlockSpec((tq, 1024), lambda i: (i, 0)),
                  pl.BlockSpec((tq, 128), lambda i: (i, 0)), full(k), full(v), full(ik2)],
        out_specs=pl.BlockSpec((tq, 512), lambda i: (i, 0)),
        out_shape=jax.ShapeDtypeStruct((s, 512), BF16),
        scratch_shapes=[pltpu.VMEM((s // tk, tq, tk), I32), pltpu.VMEM((DSA_IDX_HEADS, tq, 128), F32)]
                       + _flash_scratch(4, tq),
        compiler_params=_cp("parallel"),
        name="dsa_attn",
    )(q, iq, iw, k, v, ik2)


def _out_proj_kernel(oa_ref, ob_ref, oc_ref, od_ref, w_ref, x_ref, g_ref, b_ref, o_ref):
    mix = _dot(oa_ref[...], w_ref[0:512, :])
    mix = mix + _dot(ob_ref[...], w_ref[512:1024, :])
    mix = mix + _dot(oc_ref[...], w_ref[1024:1536, :])
    mix = mix + _dot(od_ref[...], w_ref[1536:2048, :])
    o_ref[...] = _layer_norm(DN_ALPHA * x_ref[...] + mix, g_ref[...], b_ref[...])


def _out_proj_ln(o_a, o_b, o_c, o_d, w_out, x, g, b):
    s, d = x.shape
    tm = min(ROW_TILE, s)
    r512 = pl.BlockSpec((tm, 512), lambda i: (i, 0))
    vec = lambda a: _wspec(a, (1, d), lambda i: (0, 0))
    return pl.pallas_call(
        _out_proj_kernel,
        grid=(s // tm,),
        in_specs=[r512, r512, r512, r512, _wspec(w_out, _wshape(w_out), lambda i: (0, 0)),
                  pl.BlockSpec((tm, d), lambda i: (i, 0)), vec(g), vec(b)],
        out_specs=pl.BlockSpec((tm, d), lambda i: (i, 0)),
        out_shape=jax.ShapeDtypeStruct((s, d), F32),
        compiler_params=_cp("parallel"),
        name="out_proj_ln",
    )(o_a, o_b, o_c, o_d, w_out[0], x, g[0], b[0])


def _mem_kv_kernel(mem_ref, w_ref, o_ref):
    o_ref[...] = _dot(mem_ref[...].astype(BF16), w_ref[...]).astype(o_ref.dtype)


def _mem_kv(mem, wkv):
    m, d = mem.shape
    n = _wshape(wkv)[1]
    return pl.pallas_call(
        _mem_kv_kernel,
        grid=(1,),
        in_specs=[pl.BlockSpec((m, d), lambda i: (0, 0)), _wspec(wkv, (d, n), lambda i: (0, 0))],
        out_specs=pl.BlockSpec((m, n), lambda i: (0, 0)),
        out_shape=jax.ShapeDtypeStruct((m, n), BF16),
        compiler_params=_cp("arbitrary"),
        name="mem_kv",
    )(mem, wkv[0])


def _mem_attn_kernel(x_ref, wq_ref, kv_ref, wo_ref, g_ref, b_ref, o_ref):
    x = x_ref[...]
    scale = HEAD_DIM ** -0.5
    q = _dot(x.astype(BF16), wq_ref[...]).astype(BF16)
    out = jnp.zeros(x.shape, F32)
    for hd in range(4):
        lo = hd * HEAD_DIM
        s = _dot_nt(q[:, lo:lo + HEAD_DIM], kv_ref[:, lo:lo + HEAD_DIM]) * scale
        p = jnp.exp(s - jnp.max(s, axis=1, keepdims=True))
        o = _dot(p.astype(BF16), kv_ref[:, 512 + lo:512 + lo + HEAD_DIM]) / jnp.sum(p, axis=1, keepdims=True)
        out = out + _dot(o.astype(BF16), wo_ref[lo:lo + HEAD_DIM, :])
    o_ref[...] = _layer_norm(DN_ALPHA * x + out, g_ref[...], b_ref[...])


def _mem_attn_ln(x, wq, kv, wo, g, b):
    s, d = x.shape
    tm = min(ROW_TILE, s)
    full = lambda a: _wspec(a, _wshape(a), lambda i: (0, 0))
    return pl.pallas_call(
        _mem_attn_kernel,
        grid=(s // tm,),
        in_specs=[pl.BlockSpec((tm, d), lambda i: (i, 0)), full(wq), _resident(kv.shape, lambda i: (0, 0)),
                  full(wo), full(g), full(b)],
        out_specs=pl.BlockSpec((tm, d), lambda i: (i, 0)),
        out_shape=jax.ShapeDtypeStruct((s, d), F32),
        compiler_params=_cp("parallel"),
        name="mem_attn_ln",
    )(x, wq[0], kv, wo[0], g[0], b[0])


def _pad_last(w, n):
    return jnp.pad(w, [(0, 0)] * (w.ndim - 1) + [(0, n - w.shape[-1])])


def _split_w_in(w_in):
    a = w_in[..., 0:1536]
    b = _pad_last(w_in[..., 1536:2624], 1152)
    c = _pad_last(w_in[..., 2624:3916], 1408)
    d = jnp.concatenate([w_in[..., 3916:6476], _pad_last(w_in[..., 6476:6540], 128),
                         _pad_last(w_in[..., 6540:6556], 128)], axis=-1)
    return tuple(t.astype(BF16) for t in (a, b, c, d))


def _pad_w_uq(w_uq):
    lead = w_uq.shape[:-1]
    w = _pad_last(w_uq.reshape(lead + (4, MLA_NOPE + MLA_ROPE)), 256)
    return w.reshape(lead + (4 * 256,)).astype(BF16)


def _nsa_importance_map(s):
    n = s // NSA_CMP_STRIDE
    nblk = s // NSA_SEL_BLOCK
    ni = np.arange(n)[:, None]
    bi = np.arange(nblk)[None, :]
    m = ((ni >= 4 * bi - 1) & (ni <= 4 * bi + 3)).astype(np.float32)
    return jnp.asarray(np.pad(m, ((0, 0), (0, 128 - nblk))))


def _mixer(x, tabs, imp_map, w_groups, w_out, g_cq, g_ckv, w_uq, w_ukv, cmp_pe, cmp_w1, cmp_w2, ln_g, ln_b):
    cos128, sin128, cos64, sup64, sdn64 = tabs
    s = x.shape[0]
    w_a, w_b, w_c, w_d = w_groups

    aq, ak, av, akm = _proj_a(x, w_a, cos128, sin128)
    o_a = _moba(aq, ak, av, akm)

    bq, bk, bv = _proj_b(x, w_b, g_cq, g_ckv, w_uq, w_ukv, cos64, sup64, sdn64)
    o_b = _mla(bq, bk, bv)

    cq, cqr, kvcmp, kslc, vslc, kwin, vwin, gates = _proj_c(x, w_c, cos128, sin128)
    kv16 = kvcmp.reshape(2, s // NSA_CMP_STRIDE, NSA_CMP_STRIDE * HEAD_DIM)
    kvc = _nsa_compress(kv16, cmp_pe, cmp_w1, cmp_w2)
    o_cmp, sel = _nsa_cmp(cq, kvc, imp_map)
    o_c = _nsa_slc_win(cqr, sel, kslc, vslc, kwin, vwin, gates, o_cmp)

    dq, dk, dv, diq, dik, diw = _proj_d(x, w_d, cos128, sin128, cos64, sup64, sdn64)
    o_d = _dsa(dq, dk, dv, diq, dik, diw)

    return _out_proj_ln(o_a, o_b, o_c, o_d, w_out, x, ln_g, ln_b)


def kernel(x, mem, positions, ln_g, ln_b, ffn_w_gu, ffn_w_down, w_in, w_out, mla_g_cq, mla_g_ckv, mla_w_uq,
           mla_w_ukv, nsa_cmp_pe, nsa_cmp_w1, nsa_cmp_w2, mem_wq, mem_wkv, mem_wo):
    batch, s, d = x.shape
    w_gu_b, w_dn_b = ffn_w_gu.astype(BF16), ffn_w_down.astype(BF16)
    w_groups = _split_w_in(w_in)
    w_out_b = w_out.astype(BF16)
    w_uq_b, w_ukv_b = _pad_w_uq(mla_w_uq), mla_w_ukv.astype(BF16)
    g_cq, g_ckv = mla_g_cq[:, None, :], mla_g_ckv[:, None, :]
    pe = nsa_cmp_pe.reshape(DEPTH, 2, 1, -1)
    w1_b, w2_b = nsa_cmp_w1.astype(BF16), nsa_cmp_w2.astype(BF16)
    wq_b, wkv_b, wo_b = mem_wq.astype(BF16), mem_wkv.astype(BF16), mem_wo.astype(BF16)
    g4, b4 = ln_g[:, :, None, :], ln_b[:, :, None, :]
    imp_map = _nsa_importance_map(s)

    outs = []
    for bi in range(batch):
        xb = x[bi]
        tabs = _rope_tables(positions[bi])
        for l in range(DEPTH):
            at = lambda a, *lead: (a, (l,) + lead)
            xb = _ffn_ln(xb, at(w_gu_b, 0), at(w_dn_b, 0), at(g4, 0), at(b4, 0))
            xb = _mixer(xb, tabs, imp_map, tuple(at(w) for w in w_groups), at(w_out_b), at(g_cq), at(g_ckv),
                        at(w_uq_b), at(w_ukv_b), at(pe), at(w1_b), at(w2_b), at(g4, 1), at(b4, 1))
            kv = _mem_kv(mem[bi], at(wkv_b))
            xb = _mem_attn_ln(xb, at(wq_b), kv, at(wo_b), at(g4, 2), at(b4, 2))
            xb = _ffn_ln(xb, at(w_gu_b, 1), at(w_dn_b, 1), at(g4, 3), at(b4, 3))
        outs.append(xb)
    return jnp.stack(outs)
```

```python
import functools

import numpy as np
import jax
import jax.numpy as jnp
from jax import lax
from jax.experimental import pallas as pl
from jax.experimental.pallas import tpu as pltpu

F32 = jnp.float32
BF16 = jnp.bfloat16
I32 = jnp.int32

D_MODEL = 2048
DEPTH = 4
HEAD_DIM = 128
ROPE_THETA = 10000.0
LN_EPS = 1e-5
RMS_EPS = 1e-6

MOBA_BLOCK = 256
MOBA_TOPK = 3
MLA_NOPE = 128
MLA_ROPE = 64
NSA_CMP_STRIDE = 16
NSA_SEL_BLOCK = 64
NSA_SEL_TOPK = 16
NSA_WINDOW = 512
DSA_IDX_HEADS = 16
DSA_TOPK = 256
D_FF = 5632
DN_ALPHA = (2 * DEPTH) ** 0.25

NEG = -(2.0 ** 100)
LOG2_E = 1.4426950408889634
INT_MIN = -(2 ** 31)
VMEM_LIMIT = 56 * 1024 * 1024

ROW_TILE = 512
ATT_TQ = 512
ATT_TK = 1024
DSA_TQ = 256
DSA_COUNT_ROWS = 128


def _cp(*sem):
    return pltpu.CompilerParams(dimension_semantics=sem, vmem_limit_bytes=VMEM_LIMIT)


def _wspec(w, block, index_map, resident=True):
    _, lead = w
    shape = (None,) * len(lead) + tuple(block)
    imap = lambda *g: tuple(lead) + tuple(index_map(*g))
    if resident:
        return pl.BlockSpec(shape, imap, pipeline_mode=pl.Buffered(1))
    return pl.BlockSpec(shape, imap)


def _wshape(w):
    arr, lead = w
    return arr.shape[len(lead):]


def _resident(shape, index_map):
    return pl.BlockSpec(shape, index_map, pipeline_mode=pl.Buffered(1))


def _dot(a, b):
    return jnp.dot(a, b, preferred_element_type=F32)


def _dot_nt(a, b):
    return lax.dot_general(a, b, (((1,), (1,)), ((), ())), preferred_element_type=F32)


def _layer_norm(y, g, b):
    mu = jnp.mean(y, axis=-1, keepdims=True)
    d = y - mu
    var = jnp.mean(d * d, axis=-1, keepdims=True)
    return d * lax.rsqrt(var + LN_EPS) * g + b


def _rope128(x, cos, sin_signed):
    return x * cos + pltpu.roll(x, 64, 1) * sin_signed


def _rope64(x, cos, s_up, s_dn):
    return x * cos + pltpu.roll(x, 32, 1) * s_up + pltpu.roll(x, 96, 1) * s_dn


def _log2_scale(d):
    return float(d) ** -0.5 * LOG2_E


def _with_ones(v):
    return jnp.concatenate([v, jnp.ones_like(v)], axis=1)


def _flash_update(s, v1, bias, keep, m_ref, acc_ref, slot):
    if bias is not None:
        s = s + bias
    if keep is not None:
        s = jnp.where(keep, s, NEG)
    m_old = m_ref[slot]
    m_new = jnp.maximum(m_old, jnp.max(s, axis=1, keepdims=True))
    alpha = jnp.exp2(m_old - m_new)
    p = jnp.exp2(s - m_new)
    acc_ref[slot] = alpha * acc_ref[slot] + _dot(p.astype(BF16), v1)
    m_ref[slot] = m_new


FLASH_ROWS = 256


def _flash_rows(q, k, v1, bias, keep, m_ref, acc_ref, hd):
    n = max(1, q.shape[0] // FLASH_ROWS)
    r = q.shape[0] // n
    rows = [slice(sp * r, (sp + 1) * r) for sp in range(n)]
    logits = [_dot_nt(q[rw], k) for rw in rows]
    for sp, rw in enumerate(rows):
        _flash_update(logits[sp], v1, None if bias is None else bias[rw], None if keep is None else keep[rw],
                      m_ref, acc_ref, hd * n + sp)


def _flash_heads(qs, ks, v1s, bias, keep, m_ref, acc_ref):
    def logits(hd):
        q = qs[hd]
        n = max(1, q.shape[0] // FLASH_ROWS)
        r = q.shape[0] // n
        rows = [slice(sp * r, (sp + 1) * r) for sp in range(n)]
        return rows, [_dot_nt(q[rw], ks[hd]) for rw in rows]

    ahead = logits(0)
    for hd in range(len(qs)):
        rows, s = ahead
        if hd + 1 < len(qs):
            ahead = logits(hd + 1)
        for sp, rw in enumerate(rows):
            _flash_update(s[sp], v1s[hd], None if bias is None else bias[rw], None if keep is None else keep[rw],
                          m_ref, acc_ref, hd * len(rows) + sp)


def _causal_flash(q, k_ref, v_ref, s_ref, m_ref, acc_ref, i, tk):
    tq = q.shape[0]
    half = tq // 2
    q_a, q_b = q[0:half], q[half:tq]
    n_past = (i * tq) // tk

    def k_tile(j):
        return k_ref[pl.ds(pl.multiple_of(j * tk, tk), tk), :]

    def v_tile(j):
        return _with_ones(v_ref[pl.ds(pl.multiple_of(j * tk, tk), tk), :])

    s_ref[...] = _dot_nt(q_a, k_tile(0))

    def body(j, carry):
        s_a = s_ref[...]
        v1 = v_tile(j)
        s_b = _dot_nt(q_b, k_tile(j))
        _flash_update(s_a, v1, None, None, m_ref, acc_ref, 0)
        s_ref[...] = _dot_nt(q_a, k_tile(j + 1))
        _flash_update(s_b, v1, None, None, m_ref, acc_ref, 1)
        return carry

    lax.fori_loop(0, n_past, body, 0)
    v1 = v_tile(n_past)
    s_b = _dot_nt(q_b, k_tile(n_past))
    row = i * tq + lax.broadcasted_iota(I32, (tq, tk), 0)
    col = n_past * tk + lax.broadcasted_iota(I32, (tq, tk), 1)
    keep = col <= row
    _flash_update(s_ref[...], v1, None, keep[0:half], m_ref, acc_ref, 0)
    _flash_update(s_b, v1, None, keep[half:tq], m_ref, acc_ref, 1)


def _flash_result(acc_ref, hd, heads):
    n = acc_ref.shape[0] // heads
    parts = []
    for sp in range(n):
        a = acc_ref[hd * n + sp]
        parts.append(a[:, 0:HEAD_DIM] / a[:, HEAD_DIM:2 * HEAD_DIM])
    return jnp.concatenate(parts, axis=0) if n > 1 else parts[0]


def _flash_scratch(heads, tq):
    n = max(1, tq // FLASH_ROWS)
    return [pltpu.VMEM((heads * n, tq // n, 1), F32), pltpu.VMEM((heads * n, tq // n, 2 * HEAD_DIM), F32)]


def _flash_init(m_ref, acc_ref):
    m_ref[...] = jnp.full(m_ref.shape, NEG, F32)
    acc_ref[...] = jnp.zeros(acc_ref.shape, F32)


def _top_k_mask(work, lane, k):
    n = work.shape[1]
    sel = jnp.zeros(work.shape, F32)
    for _ in range(k):
        m = jnp.max(work, axis=1, keepdims=True)
        idx = jnp.min(jnp.where(work == m, lane, n), axis=1, keepdims=True)
        pick = lane == jnp.where(m > -jnp.inf, idx, n)
        sel = jnp.where(pick, 1.0, sel)
        work = jnp.where(pick, -jnp.inf, work)
    return sel


def _key_block_table(s, block):
    return jnp.asarray((np.arange(128)[None, :] == (np.arange(s) // block)[:, None]) * -NEG, dtype=BF16)


def _rope_table_kernel(pos_ref, c_ref, cos128_ref, sin128_ref, cos64_ref, sup64_ref, sdn64_ref):
    pos = pos_ref[...]
    a128 = pos * c_ref[0:1, :]
    cos128_ref[...] = jnp.cos(a128)
    sin128_ref[...] = jnp.sin(a128) * c_ref[1:2, :]
    a64 = pos * c_ref[2:3, :]
    s64 = jnp.sin(a64)
    cos64_ref[...] = jnp.cos(a64)
    sup64_ref[...] = s64 * c_ref[3:4, :]
    sdn64_ref[...] = s64 * c_ref[4:5, :]


def _rope_consts():
    lane = np.arange(128)
    inv128 = (ROPE_THETA ** (-np.arange(0, 128, 2, dtype=np.float32) / 128)).astype(np.float32)
    inv64 = (ROPE_THETA ** (-np.arange(0, 64, 2, dtype=np.float32) / 64)).astype(np.float32)
    c = np.zeros((8, 128), np.float32)
    c[0] = inv128[lane % 64]
    c[1] = np.where(lane < 64, -1.0, 1.0)
    c[2] = inv64[(lane % 64) % 32]
    c[3] = np.where(lane % 64 >= 32, 1.0, 0.0)
    c[4] = np.where(lane % 64 < 32, -1.0, 0.0)
    return jnp.asarray(c)


def _rope_tables(positions):
    s = positions.shape[0]
    tm = min(ROW_TILE, s)
    pos = positions.astype(F32).reshape(s, 1)
    tab = jax.ShapeDtypeStruct((s, 128), F32)
    row = pl.BlockSpec((tm, 128), lambda i: (i, 0))
    return pl.pallas_call(
        _rope_table_kernel,
        grid=(s // tm,),
        in_specs=[pl.BlockSpec((tm, 1), lambda i: (i, 0)), pl.BlockSpec((8, 128), lambda i: (0, 0))],
        out_specs=[row] * 5,
        out_shape=[tab] * 5,
        compiler_params=_cp("parallel"),
        name="rope_tables",
    )(pos, _rope_consts())


def _ffn_kernel(x_ref, wg_ref, wu_ref, wd_ref, g_ref, b_ref, o_ref, xb_ref, acc_ref):
    j = pl.program_id(1)

    @pl.when(j == 0)
    def _():
        xb_ref[...] = x_ref[...].astype(BF16)
        acc_ref[...] = jnp.zeros_like(acc_ref)

    xb = xb_ref[...]
    g = _dot(xb, wg_ref[...])
    u = _dot(xb, wu_ref[...])
    h = (g * jax.nn.sigmoid(g) * u).astype(BF16)
    acc_ref[...] += _dot(h, wd_ref[...])

    @pl.when(j == pl.num_programs(1) - 1)
    def _():
        y = DN_ALPHA * x_ref[...] + 0.5 * acc_ref[...]
        o_ref[...] = _layer_norm(y, g_ref[...], b_ref[...])


def _ffn_ln(x, w_gu, w_down, g, b):
    s, d = x.shape
    f = _wshape(w_down)[0]
    tm = min(ROW_TILE, s)
    tf = 512
    nf = f // tf
    return pl.pallas_call(
        _ffn_kernel,
        grid=(s // tm, nf),
        in_specs=[
            pl.BlockSpec((tm, d), lambda i, j: (i, 0)),
            _wspec(w_gu, (d, tf), lambda i, j: (0, j), resident=False),
            _wspec(w_gu, (d, tf), lambda i, j: (0, j + nf), resident=False),
            _wspec(w_down, (tf, d), lambda i, j: (j, 0), resident=False),
            _wspec(g, (1, d), lambda i, j: (0, 0)),
            _wspec(b, (1, d), lambda i, j: (0, 0)),
        ],
        out_specs=pl.BlockSpec((tm, d), lambda i, j: (i, 0)),
        out_shape=jax.ShapeDtypeStruct((s, d), F32),
        scratch_shapes=[pltpu.VMEM((tm, d), BF16), pltpu.VMEM((tm, d), F32)],
        compiler_params=_cp("parallel", "arbitrary"),
        name="ffn_ln",
    )(x, w_gu[0], w_gu[0], w_down[0], g[0], b[0])


def _proj_a_kernel(x_ref, w_ref, cos_ref, sin_ref, kb_ref, q_ref, k_ref, v_ref, km_ref):
    h = _dot(x_ref[...].astype(BF16), w_ref[...])
    cos, sin = cos_ref[...], sin_ref[...]
    nblk = km_ref.shape[0]
    for hd in range(4):
        lo = hd * HEAD_DIM
        q_ref[:, lo:lo + HEAD_DIM] = _rope128(h[:, lo:lo + HEAD_DIM], cos, sin)
        k = _rope128(h[:, 512 + lo:512 + lo + HEAD_DIM], cos, sin)
        k_ref[:, 2 * lo:2 * lo + HEAD_DIM] = k.astype(BF16)
        k_ref[:, 2 * lo + HEAD_DIM:2 * lo + 2 * HEAD_DIM] = kb_ref[...]
        for bi in range(nblk):
            kb = k[bi * MOBA_BLOCK:(bi + 1) * MOBA_BLOCK]
            km_ref[bi, :, lo:lo + HEAD_DIM] = jnp.mean(kb, axis=0, keepdims=True)
    v_ref[...] = h[:, 1024:1536].astype(BF16)


def _proj_a(x, w, cos128, sin128):
    s, d = x.shape
    tm = min(ROW_TILE, s)
    nblk = tm // MOBA_BLOCK
    row512 = pl.BlockSpec((tm, 512), lambda i: (i, 0))
    tab = pl.BlockSpec((tm, 128), lambda i: (i, 0))
    return pl.pallas_call(
        _proj_a_kernel,
        grid=(s // tm,),
        in_specs=[pl.BlockSpec((tm, d), lambda i: (i, 0)), _wspec(w, _wshape(w), lambda i: (0, 0)), tab, tab, tab],
        out_specs=[row512, pl.BlockSpec((tm, 1024), lambda i: (i, 0)), row512,
                   pl.BlockSpec((nblk, 1, 512), lambda i: (i, 0, 0))],
        out_shape=[
            jax.ShapeDtypeStruct((s, 512), F32),
            jax.ShapeDtypeStruct((s, 1024), BF16),
            jax.ShapeDtypeStruct((s, 512), BF16),
            jax.ShapeDtypeStruct((s // MOBA_BLOCK, 1, 512), F32),
        ],
        compiler_params=_cp("parallel"),
        name="proj_moba",
    )(x, w[0], cos128, sin128, _key_block_table(s, MOBA_BLOCK))


def _rms_norm(x, g):
    return x * lax.rsqrt(jnp.mean(x * x, axis=-1, keepdims=True) + RMS_EPS) * g


def _proj_b_kernel(x_ref, w_ref, gq_ref, gkv_ref, wuq_ref, wukv_ref, cos_ref, sup_ref, sdn_ref,
                   q_ref, k_ref, v_ref):
    h = _dot(x_ref[...].astype(BF16), w_ref[...])
    cos, sup, sdn = cos_ref[...], sup_ref[...], sdn_ref[...]
    cq = _rms_norm(h[:, 0:512], gq_ref[...]).astype(BF16)
    ckv = _rms_norm(h[:, 512:1024], gkv_ref[...]).astype(BF16)
    kr = _rope64(h[:, 1024:1152], cos, sup, sdn).astype(BF16)
    q = _dot(cq, wuq_ref[...])
    kv = _dot(ckv, wukv_ref[...])
    qs = _log2_scale(MLA_NOPE + MLA_ROPE)
    for hd in range(4):
        lo = hd * 256
        q_ref[:, lo:lo + 128] = (q[:, lo:lo + 128] * qs).astype(BF16)
        q_ref[:, lo + 128:lo + 256] = (_rope64(q[:, lo + 128:lo + 256], cos, sup, sdn) * qs).astype(BF16)
        k_ref[:, lo:lo + 128] = kv[:, lo:lo + 128].astype(BF16)
        k_ref[:, lo + 128:lo + 256] = kr
        v_ref[:, hd * 128:(hd + 1) * 128] = kv[:, lo + 128:lo + 256].astype(BF16)


def _proj_b(x, w, g_cq, g_ckv, w_uq, w_ukv, cos64, sup64, sdn64):
    s, d = x.shape
    tm = min(ROW_TILE, s)
    tab = pl.BlockSpec((tm, 128), lambda i: (i, 0))
    full = lambda a: _wspec(a, _wshape(a), lambda i: (0, 0))
    return pl.pallas_call(
        _proj_b_kernel,
        grid=(s // tm,),
        in_specs=[pl.BlockSpec((tm, d), lambda i: (i, 0)), full(w), full(g_cq), full(g_ckv), full(w_uq),
                  full(w_ukv), tab, tab, tab],
        out_specs=[pl.BlockSpec((tm, 1024), lambda i: (i, 0)), pl.BlockSpec((tm, 1024), lambda i: (i, 0)),
                   pl.BlockSpec((tm, 512), lambda i: (i, 0))],
        out_shape=[jax.ShapeDtypeStruct((s, 1024), BF16), jax.ShapeDtypeStruct((s, 1024), BF16),
                   jax.ShapeDtypeStruct((s, 512), BF16)],
        compiler_params=_cp("parallel"),
        name="proj_mla",
    )(x, w[0], g_cq[0], g_ckv[0], w_uq[0], w_ukv[0], cos64, sup64, sdn64)


def _proj_c_kernel(x_ref, w_ref, cos_ref, sin_ref, kb_ref, q_ref, qr_ref, kvcmp_ref, kslc_ref, vslc_ref,
                   kwin_ref, vwin_ref, gate_ref):
    h = _dot(x_ref[...].astype(BF16), w_ref[...])
    cos, sin = cos_ref[...], sin_ref[...]
    for hd in range(4):
        lo = hd * HEAD_DIM
        qh = h[:, lo:lo + HEAD_DIM]
        q_ref[:, lo:lo + HEAD_DIM] = qh.astype(BF16)
        qr_ref[:, lo:lo + HEAD_DIM] = (_rope128(qh, cos, sin) * _log2_scale(HEAD_DIM)).astype(BF16)
    kv = lambda i: h[:, 512 + i * 128:512 + (i + 1) * 128]
    kvcmp_ref[0] = kv(0).astype(BF16)
    kvcmp_ref[1] = kv(1).astype(BF16)
    kslc_ref[:, 0:HEAD_DIM] = _rope128(kv(2), cos, sin).astype(BF16)
    kslc_ref[:, HEAD_DIM:2 * HEAD_DIM] = kb_ref[...]
    vslc_ref[...] = kv(3).astype(BF16)
    kwin_ref[...] = _rope128(kv(4), cos, sin).astype(BF16)
    vwin_ref[...] = kv(5).astype(BF16)
    gate_ref[...] = jax.nn.sigmoid(h[:, 1280:1408])


def _proj_c(x, w, cos128, sin128):
    s, d = x.shape
    tm = min(ROW_TILE, s)
    r512 = pl.BlockSpec((tm, 512), lambda i: (i, 0))
    r128 = pl.BlockSpec((tm, 128), lambda i: (i, 0))
    b512 = jax.ShapeDtypeStruct((s, 512), BF16)
    b128 = jax.ShapeDtypeStruct((s, 128), BF16)
    return pl.pallas_call(
        _proj_c_kernel,
        grid=(s // tm,),
        in_specs=[pl.BlockSpec((tm, d), lambda i: (i, 0)), _wspec(w, _wshape(w), lambda i: (0, 0)), r128, r128, r128],
        out_specs=[r512, r512, pl.BlockSpec((2, tm, 128), lambda i: (0, i, 0)),
                   pl.BlockSpec((tm, 256), lambda i: (i, 0))] + [r128] * 4,
        out_shape=[b512, b512, jax.ShapeDtypeStruct((2, s, 128), BF16), jax.ShapeDtypeStruct((s, 256), BF16)]
                  + [b128] * 3 + [jax.ShapeDtypeStruct((s, 128), F32)],
        compiler_params=_cp("parallel"),
        name="proj_nsa",
    )(x, w[0], cos128, sin128, _key_block_table(s, NSA_SEL_BLOCK))


def _proj_d_kernel(x_ref, w_ref, cos_ref, sin_ref, c64_ref, sup_ref, sdn_ref,
                   q_ref, k_ref, v_ref, iq_ref, ik_ref, iw_ref):
    h = _dot(x_ref[...].astype(BF16), w_ref[...])
    cos, sin = cos_ref[...], sin_ref[...]
    c64, sup, sdn = c64_ref[...], sup_ref[...], sdn_ref[...]
    for hd in range(4):
        lo = hd * HEAD_DIM
        q_ref[:, lo:lo + HEAD_DIM] = (_rope128(h[:, lo:lo + HEAD_DIM], cos, sin) * _log2_scale(HEAD_DIM)).astype(BF16)
        k_ref[:, lo:lo + HEAD_DIM] = _rope128(h[:, 512 + lo:512 + lo + HEAD_DIM], cos, sin).astype(BF16)
    v_ref[...] = h[:, 1024:1536].astype(BF16)
    for p in range(8):
        lo = 1536 + p * 128
        iq_ref[:, p * 128:(p + 1) * 128] = _rope64(h[:, lo:lo + 128], c64, sup, sdn).astype(BF16)
    ik = _rope64(h[:, 2560:2688], c64, sup, sdn)
    ik_ref[:, 0:128] = ik.astype(BF16)
    ik_ref[:, 128:256] = pltpu.roll(ik, 64, 1).astype(BF16)
    iw_ref[...] = h[:, 2688:2816] * (1.0 / 32.0)


def _proj_d(x, w, cos128, sin128, cos64, sup64, sdn64):
    s, d = x.shape
    tm = min(ROW_TILE, s)
    r = lambda n: pl.BlockSpec((tm, n), lambda i: (i, 0))
    return pl.pallas_call(
        _proj_d_kernel,
        grid=(s // tm,),
        in_specs=[pl.BlockSpec((tm, d), lambda i: (i, 0)), _wspec(w, _wshape(w), lambda i: (0, 0))] + [r(128)] * 5,
        out_specs=[r(512), r(512), r(512), r(1024), r(256), r(128)],
        out_shape=[jax.ShapeDtypeStruct((s, 512), BF16)] * 3 + [
            jax.ShapeDtypeStruct((s, 1024), BF16), jax.ShapeDtypeStruct((s, 256), BF16),
            jax.ShapeDtypeStruct((s, 128), F32)],
        compiler_params=_cp("parallel"),
        name="proj_dsa",
    )(x, w[0], cos128, sin128, cos64, sup64, sdn64)


def _moba_kernel(q_ref, k_ref, v_ref, km_ref, o_ref, m_ref, acc_ref, s_ref, *, tk, nb):
    i = pl.program_id(1)
    tq = q_ref.shape[0]
    q32 = q_ref[...]

    gate = lax.dot_general(q32, km_ref[...], (((1,), (1,)), ((), ())),
                           precision=lax.Precision.HIGHEST, preferred_element_type=F32)
    blk = lax.broadcasted_iota(I32, (tq, 128), 1)
    cur = (i * tq + lax.broadcasted_iota(I32, (tq, 128), 0)) // MOBA_BLOCK
    gate = jnp.where(blk < cur, gate, -jnp.inf)
    sel = _top_k_mask(gate, blk, min(MOBA_TOPK, nb - 1))
    unchosen = jnp.where(blk == cur, 1.0, sel) - 1.0
    qa = jnp.concatenate([q32 * _log2_scale(HEAD_DIM), unchosen], axis=1).astype(BF16)

    _flash_init(m_ref, acc_ref)
    _causal_flash(qa, k_ref, v_ref, s_ref, m_ref, acc_ref, i, tk)
    o_ref[...] = _flash_result(acc_ref, 0, 1).astype(o_ref.dtype)


def _moba(q32, k, v, kmean):
    s = q32.shape[0]
    nb = s // MOBA_BLOCK
    tq = min(ATT_TQ, s)
    tk = min(ATT_TK, s)
    km = jnp.pad(kmean.reshape(nb, 512), ((0, 128 - nb), (0, 0)))
    return pl.pallas_call(
        functools.partial(_moba_kernel, tk=tk, nb=nb),
        grid=(4, s // tq),
        in_specs=[
            pl.BlockSpec((tq, HEAD_DIM), lambda h, i: (i, h)),
            pl.BlockSpec((s, 2 * HEAD_DIM), lambda h, i: (0, h)),
            pl.BlockSpec((s, HEAD_DIM), lambda h, i: (0, h)),
            pl.BlockSpec((128, HEAD_DIM), lambda h, i: (0, h)),
        ],
        out_specs=pl.BlockSpec((tq, HEAD_DIM), lambda h, i: (i, h)),
        out_shape=jax.ShapeDtypeStruct((s, 512), BF16),
        scratch_shapes=_flash_scratch(1, tq) + [pltpu.VMEM((tq // 2, tk), F32)],
        compiler_params=_cp("parallel", "arbitrary"),
        name="moba_attn",
    )(q32, k, v, km)


def _mla_kernel(q_ref, k_ref, v_ref, o_ref, m_ref, acc_ref, s_ref, *, tk):
    _flash_init(m_ref, acc_ref)
    _causal_flash(q_ref[...], k_ref, v_ref, s_ref, m_ref, acc_ref, pl.program_id(1), tk)
    o_ref[...] = _flash_result(acc_ref, 0, 1).astype(o_ref.dtype)


def _mla(qcat, kcat, v):
    s = qcat.shape[0]
    tq = min(ATT_TQ, s)
    tk = min(ATT_TK, s)
    return pl.pallas_call(
        functools.partial(_mla_kernel, tk=tk),
        grid=(4, s // tq),
        in_specs=[
            pl.BlockSpec((tq, 256), lambda h, i: (i, h)),
            pl.BlockSpec((s, 256), lambda h, i: (0, h)),
            pl.BlockSpec((s, HEAD_DIM), lambda h, i: (0, h)),
        ],
        out_specs=pl.BlockSpec((tq, HEAD_DIM), lambda h, i: (i, h)),
        out_shape=jax.ShapeDtypeStruct((s, 512), BF16),
        scratch_shapes=_flash_scratch(1, tq) + [pltpu.VMEM((tq // 2, tk), F32)],
        compiler_params=_cp("parallel", "arbitrary"),
        name="mla_attn",
    )(qcat, kcat, v)


def _gelu_tanh(x):
    return 0.5 * x * (1.0 + jnp.tanh(0.7978845608028654 * (x + 0.044715 * x * x * x)))


def _nsa_compress_kernel(t_ref, pe_ref, w1_ref, w2_ref, o_ref):
    t = t_ref[0]
    w1 = w1_ref[...]
    half = t.shape[1]
    n = t.shape[0]
    lo = _dot(t, w1[0:half])
    hi = _dot(t, w1[half:2 * half])
    pe = _dot(pe_ref[...].astype(BF16), w1)
    pre = lo + pltpu.roll(hi, n - 1, 0) + pe
    o_ref[0] = _dot(_gelu_tanh(pre).astype(BF16), w2_ref[...]).astype(o_ref.dtype)


def _nsa_compress(kv16, pe, w1, w2):
    n = kv16.shape[1]
    sub = lambda a: _wspec(a, (None,) + _wshape(a)[1:], lambda i: (i, 0, 0), resident=False)
    return pl.pallas_call(
        _nsa_compress_kernel,
        grid=(2,),
        in_specs=[pl.BlockSpec((1,) + kv16.shape[1:], lambda i: (i, 0, 0)), sub(pe), sub(w1), sub(w2)],
        out_specs=pl.BlockSpec((1, n, HEAD_DIM), lambda i: (i, 0, 0)),
        out_shape=jax.ShapeDtypeStruct((2, n, HEAD_DIM), BF16),
        compiler_params=_cp("parallel"),
        name="nsa_compress",
    )(kv16, pe[0], w1[0], w2[0])


def _nsa_cmp_kernel(q_ref, kvc_ref, m_ref, o_ref, sel_ref, *, n_sel):
    i = pl.program_id(0)
    tq = q_ref.shape[0]
    n = kvc_ref.shape[1]
    nblk = m_ref.shape[1]
    scale = HEAD_DIM ** -0.5
    kc, vc = kvc_ref[0], kvc_ref[1]
    t_pos = i * tq + lax.broadcasted_iota(I32, (tq, n), 0)
    cmp_end = lax.broadcasted_iota(I32, (tq, n), 1) * NSA_CMP_STRIDE + (2 * NSA_CMP_STRIDE - 1)
    ok = cmp_end <= t_pos
    p_sum = jnp.zeros((tq, n), F32)
    for hd in range(4):
        lo = hd * HEAD_DIM
        s = jnp.where(ok, _dot_nt(q_ref[:, lo:lo + HEAD_DIM], kc) * scale, NEG)
        m = jnp.max(s, axis=1, keepdims=True)
        e = jnp.where(ok, jnp.exp(s - m), 0.0)
        l = jnp.sum(e, axis=1, keepdims=True)
        p = e / jnp.where(l > 0, l, 1.0)
        p_sum = p_sum + p
        o_ref[:, lo:lo + HEAD_DIM] = _dot(p.astype(BF16), vc)

    imp = jnp.dot(p_sum, m_ref[...], precision=lax.Precision.HIGHEST, preferred_element_type=F32)
    blk = lax.broadcasted_iota(I32, (tq, nblk), 1)
    cur = (i * tq + lax.broadcasted_iota(I32, (tq, nblk), 0)) // NSA_SEL_BLOCK
    forced = (blk == 0) | (blk == cur) | (blk == cur - 1)
    imp = jnp.where(blk > cur, -jnp.inf, jnp.where(forced, jnp.inf, imp))
    sel_ref[...] = (_top_k_mask(imp, blk, n_sel) - 1.0).astype(sel_ref.dtype)


def _nsa_cmp(q, kvc, imp_map):
    s = q.shape[0]
    n = kvc.shape[1]
    nblk = imp_map.shape[1]
    tq = min(256, s)
    return pl.pallas_call(
        functools.partial(_nsa_cmp_kernel, n_sel=min(NSA_SEL_TOPK, s // NSA_SEL_BLOCK)),
        grid=(s // tq,),
        in_specs=[pl.BlockSpec((tq, 512), lambda i: (i, 0)),
                  _resident((2, n, HEAD_DIM), lambda i: (0, 0, 0)),
                  _resident((n, nblk), lambda i: (0, 0))],
        out_specs=[pl.BlockSpec((tq, 512), lambda i: (i, 0)), pl.BlockSpec((tq, nblk), lambda i: (i, 0))],
        out_shape=[jax.ShapeDtypeStruct((s, 512), F32), jax.ShapeDtypeStruct((s, nblk), BF16)],
        compiler_params=_cp("parallel"),
        name="nsa_cmp_select",
    )(q, kvc, imp_map)


def _nsa_slc_win_kernel(q_ref, sel_ref, ks_ref, vs_ref, kw_ref, vw_ref, g_ref, oc_ref, o_ref,
                        m_ref, acc_ref, *, tk):
    i = pl.program_id(0)
    tq = q_ref.shape[0]
    _flash_init(m_ref, acc_ref)

    def tile(j, causal):
        off = pl.multiple_of(j * tk, tk)
        keep = None
        if causal:
            row = i * tq + lax.broadcasted_iota(I32, (tq, tk), 0)
            col = off + lax.broadcasted_iota(I32, (tq, tk), 1)
            keep = col <= row
        k = ks_ref[pl.ds(off, tk), 0:HEAD_DIM]
        v1 = _with_ones(vs_ref[pl.ds(off, tk), :])
        bias = _dot_nt(sel_ref[...], ks_ref[pl.ds(off, tk), HEAD_DIM:2 * HEAD_DIM])
        _flash_heads([q_ref[:, hd * HEAD_DIM:(hd + 1) * HEAD_DIM] for hd in range(4)], [k] * 4, [v1] * 4,
                     bias, keep, m_ref, acc_ref)

    n_past = (i * tq) // tk

    def body(j, carry):
        tile(j, False)
        return carry

    lax.fori_loop(0, n_past, body, 0)
    tile(n_past, True)

    wlen = NSA_WINDOW + tq
    start = pl.multiple_of(jnp.maximum(i * tq - NSA_WINDOW, 0), 256)
    kw = kw_ref[pl.ds(start, wlen), :]
    vw1 = _with_ones(vw_ref[pl.ds(start, wlen), :])
    diff = (i * tq + lax.broadcasted_iota(I32, (tq, wlen), 0)) - (start + lax.broadcasted_iota(I32, (tq, wlen), 1))
    in_win = (diff >= 0) & (diff < NSA_WINDOW)
    g = g_ref[...]
    for hd in range(4):
        lo = hd * HEAD_DIM
        s = jnp.where(in_win, _dot_nt(q_ref[:, lo:lo + HEAD_DIM], kw), NEG)
        p = jnp.exp2(s - jnp.max(s, axis=1, keepdims=True))
        pv = _dot(p.astype(BF16), vw1)
        o_win = pv[:, 0:HEAD_DIM] / pv[:, HEAD_DIM:2 * HEAD_DIM]
        o_slc = _flash_result(acc_ref, hd, 4)
        out = (g[:, 3 * hd:3 * hd + 1] * oc_ref[:, lo:lo + HEAD_DIM] + g[:, 3 * hd + 1:3 * hd + 2] * o_slc
               + g[:, 3 * hd + 2:3 * hd + 3] * o_win)
        o_ref[:, lo:lo + HEAD_DIM] = out.astype(o_ref.dtype)


def _nsa_slc_win(q_r, sel, kslc, vslc, kwin, vwin, gates, o_cmp):
    s = q_r.shape[0]
    tq = min(ATT_TQ, s)
    tk = min(ATT_TK, s)
    full = lambda a: _resident(a.shape, lambda i: (0,) * a.ndim)
    return pl.pallas_call(
        functools.partial(_nsa_slc_win_kernel, tk=tk),
        grid=(s // tq,),
        in_specs=[pl.BlockSpec((tq, 512), lambda i: (i, 0)), pl.BlockSpec((tq, 128), lambda i: (i, 0)),
                  full(kslc), full(vslc), full(kwin), full(vwin),
                  pl.BlockSpec((tq, 128), lambda i: (i, 0)), pl.BlockSpec((tq, 512), lambda i: (i, 0))],
        out_specs=pl.BlockSpec((tq, 512), lambda i: (i, 0)),
        out_shape=jax.ShapeDtypeStruct((s, 512), BF16),
        scratch_shapes=_flash_scratch(4, tq),
        compiler_params=_cp("parallel"),
        name="nsa_slc_win",
    )(q_r, sel, kslc, vslc, kwin, vwin, gates, o_cmp)


def _dsa_kernel(q_ref, iq_ref, iw_ref, k_ref, v_ref, ik_ref, o_ref, key_ref, wb_ref, m_ref, acc_ref, *, tk):
    i = pl.program_id(0)
    tq = q_ref.shape[0]
    half = tk // 2
    n_tiles = (i * tq + tq + tk - 1) // tk
    row_h = lax.broadcasted_iota(I32, (tq, half), 0) + i * tq
    col_h = lax.broadcasted_iota(I32, (tq, half), 1)

    w = iw_ref[...]
    for hd in range(DSA_IDX_HEADS):
        wb_ref[hd] = jnp.broadcast_to(w[:, hd:hd + 1], (tq, 128))

    def score_body(j, carry):
        for hf in range(2):
            off = pl.multiple_of(j * tk + hf * half, half)
            ik_even = ik_ref[pl.ds(off, half), 0:128]
            ik_odd = ik_ref[pl.ds(off, half), 128:256]
            sc = jnp.zeros((tq, half), F32)
            for p in range(DSA_IDX_HEADS // 2):
                x = iq_ref[:, p * 128:(p + 1) * 128]
                we = jnp.concatenate([wb_ref[2 * p]] * (half // 128), axis=1)
                wo = jnp.concatenate([wb_ref[2 * p + 1]] * (half // 128), axis=1)
                sc = sc + jnp.maximum(_dot_nt(x, ik_even), 0.0) * we
                sc = sc + jnp.maximum(_dot_nt(x, ik_odd), 0.0) * wo
            bits = pltpu.bitcast(sc, I32)
            key = jnp.where(bits >= 0, bits, bits ^ 0x7FFFFFFF)
            key_ref[j, :, hf * half:(hf + 1) * half] = jnp.where(col_h + off <= row_h, key, INT_MIN)
        return carry

    lax.fori_loop(0, n_tiles, score_body, 0)

    rb = min(DSA_COUNT_ROWS, tq)

    ones = jnp.ones((128, 128), BF16)
    k_f = float(DSA_TOPK)

    def count_ge(cand128):
        parts = []
        for r0 in range(0, tq, rb):
            cand_b = cand128[r0:r0 + rb]

            def body(j, acc):
                for c in range(tk // 128):
                    acc = acc + jnp.where(key_ref[j, r0:r0 + rb, c * 128:(c + 1) * 128] >= cand_b, 1, 0)
                return acc

            parts.append(lax.fori_loop(0, n_tiles, body, jnp.zeros((rb, 128), I32)))
        per_lane = jnp.concatenate(parts, axis=0).astype(F32).astype(BF16)
        return _dot(per_lane, ones)

    zero = jnp.zeros((tq, 128), I32)
    c0 = count_ge(zero)
    lo0 = jnp.where(c0 >= k_f, zero, INT_MIN)
    c_lo0 = jnp.where(c0 >= k_f, c0, -1.0)

    def bit_cond(state):
        b, _, _, more = state
        return (b < 31) & (more > 0.0)

    def bit_body(state):
        b, lo, c_lo, _ = state
        more = jnp.max(jnp.where(c_lo != k_f, 1.0, 0.0))
        cand = lo | lax.shift_left(jnp.int32(1), 30 - b)
        c = count_ge(cand)
        take = c >= k_f
        return b + 1, jnp.where(take, cand, lo), jnp.where(take, c, c_lo), more

    _, lo, _, _ = lax.while_loop(bit_cond, bit_body, (jnp.int32(0), lo0, c_lo0, jnp.float32(1.0)))
    thr128 = jnp.maximum(lo, INT_MIN + 1)

    _flash_init(m_ref, acc_ref)

    def attn_body(j, carry):
        off = pl.multiple_of(j * tk, tk)
        keep = key_ref[j] >= jnp.concatenate([thr128] * (tk // 128), axis=1)
        cols = [slice(hd * HEAD_DIM, (hd + 1) * HEAD_DIM) for hd in range(4)]
        _flash_heads([q_ref[:, c] for c in cols], [k_ref[pl.ds(off, tk), c] for c in cols],
                     [_with_ones(v_ref[pl.ds(off, tk), c]) for c in cols], None, keep, m_ref, acc_ref)
        return carry

    lax.fori_loop(0, n_tiles, attn_body, 0)
    for hd in range(4):
        o_ref[:, hd * HEAD_DIM:(hd + 1) * HEAD_DIM] = _flash_result(acc_ref, hd, 4).astype(o_ref.dtype)


def _dsa(q, k, v, iq, ik2, iw):
    s = q.shape[0]
    tq = min(DSA_TQ, s)
    tk = min(ATT_TK, s)
    full = lambda a: _resident(a.shape, lambda i: (0, 0))
    return pl.pallas_call(
        functools.partial(_dsa_kernel, tk=tk),
        grid=(s // tq,),
        in_specs=[pl.BlockSpec((tq, 512), lambda i: (i, 0)), pl.BlockSpec((tq, 1024), lambda i: (i, 0)),
                  pl.BlockSpec((tq, 128), lambda i: (i, 0)), full(k), full(v), full(ik2)],
        out_specs=pl.BlockSpec((tq, 512), lambda i: (i, 0)),
        out_shape=jax.ShapeDtypeStruct((s, 512), BF16),
        scratch_shapes=[pltpu.VMEM((s // tk, tq, tk), I32), pltpu.VMEM((DSA_IDX_HEADS, tq, 128), F32)]
                       + _flash_scratch(4, tq),
        compiler_params=_cp("parallel"),
        name="dsa_attn",
    )(q, iq, iw, k, v, ik2)


def _out_proj_kernel(oa_ref, ob_ref, oc_ref, od_ref, w_ref, x_ref, g_ref, b_ref, o_ref):
    mix = _dot(oa_ref[...], w_ref[0:512, :])
    mix = mix + _dot(ob_ref[...], w_ref[512:1024, :])
    mix = mix + _dot(oc_ref[...], w_ref[1024:1536, :])
    mix = mix + _dot(od_ref[...], w_ref[1536:2048, :])
    o_ref[...] = _layer_norm(DN_ALPHA * x_ref[...] + mix, g_ref[...], b_ref[...])


def _out_proj_ln(o_a, o_b, o_c, o_d, w_out, x, g, b):
    s, d = x.shape
    tm = min(ROW_TILE, s)
    r512 = pl.BlockSpec((tm, 512), lambda i: (i, 0))
    vec = lambda a: _wspec(a, (1, d), lambda i: (0, 0))
    return pl.pallas_call(
        _out_proj_kernel,
        grid=(s // tm,),
        in_specs=[r512, r512, r512, r512, _wspec(w_out, _wshape(w_out), lambda i: (0, 0)),
                  pl.BlockSpec((tm, d), lambda i: (i, 0)), vec(g), vec(b)],
        out_specs=pl.BlockSpec((tm, d), lambda i: (i, 0)),
        out_shape=jax.ShapeDtypeStruct((s, d), F32),
        compiler_params=_cp("parallel"),
        name="out_proj_ln",
    )(o_a, o_b, o_c, o_d, w_out[0], x, g[0], b[0])


def _mem_kv_kernel(mem_ref, w_ref, o_ref):
    o_ref[...] = _dot(mem_ref[...].astype(BF16), w_ref[...]).astype(o_ref.dtype)


def _mem_kv(mem, wkv):
    m, d = mem.shape
    n = _wshape(wkv)[1]
    return pl.pallas_call(
        _mem_kv_kernel,
        grid=(1,),
        in_specs=[pl.BlockSpec((m, d), lambda i: (0, 0)), _wspec(wkv, (d, n), lambda i: (0, 0))],
        out_specs=pl.BlockSpec((m, n), lambda i: (0, 0)),
        out_shape=jax.ShapeDtypeStruct((m, n), BF16),
        compiler_params=_cp("arbitrary"),
        name="mem_kv",
    )(mem, wkv[0])


def _mem_attn_kernel(x_ref, wq_ref, kv_ref, wo_ref, g_ref, b_ref, o_ref):
    x = x_ref[...]
    scale = HEAD_DIM ** -0.5
    q = _dot(x.astype(BF16), wq_ref[...]).astype(BF16)
    out = jnp.zeros(x.shape, F32)
    for hd in range(4):
        lo = hd * HEAD_DIM
        s = _dot_nt(q[:, lo:lo + HEAD_DIM], kv_ref[:, lo:lo + HEAD_DIM]) * scale
        p = jnp.exp(s - jnp.max(s, axis=1, keepdims=True))
        o = _dot(p.astype(BF16), kv_ref[:, 512 + lo:512 + lo + HEAD_DIM]) / jnp.sum(p, axis=1, keepdims=True)
        out = out + _dot(o.astype(BF16), wo_ref[lo:lo + HEAD_DIM, :])
    o_ref[...] = _layer_norm(DN_ALPHA * x + out, g_ref[...], b_ref[...])


def _mem_attn_ln(x, wq, kv, wo, g, b):
    s, d = x.shape
    tm = min(ROW_TILE, s)
    full = lambda a: _wspec(a, _wshape(a), lambda i: (0, 0))
    return pl.pallas_call(
        _mem_attn_kernel,
        grid=(s // tm,),
        in_specs=[pl.BlockSpec((tm, d), lambda i: (i, 0)), full(wq), _resident(kv.shape, lambda i: (0, 0)),
                  full(wo), full(g), full(b)],
        out_specs=pl.BlockSpec((tm, d), lambda i: (i, 0)),
        out_shape=jax.ShapeDtypeStruct((s, d), F32),
        compiler_params=_cp("parallel"),
        name="mem_attn_ln",
    )(x, wq[0], kv, wo[0], g[0], b[0])


def _pad_last(w, n):
    return jnp.pad(w, [(0, 0)] * (w.ndim - 1) + [(0, n - w.shape[-1])])


def _split_w_in(w_in):
    a = w_in[..., 0:1536]
    b = _pad_last(w_in[..., 1536:2624], 1152)
    c = _pad_last(w_in[..., 2624:3916], 1408)
    d = jnp.concatenate([w_in[..., 3916:6476], _pad_last(w_in[..., 6476:6540], 128),
                         _pad_last(w_in[..., 6540:6556], 128)], axis=-1)
    return tuple(t.astype(BF16) for t in (a, b, c, d))


def _pad_w_uq(w_uq):
    lead = w_uq.shape[:-1]
    w = _pad_last(w_uq.reshape(lead + (4, MLA_NOPE + MLA_ROPE)), 256)
    return w.reshape(lead + (4 * 256,)).astype(BF16)


def _nsa_importance_map(s):
    n = s // NSA_CMP_STRIDE
    nblk = s // NSA_SEL_BLOCK
    ni = np.arange(n)[:, None]
    bi = np.arange(nblk)[None, :]
    m = ((ni >= 4 * bi - 1) & (ni <= 4 * bi + 3)).astype(np.float32)
    return jnp.asarray(np.pad(m, ((0, 0), (0, 128 - nblk))))


def _mixer(x, tabs, imp_map, w_groups, w_out, g_cq, g_ckv, w_uq, w_ukv, cmp_pe, cmp_w1, cmp_w2, ln_g, ln_b):
    cos128, sin128, cos64, sup64, sdn64 = tabs
    s = x.shape[0]
    w_a, w_b, w_c, w_d = w_groups

    aq, ak, av, akm = _proj_a(x, w_a, cos128, sin128)
    o_a = _moba(aq, ak, av, akm)

    bq, bk, bv = _proj_b(x, w_b, g_cq, g_ckv, w_uq, w_ukv, cos64, sup64, sdn64)
    o_b = _mla(bq, bk, bv)

    cq, cqr, kvcmp, kslc, vslc, kwin, vwin, gates = _proj_c(x, w_c, cos128, sin128)
    kv16 = kvcmp.reshape(2, s // NSA_CMP_STRIDE, NSA_CMP_STRIDE * HEAD_DIM)
    kvc = _nsa_compress(kv16, cmp_pe, cmp_w1, cmp_w2)
    o_cmp, sel = _nsa_cmp(cq, kvc, imp_map)
    o_c = _nsa_slc_win(cqr, sel, kslc, vslc, kwin, vwin, gates, o_cmp)

    dq, dk, dv, diq, dik, diw = _proj_d(x, w_d, cos128, sin128, cos64, sup64, sdn64)
    o_d = _dsa(dq, dk, dv, diq, dik, diw)

    return _out_proj_ln(o_a, o_b, o_c, o_d, w_out, x, ln_g, ln_b)


def kernel(x, mem, positions, ln_g, ln_b, ffn_w_gu, ffn_w_down, w_in, w_out, mla_g_cq, mla_g_ckv, mla_w_uq,
           mla_w_ukv, nsa_cmp_pe, nsa_cmp_w1, nsa_cmp_w2, mem_wq, mem_wkv, mem_wo):
    batch, s, d = x.shape
    w_gu_b, w_dn_b = ffn_w_gu.astype(BF16), ffn_w_down.astype(BF16)
    w_groups = _split_w_in(w_in)
    w_out_b = w_out.astype(BF16)
    w_uq_b, w_ukv_b = _pad_w_uq(mla_w_uq), mla_w_ukv.astype(BF16)
    g_cq, g_ckv = mla_g_cq[:, None, :], mla_g_ckv[:, None, :]
    pe = nsa_cmp_pe.reshape(DEPTH, 2, 1, -1)
    w1_b, w2_b = nsa_cmp_w1.astype(BF16), nsa_cmp_w2.astype(BF16)
    wq_b, wkv_b, wo_b = mem_wq.astype(BF16), mem_wkv.astype(BF16), mem_wo.astype(BF16)
    g4, b4 = ln_g[:, :, None, :], ln_b[:, :, None, :]
    imp_map = _nsa_importance_map(s)

    outs = []
    for bi in range(batch):
        xb = x[bi]
        tabs = _rope_tables(positions[bi])
        for l in range(DEPTH):
            at = lambda a, *lead: (a, (l,) + lead)
            xb = _ffn_ln(xb, at(w_gu_b, 0), at(w_dn_b, 0), at(g4, 0), at(b4, 0))
            xb = _mixer(xb, tabs, imp_map, tuple(at(w) for w in w_groups), at(w_out_b), at(g_cq), at(g_ckv),
                        at(w_uq_b), at(w_ukv_b), at(pe), at(w1_b), at(w2_b), at(g4, 1), at(b4, 1))
            kv = _mem_kv(mem[bi], at(wkv_b))
            xb = _mem_attn_ln(xb, at(wq_b), kv, at(wo_b), at(g4, 2), at(b4, 2))
            xb = _ffn_ln(xb, at(w_gu_b, 1), at(w_dn_b, 1), at(g4, 3), at(b4, 3))
        outs.append(xb)
    return jnp.stack(outs)
```

```python
import functools

import numpy as np
import jax
import jax.numpy as jnp
from jax import lax
from jax.experimental import pallas as pl
from jax.experimental.pallas import tpu as pltpu

F32 = jnp.float32
BF16 = jnp.bfloat16
I32 = jnp.int32

D_MODEL = 2048
DEPTH = 4
HEAD_DIM = 128
ROPE_THETA = 10000.0
LN_EPS = 1e-5
RMS_EPS = 1e-6

MOBA_BLOCK = 256
MOBA_TOPK = 3
MLA_NOPE = 128
MLA_ROPE = 64
NSA_CMP_STRIDE = 16
NSA_SEL_BLOCK = 64
NSA_SEL_TOPK = 16
NSA_WINDOW = 512
DSA_IDX_HEADS = 16
DSA_TOPK = 256
D_FF = 5632
DN_ALPHA = (2 * DEPTH) ** 0.25

NEG = -(2.0 ** 100)
LOG2_E = 1.4426950408889634
INT_MIN = -(2 ** 31)
VMEM_LIMIT = 56 * 1024 * 1024

ROW_TILE = 512
ATT_TQ = 512
ATT_TK = 1024
CAUSAL_TK = 2048
DSA_TQ = 256
DSA_COUNT_ROWS = 128


def _cp(*sem):
    return pltpu.CompilerParams(dimension_semantics=sem, vmem_limit_bytes=VMEM_LIMIT)


def _wspec(w, block, index_map, resident=True):
    _, lead = w
    shape = (None,) * len(lead) + tuple(block)
    imap = lambda *g: tuple(lead) + tuple(index_map(*g))
    if resident:
        return pl.BlockSpec(shape, imap, pipeline_mode=pl.Buffered(1))
    return pl.BlockSpec(shape, imap)


def _wshape(w):
    arr, lead = w
    return arr.shape[len(lead):]


def _resident(shape, index_map):
    return pl.BlockSpec(shape, index_map, pipeline_mode=pl.Buffered(1))


def _dot(a, b):
    return jnp.dot(a, b, preferred_element_type=F32)


def _dot_nt(a, b):
    return lax.dot_general(a, b, (((1,), (1,)), ((), ())), preferred_element_type=F32)


def _layer_norm(y, g, b):
    mu = jnp.mean(y, axis=-1, keepdims=True)
    d = y - mu
    var = jnp.mean(d * d, axis=-1, keepdims=True)
    return d * lax.rsqrt(var + LN_EPS) * g + b


def _rope128(x, cos, sin_signed):
    return x * cos + pltpu.roll(x, 64, 1) * sin_signed


def _rope64(x, cos, s_up, s_dn):
    return x * cos + pltpu.roll(x, 32, 1) * s_up + pltpu.roll(x, 96, 1) * s_dn


def _log2_scale(d):
    return float(d) ** -0.5 * LOG2_E


def _with_ones(v):
    return jnp.concatenate([v, jnp.ones_like(v)], axis=1)


def _flash_update(s, v1, bias, keep, m_ref, acc_ref, slot):
    if bias is not None:
        s = s + bias
    if keep is not None:
        s = jnp.where(keep, s, NEG)
    m_old = m_ref[slot]
    m_new = jnp.maximum(m_old, jnp.max(s, axis=1, keepdims=True))
    alpha = jnp.exp2(m_old - m_new)
    p = jnp.exp2(s - m_new)
    acc_ref[slot] = alpha * acc_ref[slot] + _dot(p.astype(BF16), v1)
    m_ref[slot] = m_new


FLASH_ROWS = 256


def _flash_rows(q, k, v1, bias, keep, m_ref, acc_ref, hd):
    n = max(1, q.shape[0] // FLASH_ROWS)
    r = q.shape[0] // n
    rows = [slice(sp * r, (sp + 1) * r) for sp in range(n)]
    logits = [_dot_nt(q[rw], k) for rw in rows]
    for sp, rw in enumerate(rows):
        _flash_update(logits[sp], v1, None if bias is None else bias[rw], None if keep is None else keep[rw],
                      m_ref, acc_ref, hd * n + sp)


def _flash_heads(qs, ks, v1s, bias, keep, m_ref, acc_ref):
    def logits(hd):
        q = qs[hd]
        n = max(1, q.shape[0] // FLASH_ROWS)
        r = q.shape[0] // n
        rows = [slice(sp * r, (sp + 1) * r) for sp in range(n)]
        return rows, [_dot_nt(q[rw], ks[hd]) for rw in rows]

    ahead = logits(0)
    for hd in range(len(qs)):
        rows, s = ahead
        if hd + 1 < len(qs):
            ahead = logits(hd + 1)
        for sp, rw in enumerate(rows):
            _flash_update(s[sp], v1s[hd], None if bias is None else bias[rw], None if keep is None else keep[rw],
                          m_ref, acc_ref, hd * len(rows) + sp)


def _causal_flash(q, k_ref, v_ref, m_ref, acc_ref, i, tk):
    tq = q.shape[0]
    n_past = (i * tq) // tk

    def tile(j, keep):
        off = pl.multiple_of(j * tk, tk)
        _flash_rows(q, k_ref[pl.ds(off, tk), :], _with_ones(v_ref[pl.ds(off, tk), :]), None, keep, m_ref, acc_ref, 0)

    def body(j, carry):
        tile(j, None)
        return carry

    lax.fori_loop(0, n_past, body, 0)
    row = i * tq + lax.broadcasted_iota(I32, (tq, tk), 0)
    col = n_past * tk + lax.broadcasted_iota(I32, (tq, tk), 1)
    tile(n_past, col <= row)


def _flash_result(acc_ref, hd, heads):
    n = acc_ref.shape[0] // heads
    parts = []
    for sp in range(n):
        a = acc_ref[hd * n + sp]
        parts.append(a[:, 0:HEAD_DIM] / a[:, HEAD_DIM:2 * HEAD_DIM])
    return jnp.concatenate(parts, axis=0) if n > 1 else parts[0]


def _flash_scratch(heads, tq):
    n = max(1, tq // FLASH_ROWS)
    return [pltpu.VMEM((heads * n, tq // n, 1), F32), pltpu.VMEM((heads * n, tq // n, 2 * HEAD_DIM), F32)]


def _flash_init(m_ref, acc_ref):
    m_ref[...] = jnp.full(m_ref.shape, NEG, F32)
    acc_ref[...] = jnp.zeros(acc_ref.shape, F32)


def _top_k_mask(work, index, k, axis=1):
    n = work.shape[axis]
    sel = jnp.zeros(work.shape, F32)
    for _ in range(k):
        m = jnp.max(work, axis=axis, keepdims=True)
        idx = jnp.min(jnp.where(work == m, index, n), axis=axis, keepdims=True)
        pick = index == jnp.where(m > -jnp.inf, idx, n)
        sel = jnp.where(pick, 1.0, sel)
        work = jnp.where(pick, -jnp.inf, work)
    return sel


def _key_block_table(s, block):
    return jnp.asarray((np.arange(128)[None, :] == (np.arange(s) // block)[:, None]) * -NEG, dtype=BF16)


def _rope_table_kernel(pos_ref, c_ref, cos128_ref, sin128_ref, cos64_ref, sup64_ref, sdn64_ref):
    pos = pos_ref[...]
    a128 = pos * c_ref[0:1, :]
    cos128_ref[...] = jnp.cos(a128)
    sin128_ref[...] = jnp.sin(a128) * c_ref[1:2, :]
    a64 = pos * c_ref[2:3, :]
    s64 = jnp.sin(a64)
    cos64_ref[...] = jnp.cos(a64)
    sup64_ref[...] = s64 * c_ref[3:4, :]
    sdn64_ref[...] = s64 * c_ref[4:5, :]


def _rope_consts():
    lane = np.arange(128)
    inv128 = (ROPE_THETA ** (-np.arange(0, 128, 2, dtype=np.float32) / 128)).astype(np.float32)
    inv64 = (ROPE_THETA ** (-np.arange(0, 64, 2, dtype=np.float32) / 64)).astype(np.float32)
    c = np.zeros((8, 128), np.float32)
    c[0] = inv128[lane % 64]
    c[1] = np.where(lane < 64, -1.0, 1.0)
    c[2] = inv64[(lane % 64) % 32]
    c[3] = np.where(lane % 64 >= 32, 1.0, 0.0)
    c[4] = np.where(lane % 64 < 32, -1.0, 0.0)
    return jnp.asarray(c)


def _rope_tables(positions):
    s = positions.shape[0]
    tm = min(ROW_TILE, s)
    pos = positions.astype(F32).reshape(s, 1)
    tab = jax.ShapeDtypeStruct((s, 128), F32)
    row = pl.BlockSpec((tm, 128), lambda i: (i, 0))
    return pl.pallas_call(
        _rope_table_kernel,
        grid=(s // tm,),
        in_specs=[pl.BlockSpec((tm, 1), lambda i: (i, 0)), pl.BlockSpec((8, 128), lambda i: (0, 0))],
        out_specs=[row] * 5,
        out_shape=[tab] * 5,
        compiler_params=_cp("parallel"),
        name="rope_tables",
    )(pos, _rope_consts())


def _ffn_kernel(x_ref, wg_ref, wu_ref, wd_ref, g_ref, b_ref, o_ref, xb_ref, acc_ref):
    j = pl.program_id(1)

    @pl.when(j == 0)
    def _():
        xb_ref[...] = x_ref[...].astype(BF16)
        acc_ref[...] = jnp.zeros_like(acc_ref)

    xb = xb_ref[...]
    g = _dot(xb, wg_ref[...])
    u = _dot(xb, wu_ref[...])
    h = (g * jax.nn.sigmoid(g) * u).astype(BF16)
    acc_ref[...] += _dot(h, wd_ref[...])

    @pl.when(j == pl.num_programs(1) - 1)
    def _():
        y = DN_ALPHA * x_ref[...] + 0.5 * acc_ref[...]
        o_ref[...] = _layer_norm(y, g_ref[...], b_ref[...])


def _ffn_ln(x, w_gu, w_down, g, b):
    s, d = x.shape
    f = _wshape(w_down)[0]
    tm = min(ROW_TILE, s)
    tf = 512
    nf = f // tf
    return pl.pallas_call(
        _ffn_kernel,
        grid=(s // tm, nf),
        in_specs=[
            pl.BlockSpec((tm, d), lambda i, j: (i, 0)),
            _wspec(w_gu, (d, tf), lambda i, j: (0, j), resident=False),
            _wspec(w_gu, (d, tf), lambda i, j: (0, j + nf), resident=False),
            _wspec(w_down, (tf, d), lambda i, j: (j, 0), resident=False),
            _wspec(g, (1, d), lambda i, j: (0, 0)),
            _wspec(b, (1, d), lambda i, j: (0, 0)),
        ],
        out_specs=pl.BlockSpec((tm, d), lambda i, j: (i, 0)),
        out_shape=jax.ShapeDtypeStruct((s, d), F32),
        scratch_shapes=[pltpu.VMEM((tm, d), BF16), pltpu.VMEM((tm, d), F32)],
        compiler_params=_cp("parallel", "arbitrary"),
        name="ffn_ln",
    )(x, w_gu[0], w_gu[0], w_down[0], g[0], b[0])


def _proj_a_kernel(x_ref, w_ref, cos_ref, sin_ref, kb_ref, q_ref, k_ref, v_ref, km_ref):
    h = _dot(x_ref[...].astype(BF16), w_ref[...])
    cos, sin = cos_ref[...], sin_ref[...]
    nblk = km_ref.shape[0]
    for hd in range(4):
        lo = hd * HEAD_DIM
        q_ref[:, lo:lo + HEAD_DIM] = _rope128(h[:, lo:lo + HEAD_DIM], cos, sin)
        k = _rope128(h[:, 512 + lo:512 + lo + HEAD_DIM], cos, sin)
        k_ref[:, 2 * lo:2 * lo + HEAD_DIM] = k.astype(BF16)
        k_ref[:, 2 * lo + HEAD_DIM:2 * lo + 2 * HEAD_DIM] = kb_ref[...]
        for bi in range(nblk):
            kb = k[bi * MOBA_BLOCK:(bi + 1) * MOBA_BLOCK]
            km_ref[bi, :, lo:lo + HEAD_DIM] = jnp.mean(kb, axis=0, keepdims=True)
    v_ref[...] = h[:, 1024:1536].astype(BF16)


def _proj_a(x, w, cos128, sin128):
    s, d = x.shape
    tm = min(ROW_TILE, s)
    nblk = tm // MOBA_BLOCK
    row512 = pl.BlockSpec((tm, 512), lambda i: (i, 0))
    tab = pl.BlockSpec((tm, 128), lambda i: (i, 0))
    return pl.pallas_call(
        _proj_a_kernel,
        grid=(s // tm,),
        in_specs=[pl.BlockSpec((tm, d), lambda i: (i, 0)), _wspec(w, _wshape(w), lambda i: (0, 0)), tab, tab, tab],
        out_specs=[row512, pl.BlockSpec((tm, 1024), lambda i: (i, 0)), row512,
                   pl.BlockSpec((nblk, 1, 512), lambda i: (i, 0, 0))],
        out_shape=[
            jax.ShapeDtypeStruct((s, 512), F32),
            jax.ShapeDtypeStruct((s, 1024), BF16),
            jax.ShapeDtypeStruct((s, 512), BF16),
            jax.ShapeDtypeStruct((s // MOBA_BLOCK, 1, 512), F32),
        ],
        compiler_params=_cp("parallel"),
        name="proj_moba",
    )(x, w[0], cos128, sin128, _key_block_table(s, MOBA_BLOCK))


def _rms_norm(x, g):
    return x * lax.rsqrt(jnp.mean(x * x, axis=-1, keepdims=True) + RMS_EPS) * g


def _proj_b_kernel(x_ref, w_ref, gq_ref, gkv_ref, wuq_ref, wukv_ref, cos_ref, sup_ref, sdn_ref,
                   q_ref, k_ref, v_ref):
    h = _dot(x_ref[...].astype(BF16), w_ref[...])
    cos, sup, sdn = cos_ref[...], sup_ref[...], sdn_ref[...]
    cq = _rms_norm(h[:, 0:512], gq_ref[...]).astype(BF16)
    ckv = _rms_norm(h[:, 512:1024], gkv_ref[...]).astype(BF16)
    kr = _rope64(h[:, 1024:1152], cos, sup, sdn).astype(BF16)
    q = _dot(cq, wuq_ref[...])
    kv = _dot(ckv, wukv_ref[...])
    qs = _log2_scale(MLA_NOPE + MLA_ROPE)
    for hd in range(4):
        lo = hd * 256
        q_ref[:, lo:lo + 128] = (q[:, lo:lo + 128] * qs).astype(BF16)
        q_ref[:, lo + 128:lo + 256] = (_rope64(q[:, lo + 128:lo + 256], cos, sup, sdn) * qs).astype(BF16)
        k_ref[:, lo:lo + 128] = kv[:, lo:lo + 128].astype(BF16)
        k_ref[:, lo + 128:lo + 256] = kr
        v_ref[:, hd * 128:(hd + 1) * 128] = kv[:, lo + 128:lo + 256].astype(BF16)


def _proj_b(x, w, g_cq, g_ckv, w_uq, w_ukv, cos64, sup64, sdn64):
    s, d = x.shape
    tm = min(ROW_TILE, s)
    tab = pl.BlockSpec((tm, 128), lambda i: (i, 0))
    full = lambda a: _wspec(a, _wshape(a), lambda i: (0, 0))
    return pl.pallas_call(
        _proj_b_kernel,
        grid=(s // tm,),
        in_specs=[pl.BlockSpec((tm, d), lambda i: (i, 0)), full(w), full(g_cq), full(g_ckv), full(w_uq),
                  full(w_ukv), tab, tab, tab],
        out_specs=[pl.BlockSpec((tm, 1024), lambda i: (i, 0)), pl.BlockSpec((tm, 1024), lambda i: (i, 0)),
                   pl.BlockSpec((tm, 512), lambda i: (i, 0))],
        out_shape=[jax.ShapeDtypeStruct((s, 1024), BF16), jax.ShapeDtypeStruct((s, 1024), BF16),
                   jax.ShapeDtypeStruct((s, 512), BF16)],
        compiler_params=_cp("parallel"),
        name="proj_mla",
    )(x, w[0], g_cq[0], g_ckv[0], w_uq[0], w_ukv[0], cos64, sup64, sdn64)


def _proj_c_kernel(x_ref, w_ref, cos_ref, sin_ref, kb_ref, q_ref, qr_ref, kvcmp_ref, kslc_ref, vslc_ref,
                   kwin_ref, vwin_ref, gate_ref):
    h = _dot(x_ref[...].astype(BF16), w_ref[...])
    cos, sin = cos_ref[...], sin_ref[...]
    for hd in range(4):
        lo = hd * HEAD_DIM
        qh = h[:, lo:lo + HEAD_DIM]
        q_ref[:, lo:lo + HEAD_DIM] = qh.astype(BF16)
        qr_ref[:, lo:lo + HEAD_DIM] = (_rope128(qh, cos, sin) * _log2_scale(HEAD_DIM)).astype(BF16)
    kv = lambda i: h[:, 512 + i * 128:512 + (i + 1) * 128]
    kvcmp_ref[0] = kv(0).astype(BF16)
    kvcmp_ref[1] = kv(1).astype(BF16)
    kslc_ref[:, 0:HEAD_DIM] = _rope128(kv(2), cos, sin).astype(BF16)
    kslc_ref[:, HEAD_DIM:2 * HEAD_DIM] = kb_ref[...]
    vslc_ref[...] = kv(3).astype(BF16)
    kwin_ref[...] = _rope128(kv(4), cos, sin).astype(BF16)
    vwin_ref[...] = kv(5).astype(BF16)
    gate_ref[...] = jax.nn.sigmoid(h[:, 1280:1408])


def _proj_c(x, w, cos128, sin128):
    s, d = x.shape
    tm = min(ROW_TILE, s)
    r512 = pl.BlockSpec((tm, 512), lambda i: (i, 0))
    r128 = pl.BlockSpec((tm, 128), lambda i: (i, 0))
    b512 = jax.ShapeDtypeStruct((s, 512), BF16)
    b128 = jax.ShapeDtypeStruct((s, 128), BF16)
    return pl.pallas_call(
        _proj_c_kernel,
        grid=(s // tm,),
        in_specs=[pl.BlockSpec((tm, d), lambda i: (i, 0)), _wspec(w, _wshape(w), lambda i: (0, 0)), r128, r128, r128],
        out_specs=[r512, r512, pl.BlockSpec((2, tm, 128), lambda i: (0, i, 0)),
                   pl.BlockSpec((tm, 256), lambda i: (i, 0))] + [r128] * 4,
        out_shape=[b512, b512, jax.ShapeDtypeStruct((2, s, 128), BF16), jax.ShapeDtypeStruct((s, 256), BF16)]
                  + [b128] * 3 + [jax.ShapeDtypeStruct((s, 128), F32)],
        compiler_params=_cp("parallel"),
        name="proj_nsa",
    )(x, w[0], cos128, sin128, _key_block_table(s, NSA_SEL_BLOCK))


def _proj_d_kernel(x_ref, w_ref, cos_ref, sin_ref, c64_ref, sup_ref, sdn_ref,
                   q_ref, k_ref, v_ref, iq_ref, ik_ref, iw_ref):
    h = _dot(x_ref[...].astype(BF16), w_ref[...])
    cos, sin = cos_ref[...], sin_ref[...]
    c64, sup, sdn = c64_ref[...], sup_ref[...], sdn_ref[...]
    for hd in range(4):
        lo = hd * HEAD_DIM
        q_ref[:, lo:lo + HEAD_DIM] = (_rope128(h[:, lo:lo + HEAD_DIM], cos, sin) * _log2_scale(HEAD_DIM)).astype(BF16)
        k_ref[:, lo:lo + HEAD_DIM] = _rope128(h[:, 512 + lo:512 + lo + HEAD_DIM], cos, sin).astype(BF16)
    v_ref[...] = h[:, 1024:1536].astype(BF16)
    for p in range(8):
        lo = 1536 + p * 128
        iq_ref[:, p * 128:(p + 1) * 128] = _rope64(h[:, lo:lo + 128], c64, sup, sdn).astype(BF16)
    ik = _rope64(h[:, 2560:2688], c64, sup, sdn)
    ik_ref[:, 0:128] = ik.astype(BF16)
    ik_ref[:, 128:256] = pltpu.roll(ik, 64, 1).astype(BF16)
    iw_ref[...] = h[:, 2688:2816] * (1.0 / 32.0)


def _proj_d(x, w, cos128, sin128, cos64, sup64, sdn64):
    s, d = x.shape
    tm = min(ROW_TILE, s)
    r = lambda n: pl.BlockSpec((tm, n), lambda i: (i, 0))
    return pl.pallas_call(
        _proj_d_kernel,
        grid=(s // tm,),
        in_specs=[pl.BlockSpec((tm, d), lambda i: (i, 0)), _wspec(w, _wshape(w), lambda i: (0, 0))] + [r(128)] * 5,
        out_specs=[r(512), r(512), r(512), r(1024), r(256), r(128)],
        out_shape=[jax.ShapeDtypeStruct((s, 512), BF16)] * 3 + [
            jax.ShapeDtypeStruct((s, 1024), BF16), jax.ShapeDtypeStruct((s, 256), BF16),
            jax.ShapeDtypeStruct((s, 128), F32)],
        compiler_params=_cp("parallel"),
        name="proj_dsa",
    )(x, w[0], cos128, sin128, cos64, sup64, sdn64)


def _moba_kernel(q_ref, k_ref, v_ref, km_ref, o_ref, m_ref, acc_ref, *, tk, nb):
    i = pl.program_id(1)
    tq = q_ref.shape[0]
    q32 = q_ref[...]

    gate = lax.dot_general(q32, km_ref[...], (((1,), (1,)), ((), ())),
                           precision=lax.Precision.HIGHEST, preferred_element_type=F32)
    blk = lax.broadcasted_iota(I32, (tq, 128), 1)
    cur = (i * tq + lax.broadcasted_iota(I32, (tq, 128), 0)) // MOBA_BLOCK
    gate = jnp.where(blk < cur, gate, -jnp.inf)
    sel = _top_k_mask(gate, blk, min(MOBA_TOPK, nb - 1))
    unchosen = jnp.where(blk == cur, 1.0, sel) - 1.0
    qa = jnp.concatenate([q32 * _log2_scale(HEAD_DIM), unchosen], axis=1).astype(BF16)

    _flash_init(m_ref, acc_ref)
    _causal_flash(qa, k_ref, v_ref, m_ref, acc_ref, i, tk)
    o_ref[...] = _flash_result(acc_ref, 0, 1).astype(o_ref.dtype)


def _moba(q32, k, v, kmean):
    s = q32.shape[0]
    nb = s // MOBA_BLOCK
    tq = min(ATT_TQ, s)
    tk = min(CAUSAL_TK, s)
    km = jnp.pad(kmean.reshape(nb, 512), ((0, 128 - nb), (0, 0)))
    return pl.pallas_call(
        functools.partial(_moba_kernel, tk=tk, nb=nb),
        grid=(4, s // tq),
        in_specs=[
            pl.BlockSpec((tq, HEAD_DIM), lambda h, i: (i, h)),
            pl.BlockSpec((s, 2 * HEAD_DIM), lambda h, i: (0, h)),
            pl.BlockSpec((s, HEAD_DIM), lambda h, i: (0, h)),
            pl.BlockSpec((128, HEAD_DIM), lambda h, i: (0, h)),
        ],
        out_specs=pl.BlockSpec((tq, HEAD_DIM), lambda h, i: (i, h)),
        out_shape=jax.ShapeDtypeStruct((s, 512), BF16),
        scratch_shapes=_flash_scratch(1, tq),
        compiler_params=_cp("parallel", "arbitrary"),
        name="moba_attn",
    )(q32, k, v, km)


def _mla_kernel(q_ref, k_ref, v_ref, o_ref, m_ref, acc_ref, *, tk):
    _flash_init(m_ref, acc_ref)
    _causal_flash(q_ref[...], k_ref, v_ref, m_ref, acc_ref, pl.program_id(1), tk)
    o_ref[...] = _flash_result(acc_ref, 0, 1).astype(o_ref.dtype)


def _mla(qcat, kcat, v):
    s = qcat.shape[0]
    tq = min(ATT_TQ, s)
    tk = min(CAUSAL_TK, s)
    return pl.pallas_call(
        functools.partial(_mla_kernel, tk=tk),
        grid=(4, s // tq),
        in_specs=[
            pl.BlockSpec((tq, 256), lambda h, i: (i, h)),
            pl.BlockSpec((s, 256), lambda h, i: (0, h)),
            pl.BlockSpec((s, HEAD_DIM), lambda h, i: (0, h)),
        ],
        out_specs=pl.BlockSpec((tq, HEAD_DIM), lambda h, i: (i, h)),
        out_shape=jax.ShapeDtypeStruct((s, 512), BF16),
        scratch_shapes=_flash_scratch(1, tq),
        compiler_params=_cp("parallel", "arbitrary"),
        name="mla_attn",
    )(qcat, kcat, v)


def _gelu_tanh(x):
    return 0.5 * x * (1.0 + jnp.tanh(0.7978845608028654 * (x + 0.044715 * x * x * x)))


def _nsa_compress_kernel(t_ref, pe_ref, w1_ref, w2_ref, o_ref):
    t = t_ref[0]
    w1 = w1_ref[...]
    half = t.shape[1]
    n = t.shape[0]
    lo = _dot(t, w1[0:half])
    hi = _dot(t, w1[half:2 * half])
    pe = _dot(pe_ref[...].astype(BF16), w1)
    pre = lo + pltpu.roll(hi, n - 1, 0) + pe
    o_ref[0] = _dot(_gelu_tanh(pre).astype(BF16), w2_ref[...]).astype(o_ref.dtype)


def _nsa_compress(kv16, pe, w1, w2):
    n = kv16.shape[1]
    sub = lambda a: _wspec(a, (None,) + _wshape(a)[1:], lambda i: (i, 0, 0), resident=False)
    return pl.pallas_call(
        _nsa_compress_kernel,
        grid=(2,),
        in_specs=[pl.BlockSpec((1,) + kv16.shape[1:], lambda i: (i, 0, 0)), sub(pe), sub(w1), sub(w2)],
        out_specs=pl.BlockSpec((1, n, HEAD_DIM), lambda i: (i, 0, 0)),
        out_shape=jax.ShapeDtypeStruct((2, n, HEAD_DIM), BF16),
        compiler_params=_cp("parallel"),
        name="nsa_compress",
    )(kv16, pe[0], w1[0], w2[0])


def _nsa_cmp_kernel(q_ref, kvc_ref, m_ref, o_ref, sel_ref, *, n_sel):
    i = pl.program_id(0)
    tq = q_ref.shape[0]
    n = kvc_ref.shape[1]
    nblk = m_ref.shape[1]
    scale = HEAD_DIM ** -0.5
    kc, vc = kvc_ref[0], kvc_ref[1]
    t_pos = i * tq + lax.broadcasted_iota(I32, (tq, n), 0)
    cmp_end = lax.broadcasted_iota(I32, (tq, n), 1) * NSA_CMP_STRIDE + (2 * NSA_CMP_STRIDE - 1)
    ok = cmp_end <= t_pos
    p_sum = jnp.zeros((tq, n), F32)
    for hd in range(4):
        lo = hd * HEAD_DIM
        s = jnp.where(ok, _dot_nt(q_ref[:, lo:lo + HEAD_DIM], kc) * scale, NEG)
        m = jnp.max(s, axis=1, keepdims=True)
        e = jnp.where(ok, jnp.exp(s - m), 0.0)
        l = jnp.sum(e, axis=1, keepdims=True)
        p = e / jnp.where(l > 0, l, 1.0)
        p_sum = p_sum + p
        o_ref[:, lo:lo + HEAD_DIM] = _dot(p.astype(BF16), vc)

    imp = jnp.dot(p_sum, m_ref[...], precision=lax.Precision.HIGHEST, preferred_element_type=F32)
    imp_t = imp.T
    blk = lax.broadcasted_iota(I32, (nblk, tq), 0)
    cur = (i * tq + lax.broadcasted_iota(I32, (nblk, tq), 1)) // NSA_SEL_BLOCK
    forced = (blk == 0) | (blk == cur) | (blk == cur - 1)
    imp_t = jnp.where(blk > cur, -jnp.inf, jnp.where(forced, jnp.inf, imp_t))
    sel_ref[...] = (_top_k_mask(imp_t, blk, n_sel, axis=0).T - 1.0).astype(sel_ref.dtype)


def _nsa_cmp(q, kvc, imp_map):
    s = q.shape[0]
    n = kvc.shape[1]
    nblk = imp_map.shape[1]
    tq = min(256, s)
    return pl.pallas_call(
        functools.partial(_nsa_cmp_kernel, n_sel=min(NSA_SEL_TOPK, s // NSA_SEL_BLOCK)),
        grid=(s // tq,),
        in_specs=[pl.BlockSpec((tq, 512), lambda i: (i, 0)),
                  _resident((2, n, HEAD_DIM), lambda i: (0, 0, 0)),
                  _resident((n, nblk), lambda i: (0, 0))],
        out_specs=[pl.BlockSpec((tq, 512), lambda i: (i, 0)), pl.BlockSpec((tq, nblk), lambda i: (i, 0))],
        out_shape=[jax.ShapeDtypeStruct((s, 512), F32), jax.ShapeDtypeStruct((s, nblk), BF16)],
        compiler_params=_cp("parallel"),
        name="nsa_cmp_select",
    )(q, kvc, imp_map)


def _nsa_slc_win_kernel(q_ref, sel_ref, ks_ref, vs_ref, kw_ref, vw_ref, g_ref, oc_ref, o_ref,
                        m_ref, acc_ref, *, tk):
    i = pl.program_id(0)
    tq = q_ref.shape[0]
    _flash_init(m_ref, acc_ref)

    def tile(j, causal):
        off = pl.multiple_of(j * tk, tk)
        keep = None
        if causal:
            row = i * tq + lax.broadcasted_iota(I32, (tq, tk), 0)
            col = off + lax.broadcasted_iota(I32, (tq, tk), 1)
            keep = col <= row
        k = ks_ref[pl.ds(off, tk), 0:HEAD_DIM]
        v1 = _with_ones(vs_ref[pl.ds(off, tk), :])
        bias = _dot_nt(sel_ref[...], ks_ref[pl.ds(off, tk), HEAD_DIM:2 * HEAD_DIM])
        _flash_heads([q_ref[:, hd * HEAD_DIM:(hd + 1) * HEAD_DIM] for hd in range(4)], [k] * 4, [v1] * 4,
                     bias, keep, m_ref, acc_ref)

    n_past = (i * tq) // tk

    def body(j, carry):
        tile(j, False)
        return carry

    lax.fori_loop(0, n_past, body, 0)
    tile(n_past, True)

    wlen = NSA_WINDOW + tq
    start = pl.multiple_of(jnp.maximum(i * tq - NSA_WINDOW, 0), 256)
    kw = kw_ref[pl.ds(start, wlen), :]
    vw1 = _with_ones(vw_ref[pl.ds(start, wlen), :])
    diff = (i * tq + lax.broadcasted_iota(I32, (tq, wlen), 0)) - (start + lax.broadcasted_iota(I32, (tq, wlen), 1))
    in_win = (diff >= 0) & (diff < NSA_WINDOW)
    g = g_ref[...]
    for hd in range(4):
        lo = hd * HEAD_DIM
        s = jnp.where(in_win, _dot_nt(q_ref[:, lo:lo + HEAD_DIM], kw), NEG)
        p = jnp.exp2(s - jnp.max(s, axis=1, keepdims=True))
        pv = _dot(p.astype(BF16), vw1)
        o_win = pv[:, 0:HEAD_DIM] / pv[:, HEAD_DIM:2 * HEAD_DIM]
        o_slc = _flash_result(acc_ref, hd, 4)
        out = (g[:, 3 * hd:3 * hd + 1] * oc_ref[:, lo:lo + HEAD_DIM] + g[:, 3 * hd + 1:3 * hd + 2] * o_slc
               + g[:, 3 * hd + 2:3 * hd + 3] * o_win)
        o_ref[:, lo:lo + HEAD_DIM] = out.astype(o_ref.dtype)


def _nsa_slc_win(q_r, sel, kslc, vslc, kwin, vwin, gates, o_cmp):
    s = q_r.shape[0]
    tq = min(ATT_TQ, s)
    tk = min(ATT_TK, s)
    full = lambda a: _resident(a.shape, lambda i: (0,) * a.ndim)
    return pl.pallas_call(
        functools.partial(_nsa_slc_win_kernel, tk=tk),
        grid=(s // tq,),
        in_specs=[pl.BlockSpec((tq, 512), lambda i: (i, 0)), pl.BlockSpec((tq, 128), lambda i: (i, 0)),
                  full(kslc), full(vslc), full(kwin), full(vwin),
                  pl.BlockSpec((tq, 128), lambda i: (i, 0)), pl.BlockSpec((tq, 512), lambda i: (i, 0))],
        out_specs=pl.BlockSpec((tq, 512), lambda i: (i, 0)),
        out_shape=jax.ShapeDtypeStruct((s, 512), BF16),
        scratch_shapes=_flash_scratch(4, tq),
        compiler_params=_cp("parallel"),
        name="nsa_slc_win",
    )(q_r, sel, kslc, vslc, kwin, vwin, gates, o_cmp)


def _dsa_kernel(q_ref, iq_ref, iw_ref, k_ref, v_ref, ik_ref, o_ref, key_ref, wb_ref, m_ref, acc_ref, *, tk):
    i = pl.program_id(0)
    tq = q_ref.shape[0]
    half = tk // 2
    n_tiles = (i * tq + tq + tk - 1) // tk
    row_h = lax.broadcasted_iota(I32, (tq, half), 0) + i * tq
    col_h = lax.broadcasted_iota(I32, (tq, half), 1)

    w = iw_ref[...]
    for hd in range(DSA_IDX_HEADS):
        wb_ref[hd] = jnp.broadcast_to(w[:, hd:hd + 1], (tq, 128))

    def score_body(j, carry):
        for hf in range(2):
            off = pl.multiple_of(j * tk + hf * half, half)
            ik_even = ik_ref[pl.ds(off, half), 0:128]
            ik_odd = ik_ref[pl.ds(off, half), 128:256]
            sc = jnp.zeros((tq, half), F32)
            for p in range(DSA_IDX_HEADS // 2):
                x = iq_ref[:, p * 128:(p + 1) * 128]
                we = jnp.concatenate([wb_ref[2 * p]] * (half // 128), axis=1)
                wo = jnp.concatenate([wb_ref[2 * p + 1]] * (half // 128), axis=1)
                sc = sc + jnp.maximum(_dot_nt(x, ik_even), 0.0) * we
                sc = sc + jnp.maximum(_dot_nt(x, ik_odd), 0.0) * wo
            bits = pltpu.bitcast(sc, I32)
            key = jnp.where(bits >= 0, bits, bits ^ 0x7FFFFFFF)
            key_ref[j, :, hf * half:(hf + 1) * half] = jnp.where(col_h + off <= row_h, key, INT_MIN)
        return carry

    lax.fori_loop(0, n_tiles, score_body, 0)

    rb = min(DSA_COUNT_ROWS, tq)

    ones = jnp.ones((128, 128), BF16)
    k_f = float(DSA_TOPK)

    def count_ge(cand128):
        parts = []
        for r0 in range(0, tq, rb):
            cand_b = cand128[r0:r0 + rb]

            def body(j, acc):
                for c in range(tk // 128):
                    acc = acc + jnp.where(key_ref[j, r0:r0 + rb, c * 128:(c + 1) * 128] >= cand_b, 1, 0)
                return acc

            parts.append(lax.fori_loop(0, n_tiles, body, jnp.zeros((rb, 128), I32)))
        per_lane = jnp.concatenate(parts, axis=0).astype(F32).astype(BF16)
        return _dot(per_lane, ones)

    zero = jnp.zeros((tq, 128), I32)
    c0 = count_ge(zero)
    lo0 = jnp.where(c0 >= k_f, zero, INT_MIN)
    c_lo0 = jnp.where(c0 >= k_f, c0, -1.0)

    def bit_cond(state):
        b, _, _, more = state
        return (b < 31) & (more > 0.0)

    def bit_body(state):
        b, lo, c_lo, _ = state
        more = jnp.max(jnp.where(c_lo != k_f, 1.0, 0.0))
        cand = lo | lax.shift_left(jnp.int32(1), 30 - b)
        c = count_ge(cand)
        take = c >= k_f
        return b + 1, jnp.where(take, cand, lo), jnp.where(take, c, c_lo), more

    _, lo, _, _ = lax.while_loop(bit_cond, bit_body, (jnp.int32(0), lo0, c_lo0, jnp.float32(1.0)))
    thr128 = jnp.maximum(lo, INT_MIN + 1)

    _flash_init(m_ref, acc_ref)

    def attn_body(j, carry):
        off = pl.multiple_of(j * tk, tk)
        keep = key_ref[j] >= jnp.concatenate([thr128] * (tk // 128), axis=1)
        cols = [slice(hd * HEAD_DIM, (hd + 1) * HEAD_DIM) for hd in range(4)]
        _flash_heads([q_ref[:, c] for c in cols], [k_ref[pl.ds(off, tk), c] for c in cols],
                     [_with_ones(v_ref[pl.ds(off, tk), c]) for c in cols], None, keep, m_ref, acc_ref)
        return carry

    lax.fori_loop(0, n_tiles, attn_body, 0)
    for hd in range(4):
        o_ref[:, hd * HEAD_DIM:(hd + 1) * HEAD_DIM] = _flash_result(acc_ref, hd, 4).astype(o_ref.dtype)


def _dsa(q, k, v, iq, ik2, iw):
    s = q.shape[0]
    tq = min(DSA_TQ, s)
    tk = min(ATT_TK, s)
    full = lambda a: _resident(a.shape, lambda i: (0, 0))
    return pl.pallas_call(
        functools.partial(_dsa_kernel, tk=tk),
        grid=(s // tq,),
        in_specs=[pl.BlockSpec((tq, 512), lambda i: (i, 0)), pl.BlockSpec((tq, 1024), lambda i: (i, 0)),
                  pl.BlockSpec((tq, 128), lambda i: (i, 0)), full(k), full(v), full(ik2)],
        out_specs=pl.BlockSpec((tq, 512), lambda i: (i, 0)),
        out_shape=jax.ShapeDtypeStruct((s, 512), BF16),
        scratch_shapes=[pltpu.VMEM((s // tk, tq, tk), I32), pltpu.VMEM((DSA_IDX_HEADS, tq, 128), F32)]
                       + _flash_scratch(4, tq),
        compiler_params=_cp("parallel"),
        name="dsa_attn",
    )(q, iq, iw, k, v, ik2)


def _out_proj_kernel(oa_ref, ob_ref, oc_ref, od_ref, w_ref, x_ref, g_ref, b_ref, o_ref):
    mix = _dot(oa_ref[...], w_ref[0:512, :])
    mix = mix + _dot(ob_ref[...], w_ref[512:1024, :])
    mix = mix + _dot(oc_ref[...], w_ref[1024:1536, :])
    mix = mix + _dot(od_ref[...], w_ref[1536:2048, :])
    o_ref[...] = _layer_norm(DN_ALPHA * x_ref[...] + mix, g_ref[...], b_ref[...])


def _out_proj_ln(o_a, o_b, o_c, o_d, w_out, x, g, b):
    s, d = x.shape
    tm = min(ROW_TILE, s)
    r512 = pl.BlockSpec((tm, 512), lambda i: (i, 0))
    vec = lambda a: _wspec(a, (1, d), lambda i: (0, 0))
    return pl.pallas_call(
        _out_proj_kernel,
        grid=(s // tm,),
        in_specs=[r512, r512, r512, r512, _wspec(w_out, _wshape(w_out), lambda i: (0, 0)),
                  pl.BlockSpec((tm, d), lambda i: (i, 0)), vec(g), vec(b)],
        out_specs=pl.BlockSpec((tm, d), lambda i: (i, 0)),
        out_shape=jax.ShapeDtypeStruct((s, d), F32),
        compiler_params=_cp("parallel"),
        name="out_proj_ln",
    )(o_a, o_b, o_c, o_d, w_out[0], x, g[0], b[0])


def _mem_kv_kernel(mem_ref, w_ref, o_ref):
    o_ref[...] = _dot(mem_ref[...].astype(BF16), w_ref[...]).astype(o_ref.dtype)


def _mem_kv(mem, wkv):
    m, d = mem.shape
    n = _wshape(wkv)[1]
    return pl.pallas_call(
        _mem_kv_kernel,
        grid=(1,),
        in_specs=[pl.BlockSpec((m, d), lambda i: (0, 0)), _wspec(wkv, (d, n), lambda i: (0, 0))],
        out_specs=pl.BlockSpec((m, n), lambda i: (0, 0)),
        out_shape=jax.ShapeDtypeStruct((m, n), BF16),
        compiler_params=_cp("arbitrary"),
        name="mem_kv",
    )(mem, wkv[0])


def _mem_attn_kernel(x_ref, wq_ref, kv_ref, wo_ref, g_ref, b_ref, o_ref):
    x = x_ref[...]
    scale = HEAD_DIM ** -0.5
    q = _dot(x.astype(BF16), wq_ref[...]).astype(BF16)
    out = jnp.zeros(x.shape, F32)
    for hd in range(4):
        lo = hd * HEAD_DIM
        s = _dot_nt(q[:, lo:lo + HEAD_DIM], kv_ref[:, lo:lo + HEAD_DIM]) * scale
        p = jnp.exp(s - jnp.max(s, axis=1, keepdims=True))
        o = _dot(p.astype(BF16), kv_ref[:, 512 + lo:512 + lo + HEAD_DIM]) / jnp.sum(p, axis=1, keepdims=True)
        out = out + _dot(o.astype(BF16), wo_ref[lo:lo + HEAD_DIM, :])
    o_ref[...] = _layer_norm(DN_ALPHA * x + out, g_ref[...], b_ref[...])


def _mem_attn_ln(x, wq, kv, wo, g, b):
    s, d = x.shape
    tm = min(ROW_TILE, s)
    full = lambda a: _wspec(a, _wshape(a), lambda i: (0, 0))
    return pl.pallas_call(
        _mem_attn_kernel,
        grid=(s // tm,),
        in_specs=[pl.BlockSpec((tm, d), lambda i: (i, 0)), full(wq), _resident(kv.shape, lambda i: (0, 0)),
                  full(wo), full(g), full(b)],
        out_specs=pl.BlockSpec((tm, d), lambda i: (i, 0)),
        out_shape=jax.ShapeDtypeStruct((s, d), F32),
        compiler_params=_cp("parallel"),
        name="mem_attn_ln",
    )(x, wq[0], kv, wo[0], g[0], b[0])


def _pad_last(w, n):
    return jnp.pad(w, [(0, 0)] * (w.ndim - 1) + [(0, n - w.shape[-1])])


def _split_w_in(w_in):
    a = w_in[..., 0:1536]
    b = _pad_last(w_in[..., 1536:2624], 1152)
    c = _pad_last(w_in[..., 2624:3916], 1408)
    d = jnp.concatenate([w_in[..., 3916:6476], _pad_last(w_in[..., 6476:6540], 128),
                         _pad_last(w_in[..., 6540:6556], 128)], axis=-1)
    return tuple(t.astype(BF16) for t in (a, b, c, d))


def _pad_w_uq(w_uq):
    lead = w_uq.shape[:-1]
    w = _pad_last(w_uq.reshape(lead + (4, MLA_NOPE + MLA_ROPE)), 256)
    return w.reshape(lead + (4 * 256,)).astype(BF16)


def _nsa_importance_map(s):
    n = s // NSA_CMP_STRIDE
    nblk = s // NSA_SEL_BLOCK
    ni = np.arange(n)[:, None]
    bi = np.arange(nblk)[None, :]
    m = ((ni >= 4 * bi - 1) & (ni <= 4 * bi + 3)).astype(np.float32)
    return jnp.asarray(np.pad(m, ((0, 0), (0, 128 - nblk))))


def _mixer(x, tabs, imp_map, w_groups, w_out, g_cq, g_ckv, w_uq, w_ukv, cmp_pe, cmp_w1, cmp_w2, ln_g, ln_b):
    cos128, sin128, cos64, sup64, sdn64 = tabs
    s = x.shape[0]
    w_a, w_b, w_c, w_d = w_groups

    aq, ak, av, akm = _proj_a(x, w_a, cos128, sin128)
    o_a = _moba(aq, ak, av, akm)

    bq, bk, bv = _proj_b(x, w_b, g_cq, g_ckv, w_uq, w_ukv, cos64, sup64, sdn64)
    o_b = _mla(bq, bk, bv)

    cq, cqr, kvcmp, kslc, vslc, kwin, vwin, gates = _proj_c(x, w_c, cos128, sin128)
    kv16 = kvcmp.reshape(2, s // NSA_CMP_STRIDE, NSA_CMP_STRIDE * HEAD_DIM)
    kvc = _nsa_compress(kv16, cmp_pe, cmp_w1, cmp_w2)
    o_cmp, sel = _nsa_cmp(cq, kvc, imp_map)
    o_c = _nsa_slc_win(cqr, sel, kslc, vslc, kwin, vwin, gates, o_cmp)

    dq, dk, dv, diq, dik, diw = _proj_d(x, w_d, cos128, sin128, cos64, sup64, sdn64)
    o_d = _dsa(dq, dk, dv, diq, dik, diw)

    return _out_proj_ln(o_a, o_b, o_c, o_d, w_out, x, ln_g, ln_b)


def kernel(x, mem, positions, ln_g, ln_b, ffn_w_gu, ffn_w_down, w_in, w_out, mla_g_cq, mla_g_ckv, mla_w_uq,
           mla_w_ukv, nsa_cmp_pe, nsa_cmp_w1, nsa_cmp_w2, mem_wq, mem_wkv, mem_wo):
    batch, s, d = x.shape
    w_gu_b, w_dn_b = ffn_w_gu.astype(BF16), ffn_w_down.astype(BF16)
    w_groups = _split_w_in(w_in)
    w_out_b = w_out.astype(BF16)
    w_uq_b, w_ukv_b = _pad_w_uq(mla_w_uq), mla_w_ukv.astype(BF16)
    g_cq, g_ckv = mla_g_cq[:, None, :], mla_g_ckv[:, None, :]
    pe = nsa_cmp_pe.reshape(DEPTH, 2, 1, -1)
    w1_b, w2_b = nsa_cmp_w1.astype(BF16), nsa_cmp_w2.astype(BF16)
    wq_b, wkv_b, wo_b = mem_wq.astype(BF16), mem_wkv.astype(BF16), mem_wo.astype(BF16)
    g4, b4 = ln_g[:, :, None, :], ln_b[:, :, None, :]
    imp_map = _nsa_importance_map(s)

    outs = []
    for bi in range(batch):
        xb = x[bi]
        tabs = _rope_tables(positions[bi])
        for l in range(DEPTH):
            at = lambda a, *lead: (a, (l,) + lead)
            xb = _ffn_ln(xb, at(w_gu_b, 0), at(w_dn_b, 0), at(g4, 0), at(b4, 0))
            xb = _mixer(xb, tabs, imp_map, tuple(at(w) for w in w_groups), at(w_out_b), at(g_cq), at(g_ckv),
                        at(w_uq_b), at(w_ukv_b), at(pe), at(w1_b), at(w2_b), at(g4, 1), at(b4, 1))
            kv = _mem_kv(mem[bi], at(wkv_b))
            xb = _mem_attn_ln(xb, at(wq_b), kv, at(wo_b), at(g4, 2), at(b4, 2))
            xb = _ffn_ln(xb, at(w_gu_b, 1), at(w_dn_b, 1), at(g4, 3), at(b4, 3))
        outs.append(xb)
    return jnp.stack(outs)
```

```python
import functools

import numpy as np
import jax
import jax.numpy as jnp
from jax import lax
from jax.experimental import pallas as pl
from jax.experimental.pallas import tpu as pltpu

F32 = jnp.float32
BF16 = jnp.bfloat16
I32 = jnp.int32

D_MODEL = 2048
DEPTH = 4
HEAD_DIM = 128
ROPE_THETA = 10000.0
LN_EPS = 1e-5
RMS_EPS = 1e-6

MOBA_BLOCK = 256
MOBA_TOPK = 3
MLA_NOPE = 128
MLA_ROPE = 64
NSA_CMP_STRIDE = 16
NSA_SEL_BLOCK = 64
NSA_SEL_TOPK = 16
NSA_WINDOW = 512
DSA_IDX_HEADS = 16
DSA_TOPK = 256
D_FF = 5632
DN_ALPHA = (2 * DEPTH) ** 0.25

NEG = -(2.0 ** 100)
LOG2_E = 1.4426950408889634
INT_MIN = -(2 ** 31)
VMEM_LIMIT = 56 * 1024 * 1024

ROW_TILE = 512
ATT_TQ = 512
ATT_TK = 1024
CAUSAL_TK = 2048
DSA_TQ = 256
DSA_COUNT_ROWS = 128


def _cp(*sem):
    return pltpu.CompilerParams(dimension_semantics=sem, vmem_limit_bytes=VMEM_LIMIT)


def _wspec(w, block, index_map, resident=True):
    _, lead = w
    shape = (None,) * len(lead) + tuple(block)
    imap = lambda *g: tuple(lead) + tuple(index_map(*g))
    if resident:
        return pl.BlockSpec(shape, imap, pipeline_mode=pl.Buffered(1))
    return pl.BlockSpec(shape, imap)


def _wshape(w):
    arr, lead = w
    return arr.shape[len(lead):]


def _resident(shape, index_map):
    return pl.BlockSpec(shape, index_map, pipeline_mode=pl.Buffered(1))


def _dot(a, b):
    return jnp.dot(a, b, preferred_element_type=F32)


def _dot_nt(a, b):
    return lax.dot_general(a, b, (((1,), (1,)), ((), ())), preferred_element_type=F32)


def _layer_norm(y, g, b):
    mu = jnp.mean(y, axis=-1, keepdims=True)
    d = y - mu
    var = jnp.mean(d * d, axis=-1, keepdims=True)
    return d * lax.rsqrt(var + LN_EPS) * g + b


def _rope128(x, cos, sin_signed):
    return x * cos + pltpu.roll(x, 64, 1) * sin_signed


def _rope64(x, cos, s_up, s_dn):
    return x * cos + pltpu.roll(x, 32, 1) * s_up + pltpu.roll(x, 96, 1) * s_dn


def _log2_scale(d):
    return float(d) ** -0.5 * LOG2_E


def _with_ones(v):
    return jnp.concatenate([v, jnp.ones_like(v)], axis=1)


def _flash_update(s, v1, bias, keep, m_ref, acc_ref, slot):
    if bias is not None:
        s = s + bias
    if keep is not None:
        s = jnp.where(keep, s, NEG)
    m_old = m_ref[slot]
    m_new = jnp.maximum(m_old, jnp.max(s, axis=1, keepdims=True))
    alpha = jnp.exp2(m_old - m_new)
    p = jnp.exp2(s - m_new)
    acc_ref[slot] = alpha * acc_ref[slot] + _dot(p.astype(BF16), v1)
    m_ref[slot] = m_new


FLASH_ROWS = 256


def _flash_rows(q, k, v1, bias, keep, m_ref, acc_ref, hd):
    n = max(1, q.shape[0] // FLASH_ROWS)
    r = q.shape[0] // n
    rows = [slice(sp * r, (sp + 1) * r) for sp in range(n)]
    logits = [_dot_nt(q[rw], k) for rw in rows]
    for sp, rw in enumerate(rows):
        _flash_update(logits[sp], v1, None if bias is None else bias[rw], None if keep is None else keep[rw],
                      m_ref, acc_ref, hd * n + sp)


def _flash_heads(qs, ks, v1s, bias, keep, m_ref, acc_ref):
    def logits(hd):
        q = qs[hd]
        n = max(1, q.shape[0] // FLASH_ROWS)
        r = q.shape[0] // n
        rows = [slice(sp * r, (sp + 1) * r) for sp in range(n)]
        return rows, [_dot_nt(q[rw], ks[hd]) for rw in rows]

    ahead = logits(0)
    for hd in range(len(qs)):
        rows, s = ahead
        if hd + 1 < len(qs):
            ahead = logits(hd + 1)
        for sp, rw in enumerate(rows):
            _flash_update(s[sp], v1s[hd], None if bias is None else bias[rw], None if keep is None else keep[rw],
                          m_ref, acc_ref, hd * len(rows) + sp)


def _causal_flash(q, k_ref, v_ref, m_ref, acc_ref, i, tk):
    tq = q.shape[0]
    n_past = (i * tq) // tk
    n_rest = (i * tq - n_past * tk) // tq

    def tile(off, width, keep):
        _flash_rows(q, k_ref[pl.ds(off, width), :], _with_ones(v_ref[pl.ds(off, width), :]), None, keep,
                    m_ref, acc_ref, 0)

    def body(j, carry):
        tile(pl.multiple_of(j * tk, tk), tk, None)
        return carry

    lax.fori_loop(0, n_past, body, 0)
    for r in range(tk // tq):
        @pl.when(n_rest == r)
        def _():
            width = (r + 1) * tq
            row = lax.broadcasted_iota(I32, (tq, width), 0) + r * tq
            col = lax.broadcasted_iota(I32, (tq, width), 1)
            tile(pl.multiple_of(n_past * tk, tk), width, col <= row)


def _flash_result(acc_ref, hd, heads):
    n = acc_ref.shape[0] // heads
    parts = []
    for sp in range(n):
        a = acc_ref[hd * n + sp]
        parts.append(a[:, 0:HEAD_DIM] / a[:, HEAD_DIM:2 * HEAD_DIM])
    return jnp.concatenate(parts, axis=0) if n > 1 else parts[0]


def _flash_scratch(heads, tq):
    n = max(1, tq // FLASH_ROWS)
    return [pltpu.VMEM((heads * n, tq // n, 1), F32), pltpu.VMEM((heads * n, tq // n, 2 * HEAD_DIM), F32)]


def _flash_init(m_ref, acc_ref):
    m_ref[...] = jnp.full(m_ref.shape, NEG, F32)
    acc_ref[...] = jnp.zeros(acc_ref.shape, F32)


def _top_k_mask(work, index, k, axis=1):
    n = work.shape[axis]
    sel = jnp.zeros(work.shape, F32)
    for _ in range(k):
        m = jnp.max(work, axis=axis, keepdims=True)
        idx = jnp.min(jnp.where(work == m, index, n), axis=axis, keepdims=True)
        pick = index == jnp.where(m > -jnp.inf, idx, n)
        sel = jnp.where(pick, 1.0, sel)
        work = jnp.where(pick, -jnp.inf, work)
    return sel


def _key_block_table(s, block):
    return jnp.asarray((np.arange(128)[None, :] == (np.arange(s) // block)[:, None]) * -NEG, dtype=BF16)


def _rope_table_kernel(pos_ref, c_ref, cos128_ref, sin128_ref, cos64_ref, sup64_ref, sdn64_ref):
    pos = pos_ref[...]
    a128 = pos * c_ref[0:1, :]
    cos128_ref[...] = jnp.cos(a128)
    sin128_ref[...] = jnp.sin(a128) * c_ref[1:2, :]
    a64 = pos * c_ref[2:3, :]
    s64 = jnp.sin(a64)
    cos64_ref[...] = jnp.cos(a64)
    sup64_ref[...] = s64 * c_ref[3:4, :]
    sdn64_ref[...] = s64 * c_ref[4:5, :]


def _rope_consts():
    lane = np.arange(128)
    inv128 = (ROPE_THETA ** (-np.arange(0, 128, 2, dtype=np.float32) / 128)).astype(np.float32)
    inv64 = (ROPE_THETA ** (-np.arange(0, 64, 2, dtype=np.float32) / 64)).astype(np.float32)
    c = np.zeros((8, 128), np.float32)
    c[0] = inv128[lane % 64]
    c[1] = np.where(lane < 64, -1.0, 1.0)
    c[2] = inv64[(lane % 64) % 32]
    c[3] = np.where(lane % 64 >= 32, 1.0, 0.0)
    c[4] = np.where(lane % 64 < 32, -1.0, 0.0)
    return jnp.asarray(c)


def _rope_tables(positions):
    s = positions.shape[0]
    tm = min(ROW_TILE, s)
    pos = positions.astype(F32).reshape(s, 1)
    tab = jax.ShapeDtypeStruct((s, 128), F32)
    row = pl.BlockSpec((tm, 128), lambda i: (i, 0))
    return pl.pallas_call(
        _rope_table_kernel,
        grid=(s // tm,),
        in_specs=[pl.BlockSpec((tm, 1), lambda i: (i, 0)), pl.BlockSpec((8, 128), lambda i: (0, 0))],
        out_specs=[row] * 5,
        out_shape=[tab] * 5,
        compiler_params=_cp("parallel"),
        name="rope_tables",
    )(pos, _rope_consts())


def _ffn_kernel(x_ref, wg_ref, wu_ref, wd_ref, g_ref, b_ref, o_ref, xb_ref, acc_ref):
    j = pl.program_id(1)

    @pl.when(j == 0)
    def _():
        xb_ref[...] = x_ref[...].astype(BF16)
        acc_ref[...] = jnp.zeros_like(acc_ref)

    xb = xb_ref[...]
    g = _dot(xb, wg_ref[...])
    u = _dot(xb, wu_ref[...])
    h = (g * jax.nn.sigmoid(g) * u).astype(BF16)
    acc_ref[...] += _dot(h, wd_ref[...])

    @pl.when(j == pl.num_programs(1) - 1)
    def _():
        y = DN_ALPHA * x_ref[...] + 0.5 * acc_ref[...]
        o_ref[...] = _layer_norm(y, g_ref[...], b_ref[...])


def _ffn_ln(x, w_gu, w_down, g, b):
    s, d = x.shape
    f = _wshape(w_down)[0]
    tm = min(ROW_TILE, s)
    tf = 512
    nf = f // tf
    return pl.pallas_call(
        _ffn_kernel,
        grid=(s // tm, nf),
        in_specs=[
            pl.BlockSpec((tm, d), lambda i, j: (i, 0)),
            _wspec(w_gu, (d, tf), lambda i, j: (0, j), resident=False),
            _wspec(w_gu, (d, tf), lambda i, j: (0, j + nf), resident=False),
            _wspec(w_down, (tf, d), lambda i, j: (j, 0), resident=False),
            _wspec(g, (1, d), lambda i, j: (0, 0)),
            _wspec(b, (1, d), lambda i, j: (0, 0)),
        ],
        out_specs=pl.BlockSpec((tm, d), lambda i, j: (i, 0)),
        out_shape=jax.ShapeDtypeStruct((s, d), F32),
        scratch_shapes=[pltpu.VMEM((tm, d), BF16), pltpu.VMEM((tm, d), F32)],
        compiler_params=_cp("parallel", "arbitrary"),
        name="ffn_ln",
    )(x, w_gu[0], w_gu[0], w_down[0], g[0], b[0])


def _proj_a_kernel(x_ref, w_ref, cos_ref, sin_ref, kb_ref, q_ref, k_ref, v_ref, km_ref):
    h = _dot(x_ref[...].astype(BF16), w_ref[...])
    cos, sin = cos_ref[...], sin_ref[...]
    nblk = km_ref.shape[0]
    for hd in range(4):
        lo = hd * HEAD_DIM
        q_ref[:, lo:lo + HEAD_DIM] = _rope128(h[:, lo:lo + HEAD_DIM], cos, sin)
        k = _rope128(h[:, 512 + lo:512 + lo + HEAD_DIM], cos, sin)
        k_ref[:, 2 * lo:2 * lo + HEAD_DIM] = k.astype(BF16)
        k_ref[:, 2 * lo + HEAD_DIM:2 * lo + 2 * HEAD_DIM] = kb_ref[...]
        for bi in range(nblk):
            kb = k[bi * MOBA_BLOCK:(bi + 1) * MOBA_BLOCK]
            km_ref[bi, :, lo:lo + HEAD_DIM] = jnp.mean(kb, axis=0, keepdims=True)
    v_ref[...] = h[:, 1024:1536].astype(BF16)


def _proj_a(x, w, cos128, sin128):
    s, d = x.shape
    tm = min(ROW_TILE, s)
    nblk = tm // MOBA_BLOCK
    row512 = pl.BlockSpec((tm, 512), lambda i: (i, 0))
    tab = pl.BlockSpec((tm, 128), lambda i: (i, 0))
    return pl.pallas_call(
        _proj_a_kernel,
        grid=(s // tm,),
        in_specs=[pl.BlockSpec((tm, d), lambda i: (i, 0)), _wspec(w, _wshape(w), lambda i: (0, 0)), tab, tab, tab],
        out_specs=[row512, pl.BlockSpec((tm, 1024), lambda i: (i, 0)), row512,
                   pl.BlockSpec((nblk, 1, 512), lambda i: (i, 0, 0))],
        out_shape=[
            jax.ShapeDtypeStruct((s, 512), F32),
            jax.ShapeDtypeStruct((s, 1024), BF16),
            jax.ShapeDtypeStruct((s, 512), BF16),
            jax.ShapeDtypeStruct((s // MOBA_BLOCK, 1, 512), F32),
        ],
        compiler_params=_cp("parallel"),
        name="proj_moba",
    )(x, w[0], cos128, sin128, _key_block_table(s, MOBA_BLOCK))


def _rms_norm(x, g):
    return x * lax.rsqrt(jnp.mean(x * x, axis=-1, keepdims=True) + RMS_EPS) * g


def _proj_b_kernel(x_ref, w_ref, gq_ref, gkv_ref, wuq_ref, wukv_ref, cos_ref, sup_ref, sdn_ref,
                   q_ref, k_ref, v_ref):
    h = _dot(x_ref[...].astype(BF16), w_ref[...])
    cos, sup, sdn = cos_ref[...], sup_ref[...], sdn_ref[...]
    cq = _rms_norm(h[:, 0:512], gq_ref[...]).astype(BF16)
    ckv = _rms_norm(h[:, 512:1024], gkv_ref[...]).astype(BF16)
    kr = _rope64(h[:, 1024:1152], cos, sup, sdn).astype(BF16)
    q = _dot(cq, wuq_ref[...])
    kv = _dot(ckv, wukv_ref[...])
    qs = _log2_scale(MLA_NOPE + MLA_ROPE)
    for hd in range(4):
        lo = hd * 256
        q_ref[:, lo:lo + 128] = (q[:, lo:lo + 128] * qs).astype(BF16)
        q_ref[:, lo + 128:lo + 256] = (_rope64(q[:, lo + 128:lo + 256], cos, sup, sdn) * qs).astype(BF16)
        k_ref[:, lo:lo + 128] = kv[:, lo:lo + 128].astype(BF16)
        k_ref[:, lo + 128:lo + 256] = kr
        v_ref[:, hd * 128:(hd + 1) * 128] = kv[:, lo + 128:lo + 256].astype(BF16)


def _proj_b(x, w, g_cq, g_ckv, w_uq, w_ukv, cos64, sup64, sdn64):
    s, d = x.shape
    tm = min(ROW_TILE, s)
    tab = pl.BlockSpec((tm, 128), lambda i: (i, 0))
    full = lambda a: _wspec(a, _wshape(a), lambda i: (0, 0))
    return pl.pallas_call(
        _proj_b_kernel,
        grid=(s // tm,),
        in_specs=[pl.BlockSpec((tm, d), lambda i: (i, 0)), full(w), full(g_cq), full(g_ckv), full(w_uq),
                  full(w_ukv), tab, tab, tab],
        out_specs=[pl.BlockSpec((tm, 1024), lambda i: (i, 0)), pl.BlockSpec((tm, 1024), lambda i: (i, 0)),
                   pl.BlockSpec((tm, 512), lambda i: (i, 0))],
        out_shape=[jax.ShapeDtypeStruct((s, 1024), BF16), jax.ShapeDtypeStruct((s, 1024), BF16),
                   jax.ShapeDtypeStruct((s, 512), BF16)],
        compiler_params=_cp("parallel"),
        name="proj_mla",
    )(x, w[0], g_cq[0], g_ckv[0], w_uq[0], w_ukv[0], cos64, sup64, sdn64)


def _proj_c_kernel(x_ref, w_ref, cos_ref, sin_ref, kb_ref, q_ref, qr_ref, kvcmp_ref, kslc_ref, vslc_ref,
                   kwin_ref, vwin_ref, gate_ref):
    h = _dot(x_ref[...].astype(BF16), w_ref[...])
    cos, sin = cos_ref[...], sin_ref[...]
    for hd in range(4):
        lo = hd * HEAD_DIM
        qh = h[:, lo:lo + HEAD_DIM]
        q_ref[:, lo:lo + HEAD_DIM] = qh.astype(BF16)
        qr_ref[:, lo:lo + HEAD_DIM] = (_rope128(qh, cos, sin) * _log2_scale(HEAD_DIM)).astype(BF16)
    kv = lambda i: h[:, 512 + i * 128:512 + (i + 1) * 128]
    kvcmp_ref[0] = kv(0).astype(BF16)
    kvcmp_ref[1] = kv(1).astype(BF16)
    kslc_ref[:, 0:HEAD_DIM] = _rope128(kv(2), cos, sin).astype(BF16)
    kslc_ref[:, HEAD_DIM:2 * HEAD_DIM] = kb_ref[...]
    vslc_ref[...] = kv(3).astype(BF16)
    kwin_ref[...] = _rope128(kv(4), cos, sin).astype(BF16)
    vwin_ref[...] = kv(5).astype(BF16)
    gate_ref[...] = jax.nn.sigmoid(h[:, 1280:1408])


def _proj_c(x, w, cos128, sin128):
    s, d = x.shape
    tm = min(ROW_TILE, s)
    r512 = pl.BlockSpec((tm, 512), lambda i: (i, 0))
    r128 = pl.BlockSpec((tm, 128), lambda i: (i, 0))
    b512 = jax.ShapeDtypeStruct((s, 512), BF16)
    b128 = jax.ShapeDtypeStruct((s, 128), BF16)
    return pl.pallas_call(
        _proj_c_kernel,
        grid=(s // tm,),
        in_specs=[pl.BlockSpec((tm, d), lambda i: (i, 0)), _wspec(w, _wshape(w), lambda i: (0, 0)), r128, r128, r128],
        out_specs=[r512, r512, pl.BlockSpec((2, tm, 128), lambda i: (0, i, 0)),
                   pl.BlockSpec((tm, 256), lambda i: (i, 0))] + [r128] * 4,
        out_shape=[b512, b512, jax.ShapeDtypeStruct((2, s, 128), BF16), jax.ShapeDtypeStruct((s, 256), BF16)]
                  + [b128] * 3 + [jax.ShapeDtypeStruct((s, 128), F32)],
        compiler_params=_cp("parallel"),
        name="proj_nsa",
    )(x, w[0], cos128, sin128, _key_block_table(s, NSA_SEL_BLOCK))


def _proj_d_kernel(x_ref, w_ref, cos_ref, sin_ref, c64_ref, sup_ref, sdn_ref,
                   q_ref, k_ref, v_ref, iq_ref, ik_ref, iw_ref):
    h = _dot(x_ref[...].astype(BF16), w_ref[...])
    cos, sin = cos_ref[...], sin_ref[...]
    c64, sup, sdn = c64_ref[...], sup_ref[...], sdn_ref[...]
    for hd in range(4):
        lo = hd * HEAD_DIM
        q_ref[:, lo:lo + HEAD_DIM] = (_rope128(h[:, lo:lo + HEAD_DIM], cos, sin) * _log2_scale(HEAD_DIM)).astype(BF16)
        k_ref[:, lo:lo + HEAD_DIM] = _rope128(h[:, 512 + lo:512 + lo + HEAD_DIM], cos, sin).astype(BF16)
    v_ref[...] = h[:, 1024:1536].astype(BF16)
    for p in range(8):
        lo = 1536 + p * 128
        iq_ref[:, p * 128:(p + 1) * 128] = _rope64(h[:, lo:lo + 128], c64, sup, sdn).astype(BF16)
    ik = _rope64(h[:, 2560:2688], c64, sup, sdn)
    ik_ref[:, 0:128] = ik.astype(BF16)
    ik_ref[:, 128:256] = pltpu.roll(ik, 64, 1).astype(BF16)
    iw_ref[...] = h[:, 2688:2816] * (1.0 / 32.0)


def _proj_d(x, w, cos128, sin128, cos64, sup64, sdn64):
    s, d = x.shape
    tm = min(ROW_TILE, s)
    r = lambda n: pl.BlockSpec((tm, n), lambda i: (i, 0))
    return pl.pallas_call(
        _proj_d_kernel,
        grid=(s // tm,),
        in_specs=[pl.BlockSpec((tm, d), lambda i: (i, 0)), _wspec(w, _wshape(w), lambda i: (0, 0))] + [r(128)] * 5,
        out_specs=[r(512), r(512), r(512), r(1024), r(256), r(128)],
        out_shape=[jax.ShapeDtypeStruct((s, 512), BF16)] * 3 + [
            jax.ShapeDtypeStruct((s, 1024), BF16), jax.ShapeDtypeStruct((s, 256), BF16),
            jax.ShapeDtypeStruct((s, 128), F32)],
        compiler_params=_cp("parallel"),
        name="proj_dsa",
    )(x, w[0], cos128, sin128, cos64, sup64, sdn64)


def _moba_kernel(q_ref, k_ref, v_ref, km_ref, o_ref, m_ref, acc_ref, *, tk, nb):
    i = pl.program_id(1)
    tq = q_ref.shape[0]
    q32 = q_ref[...]

    gate = lax.dot_general(q32, km_ref[...], (((1,), (1,)), ((), ())),
                           precision=lax.Precision.HIGHEST, preferred_element_type=F32)
    blk = lax.broadcasted_iota(I32, (128, tq), 0)
    cur = (i * tq + lax.broadcasted_iota(I32, (128, tq), 1)) // MOBA_BLOCK
    sel = _top_k_mask(jnp.where(blk < cur, gate.T, -jnp.inf), blk, min(MOBA_TOPK, nb - 1), axis=0)
    unchosen = (jnp.where(blk == cur, 1.0, sel) - 1.0).T
    qa = jnp.concatenate([q32 * _log2_scale(HEAD_DIM), unchosen], axis=1).astype(BF16)

    _flash_init(m_ref, acc_ref)
    _causal_flash(qa, k_ref, v_ref, m_ref, acc_ref, i, tk)
    o_ref[...] = _flash_result(acc_ref, 0, 1).astype(o_ref.dtype)


def _moba(q32, k, v, kmean):
    s = q32.shape[0]
    nb = s // MOBA_BLOCK
    tq = min(ATT_TQ, s)
    tk = min(CAUSAL_TK, s)
    km = jnp.pad(kmean.reshape(nb, 512), ((0, 128 - nb), (0, 0)))
    return pl.pallas_call(
        functools.partial(_moba_kernel, tk=tk, nb=nb),
        grid=(4, s // tq),
        in_specs=[
            pl.BlockSpec((tq, HEAD_DIM), lambda h, i: (i, h)),
            pl.BlockSpec((s, 2 * HEAD_DIM), lambda h, i: (0, h)),
            pl.BlockSpec((s, HEAD_DIM), lambda h, i: (0, h)),
            pl.BlockSpec((128, HEAD_DIM), lambda h, i: (0, h)),
        ],
        out_specs=pl.BlockSpec((tq, HEAD_DIM), lambda h, i: (i, h)),
        out_shape=jax.ShapeDtypeStruct((s, 512), BF16),
        scratch_shapes=_flash_scratch(1, tq),
        compiler_params=_cp("parallel", "arbitrary"),
        name="moba_attn",
    )(q32, k, v, km)


def _mla_kernel(q_ref, k_ref, v_ref, o_ref, m_ref, acc_ref, *, tk):
    _flash_init(m_ref, acc_ref)
    _causal_flash(q_ref[...], k_ref, v_ref, m_ref, acc_ref, pl.program_id(1), tk)
    o_ref[...] = _flash_result(acc_ref, 0, 1).astype(o_ref.dtype)


def _mla(qcat, kcat, v):
    s = qcat.shape[0]
    tq = min(ATT_TQ, s)
    tk = min(CAUSAL_TK, s)
    return pl.pallas_call(
        functools.partial(_mla_kernel, tk=tk),
        grid=(4, s // tq),
        in_specs=[
            pl.BlockSpec((tq, 256), lambda h, i: (i, h)),
            pl.BlockSpec((s, 256), lambda h, i: (0, h)),
            pl.BlockSpec((s, HEAD_DIM), lambda h, i: (0, h)),
        ],
        out_specs=pl.BlockSpec((tq, HEAD_DIM), lambda h, i: (i, h)),
        out_shape=jax.ShapeDtypeStruct((s, 512), BF16),
        scratch_shapes=_flash_scratch(1, tq),
        compiler_params=_cp("parallel", "arbitrary"),
        name="mla_attn",
    )(qcat, kcat, v)


def _gelu_tanh(x):
    return 0.5 * x * (1.0 + jnp.tanh(0.7978845608028654 * (x + 0.044715 * x * x * x)))


def _nsa_compress_kernel(t_ref, pe_ref, w1_ref, w2_ref, o_ref):
    t = t_ref[0]
    w1 = w1_ref[...]
    half = t.shape[1]
    n = t.shape[0]
    lo = _dot(t, w1[0:half])
    hi = _dot(t, w1[half:2 * half])
    pe = _dot(pe_ref[...].astype(BF16), w1)
    pre = lo + pltpu.roll(hi, n - 1, 0) + pe
    o_ref[0] = _dot(_gelu_tanh(pre).astype(BF16), w2_ref[...]).astype(o_ref.dtype)


def _nsa_compress(kv16, pe, w1, w2):
    n = kv16.shape[1]
    sub = lambda a: _wspec(a, (None,) + _wshape(a)[1:], lambda i: (i, 0, 0), resident=False)
    return pl.pallas_call(
        _nsa_compress_kernel,
        grid=(2,),
        in_specs=[pl.BlockSpec((1,) + kv16.shape[1:], lambda i: (i, 0, 0)), sub(pe), sub(w1), sub(w2)],
        out_specs=pl.BlockSpec((1, n, HEAD_DIM), lambda i: (i, 0, 0)),
        out_shape=jax.ShapeDtypeStruct((2, n, HEAD_DIM), BF16),
        compiler_params=_cp("parallel"),
        name="nsa_compress",
    )(kv16, pe[0], w1[0], w2[0])


def _nsa_cmp_kernel(q_ref, kvc_ref, m_ref, o_ref, sel_ref, *, n_sel):
    i = pl.program_id(0)
    tq = q_ref.shape[0]
    n = kvc_ref.shape[1]
    nblk = m_ref.shape[1]
    scale = HEAD_DIM ** -0.5
    kc, vc = kvc_ref[0], kvc_ref[1]
    t_pos = i * tq + lax.broadcasted_iota(I32, (tq, n), 0)
    cmp_end = lax.broadcasted_iota(I32, (tq, n), 1) * NSA_CMP_STRIDE + (2 * NSA_CMP_STRIDE - 1)
    ok = cmp_end <= t_pos
    p_sum = jnp.zeros((tq, n), F32)
    for hd in range(4):
        lo = hd * HEAD_DIM
        s = jnp.where(ok, _dot_nt(q_ref[:, lo:lo + HEAD_DIM], kc) * scale, NEG)
        m = jnp.max(s, axis=1, keepdims=True)
        e = jnp.where(ok, jnp.exp(s - m), 0.0)
        l = jnp.sum(e, axis=1, keepdims=True)
        p = e / jnp.where(l > 0, l, 1.0)
        p_sum = p_sum + p
        o_ref[:, lo:lo + HEAD_DIM] = _dot(p.astype(BF16), vc)

    imp = jnp.dot(p_sum, m_ref[...], precision=lax.Precision.HIGHEST, preferred_element_type=F32)
    imp_t = imp.T
    blk = lax.broadcasted_iota(I32, (nblk, tq), 0)
    cur = (i * tq + lax.broadcasted_iota(I32, (nblk, tq), 1)) // NSA_SEL_BLOCK
    forced = (blk == 0) | (blk == cur) | (blk == cur - 1)
    imp_t = jnp.where(blk > cur, -jnp.inf, jnp.where(forced, jnp.inf, imp_t))
    sel_ref[...] = (_top_k_mask(imp_t, blk, n_sel, axis=0).T - 1.0).astype(sel_ref.dtype)


def _nsa_cmp(q, kvc, imp_map):
    s = q.shape[0]
    n = kvc.shape[1]
    nblk = imp_map.shape[1]
    tq = min(256, s)
    return pl.pallas_call(
        functools.partial(_nsa_cmp_kernel, n_sel=min(NSA_SEL_TOPK, s // NSA_SEL_BLOCK)),
        grid=(s // tq,),
        in_specs=[pl.BlockSpec((tq, 512), lambda i: (i, 0)),
                  _resident((2, n, HEAD_DIM), lambda i: (0, 0, 0)),
                  _resident((n, nblk), lambda i: (0, 0))],
        out_specs=[pl.BlockSpec((tq, 512), lambda i: (i, 0)), pl.BlockSpec((tq, nblk), lambda i: (i, 0))],
        out_shape=[jax.ShapeDtypeStruct((s, 512), F32), jax.ShapeDtypeStruct((s, nblk), BF16)],
        compiler_params=_cp("parallel"),
        name="nsa_cmp_select",
    )(q, kvc, imp_map)


def _nsa_slc_win_kernel(q_ref, sel_ref, ks_ref, vs_ref, kw_ref, vw_ref, g_ref, oc_ref, o_ref,
                        m_ref, acc_ref, *, tk):
    i = pl.program_id(0)
    tq = q_ref.shape[0]
    _flash_init(m_ref, acc_ref)

    def tile(j, causal):
        off = pl.multiple_of(j * tk, tk)
        keep = None
        if causal:
            row = i * tq + lax.broadcasted_iota(I32, (tq, tk), 0)
            col = off + lax.broadcasted_iota(I32, (tq, tk), 1)
            keep = col <= row
        k = ks_ref[pl.ds(off, tk), 0:HEAD_DIM]
        v1 = _with_ones(vs_ref[pl.ds(off, tk), :])
        bias = _dot_nt(sel_ref[...], ks_ref[pl.ds(off, tk), HEAD_DIM:2 * HEAD_DIM])
        _flash_heads([q_ref[:, hd * HEAD_DIM:(hd + 1) * HEAD_DIM] for hd in range(4)], [k] * 4, [v1] * 4,
                     bias, keep, m_ref, acc_ref)

    n_past = (i * tq) // tk

    def body(j, carry):
        tile(j, False)
        return carry

    lax.fori_loop(0, n_past, body, 0)
    tile(n_past, True)

    wlen = NSA_WINDOW + tq
    start = pl.multiple_of(jnp.maximum(i * tq - NSA_WINDOW, 0), 256)
    kw = kw_ref[pl.ds(start, wlen), :]
    vw1 = _with_ones(vw_ref[pl.ds(start, wlen), :])
    diff = (i * tq + lax.broadcasted_iota(I32, (tq, wlen), 0)) - (start + lax.broadcasted_iota(I32, (tq, wlen), 1))
    in_win = (diff >= 0) & (diff < NSA_WINDOW)
    g = g_ref[...]
    for hd in range(4):
        lo = hd * HEAD_DIM
        s = jnp.where(in_win, _dot_nt(q_ref[:, lo:lo + HEAD_DIM], kw), NEG)
        p = jnp.exp2(s - jnp.max(s, axis=1, keepdims=True))
        pv = _dot(p.astype(BF16), vw1)
        o_win = pv[:, 0:HEAD_DIM] / pv[:, HEAD_DIM:2 * HEAD_DIM]
        o_slc = _flash_result(acc_ref, hd, 4)
        out = (g[:, 3 * hd:3 * hd + 1] * oc_ref[:, lo:lo + HEAD_DIM] + g[:, 3 * hd + 1:3 * hd + 2] * o_slc
               + g[:, 3 * hd + 2:3 * hd + 3] * o_win)
        o_ref[:, lo:lo + HEAD_DIM] = out.astype(o_ref.dtype)


def _nsa_slc_win(q_r, sel, kslc, vslc, kwin, vwin, gates, o_cmp):
    s = q_r.shape[0]
    tq = min(ATT_TQ, s)
    tk = min(ATT_TK, s)
    full = lambda a: _resident(a.shape, lambda i: (0,) * a.ndim)
    return pl.pallas_call(
        functools.partial(_nsa_slc_win_kernel, tk=tk),
        grid=(s // tq,),
        in_specs=[pl.BlockSpec((tq, 512), lambda i: (i, 0)), pl.BlockSpec((tq, 128), lambda i: (i, 0)),
                  full(kslc), full(vslc), full(kwin), full(vwin),
                  pl.BlockSpec((tq, 128), lambda i: (i, 0)), pl.BlockSpec((tq, 512), lambda i: (i, 0))],
        out_specs=pl.BlockSpec((tq, 512), lambda i: (i, 0)),
        out_shape=jax.ShapeDtypeStruct((s, 512), BF16),
        scratch_shapes=_flash_scratch(4, tq),
        compiler_params=_cp("parallel"),
        name="nsa_slc_win",
    )(q_r, sel, kslc, vslc, kwin, vwin, gates, o_cmp)


def _dsa_kernel(q_ref, iq_ref, iw_ref, k_ref, v_ref, ik_ref, o_ref, key_ref, wb_ref, m_ref, acc_ref, *, tk):
    i = pl.program_id(0)
    tq = q_ref.shape[0]
    half = tk // 2
    n_tiles = (i * tq + tq + tk - 1) // tk
    row_h = lax.broadcasted_iota(I32, (tq, half), 0) + i * tq
    col_h = lax.broadcasted_iota(I32, (tq, half), 1)

    w = iw_ref[...]
    for hd in range(DSA_IDX_HEADS):
        wb_ref[hd] = jnp.broadcast_to(w[:, hd:hd + 1], (tq, 128))

    def score_body(j, carry):
        for hf in range(2):
            off = pl.multiple_of(j * tk + hf * half, half)
            ik_even = ik_ref[pl.ds(off, half), 0:128]
            ik_odd = ik_ref[pl.ds(off, half), 128:256]
            sc = jnp.zeros((tq, half), F32)
            for p in range(DSA_IDX_HEADS // 2):
                x = iq_ref[:, p * 128:(p + 1) * 128]
                we = jnp.concatenate([wb_ref[2 * p]] * (half // 128), axis=1)
                wo = jnp.concatenate([wb_ref[2 * p + 1]] * (half // 128), axis=1)
                sc = sc + jnp.maximum(_dot_nt(x, ik_even), 0.0) * we
                sc = sc + jnp.maximum(_dot_nt(x, ik_odd), 0.0) * wo
            bits = pltpu.bitcast(sc, I32)
            key = jnp.where(bits >= 0, bits, bits ^ 0x7FFFFFFF)
            key_ref[j, :, hf * half:(hf + 1) * half] = jnp.where(col_h + off <= row_h, key, INT_MIN)
        return carry

    lax.fori_loop(0, n_tiles, score_body, 0)

    rb = min(DSA_COUNT_ROWS, tq)

    ones = jnp.ones((128, 128), BF16)
    k_f = float(DSA_TOPK)

    def count_ge(cand128):
        parts = []
        for r0 in range(0, tq, rb):
            cand_b = cand128[r0:r0 + rb]

            def body(j, acc):
                for c in range(tk // 128):
                    acc = acc + jnp.where(key_ref[j, r0:r0 + rb, c * 128:(c + 1) * 128] >= cand_b, 1, 0)
                return acc

            parts.append(lax.fori_loop(0, n_tiles, body, jnp.zeros((rb, 128), I32)))
        per_lane = jnp.concatenate(parts, axis=0).astype(F32).astype(BF16)
        return _dot(per_lane, ones)

    zero = jnp.zeros((tq, 128), I32)
    c0 = count_ge(zero)
    lo0 = jnp.where(c0 >= k_f, zero, INT_MIN)
    c_lo0 = jnp.where(c0 >= k_f, c0, -1.0)

    def bit_cond(state):
        b, _, _, more = state
        return (b < 31) & (more > 0.0)

    def bit_body(state):
        b, lo, c_lo, _ = state
        more = jnp.max(jnp.where(c_lo != k_f, 1.0, 0.0))
        cand = lo | lax.shift_left(jnp.int32(1), 30 - b)
        c = count_ge(cand)
        take = c >= k_f
        return b + 1, jnp.where(take, cand, lo), jnp.where(take, c, c_lo), more

    _, lo, _, _ = lax.while_loop(bit_cond, bit_body, (jnp.int32(0), lo0, c_lo0, jnp.float32(1.0)))
    thr128 = jnp.maximum(lo, INT_MIN + 1)

    _flash_init(m_ref, acc_ref)

    def attn_body(j, carry):
        off = pl.multiple_of(j * tk, tk)
        keep = key_ref[j] >= jnp.concatenate([thr128] * (tk // 128), axis=1)
        cols = [slice(hd * HEAD_DIM, (hd + 1) * HEAD_DIM) for hd in range(4)]
        _flash_heads([q_ref[:, c] for c in cols], [k_ref[pl.ds(off, tk), c] for c in cols],
                     [_with_ones(v_ref[pl.ds(off, tk), c]) for c in cols], None, keep, m_ref, acc_ref)
        return carry

    lax.fori_loop(0, n_tiles, attn_body, 0)
    for hd in range(4):
        o_ref[:, hd * HEAD_DIM:(hd + 1) * HEAD_DIM] = _flash_result(acc_ref, hd, 4).astype(o_ref.dtype)


def _dsa(q, k, v, iq, ik2, iw):
    s = q.shape[0]
    tq = min(DSA_TQ, s)
    tk = min(ATT_TK, s)
    full = lambda a: _resident(a.shape, lambda i: (0, 0))
    return pl.pallas_call(
        functools.partial(_dsa_kernel, tk=tk),
        grid=(s // tq,),
        in_specs=[pl.BlockSpec((tq, 512), lambda i: (i, 0)), pl.BlockSpec((tq, 1024), lambda i: (i, 0)),
                  pl.BlockSpec((tq, 128), lambda i: (i, 0)), full(k), full(v), full(ik2)],
        out_specs=pl.BlockSpec((tq, 512), lambda i: (i, 0)),
        out_shape=jax.ShapeDtypeStruct((s, 512), BF16),
        scratch_shapes=[pltpu.VMEM((s // tk, tq, tk), I32), pltpu.VMEM((DSA_IDX_HEADS, tq, 128), F32)]
                       + _flash_scratch(4, tq),
        compiler_params=_cp("parallel"),
        name="dsa_attn",
    )(q, iq, iw, k, v, ik2)


def _out_proj_kernel(oa_ref, ob_ref, oc_ref, od_ref, w_ref, x_ref, g_ref, b_ref, o_ref):
    mix = _dot(oa_ref[...], w_ref[0:512, :])
    mix = mix + _dot(ob_ref[...], w_ref[512:1024, :])
    mix = mix + _dot(oc_ref[...], w_ref[1024:1536, :])
    mix = mix + _dot(od_ref[...], w_ref[1536:2048, :])
    o_ref[...] = _layer_norm(DN_ALPHA * x_ref[...] + mix, g_ref[...], b_ref[...])


def _out_proj_ln(o_a, o_b, o_c, o_d, w_out, x, g, b):
    s, d = x.shape
    tm = min(ROW_TILE, s)
    r512 = pl.BlockSpec((tm, 512), lambda i: (i, 0))
    vec = lambda a: _wspec(a, (1, d), lambda i: (0, 0))
    return pl.pallas_call(
        _out_proj_kernel,
        grid=(s // tm,),
        in_specs=[r512, r512, r512, r512, _wspec(w_out, _wshape(w_out), lambda i: (0, 0)),
                  pl.BlockSpec((tm, d), lambda i: (i, 0)), vec(g), vec(b)],
        out_specs=pl.BlockSpec((tm, d), lambda i: (i, 0)),
        out_shape=jax.ShapeDtypeStruct((s, d), F32),
        compiler_params=_cp("parallel"),
        name="out_proj_ln",
    )(o_a, o_b, o_c, o_d, w_out[0], x, g[0], b[0])


def _mem_kv_kernel(mem_ref, w_ref, o_ref):
    o_ref[...] = _dot(mem_ref[...].astype(BF16), w_ref[...]).astype(o_ref.dtype)


def _mem_kv(mem, wkv):
    m, d = mem.shape
    n = _wshape(wkv)[1]
    return pl.pallas_call(
        _mem_kv_kernel,
        grid=(1,),
        in_specs=[pl.BlockSpec((m, d), lambda i: (0, 0)), _wspec(wkv, (d, n), lambda i: (0, 0))],
        out_specs=pl.BlockSpec((m, n), lambda i: (0, 0)),
        out_shape=jax.ShapeDtypeStruct((m, n), BF16),
        compiler_params=_cp("arbitrary"),
        name="mem_kv",
    )(mem, wkv[0])


def _mem_attn_kernel(x_ref, wq_ref, kv_ref, wo_ref, g_ref, b_ref, o_ref):
    x = x_ref[...]
    scale = HEAD_DIM ** -0.5
    q = _dot(x.astype(BF16), wq_ref[...]).astype(BF16)
    out = jnp.zeros(x.shape, F32)
    for hd in range(4):
        lo = hd * HEAD_DIM
        s = _dot_nt(q[:, lo:lo + HEAD_DIM], kv_ref[:, lo:lo + HEAD_DIM]) * scale
        p = jnp.exp(s - jnp.max(s, axis=1, keepdims=True))
        o = _dot(p.astype(BF16), kv_ref[:, 512 + lo:512 + lo + HEAD_DIM]) / jnp.sum(p, axis=1, keepdims=True)
        out = out + _dot(o.astype(BF16), wo_ref[lo:lo + HEAD_DIM, :])
    o_ref[...] = _layer_norm(DN_ALPHA * x + out, g_ref[...], b_ref[...])


def _mem_attn_ln(x, wq, kv, wo, g, b):
    s, d = x.shape
    tm = min(ROW_TILE, s)
    full = lambda a: _wspec(a, _wshape(a), lambda i: (0, 0))
    return pl.pallas_call(
        _mem_attn_kernel,
        grid=(s // tm,),
        in_specs=[pl.BlockSpec((tm, d), lambda i: (i, 0)), full(wq), _resident(kv.shape, lambda i: (0, 0)),
                  full(wo), full(g), full(b)],
        out_specs=pl.BlockSpec((tm, d), lambda i: (i, 0)),
        out_shape=jax.ShapeDtypeStruct((s, d), F32),
        compiler_params=_cp("parallel"),
        name="mem_attn_ln",
    )(x, wq[0], kv, wo[0], g[0], b[0])


def _pad_last(w, n):
    return jnp.pad(w, [(0, 0)] * (w.ndim - 1) + [(0, n - w.shape[-1])])


def _split_w_in(w_in):
    a = w_in[..., 0:1536]
    b = _pad_last(w_in[..., 1536:2624], 1152)
    c = _pad_last(w_in[..., 2624:3916], 1408)
    d = jnp.concatenate([w_in[..., 3916:6476], _pad_last(w_in[..., 6476:6540], 128),
                         _pad_last(w_in[..., 6540:6556], 128)], axis=-1)
    return tuple(t.astype(BF16) for t in (a, b, c, d))


def _pad_w_uq(w_uq):
    lead = w_uq.shape[:-1]
    w = _pad_last(w_uq.reshape(lead + (4, MLA_NOPE + MLA_ROPE)), 256)
    return w.reshape(lead + (4 * 256,)).astype(BF16)


def _nsa_importance_map(s):
    n = s // NSA_CMP_STRIDE
    nblk = s // NSA_SEL_BLOCK
    ni = np.arange(n)[:, None]
    bi = np.arange(nblk)[None, :]
    m = ((ni >= 4 * bi - 1) & (ni <= 4 * bi + 3)).astype(np.float32)
    return jnp.asarray(np.pad(m, ((0, 0), (0, 128 - nblk))))


def _mixer(x, tabs, imp_map, w_groups, w_out, g_cq, g_ckv, w_uq, w_ukv, cmp_pe, cmp_w1, cmp_w2, ln_g, ln_b):
    cos128, sin128, cos64, sup64, sdn64 = tabs
    s = x.shape[0]
    w_a, w_b, w_c, w_d = w_groups

    aq, ak, av, akm = _proj_a(x, w_a, cos128, sin128)
    o_a = _moba(aq, ak, av, akm)

    bq, bk, bv = _proj_b(x, w_b, g_cq, g_ckv, w_uq, w_ukv, cos64, sup64, sdn64)
    o_b = _mla(bq, bk, bv)

    cq, cqr, kvcmp, kslc, vslc, kwin, vwin, gates = _proj_c(x, w_c, cos128, sin128)
    kv16 = kvcmp.reshape(2, s // NSA_CMP_STRIDE, NSA_CMP_STRIDE * HEAD_DIM)
    kvc = _nsa_compress(kv16, cmp_pe, cmp_w1, cmp_w2)
    o_cmp, sel = _nsa_cmp(cq, kvc, imp_map)
    o_c = _nsa_slc_win(cqr, sel, kslc, vslc, kwin, vwin, gates, o_cmp)

    dq, dk, dv, diq, dik, diw = _proj_d(x, w_d, cos128, sin128, cos64, sup64, sdn64)
    o_d = _dsa(dq, dk, dv, diq, dik, diw)

    return _out_proj_ln(o_a, o_b, o_c, o_d, w_out, x, ln_g, ln_b)


def kernel(x, mem, positions, ln_g, ln_b, ffn_w_gu, ffn_w_down, w_in, w_out, mla_g_cq, mla_g_ckv, mla_w_uq,
           mla_w_ukv, nsa_cmp_pe, nsa_cmp_w1, nsa_cmp_w2, mem_wq, mem_wkv, mem_wo):
    batch, s, d = x.shape
    w_gu_b, w_dn_b = ffn_w_gu.astype(BF16), ffn_w_down.astype(BF16)
    w_groups = _split_w_in(w_in)
    w_out_b = w_out.astype(BF16)
    w_uq_b, w_ukv_b = _pad_w_uq(mla_w_uq), mla_w_ukv.astype(BF16)
    g_cq, g_ckv = mla_g_cq[:, None, :], mla_g_ckv[:, None, :]
    pe = nsa_cmp_pe.reshape(DEPTH, 2, 1, -1)
    w1_b, w2_b = nsa_cmp_w1.astype(BF16), nsa_cmp_w2.astype(BF16)
    wq_b, wkv_b, wo_b = mem_wq.astype(BF16), mem_wkv.astype(BF16), mem_wo.astype(BF16)
    g4, b4 = ln_g[:, :, None, :], ln_b[:, :, None, :]
    imp_map = _nsa_importance_map(s)

    outs = []
    for bi in range(batch):
        xb = x[bi]
        tabs = _rope_tables(positions[bi])
        for l in range(DEPTH):
            at = lambda a, *lead: (a, (l,) + lead)
            xb = _ffn_ln(xb, at(w_gu_b, 0), at(w_dn_b, 0), at(g4, 0), at(b4, 0))
            xb = _mixer(xb, tabs, imp_map, tuple(at(w) for w in w_groups), at(w_out_b), at(g_cq), at(g_ckv),
                        at(w_uq_b), at(w_ukv_b), at(pe), at(w1_b), at(w2_b), at(g4, 1), at(b4, 1))
            kv = _mem_kv(mem[bi], at(wkv_b))
            xb = _mem_attn_ln(xb, at(wq_b), kv, at(wo_b), at(g4, 2), at(b4, 2))
            xb = _ffn_ln(xb, at(w_gu_b, 1), at(w_dn_b, 1), at(g4, 3), at(b4, 3))
        outs.append(xb)
    return jnp.stack(outs)
```

```python
import functools

import numpy as np
import jax
import jax.numpy as jnp
from jax import lax
from jax.experimental import pallas as pl
from jax.experimental.pallas import tpu as pltpu

F32 = jnp.float32
BF16 = jnp.bfloat16
I32 = jnp.int32

D_MODEL = 2048
DEPTH = 4
HEAD_DIM = 128
ROPE_THETA = 10000.0
LN_EPS = 1e-5
RMS_EPS = 1e-6

MOBA_BLOCK = 256
MOBA_TOPK = 3
MLA_NOPE = 128
MLA_ROPE = 64
NSA_CMP_STRIDE = 16
NSA_SEL_BLOCK = 64
NSA_SEL_TOPK = 16
NSA_WINDOW = 512
DSA_IDX_HEADS = 16
DSA_TOPK = 256
D_FF = 5632
DN_ALPHA = (2 * DEPTH) ** 0.25

NEG = -(2.0 ** 100)
LOG2_E = 1.4426950408889634
INT_MIN = -(2 ** 31)
VMEM_LIMIT = 56 * 1024 * 1024

ROW_TILE = 512
FFN_ROWS = 1024
FFN_COLS = 256
ATT_TQ = 512
ATT_TK = 1024
CAUSAL_TK = 2048
DSA_TQ = 256
DSA_COUNT_ROWS = 128


def _cp(*sem):
    return pltpu.CompilerParams(dimension_semantics=sem, vmem_limit_bytes=VMEM_LIMIT)


def _wspec(w, block, index_map, resident=True):
    _, lead = w
    shape = (None,) * len(lead) + tuple(block)
    imap = lambda *g: tuple(lead) + tuple(index_map(*g))
    if resident:
        return pl.BlockSpec(shape, imap, pipeline_mode=pl.Buffered(1))
    return pl.BlockSpec(shape, imap)


def _wshape(w):
    arr, lead = w
    return arr.shape[len(lead):]


def _resident(shape, index_map):
    return pl.BlockSpec(shape, index_map, pipeline_mode=pl.Buffered(1))


def _dot(a, b):
    return jnp.dot(a, b, preferred_element_type=F32)


def _dot_nt(a, b):
    return lax.dot_general(a, b, (((1,), (1,)), ((), ())), preferred_element_type=F32)


def _layer_norm(y, g, b):
    mu = jnp.mean(y, axis=-1, keepdims=True)
    d = y - mu
    var = jnp.mean(d * d, axis=-1, keepdims=True)
    return d * lax.rsqrt(var + LN_EPS) * g + b


def _rope128(x, cos, sin_signed):
    return x * cos + pltpu.roll(x, 64, 1) * sin_signed


def _rope64(x, cos, s_up, s_dn):
    return x * cos + pltpu.roll(x, 32, 1) * s_up + pltpu.roll(x, 96, 1) * s_dn


def _log2_scale(d):
    return float(d) ** -0.5 * LOG2_E


def _with_ones(v):
    return jnp.concatenate([v, jnp.ones_like(v)], axis=1)


def _flash_update(s, v1, bias, keep, m_ref, acc_ref, slot):
    if bias is not None:
        s = s + bias
    if keep is not None:
        s = jnp.where(keep, s, NEG)
    m_old = m_ref[slot]
    m_new = jnp.maximum(m_old, jnp.max(s, axis=1, keepdims=True))
    alpha = jnp.exp2(m_old - m_new)
    p = jnp.exp2(s - m_new)
    acc_ref[slot] = alpha * acc_ref[slot] + _dot(p.astype(BF16), v1)
    m_ref[slot] = m_new


FLASH_ROWS = 256


def _flash_rows(q, k, v1, bias, keep, m_ref, acc_ref, hd):
    n = max(1, q.shape[0] // FLASH_ROWS)
    r = q.shape[0] // n
    rows = [slice(sp * r, (sp + 1) * r) for sp in range(n)]
    logits = [_dot_nt(q[rw], k) for rw in rows]
    for sp, rw in enumerate(rows):
        _flash_update(logits[sp], v1, None if bias is None else bias[rw], None if keep is None else keep[rw],
                      m_ref, acc_ref, hd * n + sp)


def _flash_heads(qs, ks, v1s, bias, keep, m_ref, acc_ref):
    def logits(hd):
        q = qs[hd]
        n = max(1, q.shape[0] // FLASH_ROWS)
        r = q.shape[0] // n
        rows = [slice(sp * r, (sp + 1) * r) for sp in range(n)]
        return rows, [_dot_nt(q[rw], ks[hd]) for rw in rows]

    ahead = logits(0)
    for hd in range(len(qs)):
        rows, s = ahead
        if hd + 1 < len(qs):
            ahead = logits(hd + 1)
        for sp, rw in enumerate(rows):
            _flash_update(s[sp], v1s[hd], None if bias is None else bias[rw], None if keep is None else keep[rw],
                          m_ref, acc_ref, hd * len(rows) + sp)


def _causal_flash(q, k_ref, v_ref, m_ref, acc_ref, i, tk):
    tq = q.shape[0]
    n_past = (i * tq) // tk
    n_rest = (i * tq - n_past * tk) // tq

    def tile(off, width, keep):
        _flash_rows(q, k_ref[pl.ds(off, width), :], _with_ones(v_ref[pl.ds(off, width), :]), None, keep,
                    m_ref, acc_ref, 0)

    def body(j, carry):
        tile(pl.multiple_of(j * tk, tk), tk, None)
        return carry

    lax.fori_loop(0, n_past, body, 0)
    for r in range(tk // tq):
        @pl.when(n_rest == r)
        def _():
            width = (r + 1) * tq
            row = lax.broadcasted_iota(I32, (tq, width), 0) + r * tq
            col = lax.broadcasted_iota(I32, (tq, width), 1)
            tile(pl.multiple_of(n_past * tk, tk), width, col <= row)


def _flash_result(acc_ref, hd, heads):
    n = acc_ref.shape[0] // heads
    parts = []
    for sp in range(n):
        a = acc_ref[hd * n + sp]
        parts.append(a[:, 0:HEAD_DIM] / a[:, HEAD_DIM:2 * HEAD_DIM])
    return jnp.concatenate(parts, axis=0) if n > 1 else parts[0]


def _flash_scratch(heads, tq):
    n = max(1, tq // FLASH_ROWS)
    return [pltpu.VMEM((heads * n, tq // n, 1), F32), pltpu.VMEM((heads * n, tq // n, 2 * HEAD_DIM), F32)]


def _flash_init(m_ref, acc_ref):
    m_ref[...] = jnp.full(m_ref.shape, NEG, F32)
    acc_ref[...] = jnp.zeros(acc_ref.shape, F32)


def _top_k_mask(work, index, k, axis=1):
    n = work.shape[axis]
    sel = jnp.zeros(work.shape, F32)
    for _ in range(k):
        m = jnp.max(work, axis=axis, keepdims=True)
        idx = jnp.min(jnp.where(work == m, index, n), axis=axis, keepdims=True)
        pick = index == jnp.where(m > -jnp.inf, idx, n)
        sel = jnp.where(pick, 1.0, sel)
        work = jnp.where(pick, -jnp.inf, work)
    return sel


def _key_block_table(s, block):
    return jnp.asarray((np.arange(128)[None, :] == (np.arange(s) // block)[:, None]) * -NEG, dtype=BF16)


def _rope_table_kernel(pos_ref, c_ref, cos128_ref, sin128_ref, cos64_ref, sup64_ref, sdn64_ref):
    pos = pos_ref[...]
    a128 = pos * c_ref[0:1, :]
    cos128_ref[...] = jnp.cos(a128)
    sin128_ref[...] = jnp.sin(a128) * c_ref[1:2, :]
    a64 = pos * c_ref[2:3, :]
    s64 = jnp.sin(a64)
    cos64_ref[...] = jnp.cos(a64)
    sup64_ref[...] = s64 * c_ref[3:4, :]
    sdn64_ref[...] = s64 * c_ref[4:5, :]


def _rope_consts():
    lane = np.arange(128)
    inv128 = (ROPE_THETA ** (-np.arange(0, 128, 2, dtype=np.float32) / 128)).astype(np.float32)
    inv64 = (ROPE_THETA ** (-np.arange(0, 64, 2, dtype=np.float32) / 64)).astype(np.float32)
    c = np.zeros((8, 128), np.float32)
    c[0] = inv128[lane % 64]
    c[1] = np.where(lane < 64, -1.0, 1.0)
    c[2] = inv64[(lane % 64) % 32]
    c[3] = np.where(lane % 64 >= 32, 1.0, 0.0)
    c[4] = np.where(lane % 64 < 32, -1.0, 0.0)
    return jnp.asarray(c)


def _rope_tables(positions):
    s = positions.shape[0]
    tm = min(ROW_TILE, s)
    pos = positions.astype(F32).reshape(s, 1)
    tab = jax.ShapeDtypeStruct((s, 128), F32)
    row = pl.BlockSpec((tm, 128), lambda i: (i, 0))
    return pl.pallas_call(
        _rope_table_kernel,
        grid=(s // tm,),
        in_specs=[pl.BlockSpec((tm, 1), lambda i: (i, 0)), pl.BlockSpec((8, 128), lambda i: (0, 0))],
        out_specs=[row] * 5,
        out_shape=[tab] * 5,
        compiler_params=_cp("parallel"),
        name="rope_tables",
    )(pos, _rope_consts())


def _ffn_kernel(x_ref, wg_ref, wu_ref, wd_ref, g_ref, b_ref, o_ref, xb_ref):
    j = pl.program_id(1)

    @pl.when(j == 0)
    def _():
        xb_ref[...] = x_ref[...].astype(BF16)
        o_ref[...] = jnp.zeros_like(o_ref)

    xb = xb_ref[...]
    g = _dot(xb, wg_ref[...].astype(BF16))
    u = _dot(xb, wu_ref[...].astype(BF16))
    h = (g * jax.nn.sigmoid(g) * u).astype(BF16)
    o_ref[...] += _dot(h, wd_ref[...].astype(BF16))

    @pl.when(j == pl.num_programs(1) - 1)
    def _():
        y = DN_ALPHA * x_ref[...] + 0.5 * o_ref[...]
        o_ref[...] = _layer_norm(y, g_ref[...], b_ref[...])


def _ffn_ln(x, w_gu, w_down, g, b):
    s, d = x.shape
    f = _wshape(w_down)[0]
    tm = min(FFN_ROWS, s)
    tf = FFN_COLS
    nf = f // tf
    return pl.pallas_call(
        _ffn_kernel,
        grid=(s // tm, nf),
        in_specs=[
            pl.BlockSpec((tm, d), lambda i, j: (i, 0), pipeline_mode=pl.Buffered(1)),
            _wspec(w_gu, (d, tf), lambda i, j: (0, j), resident=False),
            _wspec(w_gu, (d, tf), lambda i, j: (0, j + nf), resident=False),
            _wspec(w_down, (tf, d), lambda i, j: (j, 0), resident=False),
            _wspec(g, (1, d), lambda i, j: (0, 0)),
            _wspec(b, (1, d), lambda i, j: (0, 0)),
        ],
        out_specs=pl.BlockSpec((tm, d), lambda i, j: (i, 0)),
        out_shape=jax.ShapeDtypeStruct((s, d), F32),
        scratch_shapes=[pltpu.VMEM((tm, d), BF16)],
        compiler_params=_cp("parallel", "arbitrary"),
        name="ffn_ln",
    )(x, w_gu[0], w_gu[0], w_down[0], g[0], b[0])


def _proj_a_kernel(x_ref, w_ref, cos_ref, sin_ref, kb_ref, q_ref, k_ref, v_ref, km_ref):
    h = _dot(x_ref[...].astype(BF16), w_ref[...])
    cos, sin = cos_ref[...], sin_ref[...]
    nblk = km_ref.shape[0]
    for hd in range(4):
        lo = hd * HEAD_DIM
        q_ref[:, lo:lo + HEAD_DIM] = _rope128(h[:, lo:lo + HEAD_DIM], cos, sin)
        k = _rope128(h[:, 512 + lo:512 + lo + HEAD_DIM], cos, sin)
        k_ref[:, 2 * lo:2 * lo + HEAD_DIM] = k.astype(BF16)
        k_ref[:, 2 * lo + HEAD_DIM:2 * lo + 2 * HEAD_DIM] = kb_ref[...]
        for bi in range(nblk):
            kb = k[bi * MOBA_BLOCK:(bi + 1) * MOBA_BLOCK]
            km_ref[bi, :, lo:lo + HEAD_DIM] = jnp.mean(kb, axis=0, keepdims=True)
    v_ref[...] = h[:, 1024:1536].astype(BF16)


def _proj_a(x, w, cos128, sin128):
    s, d = x.shape
    tm = min(ROW_TILE, s)
    nblk = tm // MOBA_BLOCK
    row512 = pl.BlockSpec((tm, 512), lambda i: (i, 0))
    tab = pl.BlockSpec((tm, 128), lambda i: (i, 0))
    return pl.pallas_call(
        _proj_a_kernel,
        grid=(s // tm,),
        in_specs=[pl.BlockSpec((tm, d), lambda i: (i, 0)), _wspec(w, _wshape(w), lambda i: (0, 0)), tab, tab, tab],
        out_specs=[row512, pl.BlockSpec((tm, 1024), lambda i: (i, 0)), row512,
                   pl.BlockSpec((nblk, 1, 512), lambda i: (i, 0, 0))],
        out_shape=[
            jax.ShapeDtypeStruct((s, 512), F32),
            jax.ShapeDtypeStruct((s, 1024), BF16),
            jax.ShapeDtypeStruct((s, 512), BF16),
            jax.ShapeDtypeStruct((s // MOBA_BLOCK, 1, 512), F32),
        ],
        compiler_params=_cp("parallel"),
        name="proj_moba",
    )(x, w[0], cos128, sin128, _key_block_table(s, MOBA_BLOCK))


def _rms_norm(x, g):
    return x * lax.rsqrt(jnp.mean(x * x, axis=-1, keepdims=True) + RMS_EPS) * g


def _proj_b_kernel(x_ref, w_ref, gq_ref, gkv_ref, wuq_ref, wukv_ref, cos_ref, sup_ref, sdn_ref,
                   q_ref, k_ref, v_ref):
    h = _dot(x_ref[...].astype(BF16), w_ref[...])
    cos, sup, sdn = cos_ref[...], sup_ref[...], sdn_ref[...]
    cq = _rms_norm(h[:, 0:512], gq_ref[...]).astype(BF16)
    ckv = _rms_norm(h[:, 512:1024], gkv_ref[...]).astype(BF16)
    kr = _rope64(h[:, 1024:1152], cos, sup, sdn).astype(BF16)
    q = _dot(cq, wuq_ref[...])
    kv = _dot(ckv, wukv_ref[...])
    qs = _log2_scale(MLA_NOPE + MLA_ROPE)
    for hd in range(4):
        lo = hd * 256
        q_ref[:, lo:lo + 128] = (q[:, lo:lo + 128] * qs).astype(BF16)
        q_ref[:, lo + 128:lo + 256] = (_rope64(q[:, lo + 128:lo + 256], cos, sup, sdn) * qs).astype(BF16)
        k_ref[:, lo:lo + 128] = kv[:, lo:lo + 128].astype(BF16)
        k_ref[:, lo + 128:lo + 256] = kr
        v_ref[:, hd * 128:(hd + 1) * 128] = kv[:, lo + 128:lo + 256].astype(BF16)


def _proj_b(x, w, g_cq, g_ckv, w_uq, w_ukv, cos64, sup64, sdn64):
    s, d = x.shape
    tm = min(ROW_TILE, s)
    tab = pl.BlockSpec((tm, 128), lambda i: (i, 0))
    full = lambda a: _wspec(a, _wshape(a), lambda i: (0, 0))
    return pl.pallas_call(
        _proj_b_kernel,
        grid=(s // tm,),
        in_specs=[pl.BlockSpec((tm, d), lambda i: (i, 0)), full(w), full(g_cq), full(g_ckv), full(w_uq),
                  full(w_ukv), tab, tab, tab],
        out_specs=[pl.BlockSpec((tm, 1024), lambda i: (i, 0)), pl.BlockSpec((tm, 1024), lambda i: (i, 0)),
                   pl.BlockSpec((tm, 512), lambda i: (i, 0))],
        out_shape=[jax.ShapeDtypeStruct((s, 1024), BF16), jax.ShapeDtypeStruct((s, 1024), BF16),
                   jax.ShapeDtypeStruct((s, 512), BF16)],
        compiler_params=_cp("parallel"),
        name="proj_mla",
    )(x, w[0], g_cq[0], g_ckv[0], w_uq[0], w_ukv[0], cos64, sup64, sdn64)


def _proj_c_kernel(x_ref, w_ref, cos_ref, sin_ref, kb_ref, q_ref, qr_ref, kvcmp_ref, kslc_ref, vslc_ref,
                   kwin_ref, vwin_ref, gate_ref):
    h = _dot(x_ref[...].astype(BF16), w_ref[...])
    cos, sin = cos_ref[...], sin_ref[...]
    for hd in range(4):
        lo = hd * HEAD_DIM
        qh = h[:, lo:lo + HEAD_DIM]
        q_ref[:, lo:lo + HEAD_DIM] = qh.astype(BF16)
        qr_ref[:, lo:lo + HEAD_DIM] = (_rope128(qh, cos, sin) * _log2_scale(HEAD_DIM)).astype(BF16)
    kv = lambda i: h[:, 512 + i * 128:512 + (i + 1) * 128]
    kvcmp_ref[0] = kv(0).astype(BF16)
    kvcmp_ref[1] = kv(1).astype(BF16)
    kslc_ref[:, 0:HEAD_DIM] = _rope128(kv(2), cos, sin).astype(BF16)
    kslc_ref[:, HEAD_DIM:2 * HEAD_DIM] = kb_ref[...]
    vslc_ref[...] = kv(3).astype(BF16)
    kwin_ref[...] = _rope128(kv(4), cos, sin).astype(BF16)
    vwin_ref[...] = kv(5).astype(BF16)
    gate_ref[...] = jax.nn.sigmoid(h[:, 1280:1408])


def _proj_c(x, w, cos128, sin128):
    s, d = x.shape
    tm = min(ROW_TILE, s)
    r512 = pl.BlockSpec((tm, 512), lambda i: (i, 0))
    r128 = pl.BlockSpec((tm, 128), lambda i: (i, 0))
    b512 = jax.ShapeDtypeStruct((s, 512), BF16)
    b128 = jax.ShapeDtypeStruct((s, 128), BF16)
    return pl.pallas_call(
        _proj_c_kernel,
        grid=(s // tm,),
        in_specs=[pl.BlockSpec((tm, d), lambda i: (i, 0)), _wspec(w, _wshape(w), lambda i: (0, 0)), r128, r128, r128],
        out_specs=[r512, r512, pl.BlockSpec((2, tm, 128), lambda i: (0, i, 0)),
                   pl.BlockSpec((tm, 256), lambda i: (i, 0))] + [r128] * 4,
        out_shape=[b512, b512, jax.ShapeDtypeStruct((2, s, 128), BF16), jax.ShapeDtypeStruct((s, 256), BF16)]
                  + [b128] * 3 + [jax.ShapeDtypeStruct((s, 128), F32)],
        compiler_params=_cp("parallel"),
        name="proj_nsa",
    )(x, w[0], cos128, sin128, _key_block_table(s, NSA_SEL_BLOCK))


def _proj_d_kernel(x_ref, w_ref, cos_ref, sin_ref, c64_ref, sup_ref, sdn_ref,
                   q_ref, k_ref, v_ref, iq_ref, ik_ref, iw_ref):
    h = _dot(x_ref[...].astype(BF16), w_ref[...])
    cos, sin = cos_ref[...], sin_ref[...]
    c64, sup, sdn = c64_ref[...], sup_ref[...], sdn_ref[...]
    for hd in range(4):
        lo = hd * HEAD_DIM
        q_ref[:, lo:lo + HEAD_DIM] = (_rope128(h[:, lo:lo + HEAD_DIM], cos, sin) * _log2_scale(HEAD_DIM)).astype(BF16)
        k_ref[:, lo:lo + HEAD_DIM] = _rope128(h[:, 512 + lo:512 + lo + HEAD_DIM], cos, sin).astype(BF16)
    v_ref[...] = h[:, 1024:1536].astype(BF16)
    for p in range(8):
        lo = 1536 + p * 128
        iq_ref[:, p * 128:(p + 1) * 128] = _rope64(h[:, lo:lo + 128], c64, sup, sdn).astype(BF16)
    ik = _rope64(h[:, 2560:2688], c64, sup, sdn)
    ik_ref[:, 0:128] = ik.astype(BF16)
    ik_ref[:, 128:256] = pltpu.roll(ik, 64, 1).astype(BF16)
    iw_ref[...] = h[:, 2688:2816] * (1.0 / 32.0)


def _proj_d(x, w, cos128, sin128, cos64, sup64, sdn64):
    s, d = x.shape
    tm = min(ROW_TILE, s)
    r = lambda n: pl.BlockSpec((tm, n), lambda i: (i, 0))
    return pl.pallas_call(
        _proj_d_kernel,
        grid=(s // tm,),
        in_specs=[pl.BlockSpec((tm, d), lambda i: (i, 0)), _wspec(w, _wshape(w), lambda i: (0, 0))] + [r(128)] * 5,
        out_specs=[r(512), r(512), r(512), r(1024), r(256), r(128)],
        out_shape=[jax.ShapeDtypeStruct((s, 512), BF16)] * 3 + [
            jax.ShapeDtypeStruct((s, 1024), BF16), jax.ShapeDtypeStruct((s, 256), BF16),
            jax.ShapeDtypeStruct((s, 128), F32)],
        compiler_params=_cp("parallel"),
        name="proj_dsa",
    )(x, w[0], cos128, sin128, cos64, sup64, sdn64)


def _moba_kernel(q_ref, k_ref, v_ref, km_ref, o_ref, m_ref, acc_ref, *, tk, nb):
    i = pl.program_id(1)
    tq = q_ref.shape[0]
    q32 = q_ref[...]

    gate = lax.dot_general(q32, km_ref[...], (((1,), (1,)), ((), ())),
                           precision=lax.Precision.HIGHEST, preferred_element_type=F32)
    blk = lax.broadcasted_iota(I32, (128, tq), 0)
    cur = (i * tq + lax.broadcasted_iota(I32, (128, tq), 1)) // MOBA_BLOCK
    sel = _top_k_mask(jnp.where(blk < cur, gate.T, -jnp.inf), blk, min(MOBA_TOPK, nb - 1), axis=0)
    unchosen = (jnp.where(blk == cur, 1.0, sel) - 1.0).T
    qa = jnp.concatenate([q32 * _log2_scale(HEAD_DIM), unchosen], axis=1).astype(BF16)

    _flash_init(m_ref, acc_ref)
    _causal_flash(qa, k_ref, v_ref, m_ref, acc_ref, i, tk)
    o_ref[...] = _flash_result(acc_ref, 0, 1).astype(o_ref.dtype)


def _moba(q32, k, v, kmean):
    s = q32.shape[0]
    nb = s // MOBA_BLOCK
    tq = min(ATT_TQ, s)
    tk = min(CAUSAL_TK, s)
    km = jnp.pad(kmean.reshape(nb, 512), ((0, 128 - nb), (0, 0)))
    return pl.pallas_call(
        functools.partial(_moba_kernel, tk=tk, nb=nb),
        grid=(4, s // tq),
        in_specs=[
            pl.BlockSpec((tq, HEAD_DIM), lambda h, i: (i, h)),
            pl.BlockSpec((s, 2 * HEAD_DIM), lambda h, i: (0, h)),
            pl.BlockSpec((s, HEAD_DIM), lambda h, i: (0, h)),
            pl.BlockSpec((128, HEAD_DIM), lambda h, i: (0, h)),
        ],
        out_specs=pl.BlockSpec((tq, HEAD_DIM), lambda h, i: (i, h)),
        out_shape=jax.ShapeDtypeStruct((s, 512), BF16),
        scratch_shapes=_flash_scratch(1, tq),
        compiler_params=_cp("parallel", "arbitrary"),
        name="moba_attn",
    )(q32, k, v, km)


def _mla_kernel(q_ref, k_ref, v_ref, o_ref, m_ref, acc_ref, *, tk):
    _flash_init(m_ref, acc_ref)
    _causal_flash(q_ref[...], k_ref, v_ref, m_ref, acc_ref, pl.program_id(1), tk)
    o_ref[...] = _flash_result(acc_ref, 0, 1).astype(o_ref.dtype)


def _mla(qcat, kcat, v):
    s = qcat.shape[0]
    tq = min(ATT_TQ, s)
    tk = min(CAUSAL_TK, s)
    return pl.pallas_call(
        functools.partial(_mla_kernel, tk=tk),
        grid=(4, s // tq),
        in_specs=[
            pl.BlockSpec((tq, 256), lambda h, i: (i, h)),
            pl.BlockSpec((s, 256), lambda h, i: (0, h)),
            pl.BlockSpec((s, HEAD_DIM), lambda h, i: (0, h)),
        ],
        out_specs=pl.BlockSpec((tq, HEAD_DIM), lambda h, i: (i, h)),
        out_shape=jax.ShapeDtypeStruct((s, 512), BF16),
        scratch_shapes=_flash_scratch(1, tq),
        compiler_params=_cp("parallel", "arbitrary"),
        name="mla_attn",
    )(qcat, kcat, v)


def _gelu_tanh(x):
    return 0.5 * x * (1.0 + jnp.tanh(0.7978845608028654 * (x + 0.044715 * x * x * x)))


def _nsa_compress_kernel(t_ref, pe_ref, w1_ref, w2_ref, o_ref):
    t = t_ref[0]
    w1 = w1_ref[...]
    half = t.shape[1]
    n = t.shape[0]
    lo = _dot(t, w1[0:half])
    hi = _dot(t, w1[half:2 * half])
    pe = _dot(pe_ref[...].astype(BF16), w1)
    pre = lo + pltpu.roll(hi, n - 1, 0) + pe
    o_ref[0] = _dot(_gelu_tanh(pre).astype(BF16), w2_ref[...]).astype(o_ref.dtype)


def _nsa_compress(kv16, pe, w1, w2):
    n = kv16.shape[1]
    sub = lambda a: _wspec(a, (None,) + _wshape(a)[1:], lambda i: (i, 0, 0), resident=False)
    return pl.pallas_call(
        _nsa_compress_kernel,
        grid=(2,),
        in_specs=[pl.BlockSpec((1,) + kv16.shape[1:], lambda i: (i, 0, 0)), sub(pe), sub(w1), sub(w2)],
        out_specs=pl.BlockSpec((1, n, HEAD_DIM), lambda i: (i, 0, 0)),
        out_shape=jax.ShapeDtypeStruct((2, n, HEAD_DIM), BF16),
        compiler_params=_cp("parallel"),
        name="nsa_compress",
    )(kv16, pe[0], w1[0], w2[0])


def _nsa_cmp_kernel(q_ref, kvc_ref, m_ref, o_ref, sel_ref, *, n_sel):
    i = pl.program_id(0)
    tq = q_ref.shape[0]
    n = kvc_ref.shape[1]
    nblk = m_ref.shape[1]
    scale = HEAD_DIM ** -0.5
    kc, vc = kvc_ref[0], kvc_ref[1]
    t_pos = i * tq + lax.broadcasted_iota(I32, (tq, n), 0)
    cmp_end = lax.broadcasted_iota(I32, (tq, n), 1) * NSA_CMP_STRIDE + (2 * NSA_CMP_STRIDE - 1)
    ok = cmp_end <= t_pos
    p_sum = jnp.zeros((tq, n), F32)
    for hd in range(4):
        lo = hd * HEAD_DIM
        s = jnp.where(ok, _dot_nt(q_ref[:, lo:lo + HEAD_DIM], kc) * scale, NEG)
        m = jnp.max(s, axis=1, keepdims=True)
        e = jnp.where(ok, jnp.exp(s - m), 0.0)
        l = jnp.sum(e, axis=1, keepdims=True)
        p = e / jnp.where(l > 0, l, 1.0)
        p_sum = p_sum + p
        o_ref[:, lo:lo + HEAD_DIM] = _dot(p.astype(BF16), vc)

    imp = jnp.dot(p_sum, m_ref[...], precision=lax.Precision.HIGHEST, preferred_element_type=F32)
    imp_t = imp.T
    blk = lax.broadcasted_iota(I32, (nblk, tq), 0)
    cur = (i * tq + lax.broadcasted_iota(I32, (nblk, tq), 1)) // NSA_SEL_BLOCK
    forced = (blk == 0) | (blk == cur) | (blk == cur - 1)
    imp_t = jnp.where(blk > cur, -jnp.inf, jnp.where(forced, jnp.inf, imp_t))
    sel_ref[...] = (_top_k_mask(imp_t, blk, n_sel, axis=0).T - 1.0).astype(sel_ref.dtype)


def _nsa_cmp(q, kvc, imp_map):
    s = q.shape[0]
    n = kvc.shape[1]
    nblk = imp_map.shape[1]
    tq = min(256, s)
    return pl.pallas_call(
        functools.partial(_nsa_cmp_kernel, n_sel=min(NSA_SEL_TOPK, s // NSA_SEL_BLOCK)),
        grid=(s // tq,),
        in_specs=[pl.BlockSpec((tq, 512), lambda i: (i, 0)),
                  _resident((2, n, HEAD_DIM), lambda i: (0, 0, 0)),
                  _resident((n, nblk), lambda i: (0, 0))],
        out_specs=[pl.BlockSpec((tq, 512), lambda i: (i, 0)), pl.BlockSpec((tq, nblk), lambda i: (i, 0))],
        out_shape=[jax.ShapeDtypeStruct((s, 512), F32), jax.ShapeDtypeStruct((s, nblk), BF16)],
        compiler_params=_cp("parallel"),
        name="nsa_cmp_select",
    )(q, kvc, imp_map)


def _nsa_slc_win_kernel(q_ref, sel_ref, ks_ref, vs_ref, kw_ref, vw_ref, g_ref, oc_ref, o_ref,
                        m_ref, acc_ref, *, tk):
    i = pl.program_id(0)
    tq = q_ref.shape[0]
    _flash_init(m_ref, acc_ref)

    def tile(j, causal):
        off = pl.multiple_of(j * tk, tk)
        keep = None
        if causal:
            row = i * tq + lax.broadcasted_iota(I32, (tq, tk), 0)
            col = off + lax.broadcasted_iota(I32, (tq, tk), 1)
            keep = col <= row
        k = ks_ref[pl.ds(off, tk), 0:HEAD_DIM]
        v1 = _with_ones(vs_ref[pl.ds(off, tk), :])
        bias = _dot_nt(sel_ref[...], ks_ref[pl.ds(off, tk), HEAD_DIM:2 * HEAD_DIM])
        _flash_heads([q_ref[:, hd * HEAD_DIM:(hd + 1) * HEAD_DIM] for hd in range(4)], [k] * 4, [v1] * 4,
                     bias, keep, m_ref, acc_ref)

    n_past = (i * tq) // tk

    def body(j, carry):
        tile(j, False)
        return carry

    lax.fori_loop(0, n_past, body, 0)
    tile(n_past, True)

    wlen = NSA_WINDOW + tq
    start = pl.multiple_of(jnp.maximum(i * tq - NSA_WINDOW, 0), 256)
    kw = kw_ref[pl.ds(start, wlen), :]
    vw1 = _with_ones(vw_ref[pl.ds(start, wlen), :])
    diff = (i * tq + lax.broadcasted_iota(I32, (tq, wlen), 0)) - (start + lax.broadcasted_iota(I32, (tq, wlen), 1))
    in_win = (diff >= 0) & (diff < NSA_WINDOW)
    g = g_ref[...]
    for hd in range(4):
        lo = hd * HEAD_DIM
        s = jnp.where(in_win, _dot_nt(q_ref[:, lo:lo + HEAD_DIM], kw), NEG)
        p = jnp.exp2(s - jnp.max(s, axis=1, keepdims=True))
        pv = _dot(p.astype(BF16), vw1)
        o_win = pv[:, 0:HEAD_DIM] / pv[:, HEAD_DIM:2 * HEAD_DIM]
        o_slc = _flash_result(acc_ref, hd, 4)
        out = (g[:, 3 * hd:3 * hd + 1] * oc_ref[:, lo:lo + HEAD_DIM] + g[:, 3 * hd + 1:3 * hd + 2] * o_slc
               + g[:, 3 * hd + 2:3 * hd + 3] * o_win)
        o_ref[:, lo:lo + HEAD_DIM] = out.astype(o_ref.dtype)


def _nsa_slc_win(q_r, sel, kslc, vslc, kwin, vwin, gates, o_cmp):
    s = q_r.shape[0]
    tq = min(ATT_TQ, s)
    tk = min(ATT_TK, s)
    full = lambda a: _resident(a.shape, lambda i: (0,) * a.ndim)
    return pl.pallas_call(
        functools.partial(_nsa_slc_win_kernel, tk=tk),
        grid=(s // tq,),
        in_specs=[pl.BlockSpec((tq, 512), lambda i: (i, 0)), pl.BlockSpec((tq, 128), lambda i: (i, 0)),
                  full(kslc), full(vslc), full(kwin), full(vwin),
                  pl.BlockSpec((tq, 128), lambda i: (i, 0)), pl.BlockSpec((tq, 512), lambda i: (i, 0))],
        out_specs=pl.BlockSpec((tq, 512), lambda i: (i, 0)),
        out_shape=jax.ShapeDtypeStruct((s, 512), BF16),
        scratch_shapes=_flash_scratch(4, tq),
        compiler_params=_cp("parallel"),
        name="nsa_slc_win",
    )(q_r, sel, kslc, vslc, kwin, vwin, gates, o_cmp)


def _dsa_kernel(q_ref, iq_ref, iw_ref, k_ref, v_ref, ik_ref, o_ref, key_ref, wb_ref, m_ref, acc_ref, *, tk):
    i = pl.program_id(0)
    tq = q_ref.shape[0]
    half = tk // 2
    n_tiles = (i * tq + tq + tk - 1) // tk
    row_h = lax.broadcasted_iota(I32, (tq, half), 0) + i * tq
    col_h = lax.broadcasted_iota(I32, (tq, half), 1)

    w = iw_ref[...]
    for hd in range(DSA_IDX_HEADS):
        wb_ref[hd] = jnp.broadcast_to(w[:, hd:hd + 1], (tq, 128))

    def score_body(j, carry):
        for hf in range(2):
            off = pl.multiple_of(j * tk + hf * half, half)
            ik_even = ik_ref[pl.ds(off, half), 0:128]
            ik_odd = ik_ref[pl.ds(off, half), 128:256]
            sc = jnp.zeros((tq, half), F32)
            for p in range(DSA_IDX_HEADS // 2):
                x = iq_ref[:, p * 128:(p + 1) * 128]
                we = jnp.concatenate([wb_ref[2 * p]] * (half // 128), axis=1)
                wo = jnp.concatenate([wb_ref[2 * p + 1]] * (half // 128), axis=1)
                sc = sc + jnp.maximum(_dot_nt(x, ik_even), 0.0) * we
                sc = sc + jnp.maximum(_dot_nt(x, ik_odd), 0.0) * wo
            bits = pltpu.bitcast(sc, I32)
            key = jnp.where(bits >= 0, bits, bits ^ 0x7FFFFFFF)
            key_ref[j, :, hf * half:(hf + 1) * half] = jnp.where(col_h + off <= row_h, key, INT_MIN)
        return carry

    lax.fori_loop(0, n_tiles, score_body, 0)

    rb = min(DSA_COUNT_ROWS, tq)

    ones = jnp.ones((128, 128), BF16)
    k_f = float(DSA_TOPK)

    def count_ge(cand128):
        parts = []
        for r0 in range(0, tq, rb):
            cand_b = cand128[r0:r0 + rb]

            def body(j, acc):
                for c in range(tk // 128):
                    acc = acc + jnp.where(key_ref[j, r0:r0 + rb, c * 128:(c + 1) * 128] >= cand_b, 1, 0)
                return acc

            parts.append(lax.fori_loop(0, n_tiles, body, jnp.zeros((rb, 128), I32)))
        per_lane = jnp.concatenate(parts, axis=0).astype(F32).astype(BF16)
        return _dot(per_lane, ones)

    zero = jnp.zeros((tq, 128), I32)
    c0 = count_ge(zero)
    lo0 = jnp.where(c0 >= k_f, zero, INT_MIN)
    c_lo0 = jnp.where(c0 >= k_f, c0, -1.0)

    def bit_cond(state):
        b, _, _, more = state
        return (b < 31) & (more > 0.0)

    def bit_body(state):
        b, lo, c_lo, _ = state
        more = jnp.max(jnp.where(c_lo != k_f, 1.0, 0.0))
        cand = lo | lax.shift_left(jnp.int32(1), 30 - b)
        c = count_ge(cand)
        take = c >= k_f
        return b + 1, jnp.where(take, cand, lo), jnp.where(take, c, c_lo), more

    _, lo, _, _ = lax.while_loop(bit_cond, bit_body, (jnp.int32(0), lo0, c_lo0, jnp.float32(1.0)))
    thr128 = jnp.maximum(lo, INT_MIN + 1)

    _flash_init(m_ref, acc_ref)

    def attn_body(j, carry):
        off = pl.multiple_of(j * tk, tk)
        keep = key_ref[j] >= jnp.concatenate([thr128] * (tk // 128), axis=1)
        cols = [slice(hd * HEAD_DIM, (hd + 1) * HEAD_DIM) for hd in range(4)]
        _flash_heads([q_ref[:, c] for c in cols], [k_ref[pl.ds(off, tk), c] for c in cols],
                     [_with_ones(v_ref[pl.ds(off, tk), c]) for c in cols], None, keep, m_ref, acc_ref)
        return carry

    lax.fori_loop(0, n_tiles, attn_body, 0)
    for hd in range(4):
        o_ref[:, hd * HEAD_DIM:(hd + 1) * HEAD_DIM] = _flash_result(acc_ref, hd, 4).astype(o_ref.dtype)


def _dsa(q, k, v, iq, ik2, iw):
    s = q.shape[0]
    tq = min(DSA_TQ, s)
    tk = min(ATT_TK, s)
    full = lambda a: _resident(a.shape, lambda i: (0, 0))
    return pl.pallas_call(
        functools.partial(_dsa_kernel, tk=tk),
        grid=(s // tq,),
        in_specs=[pl.BlockSpec((tq, 512), lambda i: (i, 0)), pl.BlockSpec((tq, 1024), lambda i: (i, 0)),
                  pl.BlockSpec((tq, 128), lambda i: (i, 0)), full(k), full(v), full(ik2)],
        out_specs=pl.BlockSpec((tq, 512), lambda i: (i, 0)),
        out_shape=jax.ShapeDtypeStruct((s, 512), BF16),
        scratch_shapes=[pltpu.VMEM((s // tk, tq, tk), I32), pltpu.VMEM((DSA_IDX_HEADS, tq, 128), F32)]
                       + _flash_scratch(4, tq),
        compiler_params=_cp("parallel"),
        name="dsa_attn",
    )(q, iq, iw, k, v, ik2)


def _out_proj_kernel(oa_ref, ob_ref, oc_ref, od_ref, w_ref, x_ref, g_ref, b_ref, o_ref):
    mix = _dot(oa_ref[...], w_ref[0:512, :])
    mix = mix + _dot(ob_ref[...], w_ref[512:1024, :])
    mix = mix + _dot(oc_ref[...], w_ref[1024:1536, :])
    mix = mix + _dot(od_ref[...], w_ref[1536:2048, :])
    o_ref[...] = _layer_norm(DN_ALPHA * x_ref[...] + mix, g_ref[...], b_ref[...])


def _out_proj_ln(o_a, o_b, o_c, o_d, w_out, x, g, b):
    s, d = x.shape
    tm = min(ROW_TILE, s)
    r512 = pl.BlockSpec((tm, 512), lambda i: (i, 0))
    vec = lambda a: _wspec(a, (1, d), lambda i: (0, 0))
    return pl.pallas_call(
        _out_proj_kernel,
        grid=(s // tm,),
        in_specs=[r512, r512, r512, r512, _wspec(w_out, _wshape(w_out), lambda i: (0, 0)),
                  pl.BlockSpec((tm, d), lambda i: (i, 0)), vec(g), vec(b)],
        out_specs=pl.BlockSpec((tm, d), lambda i: (i, 0)),
        out_shape=jax.ShapeDtypeStruct((s, d), F32),
        compiler_params=_cp("parallel"),
        name="out_proj_ln",
    )(o_a, o_b, o_c, o_d, w_out[0], x, g[0], b[0])


def _mem_kv_kernel(mem_ref, w_ref, o_ref):
    o_ref[...] = _dot(mem_ref[...].astype(BF16), w_ref[...]).astype(o_ref.dtype)


def _mem_kv(mem, wkv):
    m, d = mem.shape
    n = _wshape(wkv)[1]
    return pl.pallas_call(
        _mem_kv_kernel,
        grid=(1,),
        in_specs=[pl.BlockSpec((m, d), lambda i: (0, 0)), _wspec(wkv, (d, n), lambda i: (0, 0))],
        out_specs=pl.BlockSpec((m, n), lambda i: (0, 0)),
        out_shape=jax.ShapeDtypeStruct((m, n), BF16),
        compiler_params=_cp("arbitrary"),
        name="mem_kv",
    )(mem, wkv[0])


def _mem_attn_kernel(x_ref, wq_ref, kv_ref, wo_ref, g_ref, b_ref, o_ref):
    x = x_ref[...]
    scale = HEAD_DIM ** -0.5
    q = _dot(x.astype(BF16), wq_ref[...]).astype(BF16)
    out = jnp.zeros(x.shape, F32)
    for hd in range(4):
        lo = hd * HEAD_DIM
        s = _dot_nt(q[:, lo:lo + HEAD_DIM], kv_ref[:, lo:lo + HEAD_DIM]) * scale
        p = jnp.exp(s - jnp.max(s, axis=1, keepdims=True))
        o = _dot(p.astype(BF16), kv_ref[:, 512 + lo:512 + lo + HEAD_DIM]) / jnp.sum(p, axis=1, keepdims=True)
        out = out + _dot(o.astype(BF16), wo_ref[lo:lo + HEAD_DIM, :])
    o_ref[...] = _layer_norm(DN_ALPHA * x + out, g_ref[...], b_ref[...])


def _mem_attn_ln(x, wq, kv, wo, g, b):
    s, d = x.shape
    tm = min(ROW_TILE, s)
    full = lambda a: _wspec(a, _wshape(a), lambda i: (0, 0))
    return pl.pallas_call(
        _mem_attn_kernel,
        grid=(s // tm,),
        in_specs=[pl.BlockSpec((tm, d), lambda i: (i, 0)), full(wq), _resident(kv.shape, lambda i: (0, 0)),
                  full(wo), full(g), full(b)],
        out_specs=pl.BlockSpec((tm, d), lambda i: (i, 0)),
        out_shape=jax.ShapeDtypeStruct((s, d), F32),
        compiler_params=_cp("parallel"),
        name="mem_attn_ln",
    )(x, wq[0], kv, wo[0], g[0], b[0])


def _pad_last(w, n):
    return jnp.pad(w, [(0, 0)] * (w.ndim - 1) + [(0, n - w.shape[-1])])


def _split_w_in(w_in):
    a = w_in[..., 0:1536]
    b = _pad_last(w_in[..., 1536:2624], 1152)
    c = _pad_last(w_in[..., 2624:3916], 1408)
    d = jnp.concatenate([w_in[..., 3916:6476], _pad_last(w_in[..., 6476:6540], 128),
                         _pad_last(w_in[..., 6540:6556], 128)], axis=-1)
    return tuple(t.astype(BF16) for t in (a, b, c, d))


def _pad_w_uq(w_uq):
    lead = w_uq.shape[:-1]
    w = _pad_last(w_uq.reshape(lead + (4, MLA_NOPE + MLA_ROPE)), 256)
    return w.reshape(lead + (4 * 256,)).astype(BF16)


def _nsa_importance_map(s):
    n = s // NSA_CMP_STRIDE
    nblk = s // NSA_SEL_BLOCK
    ni = np.arange(n)[:, None]
    bi = np.arange(nblk)[None, :]
    m = ((ni >= 4 * bi - 1) & (ni <= 4 * bi + 3)).astype(np.float32)
    return jnp.asarray(np.pad(m, ((0, 0), (0, 128 - nblk))))


def _mixer(x, tabs, imp_map, w_groups, w_out, g_cq, g_ckv, w_uq, w_ukv, cmp_pe, cmp_w1, cmp_w2, ln_g, ln_b):
    cos128, sin128, cos64, sup64, sdn64 = tabs
    s = x.shape[0]
    w_a, w_b, w_c, w_d = w_groups

    aq, ak, av, akm = _proj_a(x, w_a, cos128, sin128)
    o_a = _moba(aq, ak, av, akm)

    bq, bk, bv = _proj_b(x, w_b, g_cq, g_ckv, w_uq, w_ukv, cos64, sup64, sdn64)
    o_b = _mla(bq, bk, bv)

    cq, cqr, kvcmp, kslc, vslc, kwin, vwin, gates = _proj_c(x, w_c, cos128, sin128)
    kv16 = kvcmp.reshape(2, s // NSA_CMP_STRIDE, NSA_CMP_STRIDE * HEAD_DIM)
    kvc = _nsa_compress(kv16, cmp_pe, cmp_w1, cmp_w2)
    o_cmp, sel = _nsa_cmp(cq, kvc, imp_map)
    o_c = _nsa_slc_win(cqr, sel, kslc, vslc, kwin, vwin, gates, o_cmp)

    dq, dk, dv, diq, dik, diw = _proj_d(x, w_d, cos128, sin128, cos64, sup64, sdn64)
    o_d = _dsa(dq, dk, dv, diq, dik, diw)

    return _out_proj_ln(o_a, o_b, o_c, o_d, w_out, x, ln_g, ln_b)


def kernel(x, mem, positions, ln_g, ln_b, ffn_w_gu, ffn_w_down, w_in, w_out, mla_g_cq, mla_g_ckv, mla_w_uq,
           mla_w_ukv, nsa_cmp_pe, nsa_cmp_w1, nsa_cmp_w2, mem_wq, mem_wkv, mem_wo):
    batch, s, d = x.shape
    w_gu_b, w_dn_b = ffn_w_gu, ffn_w_down
    w_groups = _split_w_in(w_in)
    w_out_b = w_out.astype(BF16)
    w_uq_b, w_ukv_b = _pad_w_uq(mla_w_uq), mla_w_ukv.astype(BF16)
    g_cq, g_ckv = mla_g_cq[:, None, :], mla_g_ckv[:, None, :]
    pe = nsa_cmp_pe.reshape(DEPTH, 2, 1, -1)
    w1_b, w2_b = nsa_cmp_w1.astype(BF16), nsa_cmp_w2.astype(BF16)
    wq_b, wkv_b, wo_b = mem_wq.astype(BF16), mem_wkv.astype(BF16), mem_wo.astype(BF16)
    g4, b4 = ln_g[:, :, None, :], ln_b[:, :, None, :]
    imp_map = _nsa_importance_map(s)

    outs = []
    for bi in range(batch):
        xb = x[bi]
        tabs = _rope_tables(positions[bi])
        for l in range(DEPTH):
            at = lambda a, *lead: (a, (l,) + lead)
            xb = _ffn_ln(xb, at(w_gu_b, 0), at(w_dn_b, 0), at(g4, 0), at(b4, 0))
            xb = _mixer(xb, tabs, imp_map, tuple(at(w) for w in w_groups), at(w_out_b), at(g_cq), at(g_ckv),
                        at(w_uq_b), at(w_ukv_b), at(pe), at(w1_b), at(w2_b), at(g4, 1), at(b4, 1))
            kv = _mem_kv(mem[bi], at(wkv_b))
            xb = _mem_attn_ln(xb, at(wq_b), kv, at(wo_b), at(g4, 2), at(b4, 2))
            xb = _ffn_ln(xb, at(w_gu_b, 1), at(w_dn_b, 1), at(g4, 3), at(b4, 3))
        outs.append(xb)
    return jnp.stack(outs)
```

```python
import functools

import numpy as np
import jax
import jax.numpy as jnp
from jax import lax
from jax.experimental import pallas as pl
from jax.experimental.pallas import tpu as pltpu

F32 = jnp.float32
BF16 = jnp.bfloat16
I32 = jnp.int32

D_MODEL = 2048
DEPTH = 4
HEAD_DIM = 128
ROPE_THETA = 10000.0
LN_EPS = 1e-5
RMS_EPS = 1e-6

MOBA_BLOCK = 256
MOBA_TOPK = 3
MLA_NOPE = 128
MLA_ROPE = 64
NSA_CMP_STRIDE = 16
NSA_SEL_BLOCK = 64
NSA_SEL_TOPK = 16
NSA_WINDOW = 512
DSA_IDX_HEADS = 16
DSA_TOPK = 256
D_FF = 5632
DN_ALPHA = (2 * DEPTH) ** 0.25

NEG = -(2.0 ** 100)
LOG2_E = 1.4426950408889634
INT_MIN = -(2 ** 31)
VMEM_LIMIT = 56 * 1024 * 1024

ROW_TILE = 512
FFN_ROWS = 1024
FFN_COLS = 256
ATT_TQ = 512
ATT_TK = 1024
CAUSAL_TK = 2048
DSA_TQ = 256
DSA_COUNT_ROWS = 128


def _cp(*sem):
    return pltpu.CompilerParams(dimension_semantics=sem, vmem_limit_bytes=VMEM_LIMIT)


def _wspec(w, block, index_map, resident=True):
    _, lead = w
    shape = (None,) * len(lead) + tuple(block)
    imap = lambda *g: tuple(lead) + tuple(index_map(*g))
    if resident:
        return pl.BlockSpec(shape, imap, pipeline_mode=pl.Buffered(1))
    return pl.BlockSpec(shape, imap)


def _wshape(w):
    arr, lead = w
    return arr.shape[len(lead):]


def _resident(shape, index_map):
    return pl.BlockSpec(shape, index_map, pipeline_mode=pl.Buffered(1))


def _dot(a, b):
    return jnp.dot(a, b, preferred_element_type=F32)


def _dot_nt(a, b):
    return lax.dot_general(a, b, (((1,), (1,)), ((), ())), preferred_element_type=F32)


def _layer_norm(y, g, b):
    mu = jnp.mean(y, axis=-1, keepdims=True)
    d = y - mu
    var = jnp.mean(d * d, axis=-1, keepdims=True)
    return d * lax.rsqrt(var + LN_EPS) * g + b


def _rope128(x, cos, sin_signed):
    return x * cos + pltpu.roll(x, 64, 1) * sin_signed


def _rope64(x, cos, s_up, s_dn):
    return x * cos + pltpu.roll(x, 32, 1) * s_up + pltpu.roll(x, 96, 1) * s_dn


def _log2_scale(d):
    return float(d) ** -0.5 * LOG2_E


def _with_ones(v):
    return jnp.concatenate([v, jnp.ones_like(v)], axis=1)


def _flash_update(s, v1, bias, keep, m_ref, acc_ref, slot):
    if bias is not None:
        s = s + bias
    if keep is not None:
        s = jnp.where(keep, s, NEG)
    m_old = m_ref[slot]
    m_new = jnp.maximum(m_old, jnp.max(s, axis=1, keepdims=True))
    alpha = jnp.exp2(m_old - m_new)
    p = jnp.exp2(s - m_new)
    acc_ref[slot] = alpha * acc_ref[slot] + _dot(p.astype(BF16), v1)
    m_ref[slot] = m_new


FLASH_ROWS = 256


def _flash_rows(q, k, v1, bias, keep, m_ref, acc_ref, hd):
    n = max(1, q.shape[0] // FLASH_ROWS)
    r = q.shape[0] // n
    rows = [slice(sp * r, (sp + 1) * r) for sp in range(n)]
    logits = [_dot_nt(q[rw], k) for rw in rows]
    for sp, rw in enumerate(rows):
        _flash_update(logits[sp], v1, None if bias is None else bias[rw], None if keep is None else keep[rw],
                      m_ref, acc_ref, hd * n + sp)


def _flash_heads(qs, ks, v1s, bias, keep, m_ref, acc_ref):
    def logits(hd):
        q = qs[hd]
        n = max(1, q.shape[0] // FLASH_ROWS)
        r = q.shape[0] // n
        rows = [slice(sp * r, (sp + 1) * r) for sp in range(n)]
        return rows, [_dot_nt(q[rw], ks[hd]) for rw in rows]

    ahead = logits(0)
    for hd in range(len(qs)):
        rows, s = ahead
        if hd + 1 < len(qs):
            ahead = logits(hd + 1)
        for sp, rw in enumerate(rows):
            _flash_update(s[sp], v1s[hd], None if bias is None else bias[rw], None if keep is None else keep[rw],
                          m_ref, acc_ref, hd * len(rows) + sp)


def _causal_flash(q, k_ref, v_ref, m_ref, acc_ref, i, tk):
    tq = q.shape[0]
    n_past = (i * tq) // tk
    n_rest = (i * tq - n_past * tk) // tq

    def tile(off, width, keep):
        _flash_rows(q, k_ref[pl.ds(off, width), :], _with_ones(v_ref[pl.ds(off, width), :]), None, keep,
                    m_ref, acc_ref, 0)

    def body(j, carry):
        tile(pl.multiple_of(j * tk, tk), tk, None)
        return carry

    lax.fori_loop(0, n_past, body, 0)
    for r in range(tk // tq):
        @pl.when(n_rest == r)
        def _():
            width = (r + 1) * tq
            row = lax.broadcasted_iota(I32, (tq, width), 0) + r * tq
            col = lax.broadcasted_iota(I32, (tq, width), 1)
            tile(pl.multiple_of(n_past * tk, tk), width, col <= row)


def _flash_result(acc_ref, hd, heads):
    n = acc_ref.shape[0] // heads
    parts = []
    for sp in range(n):
        a = acc_ref[hd * n + sp]
        parts.append(a[:, 0:HEAD_DIM] / a[:, HEAD_DIM:2 * HEAD_DIM])
    return jnp.concatenate(parts, axis=0) if n > 1 else parts[0]


def _flash_scratch(heads, tq):
    n = max(1, tq // FLASH_ROWS)
    return [pltpu.VMEM((heads * n, tq // n, 1), F32), pltpu.VMEM((heads * n, tq // n, 2 * HEAD_DIM), F32)]


def _flash_init(m_ref, acc_ref):
    m_ref[...] = jnp.full(m_ref.shape, NEG, F32)
    acc_ref[...] = jnp.zeros(acc_ref.shape, F32)


def _top_k_mask(work, index, k, axis=1):
    n = work.shape[axis]
    sel = jnp.zeros(work.shape, F32)
    for _ in range(k):
        m = jnp.max(work, axis=axis, keepdims=True)
        idx = jnp.min(jnp.where(work == m, index, n), axis=axis, keepdims=True)
        pick = index == jnp.where(m > -jnp.inf, idx, n)
        sel = jnp.where(pick, 1.0, sel)
        work = jnp.where(pick, -jnp.inf, work)
    return sel


def _key_block_table(s, block):
    return jnp.asarray((np.arange(128)[None, :] == (np.arange(s) // block)[:, None]) * -NEG, dtype=BF16)


def _rope_table_kernel(pos_ref, c_ref, cos128_ref, sin128_ref, cos64_ref, sup64_ref, sdn64_ref):
    pos = pos_ref[...]
    a128 = pos * c_ref[0:1, :]
    cos128_ref[...] = jnp.cos(a128)
    sin128_ref[...] = jnp.sin(a128) * c_ref[1:2, :]
    a64 = pos * c_ref[2:3, :]
    s64 = jnp.sin(a64)
    cos64_ref[...] = jnp.cos(a64)
    sup64_ref[...] = s64 * c_ref[3:4, :]
    sdn64_ref[...] = s64 * c_ref[4:5, :]


def _rope_consts():
    lane = np.arange(128)
    inv128 = (ROPE_THETA ** (-np.arange(0, 128, 2, dtype=np.float32) / 128)).astype(np.float32)
    inv64 = (ROPE_THETA ** (-np.arange(0, 64, 2, dtype=np.float32) / 64)).astype(np.float32)
    c = np.zeros((8, 128), np.float32)
    c[0] = inv128[lane % 64]
    c[1] = np.where(lane < 64, -1.0, 1.0)
    c[2] = inv64[(lane % 64) % 32]
    c[3] = np.where(lane % 64 >= 32, 1.0, 0.0)
    c[4] = np.where(lane % 64 < 32, -1.0, 0.0)
    return jnp.asarray(c)


def _rope_tables(positions):
    s = positions.shape[0]
    tm = min(ROW_TILE, s)
    pos = positions.astype(F32).reshape(s, 1)
    tab = jax.ShapeDtypeStruct((s, 128), F32)
    row = pl.BlockSpec((tm, 128), lambda i: (i, 0))
    return pl.pallas_call(
        _rope_table_kernel,
        grid=(s // tm,),
        in_specs=[pl.BlockSpec((tm, 1), lambda i: (i, 0)), pl.BlockSpec((8, 128), lambda i: (0, 0))],
        out_specs=[row] * 5,
        out_shape=[tab] * 5,
        compiler_params=_cp("parallel"),
        name="rope_tables",
    )(pos, _rope_consts())


def _ffn_kernel(x_ref, wg_ref, wu_ref, wd_ref, g_ref, b_ref, o_ref, xb_ref):
    j = pl.program_id(1)

    @pl.when(j == 0)
    def _():
        xb_ref[...] = x_ref[...].astype(BF16)
        o_ref[...] = jnp.zeros_like(o_ref)

    xb = xb_ref[...]
    g = _dot(xb, wg_ref[...].astype(BF16))
    u = _dot(xb, wu_ref[...].astype(BF16))
    h = (g * jax.nn.sigmoid(g) * u).astype(BF16)
    o_ref[...] += _dot(h, wd_ref[...].astype(BF16))

    @pl.when(j == pl.num_programs(1) - 1)
    def _():
        y = DN_ALPHA * x_ref[...] + 0.5 * o_ref[...]
        o_ref[...] = _layer_norm(y, g_ref[...], b_ref[...])


def _ffn_ln(x, w_gu, w_down, g, b):
    s, d = x.shape
    f = _wshape(w_down)[0]
    tm = min(FFN_ROWS, s)
    tf = FFN_COLS
    nf = f // tf
    return pl.pallas_call(
        _ffn_kernel,
        grid=(s // tm, nf),
        in_specs=[
            pl.BlockSpec((tm, d), lambda i, j: (i, 0), pipeline_mode=pl.Buffered(1)),
            _wspec(w_gu, (d, tf), lambda i, j: (0, j), resident=False),
            _wspec(w_gu, (d, tf), lambda i, j: (0, j + nf), resident=False),
            _wspec(w_down, (tf, d), lambda i, j: (j, 0), resident=False),
            _wspec(g, (1, d), lambda i, j: (0, 0)),
            _wspec(b, (1, d), lambda i, j: (0, 0)),
        ],
        out_specs=pl.BlockSpec((tm, d), lambda i, j: (i, 0)),
        out_shape=jax.ShapeDtypeStruct((s, d), F32),
        scratch_shapes=[pltpu.VMEM((tm, d), BF16)],
        compiler_params=_cp("parallel", "arbitrary"),
        name="ffn_ln",
    )(x, w_gu[0], w_gu[0], w_down[0], g[0], b[0])


def _proj_a_kernel(x_ref, w_ref, cos_ref, sin_ref, kb_ref, q_ref, k_ref, v_ref, km_ref):
    h = _dot(x_ref[...].astype(BF16), w_ref[...])
    cos, sin = cos_ref[...], sin_ref[...]
    nblk = km_ref.shape[0]
    for hd in range(4):
        lo = hd * HEAD_DIM
        q_ref[:, lo:lo + HEAD_DIM] = _rope128(h[:, lo:lo + HEAD_DIM], cos, sin)
        k = _rope128(h[:, 512 + lo:512 + lo + HEAD_DIM], cos, sin)
        k_ref[:, 2 * lo:2 * lo + HEAD_DIM] = k.astype(BF16)
        k_ref[:, 2 * lo + HEAD_DIM:2 * lo + 2 * HEAD_DIM] = kb_ref[...]
        for bi in range(nblk):
            kb = k[bi * MOBA_BLOCK:(bi + 1) * MOBA_BLOCK]
            km_ref[bi, :, lo:lo + HEAD_DIM] = jnp.mean(kb, axis=0, keepdims=True)
    v_ref[...] = h[:, 1024:1536].astype(BF16)


def _proj_a(x, w, cos128, sin128):
    s, d = x.shape
    tm = min(ROW_TILE, s)
    nblk = tm // MOBA_BLOCK
    row512 = pl.BlockSpec((tm, 512), lambda i: (i, 0))
    tab = pl.BlockSpec((tm, 128), lambda i: (i, 0))
    return pl.pallas_call(
        _proj_a_kernel,
        grid=(s // tm,),
        in_specs=[pl.BlockSpec((tm, d), lambda i: (i, 0)), _wspec(w, _wshape(w), lambda i: (0, 0)), tab, tab, tab],
        out_specs=[row512, pl.BlockSpec((tm, 1024), lambda i: (i, 0)), row512,
                   pl.BlockSpec((nblk, 1, 512), lambda i: (i, 0, 0))],
        out_shape=[
            jax.ShapeDtypeStruct((s, 512), F32),
            jax.ShapeDtypeStruct((s, 1024), BF16),
            jax.ShapeDtypeStruct((s, 512), BF16),
            jax.ShapeDtypeStruct((s // MOBA_BLOCK, 1, 512), F32),
        ],
        compiler_params=_cp("parallel"),
        name="proj_moba",
    )(x, w[0], cos128, sin128, _key_block_table(s, MOBA_BLOCK))


def _rms_norm(x, g):
    return x * lax.rsqrt(jnp.mean(x * x, axis=-1, keepdims=True) + RMS_EPS) * g


def _proj_b_kernel(x_ref, w_ref, gq_ref, gkv_ref, wuq_ref, wukv_ref, cos_ref, sup_ref, sdn_ref,
                   q_ref, k_ref, v_ref):
    h = _dot(x_ref[...].astype(BF16), w_ref[...])
    cos, sup, sdn = cos_ref[...], sup_ref[...], sdn_ref[...]
    cq = _rms_norm(h[:, 0:512], gq_ref[...]).astype(BF16)
    ckv = _rms_norm(h[:, 512:1024], gkv_ref[...]).astype(BF16)
    kr = _rope64(h[:, 1024:1152], cos, sup, sdn).astype(BF16)
    q = _dot(cq, wuq_ref[...])
    kv = _dot(ckv, wukv_ref[...])
    qs = _log2_scale(MLA_NOPE + MLA_ROPE)
    for hd in range(4):
        lo = hd * 256
        q_ref[:, lo:lo + 128] = (q[:, lo:lo + 128] * qs).astype(BF16)
        q_ref[:, lo + 128:lo + 256] = (_rope64(q[:, lo + 128:lo + 256], cos, sup, sdn) * qs).astype(BF16)
        k_ref[:, lo:lo + 128] = kv[:, lo:lo + 128].astype(BF16)
        k_ref[:, lo + 128:lo + 256] = kr
        v_ref[:, hd * 128:(hd + 1) * 128] = kv[:, lo + 128:lo + 256].astype(BF16)


def _proj_b(x, w, g_cq, g_ckv, w_uq, w_ukv, cos64, sup64, sdn64):
    s, d = x.shape
    tm = min(ROW_TILE, s)
    tab = pl.BlockSpec((tm, 128), lambda i: (i, 0))
    full = lambda a: _wspec(a, _wshape(a), lambda i: (0, 0))
    return pl.pallas_call(
        _proj_b_kernel,
        grid=(s // tm,),
        in_specs=[pl.BlockSpec((tm, d), lambda i: (i, 0)), full(w), full(g_cq), full(g_ckv), full(w_uq),
                  full(w_ukv), tab, tab, tab],
        out_specs=[pl.BlockSpec((tm, 1024), lambda i: (i, 0)), pl.BlockSpec((tm, 1024), lambda i: (i, 0)),
                   pl.BlockSpec((tm, 512), lambda i: (i, 0))],
        out_shape=[jax.ShapeDtypeStruct((s, 1024), BF16), jax.ShapeDtypeStruct((s, 1024), BF16),
                   jax.ShapeDtypeStruct((s, 512), BF16)],
        compiler_params=_cp("parallel"),
        name="proj_mla",
    )(x, w[0], g_cq[0], g_ckv[0], w_uq[0], w_ukv[0], cos64, sup64, sdn64)


def _proj_c_kernel(x_ref, w_ref, cos_ref, sin_ref, kb_ref, q_ref, qr_ref, kvcmp_ref, kslc_ref, vslc_ref,
                   kwin_ref, vwin_ref, gate_ref):
    h = _dot(x_ref[...].astype(BF16), w_ref[...])
    cos, sin = cos_ref[...], sin_ref[...]
    for hd in range(4):
        lo = hd * HEAD_DIM
        qh = h[:, lo:lo + HEAD_DIM]
        q_ref[:, lo:lo + HEAD_DIM] = qh.astype(BF16)
        qr_ref[:, lo:lo + HEAD_DIM] = (_rope128(qh, cos, sin) * _log2_scale(HEAD_DIM)).astype(BF16)
    kv = lambda i: h[:, 512 + i * 128:512 + (i + 1) * 128]
    kvcmp_ref[0] = kv(0).astype(BF16)
    kvcmp_ref[1] = kv(1).astype(BF16)
    kslc_ref[:, 0:HEAD_DIM] = _rope128(kv(2), cos, sin).astype(BF16)
    kslc_ref[:, HEAD_DIM:2 * HEAD_DIM] = kb_ref[...]
    vslc_ref[...] = kv(3).astype(BF16)
    kwin_ref[...] = _rope128(kv(4), cos, sin).astype(BF16)
    vwin_ref[...] = kv(5).astype(BF16)
    gate_ref[...] = jax.nn.sigmoid(h[:, 1280:1408])


def _proj_c(x, w, cos128, sin128):
    s, d = x.shape
    tm = min(ROW_TILE, s)
    r512 = pl.BlockSpec((tm, 512), lambda i: (i, 0))
    r128 = pl.BlockSpec((tm, 128), lambda i: (i, 0))
    b512 = jax.ShapeDtypeStruct((s, 512), BF16)
    b128 = jax.ShapeDtypeStruct((s, 128), BF16)
    return pl.pallas_call(
        _proj_c_kernel,
        grid=(s // tm,),
        in_specs=[pl.BlockSpec((tm, d), lambda i: (i, 0)), _wspec(w, _wshape(w), lambda i: (0, 0)), r128, r128, r128],
        out_specs=[r512, r512, pl.BlockSpec((2, tm, 128), lambda i: (0, i, 0)),
                   pl.BlockSpec((tm, 256), lambda i: (i, 0))] + [r128] * 4,
        out_shape=[b512, b512, jax.ShapeDtypeStruct((2, s, 128), BF16), jax.ShapeDtypeStruct((s, 256), BF16)]
                  + [b128] * 3 + [jax.ShapeDtypeStruct((s, 128), F32)],
        compiler_params=_cp("parallel"),
        name="proj_nsa",
    )(x, w[0], cos128, sin128, _key_block_table(s, NSA_SEL_BLOCK))


def _proj_d_kernel(x_ref, w_ref, cos_ref, sin_ref, c64_ref, sup_ref, sdn_ref,
                   q_ref, k_ref, v_ref, iq_ref, ik_ref, iw_ref):
    h = _dot(x_ref[...].astype(BF16), w_ref[...])
    cos, sin = cos_ref[...], sin_ref[...]
    c64, sup, sdn = c64_ref[...], sup_ref[...], sdn_ref[...]
    for hd in range(4):
        lo = hd * HEAD_DIM
        q_ref[:, lo:lo + HEAD_DIM] = (_rope128(h[:, lo:lo + HEAD_DIM], cos, sin) * _log2_scale(HEAD_DIM)).astype(BF16)
        k_ref[:, lo:lo + HEAD_DIM] = _rope128(h[:, 512 + lo:512 + lo + HEAD_DIM], cos, sin).astype(BF16)
    v_ref[...] = h[:, 1024:1536].astype(BF16)
    for p in range(8):
        lo = 1536 + p * 128
        iq_ref[:, p * 128:(p + 1) * 128] = _rope64(h[:, lo:lo + 128], c64, sup, sdn).astype(BF16)
    ik = _rope64(h[:, 2560:2688], c64, sup, sdn)
    ik_ref[:, 0:128] = ik.astype(BF16)
    ik_ref[:, 128:256] = pltpu.roll(ik, 64, 1).astype(BF16)
    iw_ref[...] = h[:, 2688:2816] * (1.0 / 32.0)


def _proj_d(x, w, cos128, sin128, cos64, sup64, sdn64):
    s, d = x.shape
    tm = min(ROW_TILE, s)
    r = lambda n: pl.BlockSpec((tm, n), lambda i: (i, 0))
    return pl.pallas_call(
        _proj_d_kernel,
        grid=(s // tm,),
        in_specs=[pl.BlockSpec((tm, d), lambda i: (i, 0)), _wspec(w, _wshape(w), lambda i: (0, 0))] + [r(128)] * 5,
        out_specs=[r(512), r(512), r(512), r(1024), r(256), r(128)],
        out_shape=[jax.ShapeDtypeStruct((s, 512), BF16)] * 3 + [
            jax.ShapeDtypeStruct((s, 1024), BF16), jax.ShapeDtypeStruct((s, 256), BF16),
            jax.ShapeDtypeStruct((s, 128), F32)],
        compiler_params=_cp("parallel"),
        name="proj_dsa",
    )(x, w[0], cos128, sin128, cos64, sup64, sdn64)


def _moba_kernel(q_ref, k_ref, v_ref, km_ref, o_ref, m_ref, acc_ref, *, tk, nb):
    i = pl.program_id(1)
    tq = q_ref.shape[0]
    q32 = q_ref[...]

    gate = lax.dot_general(q32, km_ref[...], (((1,), (1,)), ((), ())),
                           precision=lax.Precision.HIGHEST, preferred_element_type=F32)
    blk = lax.broadcasted_iota(I32, (128, tq), 0)
    cur = (i * tq + lax.broadcasted_iota(I32, (128, tq), 1)) // MOBA_BLOCK
    sel = _top_k_mask(jnp.where(blk < cur, gate.T, -jnp.inf), blk, min(MOBA_TOPK, nb - 1), axis=0)
    unchosen = (jnp.where(blk == cur, 1.0, sel) - 1.0).T
    qa = jnp.concatenate([q32 * _log2_scale(HEAD_DIM), unchosen], axis=1).astype(BF16)

    _flash_init(m_ref, acc_ref)
    _causal_flash(qa, k_ref, v_ref, m_ref, acc_ref, i, tk)
    o_ref[...] = _flash_result(acc_ref, 0, 1).astype(o_ref.dtype)


def _moba(q32, k, v, kmean):
    s = q32.shape[0]
    nb = s // MOBA_BLOCK
    tq = min(ATT_TQ, s)
    tk = min(CAUSAL_TK, s)
    km = jnp.pad(kmean.reshape(nb, 512), ((0, 128 - nb), (0, 0)))
    return pl.pallas_call(
        functools.partial(_moba_kernel, tk=tk, nb=nb),
        grid=(4, s // tq),
        in_specs=[
            pl.BlockSpec((tq, HEAD_DIM), lambda h, i: (i, h)),
            pl.BlockSpec((s, 2 * HEAD_DIM), lambda h, i: (0, h)),
            pl.BlockSpec((s, HEAD_DIM), lambda h, i: (0, h)),
            pl.BlockSpec((128, HEAD_DIM), lambda h, i: (0, h)),
        ],
        out_specs=pl.BlockSpec((tq, HEAD_DIM), lambda h, i: (i, h)),
        out_shape=jax.ShapeDtypeStruct((s, 512), BF16),
        scratch_shapes=_flash_scratch(1, tq),
        compiler_params=_cp("parallel", "arbitrary"),
        name="moba_attn",
    )(q32, k, v, km)


def _mla_kernel(q_ref, k_ref, v_ref, o_ref, m_ref, acc_ref, *, tk):
    _flash_init(m_ref, acc_ref)
    _causal_flash(q_ref[...], k_ref, v_ref, m_ref, acc_ref, pl.program_id(1), tk)
    o_ref[...] = _flash_result(acc_ref, 0, 1).astype(o_ref.dtype)


def _mla(qcat, kcat, v):
    s = qcat.shape[0]
    tq = min(ATT_TQ, s)
    tk = min(CAUSAL_TK, s)
    return pl.pallas_call(
        functools.partial(_mla_kernel, tk=tk),
        grid=(4, s // tq),
        in_specs=[
            pl.BlockSpec((tq, 256), lambda h, i: (i, h)),
            pl.BlockSpec((s, 256), lambda h, i: (0, h)),
            pl.BlockSpec((s, HEAD_DIM), lambda h, i: (0, h)),
        ],
        out_specs=pl.BlockSpec((tq, HEAD_DIM), lambda h, i: (i, h)),
        out_shape=jax.ShapeDtypeStruct((s, 512), BF16),
        scratch_shapes=_flash_scratch(1, tq),
        compiler_params=_cp("parallel", "arbitrary"),
        name="mla_attn",
    )(qcat, kcat, v)


def _gelu_tanh(x):
    return 0.5 * x * (1.0 + jnp.tanh(0.7978845608028654 * (x + 0.044715 * x * x * x)))


def _nsa_compress_kernel(t_ref, pe_ref, w1_ref, w2_ref, o_ref):
    t = t_ref[0]
    w1 = w1_ref[...]
    half = t.shape[1]
    n = t.shape[0]
    lo = _dot(t, w1[0:half])
    hi = _dot(t, w1[half:2 * half])
    pe = _dot(pe_ref[...].astype(BF16), w1)
    pre = lo + pltpu.roll(hi, n - 1, 0) + pe
    o_ref[0] = _dot(_gelu_tanh(pre).astype(BF16), w2_ref[...]).astype(o_ref.dtype)


def _nsa_compress(kv16, pe, w1, w2):
    n = kv16.shape[1]
    sub = lambda a: _wspec(a, (None,) + _wshape(a)[1:], lambda i: (i, 0, 0), resident=False)
    return pl.pallas_call(
        _nsa_compress_kernel,
        grid=(2,),
        in_specs=[pl.BlockSpec((1,) + kv16.shape[1:], lambda i: (i, 0, 0)), sub(pe), sub(w1), sub(w2)],
        out_specs=pl.BlockSpec((1, n, HEAD_DIM), lambda i: (i, 0, 0)),
        out_shape=jax.ShapeDtypeStruct((2, n, HEAD_DIM), BF16),
        compiler_params=_cp("parallel"),
        name="nsa_compress",
    )(kv16, pe[0], w1[0], w2[0])


def _nsa_cmp_kernel(q_ref, kvc_ref, m_ref, o_ref, sel_ref, *, n_sel):
    i = pl.program_id(0)
    tq = q_ref.shape[0]
    n = kvc_ref.shape[1]
    nblk = m_ref.shape[1]
    scale = HEAD_DIM ** -0.5
    kc, vc = kvc_ref[0], kvc_ref[1]
    t_pos = i * tq + lax.broadcasted_iota(I32, (tq, n), 0)
    cmp_end = lax.broadcasted_iota(I32, (tq, n), 1) * NSA_CMP_STRIDE + (2 * NSA_CMP_STRIDE - 1)
    ok = cmp_end <= t_pos
    p_sum = jnp.zeros((tq, n), F32)
    for hd in range(4):
        lo = hd * HEAD_DIM
        s = jnp.where(ok, _dot_nt(q_ref[:, lo:lo + HEAD_DIM], kc) * scale, NEG)
        m = jnp.max(s, axis=1, keepdims=True)
        e = jnp.where(ok, jnp.exp(s - m), 0.0)
        l = jnp.sum(e, axis=1, keepdims=True)
        p = e / jnp.where(l > 0, l, 1.0)
        p_sum = p_sum + p
        o_ref[:, lo:lo + HEAD_DIM] = _dot(p.astype(BF16), vc)

    imp = jnp.dot(p_sum, m_ref[...], precision=lax.Precision.HIGHEST, preferred_element_type=F32)
    imp_t = imp.T
    blk = lax.broadcasted_iota(I32, (nblk, tq), 0)
    cur = (i * tq + lax.broadcasted_iota(I32, (nblk, tq), 1)) // NSA_SEL_BLOCK
    forced = (blk == 0) | (blk == cur) | (blk == cur - 1)
    imp_t = jnp.where(blk > cur, -jnp.inf, jnp.where(forced, jnp.inf, imp_t))
    sel_ref[...] = (_top_k_mask(imp_t, blk, n_sel, axis=0).T - 1.0).astype(sel_ref.dtype)


def _nsa_cmp(q, kvc, imp_map):
    s = q.shape[0]
    n = kvc.shape[1]
    nblk = imp_map.shape[1]
    tq = min(256, s)
    return pl.pallas_call(
        functools.partial(_nsa_cmp_kernel, n_sel=min(NSA_SEL_TOPK, s // NSA_SEL_BLOCK)),
        grid=(s // tq,),
        in_specs=[pl.BlockSpec((tq, 512), lambda i: (i, 0)),
                  _resident((2, n, HEAD_DIM), lambda i: (0, 0, 0)),
                  _resident((n, nblk), lambda i: (0, 0))],
        out_specs=[pl.BlockSpec((tq, 512), lambda i: (i, 0)), pl.BlockSpec((tq, nblk), lambda i: (i, 0))],
        out_shape=[jax.ShapeDtypeStruct((s, 512), F32), jax.ShapeDtypeStruct((s, nblk), BF16)],
        compiler_params=_cp("parallel"),
        name="nsa_cmp_select",
    )(q, kvc, imp_map)


def _nsa_slc_win_kernel(q_ref, sel_ref, ks_ref, vs_ref, kw_ref, vw_ref, g_ref, oc_ref, o_ref,
                        m_ref, acc_ref, *, tk):
    i = pl.program_id(0)
    tq = q_ref.shape[0]
    _flash_init(m_ref, acc_ref)

    def tile(j, causal):
        off = pl.multiple_of(j * tk, tk)
        keep = None
        if causal:
            row = i * tq + lax.broadcasted_iota(I32, (tq, tk), 0)
            col = off + lax.broadcasted_iota(I32, (tq, tk), 1)
            keep = col <= row
        k = ks_ref[pl.ds(off, tk), 0:HEAD_DIM]
        v1 = _with_ones(vs_ref[pl.ds(off, tk), :])
        bias = _dot_nt(sel_ref[...], ks_ref[pl.ds(off, tk), HEAD_DIM:2 * HEAD_DIM])
        _flash_heads([q_ref[:, hd * HEAD_DIM:(hd + 1) * HEAD_DIM] for hd in range(4)], [k] * 4, [v1] * 4,
                     bias, keep, m_ref, acc_ref)

    n_past = (i * tq) // tk

    def body(j, carry):
        tile(j, False)
        return carry

    lax.fori_loop(0, n_past, body, 0)
    tile(n_past, True)

    wlen = NSA_WINDOW + tq
    start = pl.multiple_of(jnp.maximum(i * tq - NSA_WINDOW, 0), 256)
    kw = kw_ref[pl.ds(start, wlen), :]
    vw1 = _with_ones(vw_ref[pl.ds(start, wlen), :])
    diff = (i * tq + lax.broadcasted_iota(I32, (tq, wlen), 0)) - (start + lax.broadcasted_iota(I32, (tq, wlen), 1))
    in_win = (diff >= 0) & (diff < NSA_WINDOW)
    g = g_ref[...]
    for hd in range(4):
        lo = hd * HEAD_DIM
        s = jnp.where(in_win, _dot_nt(q_ref[:, lo:lo + HEAD_DIM], kw), NEG)
        p = jnp.exp2(s - jnp.max(s, axis=1, keepdims=True))
        pv = _dot(p.astype(BF16), vw1)
        o_win = pv[:, 0:HEAD_DIM] / pv[:, HEAD_DIM:2 * HEAD_DIM]
        o_slc = _flash_result(acc_ref, hd, 4)
        out = (g[:, 3 * hd:3 * hd + 1] * oc_ref[:, lo:lo + HEAD_DIM] + g[:, 3 * hd + 1:3 * hd + 2] * o_slc
               + g[:, 3 * hd + 2:3 * hd + 3] * o_win)
        o_ref[:, lo:lo + HEAD_DIM] = out.astype(o_ref.dtype)


def _nsa_slc_win(q_r, sel, kslc, vslc, kwin, vwin, gates, o_cmp):
    s = q_r.shape[0]
    tq = min(ATT_TQ, s)
    tk = min(ATT_TK, s)
    full = lambda a: _resident(a.shape, lambda i: (0,) * a.ndim)
    return pl.pallas_call(
        functools.partial(_nsa_slc_win_kernel, tk=tk),
        grid=(s // tq,),
        in_specs=[pl.BlockSpec((tq, 512), lambda i: (i, 0)), pl.BlockSpec((tq, 128), lambda i: (i, 0)),
                  full(kslc), full(vslc), full(kwin), full(vwin),
                  pl.BlockSpec((tq, 128), lambda i: (i, 0)), pl.BlockSpec((tq, 512), lambda i: (i, 0))],
        out_specs=pl.BlockSpec((tq, 512), lambda i: (i, 0)),
        out_shape=jax.ShapeDtypeStruct((s, 512), BF16),
        scratch_shapes=_flash_scratch(4, tq),
        compiler_params=_cp("parallel"),
        name="nsa_slc_win",
    )(q_r, sel, kslc, vslc, kwin, vwin, gates, o_cmp)


def _dsa_kernel(q_ref, iq_ref, iw_ref, k_ref, v_ref, ik_ref, o_ref, key_ref, wb_ref, m_ref, acc_ref, *, tk):
    i = pl.program_id(0)
    tq = q_ref.shape[0]
    half = tk // 2
    n_tiles = (i * tq + tq + tk - 1) // tk
    row_h = lax.broadcasted_iota(I32, (tq, half), 0) + i * tq
    col_h = lax.broadcasted_iota(I32, (tq, half), 1)

    w = iw_ref[...]
    for hd in range(DSA_IDX_HEADS):
        wb_ref[hd] = jnp.broadcast_to(w[:, hd:hd + 1], (tq, 128))

    def score_body(j, carry):
        for hf in range(2):
            off = pl.multiple_of(j * tk + hf * half, half)
            ik_even = ik_ref[pl.ds(off, half), 0:128]
            ik_odd = ik_ref[pl.ds(off, half), 128:256]
            sc = jnp.zeros((tq, half), F32)
            for p in range(DSA_IDX_HEADS // 2):
                x = iq_ref[:, p * 128:(p + 1) * 128]
                we = jnp.concatenate([wb_ref[2 * p]] * (half // 128), axis=1)
                wo = jnp.concatenate([wb_ref[2 * p + 1]] * (half // 128), axis=1)
                sc = sc + jnp.maximum(_dot_nt(x, ik_even), 0.0) * we
                sc = sc + jnp.maximum(_dot_nt(x, ik_odd), 0.0) * wo
            bits = pltpu.bitcast(sc, I32)
            key = jnp.where(bits >= 0, bits, bits ^ 0x7FFFFFFF)
            key_ref[j, :, hf * half:(hf + 1) * half] = jnp.where(col_h + off <= row_h, key, INT_MIN)
        return carry

    lax.fori_loop(0, n_tiles, score_body, 0)

    rb = min(DSA_COUNT_ROWS, tq)

    ones = jnp.ones((128, 128), BF16)
    k_f = float(DSA_TOPK)

    def count_ge(cand128):
        parts = []
        for r0 in range(0, tq, rb):
            cand_b = cand128[r0:r0 + rb]

            def body(j, acc):
                for c in range(tk // 128):
                    acc = acc + jnp.where(key_ref[j, r0:r0 + rb, c * 128:(c + 1) * 128] >= cand_b, 1, 0)
                return acc

            parts.append(lax.fori_loop(0, n_tiles, body, jnp.zeros((rb, 128), I32)))
        per_lane = jnp.concatenate(parts, axis=0).astype(F32).astype(BF16)
        return _dot(per_lane, ones)

    zero = jnp.zeros((tq, 128), I32)
    c0 = count_ge(zero)
    lo0 = jnp.where(c0 >= k_f, zero, INT_MIN)
    c_lo0 = jnp.where(c0 >= k_f, c0, -1.0)

    def bit_cond(state):
        b, _, _, more = state
        return (b < 31) & (more > 0.0)

    def bit_body(state):
        b, lo, c_lo, _ = state
        more = jnp.max(jnp.where(c_lo != k_f, 1.0, 0.0))
        cand = lo | lax.shift_left(jnp.int32(1), 30 - b)
        c = count_ge(cand)
        take = c >= k_f
        return b + 1, jnp.where(take, cand, lo), jnp.where(take, c, c_lo), more

    _, lo, _, _ = lax.while_loop(bit_cond, bit_body, (jnp.int32(0), lo0, c_lo0, jnp.float32(1.0)))
    thr128 = jnp.maximum(lo, INT_MIN + 1)

    _flash_init(m_ref, acc_ref)

    def attn_body(j, carry):
        off = pl.multiple_of(j * tk, tk)
        keep = key_ref[j] >= jnp.concatenate([thr128] * (tk // 128), axis=1)
        cols = [slice(hd * HEAD_DIM, (hd + 1) * HEAD_DIM) for hd in range(4)]
        _flash_heads([q_ref[:, c] for c in cols], [k_ref[pl.ds(off, tk), c] for c in cols],
                     [_with_ones(v_ref[pl.ds(off, tk), c]) for c in cols], None, keep, m_ref, acc_ref)
        return carry

    lax.fori_loop(0, n_tiles, attn_body, 0)
    for hd in range(4):
        o_ref[:, hd * HEAD_DIM:(hd + 1) * HEAD_DIM] = _flash_result(acc_ref, hd, 4).astype(o_ref.dtype)


def _dsa(q, k, v, iq, ik2, iw):
    s = q.shape[0]
    tq = min(DSA_TQ, s)
    tk = min(ATT_TK, s)
    full = lambda a: _resident(a.shape, lambda i: (0, 0))
    return pl.pallas_call(
        functools.partial(_dsa_kernel, tk=tk),
        grid=(s // tq,),
        in_specs=[pl.BlockSpec((tq, 512), lambda i: (i, 0)), pl.BlockSpec((tq, 1024), lambda i: (i, 0)),
                  pl.BlockSpec((tq, 128), lambda i: (i, 0)), full(k), full(v), full(ik2)],
        out_specs=pl.BlockSpec((tq, 512), lambda i: (i, 0)),
        out_shape=jax.ShapeDtypeStruct((s, 512), BF16),
        scratch_shapes=[pltpu.VMEM((s // tk, tq, tk), I32), pltpu.VMEM((DSA_IDX_HEADS, tq, 128), F32)]
                       + _flash_scratch(4, tq),
        compiler_params=_cp("parallel"),
        name="dsa_attn",
    )(q, iq, iw, k, v, ik2)


def _mix_mem_kernel(oa_ref, ob_ref, oc_ref, od_ref, w_ref, x_ref, g1_ref, b1_ref, wq_ref, kv_ref, wo_ref,
                    g2_ref, b2_ref, o_ref):
    mix = _dot(oa_ref[...], w_ref[0:512, :])
    mix = mix + _dot(ob_ref[...], w_ref[512:1024, :])
    mix = mix + _dot(oc_ref[...], w_ref[1024:1536, :])
    mix = mix + _dot(od_ref[...], w_ref[1536:2048, :])
    x = _layer_norm(DN_ALPHA * x_ref[...] + mix, g1_ref[...], b1_ref[...])

    q = _dot(x.astype(BF16), wq_ref[...])
    heads = []
    for hd in range(4):
        lo = hd * HEAD_DIM
        qh = (q[:, lo:lo + HEAD_DIM] * _log2_scale(HEAD_DIM)).astype(BF16)
        s = _dot_nt(qh, kv_ref[:, lo:lo + HEAD_DIM])
        p = jnp.exp2(s - jnp.max(s, axis=1, keepdims=True))
        o = _dot(p.astype(BF16), kv_ref[:, 512 + lo:512 + lo + HEAD_DIM]) / jnp.sum(p, axis=1, keepdims=True)
        heads.append(o.astype(BF16))
    out = _dot(jnp.concatenate(heads, axis=1), wo_ref[...])
    o_ref[...] = _layer_norm(DN_ALPHA * x + out, g2_ref[...], b2_ref[...])


def _mix_mem_ln(o_a, o_b, o_c, o_d, w_out, x, g1, b1, wq, kv, wo, g2, b2):
    s, d = x.shape
    tm = min(ROW_TILE, s)
    r512 = pl.BlockSpec((tm, 512), lambda i: (i, 0))
    full = lambda a: _wspec(a, _wshape(a), lambda i: (0, 0))
    return pl.pallas_call(
        _mix_mem_kernel,
        grid=(s // tm,),
        in_specs=[r512, r512, r512, r512, full(w_out), pl.BlockSpec((tm, d), lambda i: (i, 0)), full(g1), full(b1),
                  full(wq), _resident(kv.shape, lambda i: (0, 0)), full(wo), full(g2), full(b2)],
        out_specs=pl.BlockSpec((tm, d), lambda i: (i, 0)),
        out_shape=jax.ShapeDtypeStruct((s, d), F32),
        compiler_params=_cp("parallel"),
        name="mix_mem_ln",
    )(o_a, o_b, o_c, o_d, w_out[0], x, g1[0], b1[0], wq[0], kv, wo[0], g2[0], b2[0])


def _mem_kv_kernel(mem_ref, w_ref, o_ref):
    o_ref[...] = _dot(mem_ref[...].astype(BF16), w_ref[...]).astype(o_ref.dtype)


def _mem_kv(mem, wkv):
    m, d = mem.shape
    n = _wshape(wkv)[1]
    return pl.pallas_call(
        _mem_kv_kernel,
        grid=(1,),
        in_specs=[pl.BlockSpec((m, d), lambda i: (0, 0)), _wspec(wkv, (d, n), lambda i: (0, 0))],
        out_specs=pl.BlockSpec((m, n), lambda i: (0, 0)),
        out_shape=jax.ShapeDtypeStruct((m, n), BF16),
        compiler_params=_cp("arbitrary"),
        name="mem_kv",
    )(mem, wkv[0])


def _pad_last(w, n):
    return jnp.pad(w, [(0, 0)] * (w.ndim - 1) + [(0, n - w.shape[-1])])


def _split_w_in(w_in):
    a = w_in[..., 0:1536]
    b = _pad_last(w_in[..., 1536:2624], 1152)
    c = _pad_last(w_in[..., 2624:3916], 1408)
    d = jnp.concatenate([w_in[..., 3916:6476], _pad_last(w_in[..., 6476:6540], 128),
                         _pad_last(w_in[..., 6540:6556], 128)], axis=-1)
    return tuple(t.astype(BF16) for t in (a, b, c, d))


def _pad_w_uq(w_uq):
    lead = w_uq.shape[:-1]
    w = _pad_last(w_uq.reshape(lead + (4, MLA_NOPE + MLA_ROPE)), 256)
    return w.reshape(lead + (4 * 256,)).astype(BF16)


def _nsa_importance_map(s):
    n = s // NSA_CMP_STRIDE
    nblk = s // NSA_SEL_BLOCK
    ni = np.arange(n)[:, None]
    bi = np.arange(nblk)[None, :]
    m = ((ni >= 4 * bi - 1) & (ni <= 4 * bi + 3)).astype(np.float32)
    return jnp.asarray(np.pad(m, ((0, 0), (0, 128 - nblk))))


def _mixer_heads(x, tabs, imp_map, w_groups, g_cq, g_ckv, w_uq, w_ukv, cmp_pe, cmp_w1, cmp_w2):
    cos128, sin128, cos64, sup64, sdn64 = tabs
    s = x.shape[0]
    w_a, w_b, w_c, w_d = w_groups

    aq, ak, av, akm = _proj_a(x, w_a, cos128, sin128)
    o_a = _moba(aq, ak, av, akm)

    bq, bk, bv = _proj_b(x, w_b, g_cq, g_ckv, w_uq, w_ukv, cos64, sup64, sdn64)
    o_b = _mla(bq, bk, bv)

    cq, cqr, kvcmp, kslc, vslc, kwin, vwin, gates = _proj_c(x, w_c, cos128, sin128)
    kv16 = kvcmp.reshape(2, s // NSA_CMP_STRIDE, NSA_CMP_STRIDE * HEAD_DIM)
    kvc = _nsa_compress(kv16, cmp_pe, cmp_w1, cmp_w2)
    o_cmp, sel = _nsa_cmp(cq, kvc, imp_map)
    o_c = _nsa_slc_win(cqr, sel, kslc, vslc, kwin, vwin, gates, o_cmp)

    dq, dk, dv, diq, dik, diw = _proj_d(x, w_d, cos128, sin128, cos64, sup64, sdn64)
    o_d = _dsa(dq, dk, dv, diq, dik, diw)
    return o_a, o_b, o_c, o_d


def kernel(x, mem, positions, ln_g, ln_b, ffn_w_gu, ffn_w_down, w_in, w_out, mla_g_cq, mla_g_ckv, mla_w_uq,
           mla_w_ukv, nsa_cmp_pe, nsa_cmp_w1, nsa_cmp_w2, mem_wq, mem_wkv, mem_wo):
    batch, s, d = x.shape
    w_gu_b, w_dn_b = ffn_w_gu, ffn_w_down
    w_groups = _split_w_in(w_in)
    w_out_b = w_out.astype(BF16)
    w_uq_b, w_ukv_b = _pad_w_uq(mla_w_uq), mla_w_ukv.astype(BF16)
    g_cq, g_ckv = mla_g_cq[:, None, :], mla_g_ckv[:, None, :]
    pe = nsa_cmp_pe.reshape(DEPTH, 2, 1, -1)
    w1_b, w2_b = nsa_cmp_w1.astype(BF16), nsa_cmp_w2.astype(BF16)
    wq_b, wkv_b, wo_b = mem_wq.astype(BF16), mem_wkv.astype(BF16), mem_wo.astype(BF16)
    g4, b4 = ln_g[:, :, None, :], ln_b[:, :, None, :]
    imp_map = _nsa_importance_map(s)

    outs = []
    for bi in range(batch):
        xb = x[bi]
        tabs = _rope_tables(positions[bi])
        for l in range(DEPTH):
            at = lambda a, *lead: (a, (l,) + lead)
            xb = _ffn_ln(xb, at(w_gu_b, 0), at(w_dn_b, 0), at(g4, 0), at(b4, 0))
            o_heads = _mixer_heads(xb, tabs, imp_map, tuple(at(w) for w in w_groups), at(g_cq), at(g_ckv),
                                   at(w_uq_b), at(w_ukv_b), at(pe), at(w1_b), at(w2_b))
            kv = _mem_kv(mem[bi], at(wkv_b))
            xb = _mix_mem_ln(*o_heads, at(w_out_b), xb, at(g4, 1), at(b4, 1), at(wq_b), kv, at(wo_b),
                             at(g4, 2), at(b4, 2))
            xb = _ffn_ln(xb, at(w_gu_b, 1), at(w_dn_b, 1), at(g4, 3), at(b4, 3))
        outs.append(xb)
    return jnp.stack(outs)
```

```python
import functools

import numpy as np
import jax
import jax.numpy as jnp
from jax import lax
from jax.experimental import pallas as pl
from jax.experimental.pallas import tpu as pltpu

F32 = jnp.float32
BF16 = jnp.bfloat16
I32 = jnp.int32

D_MODEL = 2048
DEPTH = 4
HEAD_DIM = 128
ROPE_THETA = 10000.0
LN_EPS = 1e-5
RMS_EPS = 1e-6

MOBA_BLOCK = 256
MOBA_TOPK = 3
MLA_NOPE = 128
MLA_ROPE = 64
NSA_CMP_STRIDE = 16
NSA_SEL_BLOCK = 64
NSA_SEL_TOPK = 16
NSA_WINDOW = 512
DSA_IDX_HEADS = 16
DSA_TOPK = 256
D_FF = 5632
DN_ALPHA = (2 * DEPTH) ** 0.25

NEG = -(2.0 ** 100)
LOG2_E = 1.4426950408889634
INT_MIN = -(2 ** 31)
VMEM_LIMIT = 56 * 1024 * 1024

ROW_TILE = 512
FFN_ROWS = 1024
FFN_COLS = 256
ATT_TQ = 512
ATT_TK = 1024
CAUSAL_TK = 2048
DSA_TQ = 256
DSA_COUNT_ROWS = 128


def _cp(*sem):
    return pltpu.CompilerParams(dimension_semantics=sem, vmem_limit_bytes=VMEM_LIMIT)


def _wspec(w, block, index_map, resident=True):
    _, lead = w
    shape = (None,) * len(lead) + tuple(block)
    imap = lambda *g: tuple(lead) + tuple(index_map(*g))
    if resident:
        return pl.BlockSpec(shape, imap, pipeline_mode=pl.Buffered(1))
    return pl.BlockSpec(shape, imap)


def _wshape(w):
    arr, lead = w
    return arr.shape[len(lead):]


def _resident(shape, index_map):
    return pl.BlockSpec(shape, index_map, pipeline_mode=pl.Buffered(1))


def _dot(a, b):
    return jnp.dot(a, b, preferred_element_type=F32)


def _dot_nt(a, b):
    return lax.dot_general(a, b, (((1,), (1,)), ((), ())), preferred_element_type=F32)


def _layer_norm(y, g, b):
    mu = jnp.mean(y, axis=-1, keepdims=True)
    d = y - mu
    var = jnp.mean(d * d, axis=-1, keepdims=True)
    return d * lax.rsqrt(var + LN_EPS) * g + b


def _rope128(x, cos, sin_signed):
    return x * cos + pltpu.roll(x, 64, 1) * sin_signed


def _rope64(x, cos, s_up, s_dn):
    return x * cos + pltpu.roll(x, 32, 1) * s_up + pltpu.roll(x, 96, 1) * s_dn


def _log2_scale(d):
    return float(d) ** -0.5 * LOG2_E


def _with_ones(v):
    return jnp.concatenate([v, jnp.ones_like(v)], axis=1)


def _flash_update(s, v1, bias, keep, m_ref, acc_ref, slot):
    if bias is not None:
        s = s + bias
    if keep is not None:
        s = jnp.where(keep, s, NEG)
    m_old = m_ref[slot]
    m_new = jnp.maximum(m_old, jnp.max(s, axis=1, keepdims=True))
    alpha = jnp.exp2(m_old - m_new)
    p = jnp.exp2(s - m_new)
    acc_ref[slot] = alpha * acc_ref[slot] + _dot(p.astype(BF16), v1)
    m_ref[slot] = m_new


FLASH_ROWS = 256


def _flash_rows(q, k, v1, bias, keep, m_ref, acc_ref, hd):
    n = max(1, q.shape[0] // FLASH_ROWS)
    r = q.shape[0] // n
    rows = [slice(sp * r, (sp + 1) * r) for sp in range(n)]
    logits = [_dot_nt(q[rw], k) for rw in rows]
    for sp, rw in enumerate(rows):
        _flash_update(logits[sp], v1, None if bias is None else bias[rw], None if keep is None else keep[rw],
                      m_ref, acc_ref, hd * n + sp)


def _flash_heads(qs, ks, v1s, bias, keep, m_ref, acc_ref):
    def logits(hd):
        q = qs[hd]
        n = max(1, q.shape[0] // FLASH_ROWS)
        r = q.shape[0] // n
        rows = [slice(sp * r, (sp + 1) * r) for sp in range(n)]
        return rows, [_dot_nt(q[rw], ks[hd]) for rw in rows]

    ahead = logits(0)
    for hd in range(len(qs)):
        rows, s = ahead
        if hd + 1 < len(qs):
            ahead = logits(hd + 1)
        for sp, rw in enumerate(rows):
            _flash_update(s[sp], v1s[hd], None if bias is None else bias[rw], None if keep is None else keep[rw],
                          m_ref, acc_ref, hd * len(rows) + sp)


def _causal_flash(q, k_ref, v_ref, m_ref, acc_ref, i, tk):
    tq = q.shape[0]
    n_past = (i * tq) // tk
    n_rest = (i * tq - n_past * tk) // tq

    def tile(off, width, keep):
        _flash_rows(q, k_ref[pl.ds(off, width), :], _with_ones(v_ref[pl.ds(off, width), :]), None, keep,
                    m_ref, acc_ref, 0)

    def body(j, carry):
        tile(pl.multiple_of(j * tk, tk), tk, None)
        return carry

    lax.fori_loop(0, n_past, body, 0)
    for r in range(tk // tq):
        @pl.when(n_rest == r)
        def _():
            width = (r + 1) * tq
            row = lax.broadcasted_iota(I32, (tq, width), 0) + r * tq
            col = lax.broadcasted_iota(I32, (tq, width), 1)
            tile(pl.multiple_of(n_past * tk, tk), width, col <= row)


def _flash_result(acc_ref, hd, heads):
    n = acc_ref.shape[0] // heads
    parts = []
    for sp in range(n):
        a = acc_ref[hd * n + sp]
        parts.append(a[:, 0:HEAD_DIM] / a[:, HEAD_DIM:2 * HEAD_DIM])
    return jnp.concatenate(parts, axis=0) if n > 1 else parts[0]


def _flash_scratch(heads, tq):
    n = max(1, tq // FLASH_ROWS)
    return [pltpu.VMEM((heads * n, tq // n, 1), F32), pltpu.VMEM((heads * n, tq // n, 2 * HEAD_DIM), F32)]


def _flash_init(m_ref, acc_ref):
    m_ref[...] = jnp.full(m_ref.shape, NEG, F32)
    acc_ref[...] = jnp.zeros(acc_ref.shape, F32)


def _top_k_mask(work, index, k, axis=1):
    n = work.shape[axis]
    sel = jnp.zeros(work.shape, F32)
    for _ in range(k):
        m = jnp.max(work, axis=axis, keepdims=True)
        idx = jnp.min(jnp.where(work == m, index, n), axis=axis, keepdims=True)
        pick = index == jnp.where(m > -jnp.inf, idx, n)
        sel = jnp.where(pick, 1.0, sel)
        work = jnp.where(pick, -jnp.inf, work)
    return sel


def _key_block_table(s, block):
    return jnp.asarray((np.arange(128)[None, :] == (np.arange(s) // block)[:, None]) * -NEG, dtype=BF16)


def _rope_table_kernel(pos_ref, c_ref, cos128_ref, sin128_ref, cos64_ref, sup64_ref, sdn64_ref):
    pos = pos_ref[...]
    a128 = pos * c_ref[0:1, :]
    cos128_ref[...] = jnp.cos(a128)
    sin128_ref[...] = jnp.sin(a128) * c_ref[1:2, :]
    a64 = pos * c_ref[2:3, :]
    s64 = jnp.sin(a64)
    cos64_ref[...] = jnp.cos(a64)
    sup64_ref[...] = s64 * c_ref[3:4, :]
    sdn64_ref[...] = s64 * c_ref[4:5, :]


def _rope_consts():
    lane = np.arange(128)
    inv128 = (ROPE_THETA ** (-np.arange(0, 128, 2, dtype=np.float32) / 128)).astype(np.float32)
    inv64 = (ROPE_THETA ** (-np.arange(0, 64, 2, dtype=np.float32) / 64)).astype(np.float32)
    c = np.zeros((8, 128), np.float32)
    c[0] = inv128[lane % 64]
    c[1] = np.where(lane < 64, -1.0, 1.0)
    c[2] = inv64[(lane % 64) % 32]
    c[3] = np.where(lane % 64 >= 32, 1.0, 0.0)
    c[4] = np.where(lane % 64 < 32, -1.0, 0.0)
    return jnp.asarray(c)


def _rope_tables(positions):
    s = positions.shape[0]
    tm = min(ROW_TILE, s)
    pos = positions.astype(F32).reshape(s, 1)
    tab = jax.ShapeDtypeStruct((s, 128), F32)
    row = pl.BlockSpec((tm, 128), lambda i: (i, 0))
    return pl.pallas_call(
        _rope_table_kernel,
        grid=(s // tm,),
        in_specs=[pl.BlockSpec((tm, 1), lambda i: (i, 0)), pl.BlockSpec((8, 128), lambda i: (0, 0))],
        out_specs=[row] * 5,
        out_shape=[tab] * 5,
        compiler_params=_cp("parallel"),
        name="rope_tables",
    )(pos, _rope_consts())


def _ffn_kernel(x_ref, wg_ref, wu_ref, wd_ref, g_ref, b_ref, o_ref, xb_ref):
    j = pl.program_id(1)

    @pl.when(j == 0)
    def _():
        xb_ref[...] = x_ref[...].astype(BF16)
        o_ref[...] = jnp.zeros_like(o_ref)

    xb = xb_ref[...]
    g = _dot(xb, wg_ref[...].astype(BF16))
    u = _dot(xb, wu_ref[...].astype(BF16))
    h = (g * jax.nn.sigmoid(g) * u).astype(BF16)
    o_ref[...] += _dot(h, wd_ref[...].astype(BF16))

    @pl.when(j == pl.num_programs(1) - 1)
    def _():
        y = DN_ALPHA * x_ref[...] + 0.5 * o_ref[...]
        o_ref[...] = _layer_norm(y, g_ref[...], b_ref[...])


def _ffn_ln(x, w_gu, w_down, g, b):
    s, d = x.shape
    f = _wshape(w_down)[0]
    tm = min(FFN_ROWS, s)
    tf = FFN_COLS
    nf = f // tf
    return pl.pallas_call(
        _ffn_kernel,
        grid=(s // tm, nf),
        in_specs=[
            pl.BlockSpec((tm, d), lambda i, j: (i, 0), pipeline_mode=pl.Buffered(1)),
            _wspec(w_gu, (d, tf), lambda i, j: (0, j), resident=False),
            _wspec(w_gu, (d, tf), lambda i, j: (0, j + nf), resident=False),
            _wspec(w_down, (tf, d), lambda i, j: (j, 0), resident=False),
            _wspec(g, (1, d), lambda i, j: (0, 0)),
            _wspec(b, (1, d), lambda i, j: (0, 0)),
        ],
        out_specs=pl.BlockSpec((tm, d), lambda i, j: (i, 0)),
        out_shape=jax.ShapeDtypeStruct((s, d), F32),
        scratch_shapes=[pltpu.VMEM((tm, d), BF16)],
        compiler_params=_cp("parallel", "arbitrary"),
        name="ffn_ln",
    )(x, w_gu[0], w_gu[0], w_down[0], g[0], b[0])


def _proj_a_kernel(x_ref, w_ref, cos_ref, sin_ref, kb_ref, q_ref, k_ref, v_ref, km_ref):
    h = _dot(x_ref[...].astype(BF16), w_ref[...])
    cos, sin = cos_ref[...], sin_ref[...]
    nblk = km_ref.shape[0]
    for hd in range(4):
        lo = hd * HEAD_DIM
        q_ref[:, lo:lo + HEAD_DIM] = _rope128(h[:, lo:lo + HEAD_DIM], cos, sin)
        k = _rope128(h[:, 512 + lo:512 + lo + HEAD_DIM], cos, sin)
        k_ref[:, 2 * lo:2 * lo + HEAD_DIM] = k.astype(BF16)
        k_ref[:, 2 * lo + HEAD_DIM:2 * lo + 2 * HEAD_DIM] = kb_ref[...]
        for bi in range(nblk):
            kb = k[bi * MOBA_BLOCK:(bi + 1) * MOBA_BLOCK]
            km_ref[bi, :, lo:lo + HEAD_DIM] = jnp.mean(kb, axis=0, keepdims=True)
    v_ref[...] = h[:, 1024:1536].astype(BF16)


def _proj_a(x, w, cos128, sin128):
    s, d = x.shape
    tm = min(ROW_TILE, s)
    nblk = tm // MOBA_BLOCK
    row512 = pl.BlockSpec((tm, 512), lambda i: (i, 0))
    tab = pl.BlockSpec((tm, 128), lambda i: (i, 0))
    return pl.pallas_call(
        _proj_a_kernel,
        grid=(s // tm,),
        in_specs=[pl.BlockSpec((tm, d), lambda i: (i, 0)), _wspec(w, _wshape(w), lambda i: (0, 0)), tab, tab, tab],
        out_specs=[row512, pl.BlockSpec((tm, 1024), lambda i: (i, 0)), row512,
                   pl.BlockSpec((nblk, 1, 512), lambda i: (i, 0, 0))],
        out_shape=[
            jax.ShapeDtypeStruct((s, 512), F32),
            jax.ShapeDtypeStruct((s, 1024), BF16),
            jax.ShapeDtypeStruct((s, 512), BF16),
            jax.ShapeDtypeStruct((s // MOBA_BLOCK, 1, 512), F32),
        ],
        compiler_params=_cp("parallel"),
        name="proj_moba",
    )(x, w[0], cos128, sin128, _key_block_table(s, MOBA_BLOCK))


def _rms_norm(x, g):
    return x * lax.rsqrt(jnp.mean(x * x, axis=-1, keepdims=True) + RMS_EPS) * g


def _proj_b_kernel(x_ref, w_ref, gq_ref, gkv_ref, wuq_ref, wukv_ref, cos_ref, sup_ref, sdn_ref,
                   q_ref, k_ref, v_ref):
    h = _dot(x_ref[...].astype(BF16), w_ref[...])
    cos, sup, sdn = cos_ref[...], sup_ref[...], sdn_ref[...]
    cq = _rms_norm(h[:, 0:512], gq_ref[...]).astype(BF16)
    ckv = _rms_norm(h[:, 512:1024], gkv_ref[...]).astype(BF16)
    kr = _rope64(h[:, 1024:1152], cos, sup, sdn).astype(BF16)
    q = _dot(cq, wuq_ref[...])
    kv = _dot(ckv, wukv_ref[...])
    qs = _log2_scale(MLA_NOPE + MLA_ROPE)
    for hd in range(4):
        lo = hd * 256
        q_ref[:, lo:lo + 128] = (q[:, lo:lo + 128] * qs).astype(BF16)
        q_ref[:, lo + 128:lo + 256] = (_rope64(q[:, lo + 128:lo + 256], cos, sup, sdn) * qs).astype(BF16)
        k_ref[:, lo:lo + 128] = kv[:, lo:lo + 128].astype(BF16)
        k_ref[:, lo + 128:lo + 256] = kr
        v_ref[:, hd * 128:(hd + 1) * 128] = kv[:, lo + 128:lo + 256].astype(BF16)


def _proj_b(x, w, g_cq, g_ckv, w_uq, w_ukv, cos64, sup64, sdn64):
    s, d = x.shape
    tm = min(ROW_TILE, s)
    tab = pl.BlockSpec((tm, 128), lambda i: (i, 0))
    full = lambda a: _wspec(a, _wshape(a), lambda i: (0, 0))
    return pl.pallas_call(
        _proj_b_kernel,
        grid=(s // tm,),
        in_specs=[pl.BlockSpec((tm, d), lambda i: (i, 0)), full(w), full(g_cq), full(g_ckv), full(w_uq),
                  full(w_ukv), tab, tab, tab],
        out_specs=[pl.BlockSpec((tm, 1024), lambda i: (i, 0)), pl.BlockSpec((tm, 1024), lambda i: (i, 0)),
                   pl.BlockSpec((tm, 512), lambda i: (i, 0))],
        out_shape=[jax.ShapeDtypeStruct((s, 1024), BF16), jax.ShapeDtypeStruct((s, 1024), BF16),
                   jax.ShapeDtypeStruct((s, 512), BF16)],
        compiler_params=_cp("parallel"),
        name="proj_mla",
    )(x, w[0], g_cq[0], g_ckv[0], w_uq[0], w_ukv[0], cos64, sup64, sdn64)


def _proj_c_kernel(x_ref, w_ref, cos_ref, sin_ref, kb_ref, q_ref, qr_ref, kvcmp_ref, kslc_ref, vslc_ref,
                   kwin_ref, vwin_ref, gate_ref):
    h = _dot(x_ref[...].astype(BF16), w_ref[...])
    cos, sin = cos_ref[...], sin_ref[...]
    for hd in range(4):
        lo = hd * HEAD_DIM
        qh = h[:, lo:lo + HEAD_DIM]
        q_ref[:, lo:lo + HEAD_DIM] = qh.astype(BF16)
        qr_ref[:, lo:lo + HEAD_DIM] = (_rope128(qh, cos, sin) * _log2_scale(HEAD_DIM)).astype(BF16)
    kv = lambda i: h[:, 512 + i * 128:512 + (i + 1) * 128]
    kvcmp_ref[0] = kv(0).astype(BF16)
    kvcmp_ref[1] = kv(1).astype(BF16)
    kslc_ref[:, 0:HEAD_DIM] = _rope128(kv(2), cos, sin).astype(BF16)
    kslc_ref[:, HEAD_DIM:2 * HEAD_DIM] = kb_ref[...]
    vslc_ref[...] = kv(3).astype(BF16)
    kwin_ref[...] = _rope128(kv(4), cos, sin).astype(BF16)
    vwin_ref[...] = kv(5).astype(BF16)
    gate_ref[...] = jax.nn.sigmoid(h[:, 1280:1408])


def _proj_c(x, w, cos128, sin128):
    s, d = x.shape
    tm = min(ROW_TILE, s)
    r512 = pl.BlockSpec((tm, 512), lambda i: (i, 0))
    r128 = pl.BlockSpec((tm, 128), lambda i: (i, 0))
    b512 = jax.ShapeDtypeStruct((s, 512), BF16)
    b128 = jax.ShapeDtypeStruct((s, 128), BF16)
    return pl.pallas_call(
        _proj_c_kernel,
        grid=(s // tm,),
        in_specs=[pl.BlockSpec((tm, d), lambda i: (i, 0)), _wspec(w, _wshape(w), lambda i: (0, 0)), r128, r128, r128],
        out_specs=[r512, r512, pl.BlockSpec((2, tm, 128), lambda i: (0, i, 0)),
                   pl.BlockSpec((tm, 256), lambda i: (i, 0))] + [r128] * 4,
        out_shape=[b512, b512, jax.ShapeDtypeStruct((2, s, 128), BF16), jax.ShapeDtypeStruct((s, 256), BF16)]
                  + [b128] * 3 + [jax.ShapeDtypeStruct((s, 128), F32)],
        compiler_params=_cp("parallel"),
        name="proj_nsa",
    )(x, w[0], cos128, sin128, _key_block_table(s, NSA_SEL_BLOCK))


def _proj_d_kernel(x_ref, w_ref, cos_ref, sin_ref, c64_ref, sup_ref, sdn_ref,
                   q_ref, k_ref, v_ref, iq_ref, ik_ref, iw_ref):
    h = _dot(x_ref[...].astype(BF16), w_ref[...])
    cos, sin = cos_ref[...], sin_ref[...]
    c64, sup, sdn = c64_ref[...], sup_ref[...], sdn_ref[...]
    for hd in range(4):
        lo = hd * HEAD_DIM
        q_ref[:, lo:lo + HEAD_DIM] = (_rope128(h[:, lo:lo + HEAD_DIM], cos, sin) * _log2_scale(HEAD_DIM)).astype(BF16)
        k_ref[:, lo:lo + HEAD_DIM] = _rope128(h[:, 512 + lo:512 + lo + HEAD_DIM], cos, sin).astype(BF16)
    v_ref[...] = h[:, 1024:1536].astype(BF16)
    for p in range(8):
        lo = 1536 + p * 128
        iq_ref[:, p * 128:(p + 1) * 128] = _rope64(h[:, lo:lo + 128], c64, sup, sdn).astype(BF16)
    ik = _rope64(h[:, 2560:2688], c64, sup, sdn)
    ik_ref[:, 0:128] = ik.astype(BF16)
    ik_ref[:, 128:256] = pltpu.roll(ik, 64, 1).astype(BF16)
    iw_ref[...] = h[:, 2688:2816] * (1.0 / 32.0)


def _proj_d(x, w, cos128, sin128, cos64, sup64, sdn64):
    s, d = x.shape
    tm = min(ROW_TILE, s)
    r = lambda n: pl.BlockSpec((tm, n), lambda i: (i, 0))
    return pl.pallas_call(
        _proj_d_kernel,
        grid=(s // tm,),
        in_specs=[pl.BlockSpec((tm, d), lambda i: (i, 0)), _wspec(w, _wshape(w), lambda i: (0, 0))] + [r(128)] * 5,
        out_specs=[r(512), r(512), r(512), r(1024), r(256), r(128)],
        out_shape=[jax.ShapeDtypeStruct((s, 512), BF16)] * 3 + [
            jax.ShapeDtypeStruct((s, 1024), BF16), jax.ShapeDtypeStruct((s, 256), BF16),
            jax.ShapeDtypeStruct((s, 128), F32)],
        compiler_params=_cp("parallel"),
        name="proj_dsa",
    )(x, w[0], cos128, sin128, cos64, sup64, sdn64)


def _moba_kernel(q_ref, k_ref, v_ref, km_ref, o_ref, m_ref, acc_ref, *, tk, nb):
    i = pl.program_id(1)
    tq = q_ref.shape[0]
    q32 = q_ref[...]

    gate = lax.dot_general(q32, km_ref[...], (((1,), (1,)), ((), ())),
                           precision=lax.Precision.HIGHEST, preferred_element_type=F32)
    blk = lax.broadcasted_iota(I32, (128, tq), 0)
    cur = (i * tq + lax.broadcasted_iota(I32, (128, tq), 1)) // MOBA_BLOCK
    sel = _top_k_mask(jnp.where(blk < cur, gate.T, -jnp.inf), blk, min(MOBA_TOPK, nb - 1), axis=0)
    unchosen = (jnp.where(blk == cur, 1.0, sel) - 1.0).T
    qa = jnp.concatenate([q32 * _log2_scale(HEAD_DIM), unchosen], axis=1).astype(BF16)

    _flash_init(m_ref, acc_ref)
    _causal_flash(qa, k_ref, v_ref, m_ref, acc_ref, i, tk)
    o_ref[...] = _flash_result(acc_ref, 0, 1).astype(o_ref.dtype)


def _moba(q32, k, v, kmean):
    s = q32.shape[0]
    nb = s // MOBA_BLOCK
    tq = min(ATT_TQ, s)
    tk = min(CAUSAL_TK, s)
    km = jnp.pad(kmean.reshape(nb, 512), ((0, 128 - nb), (0, 0)))
    return pl.pallas_call(
        functools.partial(_moba_kernel, tk=tk, nb=nb),
        grid=(4, s // tq),
        in_specs=[
            pl.BlockSpec((tq, HEAD_DIM), lambda h, i: (i, h)),
            pl.BlockSpec((s, 2 * HEAD_DIM), lambda h, i: (0, h)),
            pl.BlockSpec((s, HEAD_DIM), lambda h, i: (0, h)),
            pl.BlockSpec((128, HEAD_DIM), lambda h, i: (0, h)),
        ],
        out_specs=pl.BlockSpec((tq, HEAD_DIM), lambda h, i: (i, h)),
        out_shape=jax.ShapeDtypeStruct((s, 512), BF16),
        scratch_shapes=_flash_scratch(1, tq),
        compiler_params=_cp("parallel", "arbitrary"),
        name="moba_attn",
    )(q32, k, v, km)


def _mla_kernel(q_ref, k_ref, v_ref, o_ref, m_ref, acc_ref, *, tk):
    _flash_init(m_ref, acc_ref)
    _causal_flash(q_ref[...], k_ref, v_ref, m_ref, acc_ref, pl.program_id(1), tk)
    o_ref[...] = _flash_result(acc_ref, 0, 1).astype(o_ref.dtype)


def _mla(qcat, kcat, v):
    s = qcat.shape[0]
    tq = min(ATT_TQ, s)
    tk = min(CAUSAL_TK, s)
    return pl.pallas_call(
        functools.partial(_mla_kernel, tk=tk),
        grid=(4, s // tq),
        in_specs=[
            pl.BlockSpec((tq, 256), lambda h, i: (i, h)),
            pl.BlockSpec((s, 256), lambda h, i: (0, h)),
            pl.BlockSpec((s, HEAD_DIM), lambda h, i: (0, h)),
        ],
        out_specs=pl.BlockSpec((tq, HEAD_DIM), lambda h, i: (i, h)),
        out_shape=jax.ShapeDtypeStruct((s, 512), BF16),
        scratch_shapes=_flash_scratch(1, tq),
        compiler_params=_cp("parallel", "arbitrary"),
        name="mla_attn",
    )(qcat, kcat, v)


def _gelu_tanh(x):
    return 0.5 * x * (1.0 + jnp.tanh(0.7978845608028654 * (x + 0.044715 * x * x * x)))


def _nsa_compress_kernel(t_ref, pe_ref, w1_ref, w2_ref, o_ref):
    t = t_ref[0]
    w1 = w1_ref[...]
    half = t.shape[1]
    n = t.shape[0]
    lo = _dot(t, w1[0:half])
    hi = _dot(t, w1[half:2 * half])
    pe = _dot(pe_ref[...].astype(BF16), w1)
    pre = lo + pltpu.roll(hi, n - 1, 0) + pe
    o_ref[0] = _dot(_gelu_tanh(pre).astype(BF16), w2_ref[...]).astype(o_ref.dtype)


def _nsa_compress(kv16, pe, w1, w2):
    n = kv16.shape[1]
    sub = lambda a: _wspec(a, (None,) + _wshape(a)[1:], lambda i: (i, 0, 0), resident=False)
    return pl.pallas_call(
        _nsa_compress_kernel,
        grid=(2,),
        in_specs=[pl.BlockSpec((1,) + kv16.shape[1:], lambda i: (i, 0, 0)), sub(pe), sub(w1), sub(w2)],
        out_specs=pl.BlockSpec((1, n, HEAD_DIM), lambda i: (i, 0, 0)),
        out_shape=jax.ShapeDtypeStruct((2, n, HEAD_DIM), BF16),
        compiler_params=_cp("parallel"),
        name="nsa_compress",
    )(kv16, pe[0], w1[0], w2[0])


def _nsa_cmp_kernel(q_ref, kvc_ref, m_ref, o_ref, sel_ref, *, n_sel):
    i = pl.program_id(0)
    tq = q_ref.shape[0]
    n = kvc_ref.shape[1]
    nblk = m_ref.shape[1]
    scale = HEAD_DIM ** -0.5
    kc, vc = kvc_ref[0], kvc_ref[1]
    t_pos = i * tq + lax.broadcasted_iota(I32, (tq, n), 0)
    cmp_end = lax.broadcasted_iota(I32, (tq, n), 1) * NSA_CMP_STRIDE + (2 * NSA_CMP_STRIDE - 1)
    ok = cmp_end <= t_pos
    p_sum = jnp.zeros((tq, n), F32)
    for hd in range(4):
        lo = hd * HEAD_DIM
        s = jnp.where(ok, _dot_nt(q_ref[:, lo:lo + HEAD_DIM], kc) * scale, NEG)
        m = jnp.max(s, axis=1, keepdims=True)
        e = jnp.where(ok, jnp.exp(s - m), 0.0)
        l = jnp.sum(e, axis=1, keepdims=True)
        p = e / jnp.where(l > 0, l, 1.0)
        p_sum = p_sum + p
        o_ref[:, lo:lo + HEAD_DIM] = _dot(p.astype(BF16), vc)

    imp = jnp.dot(p_sum, m_ref[...], precision=lax.Precision.HIGHEST, preferred_element_type=F32)
    imp_t = imp.T
    blk = lax.broadcasted_iota(I32, (nblk, tq), 0)
    cur = (i * tq + lax.broadcasted_iota(I32, (nblk, tq), 1)) // NSA_SEL_BLOCK
    forced = (blk == 0) | (blk == cur) | (blk == cur - 1)
    imp_t = jnp.where(blk > cur, -jnp.inf, jnp.where(forced, jnp.inf, imp_t))
    sel_ref[...] = (_top_k_mask(imp_t, blk, n_sel, axis=0).T - 1.0).astype(sel_ref.dtype)


def _nsa_cmp(q, kvc, imp_map):
    s = q.shape[0]
    n = kvc.shape[1]
    nblk = imp_map.shape[1]
    tq = min(256, s)
    return pl.pallas_call(
        functools.partial(_nsa_cmp_kernel, n_sel=min(NSA_SEL_TOPK, s // NSA_SEL_BLOCK)),
        grid=(s // tq,),
        in_specs=[pl.BlockSpec((tq, 512), lambda i: (i, 0)),
                  _resident((2, n, HEAD_DIM), lambda i: (0, 0, 0)),
                  _resident((n, nblk), lambda i: (0, 0))],
        out_specs=[pl.BlockSpec((tq, 512), lambda i: (i, 0)), pl.BlockSpec((tq, nblk), lambda i: (i, 0))],
        out_shape=[jax.ShapeDtypeStruct((s, 512), F32), jax.ShapeDtypeStruct((s, nblk), BF16)],
        compiler_params=_cp("parallel"),
        name="nsa_cmp_select",
    )(q, kvc, imp_map)


def _nsa_slc_win_kernel(q_ref, sel_ref, ks_ref, vs_ref, kw_ref, vw_ref, g_ref, oc_ref, o_ref,
                        m_ref, acc_ref, *, tk):
    i = pl.program_id(0)
    tq = q_ref.shape[0]
    _flash_init(m_ref, acc_ref)

    def tile(off, width, keep):
        k = ks_ref[pl.ds(off, width), 0:HEAD_DIM]
        v1 = _with_ones(vs_ref[pl.ds(off, width), :])
        bias = _dot_nt(sel_ref[...], ks_ref[pl.ds(off, width), HEAD_DIM:2 * HEAD_DIM])
        _flash_heads([q_ref[:, hd * HEAD_DIM:(hd + 1) * HEAD_DIM] for hd in range(4)], [k] * 4, [v1] * 4,
                     bias, keep, m_ref, acc_ref)

    n_past = (i * tq) // tk
    n_rest = (i * tq - n_past * tk) // tq

    def body(j, carry):
        tile(pl.multiple_of(j * tk, tk), tk, None)
        return carry

    lax.fori_loop(0, n_past, body, 0)
    for r in range(tk // tq):
        @pl.when(n_rest == r)
        def _():
            width = (r + 1) * tq
            row = lax.broadcasted_iota(I32, (tq, width), 0) + r * tq
            col = lax.broadcasted_iota(I32, (tq, width), 1)
            tile(pl.multiple_of(n_past * tk, tk), width, col <= row)

    wlen = NSA_WINDOW + tq
    start = pl.multiple_of(jnp.maximum(i * tq - NSA_WINDOW, 0), 256)
    kw = kw_ref[pl.ds(start, wlen), :]
    vw1 = _with_ones(vw_ref[pl.ds(start, wlen), :])
    diff = (i * tq + lax.broadcasted_iota(I32, (tq, wlen), 0)) - (start + lax.broadcasted_iota(I32, (tq, wlen), 1))
    in_win = (diff >= 0) & (diff < NSA_WINDOW)
    g = g_ref[...]
    for hd in range(4):
        lo = hd * HEAD_DIM
        s = jnp.where(in_win, _dot_nt(q_ref[:, lo:lo + HEAD_DIM], kw), NEG)
        p = jnp.exp2(s - jnp.max(s, axis=1, keepdims=True))
        pv = _dot(p.astype(BF16), vw1)
        o_win = pv[:, 0:HEAD_DIM] / pv[:, HEAD_DIM:2 * HEAD_DIM]
        o_slc = _flash_result(acc_ref, hd, 4)
        out = (g[:, 3 * hd:3 * hd + 1] * oc_ref[:, lo:lo + HEAD_DIM] + g[:, 3 * hd + 1:3 * hd + 2] * o_slc
               + g[:, 3 * hd + 2:3 * hd + 3] * o_win)
        o_ref[:, lo:lo + HEAD_DIM] = out.astype(o_ref.dtype)


def _nsa_slc_win(q_r, sel, kslc, vslc, kwin, vwin, gates, o_cmp):
    s = q_r.shape[0]
    tq = min(ATT_TQ, s)
    tk = min(ATT_TK, s)
    full = lambda a: _resident(a.shape, lambda i: (0,) * a.ndim)
    return pl.pallas_call(
        functools.partial(_nsa_slc_win_kernel, tk=tk),
        grid=(s // tq,),
        in_specs=[pl.BlockSpec((tq, 512), lambda i: (i, 0)), pl.BlockSpec((tq, 128), lambda i: (i, 0)),
                  full(kslc), full(vslc), full(kwin), full(vwin),
                  pl.BlockSpec((tq, 128), lambda i: (i, 0)), pl.BlockSpec((tq, 512), lambda i: (i, 0))],
        out_specs=pl.BlockSpec((tq, 512), lambda i: (i, 0)),
        out_shape=jax.ShapeDtypeStruct((s, 512), BF16),
        scratch_shapes=_flash_scratch(4, tq),
        compiler_params=_cp("parallel"),
        name="nsa_slc_win",
    )(q_r, sel, kslc, vslc, kwin, vwin, gates, o_cmp)


def _dsa_kernel(q_ref, iq_ref, iw_ref, k_ref, v_ref, ik_ref, o_ref, key_ref, wb_ref, m_ref, acc_ref, *, tk):
    i = pl.program_id(0)
    tq = q_ref.shape[0]
    half = tk // 2
    n_tiles = (i * tq + tq + tk - 1) // tk
    row_h = lax.broadcasted_iota(I32, (tq, half), 0) + i * tq
    col_h = lax.broadcasted_iota(I32, (tq, half), 1)

    w = iw_ref[...]
    for hd in range(DSA_IDX_HEADS):
        wb_ref[hd] = jnp.broadcast_to(w[:, hd:hd + 1], (tq, 128))

    def score_body(j, carry):
        for hf in range(2):
            off = pl.multiple_of(j * tk + hf * half, half)
            ik_even = ik_ref[pl.ds(off, half), 0:128]
            ik_odd = ik_ref[pl.ds(off, half), 128:256]
            sc = jnp.zeros((tq, half), F32)
            for p in range(DSA_IDX_HEADS // 2):
                x = iq_ref[:, p * 128:(p + 1) * 128]
                we = jnp.concatenate([wb_ref[2 * p]] * (half // 128), axis=1)
                wo = jnp.concatenate([wb_ref[2 * p + 1]] * (half // 128), axis=1)
                sc = sc + jnp.maximum(_dot_nt(x, ik_even), 0.0) * we
                sc = sc + jnp.maximum(_dot_nt(x, ik_odd), 0.0) * wo
            bits = pltpu.bitcast(sc, I32)
            key = jnp.where(bits >= 0, bits, bits ^ 0x7FFFFFFF)
            key_ref[j, :, hf * half:(hf + 1) * half] = jnp.where(col_h + off <= row_h, key, INT_MIN)
        return carry

    lax.fori_loop(0, n_tiles, score_body, 0)

    rb = min(DSA_COUNT_ROWS, tq)

    ones = jnp.ones((128, 128), BF16)
    k_f = float(DSA_TOPK)

    def count_ge(cand128):
        parts = []
        for r0 in range(0, tq, rb):
            cand_b = cand128[r0:r0 + rb]

            def body(j, acc):
                for c in range(tk // 128):
                    acc = acc + jnp.where(key_ref[j, r0:r0 + rb, c * 128:(c + 1) * 128] >= cand_b, 1, 0)
                return acc

            parts.append(lax.fori_loop(0, n_tiles, body, jnp.zeros((rb, 128), I32)))
        per_lane = jnp.concatenate(parts, axis=0).astype(F32).astype(BF16)
        return _dot(per_lane, ones)

    zero = jnp.zeros((tq, 128), I32)
    c0 = count_ge(zero)
    lo0 = jnp.where(c0 >= k_f, zero, INT_MIN)
    c_lo0 = jnp.where(c0 >= k_f, c0, -1.0)

    def bit_cond(state):
        b, _, _, more = state
        return (b < 31) & (more > 0.0)

    def bit_body(state):
        b, lo, c_lo, _ = state
        more = jnp.max(jnp.where(c_lo != k_f, 1.0, 0.0))
        cand = lo | lax.shift_left(jnp.int32(1), 30 - b)
        c = count_ge(cand)
        take = c >= k_f
        return b + 1, jnp.where(take, cand, lo), jnp.where(take, c, c_lo), more

    _, lo, c_lo, _ = lax.while_loop(bit_cond, bit_body, (jnp.int32(0), lo0, c_lo0, jnp.float32(1.0)))
    thr128 = jnp.maximum(lo, INT_MIN + 1)

    @pl.when(jnp.max(jnp.where(c_lo > k_f, 1.0, 0.0)) > 0.0)
    def _():
        need = k_f - count_ge(lo + 1)
        lane = lax.broadcasted_iota(I32, (rb, 128), 1)

        def tied_pos(j, r0, c):
            tied = key_ref[j, r0:r0 + rb, c * 128:(c + 1) * 128] == lo[r0:r0 + rb]
            return jnp.where(tied, j * tk + c * 128 + lane, -1)

        def count_tied_before(limit):
            parts = []
            for r0 in range(0, tq, rb):
                lim_b = limit[r0:r0 + rb]

                def body(j, acc, r0=r0, lim_b=lim_b):
                    for c in range(tk // 128):
                        pos = tied_pos(j, r0, c)
                        acc = acc + jnp.where(jnp.where(pos >= 0, pos, lim_b) < lim_b, 1, 0)
                    return acc

                parts.append(lax.fori_loop(0, n_tiles, body, jnp.zeros((rb, 128), I32)))
            return _dot(jnp.concatenate(parts, axis=0).astype(F32).astype(BF16), ones)

        last = jnp.zeros((tq, 128), I32)
        for bit in reversed(range((key_ref.shape[0] * tk - 1).bit_length())):
            cand = last | (1 << bit)
            last = jnp.where(count_tied_before(cand) < need, cand, last)
        last = jnp.where(c_lo > k_f, last, 2 ** 31 - 1)

        for r0 in range(0, tq, rb):
            def drop_body(j, carry, r0=r0):
                for c in range(tk // 128):
                    cols = slice(c * 128, (c + 1) * 128)
                    beyond = tied_pos(j, r0, c) > last[r0:r0 + rb]
                    key_ref[j, r0:r0 + rb, cols] = jnp.where(beyond, INT_MIN, key_ref[j, r0:r0 + rb, cols])
                return carry

            lax.fori_loop(0, n_tiles, drop_body, 0)

    _flash_init(m_ref, acc_ref)

    def attn_body(j, carry):
        off = pl.multiple_of(j * tk, tk)
        keep = key_ref[j] >= jnp.concatenate([thr128] * (tk // 128), axis=1)
        cols = [slice(hd * HEAD_DIM, (hd + 1) * HEAD_DIM) for hd in range(4)]
        _flash_heads([q_ref[:, c] for c in cols], [k_ref[pl.ds(off, tk), c] for c in cols],
                     [_with_ones(v_ref[pl.ds(off, tk), c]) for c in cols], None, keep, m_ref, acc_ref)
        return carry

    lax.fori_loop(0, n_tiles, attn_body, 0)
    for hd in range(4):
        o_ref[:, hd * HEAD_DIM:(hd + 1) * HEAD_DIM] = _flash_result(acc_ref, hd, 4).astype(o_ref.dtype)


def _dsa(q, k, v, iq, ik2, iw):
    s = q.shape[0]
    tq = min(DSA_TQ, s)
    tk = min(ATT_TK, s)
    full = lambda a: _resident(a.shape, lambda i: (0, 0))
    return pl.pallas_call(
        functools.partial(_dsa_kernel, tk=tk),
        grid=(s // tq,),
        in_specs=[pl.BlockSpec((tq, 512), lambda i: (i, 0)), pl.BlockSpec((tq, 1024), lambda i: (i, 0)),
                  pl.BlockSpec((tq, 128), lambda i: (i, 0)), full(k), full(v), full(ik2)],
        out_specs=pl.BlockSpec((tq, 512), lambda i: (i, 0)),
        out_shape=jax.ShapeDtypeStruct((s, 512), BF16),
        scratch_shapes=[pltpu.VMEM((s // tk, tq, tk), I32), pltpu.VMEM((DSA_IDX_HEADS, tq, 128), F32)]
                       + _flash_scratch(4, tq),
        compiler_params=_cp("parallel"),
        name="dsa_attn",
    )(q, iq, iw, k, v, ik2)


def _mix_mem_kernel(oa_ref, ob_ref, oc_ref, od_ref, w_ref, x_ref, g1_ref, b1_ref, wq_ref, kv_ref, wo_ref,
                    g2_ref, b2_ref, o_ref):
    mix = _dot(oa_ref[...], w_ref[0:512, :])
    mix = mix + _dot(ob_ref[...], w_ref[512:1024, :])
    mix = mix + _dot(oc_ref[...], w_ref[1024:1536, :])
    mix = mix + _dot(od_ref[...], w_ref[1536:2048, :])
    x = _layer_norm(DN_ALPHA * x_ref[...] + mix, g1_ref[...], b1_ref[...])

    q = _dot(x.astype(BF16), wq_ref[...])
    heads = []
    for hd in range(4):
        lo = hd * HEAD_DIM
        qh = (q[:, lo:lo + HEAD_DIM] * _log2_scale(HEAD_DIM)).astype(BF16)
        s = _dot_nt(qh, kv_ref[:, lo:lo + HEAD_DIM])
        p = jnp.exp2(s - jnp.max(s, axis=1, keepdims=True))
        o = _dot(p.astype(BF16), kv_ref[:, 512 + lo:512 + lo + HEAD_DIM]) / jnp.sum(p, axis=1, keepdims=True)
        heads.append(o.astype(BF16))
    out = _dot(jnp.concatenate(heads, axis=1), wo_ref[...])
    o_ref[...] = _layer_norm(DN_ALPHA * x + out, g2_ref[...], b2_ref[...])


def _mix_mem_ln(o_a, o_b, o_c, o_d, w_out, x, g1, b1, wq, kv, wo, g2, b2):
    s, d = x.shape
    tm = min(ROW_TILE, s)
    r512 = pl.BlockSpec((tm, 512), lambda i: (i, 0))
    full = lambda a: _wspec(a, _wshape(a), lambda i: (0, 0))
    return pl.pallas_call(
        _mix_mem_kernel,
        grid=(s // tm,),
        in_specs=[r512, r512, r512, r512, full(w_out), pl.BlockSpec((tm, d), lambda i: (i, 0)), full(g1), full(b1),
                  full(wq), _resident(kv.shape, lambda i: (0, 0)), full(wo), full(g2), full(b2)],
        out_specs=pl.BlockSpec((tm, d), lambda i: (i, 0)),
        out_shape=jax.ShapeDtypeStruct((s, d), F32),
        compiler_params=_cp("parallel"),
        name="mix_mem_ln",
    )(o_a, o_b, o_c, o_d, w_out[0], x, g1[0], b1[0], wq[0], kv, wo[0], g2[0], b2[0])


def _mem_kv_kernel(mem_ref, w_ref, o_ref):
    o_ref[...] = _dot(mem_ref[...].astype(BF16), w_ref[...]).astype(o_ref.dtype)


def _mem_kv(mem, wkv):
    m, d = mem.shape
    n = _wshape(wkv)[1]
    return pl.pallas_call(
        _mem_kv_kernel,
        grid=(1,),
        in_specs=[pl.BlockSpec((m, d), lambda i: (0, 0)), _wspec(wkv, (d, n), lambda i: (0, 0))],
        out_specs=pl.BlockSpec((m, n), lambda i: (0, 0)),
        out_shape=jax.ShapeDtypeStruct((m, n), BF16),
        compiler_params=_cp("arbitrary"),
        name="mem_kv",
    )(mem, wkv[0])


def _pad_last(w, n):
    return jnp.pad(w, [(0, 0)] * (w.ndim - 1) + [(0, n - w.shape[-1])])


def _split_w_in(w_in):
    a = w_in[..., 0:1536]
    b = _pad_last(w_in[..., 1536:2624], 1152)
    c = _pad_last(w_in[..., 2624:3916], 1408)
    d = jnp.concatenate([w_in[..., 3916:6476], _pad_last(w_in[..., 6476:6540], 128),
                         _pad_last(w_in[..., 6540:6556], 128)], axis=-1)
    return tuple(t.astype(BF16) for t in (a, b, c, d))


def _pad_w_uq(w_uq):
    lead = w_uq.shape[:-1]
    w = _pad_last(w_uq.reshape(lead + (4, MLA_NOPE + MLA_ROPE)), 256)
    return w.reshape(lead + (4 * 256,)).astype(BF16)


def _nsa_importance_map(s):
    n = s // NSA_CMP_STRIDE
    nblk = s // NSA_SEL_BLOCK
    ni = np.arange(n)[:, None]
    bi = np.arange(nblk)[None, :]
    m = ((ni >= 4 * bi - 1) & (ni <= 4 * bi + 3)).astype(np.float32)
    return jnp.asarray(np.pad(m, ((0, 0), (0, 128 - nblk))))


def _mixer_heads(x, tabs, imp_map, w_groups, g_cq, g_ckv, w_uq, w_ukv, cmp_pe, cmp_w1, cmp_w2):
    cos128, sin128, cos64, sup64, sdn64 = tabs
    s = x.shape[0]
    w_a, w_b, w_c, w_d = w_groups

    aq, ak, av, akm = _proj_a(x, w_a, cos128, sin128)
    o_a = _moba(aq, ak, av, akm)

    bq, bk, bv = _proj_b(x, w_b, g_cq, g_ckv, w_uq, w_ukv, cos64, sup64, sdn64)
    o_b = _mla(bq, bk, bv)

    cq, cqr, kvcmp, kslc, vslc, kwin, vwin, gates = _proj_c(x, w_c, cos128, sin128)
    kv16 = kvcmp.reshape(2, s // NSA_CMP_STRIDE, NSA_CMP_STRIDE * HEAD_DIM)
    kvc = _nsa_compress(kv16, cmp_pe, cmp_w1, cmp_w2)
    o_cmp, sel = _nsa_cmp(cq, kvc, imp_map)
    o_c = _nsa_slc_win(cqr, sel, kslc, vslc, kwin, vwin, gates, o_cmp)

    dq, dk, dv, diq, dik, diw = _proj_d(x, w_d, cos128, sin128, cos64, sup64, sdn64)
    o_d = _dsa(dq, dk, dv, diq, dik, diw)
    return o_a, o_b, o_c, o_d


def kernel(x, mem, positions, ln_g, ln_b, ffn_w_gu, ffn_w_down, w_in, w_out, mla_g_cq, mla_g_ckv, mla_w_uq,
           mla_w_ukv, nsa_cmp_pe, nsa_cmp_w1, nsa_cmp_w2, mem_wq, mem_wkv, mem_wo):
    batch, s, d = x.shape
    w_gu_b, w_dn_b = ffn_w_gu, ffn_w_down
    w_groups = _split_w_in(w_in)
    w_out_b = w_out.astype(BF16)
    w_uq_b, w_ukv_b = _pad_w_uq(mla_w_uq), mla_w_ukv.astype(BF16)
    g_cq, g_ckv = mla_g_cq[:, None, :], mla_g_ckv[:, None, :]
    pe = nsa_cmp_pe.reshape(DEPTH, 2, 1, -1)
    w1_b, w2_b = nsa_cmp_w1.astype(BF16), nsa_cmp_w2.astype(BF16)
    wq_b, wkv_b, wo_b = mem_wq.astype(BF16), mem_wkv.astype(BF16), mem_wo.astype(BF16)
    g4, b4 = ln_g[:, :, None, :], ln_b[:, :, None, :]
    imp_map = _nsa_importance_map(s)

    outs = []
    for bi in range(batch):
        xb = x.reshape(s, d) if batch == 1 else x[bi]
        tabs = _rope_tables(positions[bi])
        for l in range(DEPTH):
            at = lambda a, *lead: (a, (l,) + lead)
            xb = _ffn_ln(xb, at(w_gu_b, 0), at(w_dn_b, 0), at(g4, 0), at(b4, 0))
            o_heads = _mixer_heads(xb, tabs, imp_map, tuple(at(w) for w in w_groups), at(g_cq), at(g_ckv),
                                   at(w_uq_b), at(w_ukv_b), at(pe), at(w1_b), at(w2_b))
            kv = _mem_kv(mem[bi], at(wkv_b))
            xb = _mix_mem_ln(*o_heads, at(w_out_b), xb, at(g4, 1), at(b4, 1), at(wq_b), kv, at(wo_b),
                             at(g4, 2), at(b4, 2))
            xb = _ffn_ln(xb, at(w_gu_b, 1), at(w_dn_b, 1), at(g4, 3), at(b4, 3))
        outs.append(xb)
    return outs[0].reshape(1, s, d) if batch == 1 else jnp.stack(outs)
```

```python
import functools

import numpy as np
import jax
import jax.numpy as jnp
from jax import lax
from jax.experimental import pallas as pl
from jax.experimental.pallas import tpu as pltpu

F32 = jnp.float32
BF16 = jnp.bfloat16
I32 = jnp.int32

D_MODEL = 2048
DEPTH = 4
HEAD_DIM = 128
ROPE_THETA = 10000.0
LN_EPS = 1e-5
RMS_EPS = 1e-6

MOBA_BLOCK = 256
MOBA_TOPK = 3
MLA_NOPE = 128
MLA_ROPE = 64
NSA_CMP_STRIDE = 16
NSA_SEL_BLOCK = 64
NSA_SEL_TOPK = 16
NSA_WINDOW = 512
DSA_IDX_HEADS = 16
DSA_TOPK = 256
D_FF = 5632
DN_ALPHA = (2 * DEPTH) ** 0.25

NEG = -(2.0 ** 100)
LOG2_E = 1.4426950408889634
INT_MIN = -(2 ** 31)
VMEM_LIMIT = 56 * 1024 * 1024

ROW_TILE = 512
FFN_ROWS = 1024
FFN_COLS = 256
ATT_TQ = 512
ATT_TK = 1024
CAUSAL_TK = 2048
DSA_TQ = 256
DSA_COUNT_ROWS = 128


def _cp(*sem):
    return pltpu.CompilerParams(dimension_semantics=sem, vmem_limit_bytes=VMEM_LIMIT)


def _wspec(w, block, index_map, resident=True):
    _, lead = w
    shape = (None,) * len(lead) + tuple(block)
    imap = lambda *g: tuple(lead) + tuple(index_map(*g))
    if resident:
        return pl.BlockSpec(shape, imap, pipeline_mode=pl.Buffered(1))
    return pl.BlockSpec(shape, imap)


def _wshape(w):
    arr, lead = w
    return arr.shape[len(lead):]


def _resident(shape, index_map):
    return pl.BlockSpec(shape, index_map, pipeline_mode=pl.Buffered(1))


def _dot(a, b):
    return jnp.dot(a, b, preferred_element_type=F32)


def _dot_nt(a, b):
    return lax.dot_general(a, b, (((1,), (1,)), ((), ())), preferred_element_type=F32)


def _layer_norm(y, g, b):
    mu = jnp.mean(y, axis=-1, keepdims=True)
    d = y - mu
    var = jnp.mean(d * d, axis=-1, keepdims=True)
    return d * lax.rsqrt(var + LN_EPS) * g + b


def _rope128(x, cos, sin_signed):
    return x * cos + pltpu.roll(x, 64, 1) * sin_signed


def _rope64(x, cos, s_up, s_dn):
    return x * cos + pltpu.roll(x, 32, 1) * s_up + pltpu.roll(x, 96, 1) * s_dn


def _log2_scale(d):
    return float(d) ** -0.5 * LOG2_E


def _with_ones(v):
    return jnp.concatenate([v, jnp.ones_like(v)], axis=1)


def _flash_update(s, v1, bias, keep, m_ref, acc_ref, slot):
    if bias is not None:
        s = s + bias
    if keep is not None:
        s = jnp.where(keep, s, NEG)
    m_old = m_ref[slot]
    m_new = jnp.maximum(m_old, jnp.max(s, axis=1, keepdims=True))
    alpha = jnp.exp2(m_old - m_new)
    p = jnp.exp2(s - m_new)
    acc_ref[slot] = alpha * acc_ref[slot] + _dot(p.astype(BF16), v1)
    m_ref[slot] = m_new


FLASH_ROWS = 256


def _flash_rows(q, k, v1, bias, keep, m_ref, acc_ref, hd):
    n = max(1, q.shape[0] // FLASH_ROWS)
    r = q.shape[0] // n
    rows = [slice(sp * r, (sp + 1) * r) for sp in range(n)]
    logits = [_dot_nt(q[rw], k) for rw in rows]
    for sp, rw in enumerate(rows):
        _flash_update(logits[sp], v1, None if bias is None else bias[rw], None if keep is None else keep[rw],
                      m_ref, acc_ref, hd * n + sp)


def _flash_heads(qs, ks, v1s, bias, keep, m_ref, acc_ref):
    def logits(hd):
        q = qs[hd]
        n = max(1, q.shape[0] // FLASH_ROWS)
        r = q.shape[0] // n
        rows = [slice(sp * r, (sp + 1) * r) for sp in range(n)]
        return rows, [_dot_nt(q[rw], ks[hd]) for rw in rows]

    ahead = logits(0)
    for hd in range(len(qs)):
        rows, s = ahead
        if hd + 1 < len(qs):
            ahead = logits(hd + 1)
        for sp, rw in enumerate(rows):
            _flash_update(s[sp], v1s[hd], None if bias is None else bias[rw], None if keep is None else keep[rw],
                          m_ref, acc_ref, hd * len(rows) + sp)


def _causal_flash(q, k_ref, v_ref, m_ref, acc_ref, i, tk):
    tq = q.shape[0]
    n_past = (i * tq) // tk
    n_rest = (i * tq - n_past * tk) // tq

    def tile(off, width, keep):
        _flash_rows(q, k_ref[pl.ds(off, width), :], _with_ones(v_ref[pl.ds(off, width), :]), None, keep,
                    m_ref, acc_ref, 0)

    def body(j, carry):
        tile(pl.multiple_of(j * tk, tk), tk, None)
        return carry

    lax.fori_loop(0, n_past, body, 0)
    for r in range(tk // tq):
        @pl.when(n_rest == r)
        def _():
            width = (r + 1) * tq
            row = lax.broadcasted_iota(I32, (tq, width), 0) + r * tq
            col = lax.broadcasted_iota(I32, (tq, width), 1)
            tile(pl.multiple_of(n_past * tk, tk), width, col <= row)


def _flash_result(acc_ref, hd, heads):
    n = acc_ref.shape[0] // heads
    parts = []
    for sp in range(n):
        a = acc_ref[hd * n + sp]
        parts.append(a[:, 0:HEAD_DIM] / a[:, HEAD_DIM:2 * HEAD_DIM])
    return jnp.concatenate(parts, axis=0) if n > 1 else parts[0]


def _flash_scratch(heads, tq):
    n = max(1, tq // FLASH_ROWS)
    return [pltpu.VMEM((heads * n, tq // n, 1), F32), pltpu.VMEM((heads * n, tq // n, 2 * HEAD_DIM), F32)]


def _flash_init(m_ref, acc_ref):
    m_ref[...] = jnp.full(m_ref.shape, NEG, F32)
    acc_ref[...] = jnp.zeros(acc_ref.shape, F32)


def _top_k_mask(work, index, k, axis=1):
    n = work.shape[axis]
    sel = jnp.zeros(work.shape, F32)
    for _ in range(k):
        m = jnp.max(work, axis=axis, keepdims=True)
        idx = jnp.min(jnp.where(work == m, index, n), axis=axis, keepdims=True)
        pick = index == jnp.where(m > -jnp.inf, idx, n)
        sel = jnp.where(pick, 1.0, sel)
        work = jnp.where(pick, -jnp.inf, work)
    return sel


def _key_block_table(s, block):
    return jnp.asarray((np.arange(128)[None, :] == (np.arange(s) // block)[:, None]) * -NEG, dtype=BF16)


def _rope_table_kernel(pos_ref, c_ref, cos128_ref, sin128_ref, cos64_ref, sup64_ref, sdn64_ref):
    pos = pos_ref[...]
    a128 = pos * c_ref[0:1, :]
    cos128_ref[...] = jnp.cos(a128)
    sin128_ref[...] = jnp.sin(a128) * c_ref[1:2, :]
    a64 = pos * c_ref[2:3, :]
    s64 = jnp.sin(a64)
    cos64_ref[...] = jnp.cos(a64)
    sup64_ref[...] = s64 * c_ref[3:4, :]
    sdn64_ref[...] = s64 * c_ref[4:5, :]


def _rope_consts():
    lane = np.arange(128)
    inv128 = (ROPE_THETA ** (-np.arange(0, 128, 2, dtype=np.float32) / 128)).astype(np.float32)
    inv64 = (ROPE_THETA ** (-np.arange(0, 64, 2, dtype=np.float32) / 64)).astype(np.float32)
    c = np.zeros((8, 128), np.float32)
    c[0] = inv128[lane % 64]
    c[1] = np.where(lane < 64, -1.0, 1.0)
    c[2] = inv64[(lane % 64) % 32]
    c[3] = np.where(lane % 64 >= 32, 1.0, 0.0)
    c[4] = np.where(lane % 64 < 32, -1.0, 0.0)
    return jnp.asarray(c)


def _rope_tables(positions):
    s = positions.shape[0]
    tm = min(ROW_TILE, s)
    pos = positions.astype(F32).reshape(s, 1)
    tab = jax.ShapeDtypeStruct((s, 128), F32)
    row = pl.BlockSpec((tm, 128), lambda i: (i, 0))
    return pl.pallas_call(
        _rope_table_kernel,
        grid=(s // tm,),
        in_specs=[pl.BlockSpec((tm, 1), lambda i: (i, 0)), pl.BlockSpec((8, 128), lambda i: (0, 0))],
        out_specs=[row] * 5,
        out_shape=[tab] * 5,
        compiler_params=_cp("parallel"),
        name="rope_tables",
    )(pos, _rope_consts())


def _ffn_kernel(x_ref, wg_ref, wu_ref, wd_ref, g_ref, b_ref, o_ref, xb_ref):
    j = pl.program_id(1)

    @pl.when(j == 0)
    def _():
        xb_ref[...] = x_ref[...].astype(BF16)
        o_ref[...] = jnp.zeros_like(o_ref)

    xb = xb_ref[...]
    g = _dot(xb, wg_ref[...].astype(BF16))
    u = _dot(xb, wu_ref[...].astype(BF16))
    h = (g * jax.nn.sigmoid(g) * u).astype(BF16)
    o_ref[...] += _dot(h, wd_ref[...].astype(BF16))

    @pl.when(j == pl.num_programs(1) - 1)
    def _():
        y = DN_ALPHA * x_ref[...] + 0.5 * o_ref[...]
        o_ref[...] = _layer_norm(y, g_ref[...], b_ref[...])


def _ffn_ln(x, w_gu, w_down, g, b):
    s, d = x.shape
    f = _wshape(w_down)[0]
    tm = min(FFN_ROWS, s)
    tf = FFN_COLS
    nf = f // tf
    return pl.pallas_call(
        _ffn_kernel,
        grid=(s // tm, nf),
        in_specs=[
            pl.BlockSpec((tm, d), lambda i, j: (i, 0), pipeline_mode=pl.Buffered(1)),
            _wspec(w_gu, (d, tf), lambda i, j: (0, j), resident=False),
            _wspec(w_gu, (d, tf), lambda i, j: (0, j + nf), resident=False),
            _wspec(w_down, (tf, d), lambda i, j: (j, 0), resident=False),
            _wspec(g, (1, d), lambda i, j: (0, 0)),
            _wspec(b, (1, d), lambda i, j: (0, 0)),
        ],
        out_specs=pl.BlockSpec((tm, d), lambda i, j: (i, 0)),
        out_shape=jax.ShapeDtypeStruct((s, d), F32),
        scratch_shapes=[pltpu.VMEM((tm, d), BF16)],
        compiler_params=_cp("parallel", "arbitrary"),
        name="ffn_ln",
    )(x, w_gu[0], w_gu[0], w_down[0], g[0], b[0])


def _proj_a_kernel(x_ref, w_ref, cos_ref, sin_ref, kb_ref, q_ref, k_ref, v_ref, km_ref):
    h = _dot(x_ref[...].astype(BF16), w_ref[...])
    cos, sin = cos_ref[...], sin_ref[...]
    nblk = km_ref.shape[0]
    for hd in range(4):
        lo = hd * HEAD_DIM
        q_ref[:, lo:lo + HEAD_DIM] = _rope128(h[:, lo:lo + HEAD_DIM], cos, sin)
        k = _rope128(h[:, 512 + lo:512 + lo + HEAD_DIM], cos, sin)
        k_ref[:, 2 * lo:2 * lo + HEAD_DIM] = k.astype(BF16)
        k_ref[:, 2 * lo + HEAD_DIM:2 * lo + 2 * HEAD_DIM] = kb_ref[...]
        for bi in range(nblk):
            kb = k[bi * MOBA_BLOCK:(bi + 1) * MOBA_BLOCK]
            km_ref[bi, :, lo:lo + HEAD_DIM] = jnp.mean(kb, axis=0, keepdims=True)
    v_ref[...] = h[:, 1024:1536].astype(BF16)


def _proj_a(x, w, cos128, sin128):
    s, d = x.shape
    tm = min(ROW_TILE, s)
    nblk = tm // MOBA_BLOCK
    row512 = pl.BlockSpec((tm, 512), lambda i: (i, 0))
    tab = pl.BlockSpec((tm, 128), lambda i: (i, 0))
    return pl.pallas_call(
        _proj_a_kernel,
        grid=(s // tm,),
        in_specs=[pl.BlockSpec((tm, d), lambda i: (i, 0)), _wspec(w, _wshape(w), lambda i: (0, 0)), tab, tab, tab],
        out_specs=[row512, pl.BlockSpec((tm, 1024), lambda i: (i, 0)), row512,
                   pl.BlockSpec((nblk, 1, 512), lambda i: (i, 0, 0))],
        out_shape=[
            jax.ShapeDtypeStruct((s, 512), F32),
            jax.ShapeDtypeStruct((s, 1024), BF16),
            jax.ShapeDtypeStruct((s, 512), BF16),
            jax.ShapeDtypeStruct((s // MOBA_BLOCK, 1, 512), F32),
        ],
        compiler_params=_cp("parallel"),
        name="proj_moba",
    )(x, w[0], cos128, sin128, _key_block_table(s, MOBA_BLOCK))


def _rms_norm(x, g):
    return x * lax.rsqrt(jnp.mean(x * x, axis=-1, keepdims=True) + RMS_EPS) * g


def _proj_b_kernel(x_ref, w_ref, gq_ref, gkv_ref, wuq_ref, wukv_ref, cos_ref, sup_ref, sdn_ref,
                   q_ref, k_ref, v_ref):
    h = _dot(x_ref[...].astype(BF16), w_ref[...])
    cos, sup, sdn = cos_ref[...], sup_ref[...], sdn_ref[...]
    cq = _rms_norm(h[:, 0:512], gq_ref[...]).astype(BF16)
    ckv = _rms_norm(h[:, 512:1024], gkv_ref[...]).astype(BF16)
    kr = _rope64(h[:, 1024:1152], cos, sup, sdn).astype(BF16)
    q = _dot(cq, wuq_ref[...])
    kv = _dot(ckv, wukv_ref[...])
    qs = _log2_scale(MLA_NOPE + MLA_ROPE)
    for hd in range(4):
        lo = hd * 256
        q_ref[:, lo:lo + 128] = (q[:, lo:lo + 128] * qs).astype(BF16)
        q_ref[:, lo + 128:lo + 256] = (_rope64(q[:, lo + 128:lo + 256], cos, sup, sdn) * qs).astype(BF16)
        k_ref[:, lo:lo + 128] = kv[:, lo:lo + 128].astype(BF16)
        k_ref[:, lo + 128:lo + 256] = kr
        v_ref[:, hd * 128:(hd + 1) * 128] = kv[:, lo + 128:lo + 256].astype(BF16)


def _proj_b(x, w, g_cq, g_ckv, w_uq, w_ukv, cos64, sup64, sdn64):
    s, d = x.shape
    tm = min(ROW_TILE, s)
    tab = pl.BlockSpec((tm, 128), lambda i: (i, 0))
    full = lambda a: _wspec(a, _wshape(a), lambda i: (0, 0))
    return pl.pallas_call(
        _proj_b_kernel,
        grid=(s // tm,),
        in_specs=[pl.BlockSpec((tm, d), lambda i: (i, 0)), full(w), full(g_cq), full(g_ckv), full(w_uq),
                  full(w_ukv), tab, tab, tab],
        out_specs=[pl.BlockSpec((tm, 1024), lambda i: (i, 0)), pl.BlockSpec((tm, 1024), lambda i: (i, 0)),
                   pl.BlockSpec((tm, 512), lambda i: (i, 0))],
        out_shape=[jax.ShapeDtypeStruct((s, 1024), BF16), jax.ShapeDtypeStruct((s, 1024), BF16),
                   jax.ShapeDtypeStruct((s, 512), BF16)],
        compiler_params=_cp("parallel"),
        name="proj_mla",
    )(x, w[0], g_cq[0], g_ckv[0], w_uq[0], w_ukv[0], cos64, sup64, sdn64)


def _proj_c_kernel(x_ref, w_ref, cos_ref, sin_ref, kb_ref, q_ref, qr_ref, kvcmp_ref, kslc_ref, vslc_ref,
                   kwin_ref, vwin_ref, gate_ref):
    h = _dot(x_ref[...].astype(BF16), w_ref[...])
    cos, sin = cos_ref[...], sin_ref[...]
    for hd in range(4):
        lo = hd * HEAD_DIM
        qh = h[:, lo:lo + HEAD_DIM]
        q_ref[:, lo:lo + HEAD_DIM] = qh.astype(BF16)
        qr_ref[:, lo:lo + HEAD_DIM] = (_rope128(qh, cos, sin) * _log2_scale(HEAD_DIM)).astype(BF16)
    kv = lambda i: h[:, 512 + i * 128:512 + (i + 1) * 128]
    kvcmp_ref[0] = kv(0).astype(BF16)
    kvcmp_ref[1] = kv(1).astype(BF16)
    kslc_ref[:, 0:HEAD_DIM] = _rope128(kv(2), cos, sin).astype(BF16)
    kslc_ref[:, HEAD_DIM:2 * HEAD_DIM] = kb_ref[...]
    vslc_ref[...] = kv(3).astype(BF16)
    kwin_ref[...] = _rope128(kv(4), cos, sin).astype(BF16)
    vwin_ref[...] = kv(5).astype(BF16)
    gate_ref[...] = jax.nn.sigmoid(h[:, 1280:1408])


def _proj_c(x, w, cos128, sin128):
    s, d = x.shape
    tm = min(ROW_TILE, s)
    r512 = pl.BlockSpec((tm, 512), lambda i: (i, 0))
    r128 = pl.BlockSpec((tm, 128), lambda i: (i, 0))
    b512 = jax.ShapeDtypeStruct((s, 512), BF16)
    b128 = jax.ShapeDtypeStruct((s, 128), BF16)
    return pl.pallas_call(
        _proj_c_kernel,
        grid=(s // tm,),
        in_specs=[pl.BlockSpec((tm, d), lambda i: (i, 0)), _wspec(w, _wshape(w), lambda i: (0, 0)), r128, r128, r128],
        out_specs=[r512, r512, pl.BlockSpec((2, tm, 128), lambda i: (0, i, 0)),
                   pl.BlockSpec((tm, 256), lambda i: (i, 0))] + [r128] * 4,
        out_shape=[b512, b512, jax.ShapeDtypeStruct((2, s, 128), BF16), jax.ShapeDtypeStruct((s, 256), BF16)]
                  + [b128] * 3 + [jax.ShapeDtypeStruct((s, 128), F32)],
        compiler_params=_cp("parallel"),
        name="proj_nsa",
    )(x, w[0], cos128, sin128, _key_block_table(s, NSA_SEL_BLOCK))


def _proj_d_kernel(x_ref, w_ref, cos_ref, sin_ref, c64_ref, sup_ref, sdn_ref,
                   q_ref, k_ref, v_ref, iq_ref, ik_ref, iw_ref):
    h = _dot(x_ref[...].astype(BF16), w_ref[...])
    cos, sin = cos_ref[...], sin_ref[...]
    c64, sup, sdn = c64_ref[...], sup_ref[...], sdn_ref[...]
    for hd in range(4):
        lo = hd * HEAD_DIM
        q_ref[:, lo:lo + HEAD_DIM] = (_rope128(h[:, lo:lo + HEAD_DIM], cos, sin) * _log2_scale(HEAD_DIM)).astype(BF16)
        k_ref[:, lo:lo + HEAD_DIM] = _rope128(h[:, 512 + lo:512 + lo + HEAD_DIM], cos, sin).astype(BF16)
    v_ref[...] = h[:, 1024:1536].astype(BF16)
    for p in range(8):
        lo = 1536 + p * 128
        iq_ref[:, p * 128:(p + 1) * 128] = _rope64(h[:, lo:lo + 128], c64, sup, sdn).astype(BF16)
    ik = _rope64(h[:, 2560:2688], c64, sup, sdn)
    ik_ref[:, 0:128] = ik.astype(BF16)
    ik_ref[:, 128:256] = pltpu.roll(ik, 64, 1).astype(BF16)
    iw_ref[...] = h[:, 2688:2816] * (1.0 / 32.0)


def _proj_d(x, w, cos128, sin128, cos64, sup64, sdn64):
    s, d = x.shape
    tm = min(ROW_TILE, s)
    r = lambda n: pl.BlockSpec((tm, n), lambda i: (i, 0))
    return pl.pallas_call(
        _proj_d_kernel,
        grid=(s // tm,),
        in_specs=[pl.BlockSpec((tm, d), lambda i: (i, 0)), _wspec(w, _wshape(w), lambda i: (0, 0))] + [r(128)] * 5,
        out_specs=[r(512), r(512), r(512), r(1024), r(256), r(128)],
        out_shape=[jax.ShapeDtypeStruct((s, 512), BF16)] * 3 + [
            jax.ShapeDtypeStruct((s, 1024), BF16), jax.ShapeDtypeStruct((s, 256), BF16),
            jax.ShapeDtypeStruct((s, 128), F32)],
        compiler_params=_cp("parallel"),
        name="proj_dsa",
    )(x, w[0], cos128, sin128, cos64, sup64, sdn64)


def _moba_kernel(q_ref, k_ref, v_ref, km_ref, o_ref, m_ref, acc_ref, *, tk, nb):
    i = pl.program_id(1)
    tq = q_ref.shape[0]
    q32 = q_ref[...]

    gate = lax.dot_general(q32, km_ref[...], (((1,), (1,)), ((), ())),
                           precision=lax.Precision.HIGHEST, preferred_element_type=F32)
    blk = lax.broadcasted_iota(I32, (128, tq), 0)
    cur = (i * tq + lax.broadcasted_iota(I32, (128, tq), 1)) // MOBA_BLOCK
    sel = _top_k_mask(jnp.where(blk < cur, gate.T, -jnp.inf), blk, min(MOBA_TOPK, nb - 1), axis=0)
    unchosen = (jnp.where(blk == cur, 1.0, sel) - 1.0).T
    qa = jnp.concatenate([q32 * _log2_scale(HEAD_DIM), unchosen], axis=1).astype(BF16)

    _flash_init(m_ref, acc_ref)
    _causal_flash(qa, k_ref, v_ref, m_ref, acc_ref, i, tk)
    o_ref[...] = _flash_result(acc_ref, 0, 1).astype(o_ref.dtype)


def _moba(q32, k, v, kmean):
    s = q32.shape[0]
    nb = s // MOBA_BLOCK
    tq = min(ATT_TQ, s)
    tk = min(CAUSAL_TK, s)
    km = jnp.pad(kmean.reshape(nb, 512), ((0, 128 - nb), (0, 0)))
    return pl.pallas_call(
        functools.partial(_moba_kernel, tk=tk, nb=nb),
        grid=(4, s // tq),
        in_specs=[
            pl.BlockSpec((tq, HEAD_DIM), lambda h, i: (i, h)),
            pl.BlockSpec((s, 2 * HEAD_DIM), lambda h, i: (0, h)),
            pl.BlockSpec((s, HEAD_DIM), lambda h, i: (0, h)),
            pl.BlockSpec((128, HEAD_DIM), lambda h, i: (0, h)),
        ],
        out_specs=pl.BlockSpec((tq, HEAD_DIM), lambda h, i: (i, h)),
        out_shape=jax.ShapeDtypeStruct((s, 512), BF16),
        scratch_shapes=_flash_scratch(1, tq),
        compiler_params=_cp("parallel", "arbitrary"),
        name="moba_attn",
    )(q32, k, v, km)


def _mla_kernel(q_ref, k_ref, v_ref, o_ref, m_ref, acc_ref, *, tk):
    _flash_init(m_ref, acc_ref)
    _causal_flash(q_ref[...], k_ref, v_ref, m_ref, acc_ref, pl.program_id(1), tk)
    o_ref[...] = _flash_result(acc_ref, 0, 1).astype(o_ref.dtype)


def _mla(qcat, kcat, v):
    s = qcat.shape[0]
    tq = min(ATT_TQ, s)
    tk = min(CAUSAL_TK, s)
    return pl.pallas_call(
        functools.partial(_mla_kernel, tk=tk),
        grid=(4, s // tq),
        in_specs=[
            pl.BlockSpec((tq, 256), lambda h, i: (i, h)),
            pl.BlockSpec((s, 256), lambda h, i: (0, h)),
            pl.BlockSpec((s, HEAD_DIM), lambda h, i: (0, h)),
        ],
        out_specs=pl.BlockSpec((tq, HEAD_DIM), lambda h, i: (i, h)),
        out_shape=jax.ShapeDtypeStruct((s, 512), BF16),
        scratch_shapes=_flash_scratch(1, tq),
        compiler_params=_cp("parallel", "arbitrary"),
        name="mla_attn",
    )(qcat, kcat, v)


def _gelu_tanh(x):
    return 0.5 * x * (1.0 + jnp.tanh(0.7978845608028654 * (x + 0.044715 * x * x * x)))


def _nsa_compress_kernel(t_ref, pe_ref, w1_ref, w2_ref, o_ref):
    t = t_ref[0]
    w1 = w1_ref[...]
    half = t.shape[1]
    n = t.shape[0]
    lo = _dot(t, w1[0:half])
    hi = _dot(t, w1[half:2 * half])
    pe = _dot(pe_ref[...].astype(BF16), w1)
    pre = lo + pltpu.roll(hi, n - 1, 0) + pe
    o_ref[0] = _dot(_gelu_tanh(pre).astype(BF16), w2_ref[...]).astype(o_ref.dtype)


def _nsa_compress(kv16, pe, w1, w2):
    n = kv16.shape[1]
    sub = lambda a: _wspec(a, (None,) + _wshape(a)[1:], lambda i: (i, 0, 0), resident=False)
    return pl.pallas_call(
        _nsa_compress_kernel,
        grid=(2,),
        in_specs=[pl.BlockSpec((1,) + kv16.shape[1:], lambda i: (i, 0, 0)), sub(pe), sub(w1), sub(w2)],
        out_specs=pl.BlockSpec((1, n, HEAD_DIM), lambda i: (i, 0, 0)),
        out_shape=jax.ShapeDtypeStruct((2, n, HEAD_DIM), BF16),
        compiler_params=_cp("parallel"),
        name="nsa_compress",
    )(kv16, pe[0], w1[0], w2[0])


def _nsa_cmp_kernel(q_ref, kvc_ref, m_ref, o_ref, sel_ref, *, n_sel):
    i = pl.program_id(0)
    tq = q_ref.shape[0]
    n = kvc_ref.shape[1]
    nblk = m_ref.shape[1]
    scale = HEAD_DIM ** -0.5
    kc, vc = kvc_ref[0], kvc_ref[1]
    t_pos = i * tq + lax.broadcasted_iota(I32, (tq, n), 0)
    cmp_end = lax.broadcasted_iota(I32, (tq, n), 1) * NSA_CMP_STRIDE + (2 * NSA_CMP_STRIDE - 1)
    ok = cmp_end <= t_pos
    p_sum = jnp.zeros((tq, n), F32)
    for hd in range(4):
        lo = hd * HEAD_DIM
        s = jnp.where(ok, _dot_nt(q_ref[:, lo:lo + HEAD_DIM], kc) * scale, NEG)
        m = jnp.max(s, axis=1, keepdims=True)
        e = jnp.where(ok, jnp.exp(s - m), 0.0)
        l = jnp.sum(e, axis=1, keepdims=True)
        p = e / jnp.where(l > 0, l, 1.0)
        p_sum = p_sum + p
        o_ref[:, lo:lo + HEAD_DIM] = _dot(p.astype(BF16), vc)

    imp = jnp.dot(p_sum, m_ref[...], precision=lax.Precision.HIGHEST, preferred_element_type=F32)
    imp_t = imp.T
    blk = lax.broadcasted_iota(I32, (nblk, tq), 0)
    cur = (i * tq + lax.broadcasted_iota(I32, (nblk, tq), 1)) // NSA_SEL_BLOCK
    forced = (blk == 0) | (blk == cur) | (blk == cur - 1)
    imp_t = jnp.where(blk > cur, -jnp.inf, jnp.where(forced, jnp.inf, imp_t))
    sel_ref[...] = (_top_k_mask(imp_t, blk, n_sel, axis=0).T - 1.0).astype(sel_ref.dtype)


def _nsa_cmp(q, kvc, imp_map):
    s = q.shape[0]
    n = kvc.shape[1]
    nblk = imp_map.shape[1]
    tq = min(256, s)
    return pl.pallas_call(
        functools.partial(_nsa_cmp_kernel, n_sel=min(NSA_SEL_TOPK, s // NSA_SEL_BLOCK)),
        grid=(s // tq,),
        in_specs=[pl.BlockSpec((tq, 512), lambda i: (i, 0)),
                  _resident((2, n, HEAD_DIM), lambda i: (0, 0, 0)),
                  _resident((n, nblk), lambda i: (0, 0))],
        out_specs=[pl.BlockSpec((tq, 512), lambda i: (i, 0)), pl.BlockSpec((tq, nblk), lambda i: (i, 0))],
        out_shape=[jax.ShapeDtypeStruct((s, 512), F32), jax.ShapeDtypeStruct((s, nblk), BF16)],
        compiler_params=_cp("parallel"),
        name="nsa_cmp_select",
    )(q, kvc, imp_map)


def _nsa_slc_win_kernel(q_ref, sel_ref, ks_ref, vs_ref, kw_ref, vw_ref, g_ref, oc_ref, o_ref,
                        m_ref, acc_ref, *, tk):
    i = pl.program_id(0)
    tq = q_ref.shape[0]
    _flash_init(m_ref, acc_ref)

    def tile(off, width, keep):
        k = ks_ref[pl.ds(off, width), 0:HEAD_DIM]
        v1 = _with_ones(vs_ref[pl.ds(off, width), :])
        bias = _dot_nt(sel_ref[...], ks_ref[pl.ds(off, width), HEAD_DIM:2 * HEAD_DIM])
        _flash_heads([q_ref[:, hd * HEAD_DIM:(hd + 1) * HEAD_DIM] for hd in range(4)], [k] * 4, [v1] * 4,
                     bias, keep, m_ref, acc_ref)

    n_past = (i * tq) // tk
    n_rest = (i * tq - n_past * tk) // tq

    def body(j, carry):
        tile(pl.multiple_of(j * tk, tk), tk, None)
        return carry

    lax.fori_loop(0, n_past, body, 0)
    for r in range(tk // tq):
        @pl.when(n_rest == r)
        def _():
            width = (r + 1) * tq
            row = lax.broadcasted_iota(I32, (tq, width), 0) + r * tq
            col = lax.broadcasted_iota(I32, (tq, width), 1)
            tile(pl.multiple_of(n_past * tk, tk), width, col <= row)

    wlen = NSA_WINDOW + tq
    start = pl.multiple_of(jnp.maximum(i * tq - NSA_WINDOW, 0), 256)
    kw = kw_ref[pl.ds(start, wlen), :]
    vw1 = _with_ones(vw_ref[pl.ds(start, wlen), :])
    diff = (i * tq + lax.broadcasted_iota(I32, (tq, wlen), 0)) - (start + lax.broadcasted_iota(I32, (tq, wlen), 1))
    in_win = (diff >= 0) & (diff < NSA_WINDOW)
    g = g_ref[...]
    for hd in range(4):
        lo = hd * HEAD_DIM
        s = jnp.where(in_win, _dot_nt(q_ref[:, lo:lo + HEAD_DIM], kw), NEG)
        p = jnp.exp2(s - jnp.max(s, axis=1, keepdims=True))
        pv = _dot(p.astype(BF16), vw1)
        o_win = pv[:, 0:HEAD_DIM] / pv[:, HEAD_DIM:2 * HEAD_DIM]
        o_slc = _flash_result(acc_ref, hd, 4)
        out = (g[:, 3 * hd:3 * hd + 1] * oc_ref[:, lo:lo + HEAD_DIM] + g[:, 3 * hd + 1:3 * hd + 2] * o_slc
               + g[:, 3 * hd + 2:3 * hd + 3] * o_win)
        o_ref[:, lo:lo + HEAD_DIM] = out.astype(o_ref.dtype)


def _nsa_slc_win(q_r, sel, kslc, vslc, kwin, vwin, gates, o_cmp):
    s = q_r.shape[0]
    tq = min(ATT_TQ, s)
    tk = min(ATT_TK, s)
    full = lambda a: _resident(a.shape, lambda i: (0,) * a.ndim)
    return pl.pallas_call(
        functools.partial(_nsa_slc_win_kernel, tk=tk),
        grid=(s // tq,),
        in_specs=[pl.BlockSpec((tq, 512), lambda i: (i, 0)), pl.BlockSpec((tq, 128), lambda i: (i, 0)),
                  full(kslc), full(vslc), full(kwin), full(vwin),
                  pl.BlockSpec((tq, 128), lambda i: (i, 0)), pl.BlockSpec((tq, 512), lambda i: (i, 0))],
        out_specs=pl.BlockSpec((tq, 512), lambda i: (i, 0)),
        out_shape=jax.ShapeDtypeStruct((s, 512), BF16),
        scratch_shapes=_flash_scratch(4, tq),
        compiler_params=_cp("parallel"),
        name="nsa_slc_win",
    )(q_r, sel, kslc, vslc, kwin, vwin, gates, o_cmp)


def _dsa_kernel(q_ref, iq_ref, iw_ref, k_ref, v_ref, ik_ref, o_ref, key_ref, wb_ref, m_ref, acc_ref, *, tk):
    i = pl.program_id(0)
    tq = q_ref.shape[0]
    half = tk // 2
    n_tiles = (i * tq + tq + tk - 1) // tk
    row_h = lax.broadcasted_iota(I32, (tq, half), 0) + i * tq
    col_h = lax.broadcasted_iota(I32, (tq, half), 1)

    w = iw_ref[...]
    for hd in range(DSA_IDX_HEADS):
        wb_ref[hd] = jnp.broadcast_to(w[:, hd:hd + 1], (tq, 128))

    def score_body(j, carry):
        for hf in range(2):
            off = pl.multiple_of(j * tk + hf * half, half)
            ik_even = ik_ref[pl.ds(off, half), 0:128]
            ik_odd = ik_ref[pl.ds(off, half), 128:256]
            sc = jnp.zeros((tq, half), F32)
            for p in range(DSA_IDX_HEADS // 2):
                x = iq_ref[:, p * 128:(p + 1) * 128]
                we = jnp.concatenate([wb_ref[2 * p]] * (half // 128), axis=1)
                wo = jnp.concatenate([wb_ref[2 * p + 1]] * (half // 128), axis=1)
                sc = sc + jnp.maximum(_dot_nt(x, ik_even), 0.0) * we
                sc = sc + jnp.maximum(_dot_nt(x, ik_odd), 0.0) * wo
            bits = pltpu.bitcast(sc, I32)
            key = jnp.where(bits >= 0, bits, bits ^ 0x7FFFFFFF)
            key_ref[j, :, hf * half:(hf + 1) * half] = jnp.where(col_h + off <= row_h, key, INT_MIN)
        return carry

    lax.fori_loop(0, n_tiles, score_body, 0)

    rb = min(DSA_COUNT_ROWS, tq)

    ones = jnp.ones((128, 128), BF16)
    k_f = float(DSA_TOPK)

    def count_ge(cand128):
        parts = []
        for r0 in range(0, tq, rb):
            cand_b = cand128[r0:r0 + rb]

            def body(j, acc):
                for c in range(tk // 128):
                    acc = acc + jnp.where(key_ref[j, r0:r0 + rb, c * 128:(c + 1) * 128] >= cand_b, 1, 0)
                return acc

            parts.append(lax.fori_loop(0, n_tiles, body, jnp.zeros((rb, 128), I32)))
        per_lane = jnp.concatenate(parts, axis=0).astype(F32).astype(BF16)
        return _dot(per_lane, ones)

    zero = jnp.zeros((tq, 128), I32)
    c0 = count_ge(zero)
    lo0 = jnp.where(c0 >= k_f, zero, INT_MIN)
    c_lo0 = jnp.where(c0 >= k_f, c0, -1.0)

    def bit_cond(state):
        b, _, _, more = state
        return (b < 31) & (more > 0.0)

    def bit_body(state):
        b, lo, c_lo, _ = state
        more = jnp.max(jnp.where(c_lo != k_f, 1.0, 0.0))
        cand = lo | lax.shift_left(jnp.int32(1), 30 - b)
        c = count_ge(cand)
        take = c >= k_f
        return b + 1, jnp.where(take, cand, lo), jnp.where(take, c, c_lo), more

    _, lo, c_lo, _ = lax.while_loop(bit_cond, bit_body, (jnp.int32(0), lo0, c_lo0, jnp.float32(1.0)))
    thr128 = jnp.maximum(lo, INT_MIN + 1)

    @pl.when(jnp.max(jnp.where(c_lo > k_f, 1.0, 0.0)) > 0.0)
    def _():
        need = k_f - count_ge(lo + 1)
        lane = lax.broadcasted_iota(I32, (rb, 128), 1)

        def tied_pos(j, r0, c):
            tied = key_ref[j, r0:r0 + rb, c * 128:(c + 1) * 128] == lo[r0:r0 + rb]
            return jnp.where(tied, j * tk + c * 128 + lane, -1)

        def count_tied_before(limit):
            parts = []
            for r0 in range(0, tq, rb):
                lim_b = limit[r0:r0 + rb]

                def body(j, acc, r0=r0, lim_b=lim_b):
                    for c in range(tk // 128):
                        pos = tied_pos(j, r0, c)
                        acc = acc + jnp.where(jnp.where(pos >= 0, pos, lim_b) < lim_b, 1, 0)
                    return acc

                parts.append(lax.fori_loop(0, n_tiles, body, jnp.zeros((rb, 128), I32)))
            return _dot(jnp.concatenate(parts, axis=0).astype(F32).astype(BF16), ones)

        n_bits = (key_ref.shape[0] * tk - 1).bit_length()

        def pos_body(b, last):
            cand = last | lax.shift_left(jnp.int32(1), n_bits - 1 - b)
            return jnp.where(count_tied_before(cand) < need, cand, last)

        last = lax.fori_loop(0, n_bits, pos_body, jnp.zeros((tq, 128), I32))
        last = jnp.where(c_lo > k_f, last, 2 ** 31 - 1)

        for r0 in range(0, tq, rb):
            def drop_body(j, carry, r0=r0):
                for c in range(tk // 128):
                    cols = slice(c * 128, (c + 1) * 128)
                    beyond = tied_pos(j, r0, c) > last[r0:r0 + rb]
                    key_ref[j, r0:r0 + rb, cols] = jnp.where(beyond, INT_MIN, key_ref[j, r0:r0 + rb, cols])
                return carry

            lax.fori_loop(0, n_tiles, drop_body, 0)

    _flash_init(m_ref, acc_ref)

    def attn_body(j, carry):
        off = pl.multiple_of(j * tk, tk)
        keep = key_ref[j] >= jnp.concatenate([thr128] * (tk // 128), axis=1)
        cols = [slice(hd * HEAD_DIM, (hd + 1) * HEAD_DIM) for hd in range(4)]
        _flash_heads([q_ref[:, c] for c in cols], [k_ref[pl.ds(off, tk), c] for c in cols],
                     [_with_ones(v_ref[pl.ds(off, tk), c]) for c in cols], None, keep, m_ref, acc_ref)
        return carry

    lax.fori_loop(0, n_tiles, attn_body, 0)
    for hd in range(4):
        o_ref[:, hd * HEAD_DIM:(hd + 1) * HEAD_DIM] = _flash_result(acc_ref, hd, 4).astype(o_ref.dtype)


def _dsa(q, k, v, iq, ik2, iw):
    s = q.shape[0]
    tq = min(DSA_TQ, s)
    tk = min(ATT_TK, s)
    full = lambda a: _resident(a.shape, lambda i: (0, 0))
    return pl.pallas_call(
        functools.partial(_dsa_kernel, tk=tk),
        grid=(s // tq,),
        in_specs=[pl.BlockSpec((tq, 512), lambda i: (i, 0)), pl.BlockSpec((tq, 1024), lambda i: (i, 0)),
                  pl.BlockSpec((tq, 128), lambda i: (i, 0)), full(k), full(v), full(ik2)],
        out_specs=pl.BlockSpec((tq, 512), lambda i: (i, 0)),
        out_shape=jax.ShapeDtypeStruct((s, 512), BF16),
        scratch_shapes=[pltpu.VMEM((s // tk, tq, tk), I32), pltpu.VMEM((DSA_IDX_HEADS, tq, 128), F32)]
                       + _flash_scratch(4, tq),
        compiler_params=_cp("parallel"),
        name="dsa_attn",
    )(q, iq, iw, k, v, ik2)


def _mix_mem_kernel(oa_ref, ob_ref, oc_ref, od_ref, w_ref, x_ref, g1_ref, b1_ref, wq_ref, kv_ref, wo_ref,
                    g2_ref, b2_ref, o_ref):
    mix = _dot(oa_ref[...], w_ref[0:512, :])
    mix = mix + _dot(ob_ref[...], w_ref[512:1024, :])
    mix = mix + _dot(oc_ref[...], w_ref[1024:1536, :])
    mix = mix + _dot(od_ref[...], w_ref[1536:2048, :])
    x = _layer_norm(DN_ALPHA * x_ref[...] + mix, g1_ref[...], b1_ref[...])

    q = _dot(x.astype(BF16), wq_ref[...])
    heads = []
    for hd in range(4):
        lo = hd * HEAD_DIM
        qh = (q[:, lo:lo + HEAD_DIM] * _log2_scale(HEAD_DIM)).astype(BF16)
        s = _dot_nt(qh, kv_ref[:, lo:lo + HEAD_DIM])
        p = jnp.exp2(s - jnp.max(s, axis=1, keepdims=True))
        o = _dot(p.astype(BF16), kv_ref[:, 512 + lo:512 + lo + HEAD_DIM]) / jnp.sum(p, axis=1, keepdims=True)
        heads.append(o.astype(BF16))
    out = _dot(jnp.concatenate(heads, axis=1), wo_ref[...])
    o_ref[...] = _layer_norm(DN_ALPHA * x + out, g2_ref[...], b2_ref[...])


def _mix_mem_ln(o_a, o_b, o_c, o_d, w_out, x, g1, b1, wq, kv, wo, g2, b2):
    s, d = x.shape
    tm = min(ROW_TILE, s)
    r512 = pl.BlockSpec((tm, 512), lambda i: (i, 0))
    full = lambda a: _wspec(a, _wshape(a), lambda i: (0, 0))
    return pl.pallas_call(
        _mix_mem_kernel,
        grid=(s // tm,),
        in_specs=[r512, r512, r512, r512, full(w_out), pl.BlockSpec((tm, d), lambda i: (i, 0)), full(g1), full(b1),
                  full(wq), _resident(kv.shape, lambda i: (0, 0)), full(wo), full(g2), full(b2)],
        out_specs=pl.BlockSpec((tm, d), lambda i: (i, 0)),
        out_shape=jax.ShapeDtypeStruct((s, d), F32),
        compiler_params=_cp("parallel"),
        name="mix_mem_ln",
    )(o_a, o_b, o_c, o_d, w_out[0], x, g1[0], b1[0], wq[0], kv, wo[0], g2[0], b2[0])


def _mem_kv_kernel(mem_ref, w_ref, o_ref):
    o_ref[...] = _dot(mem_ref[...].astype(BF16), w_ref[...]).astype(o_ref.dtype)


def _mem_kv(mem, wkv):
    m, d = mem.shape
    n = _wshape(wkv)[1]
    return pl.pallas_call(
        _mem_kv_kernel,
        grid=(1,),
        in_specs=[pl.BlockSpec((m, d), lambda i: (0, 0)), _wspec(wkv, (d, n), lambda i: (0, 0))],
        out_specs=pl.BlockSpec((m, n), lambda i: (0, 0)),
        out_shape=jax.ShapeDtypeStruct((m, n), BF16),
        compiler_params=_cp("arbitrary"),
        name="mem_kv",
    )(mem, wkv[0])


def _pad_last(w, n):
    return jnp.pad(w, [(0, 0)] * (w.ndim - 1) + [(0, n - w.shape[-1])])


def _split_w_in(w_in):
    a = w_in[..., 0:1536]
    b = _pad_last(w_in[..., 1536:2624], 1152)
    c = _pad_last(w_in[..., 2624:3916], 1408)
    d = jnp.concatenate([w_in[..., 3916:6476], _pad_last(w_in[..., 6476:6540], 128),
                         _pad_last(w_in[..., 6540:6556], 128)], axis=-1)
    return tuple(t.astype(BF16) for t in (a, b, c, d))


def _pad_w_uq(w_uq):
    lead = w_uq.shape[:-1]
    w = _pad_last(w_uq.reshape(lead + (4, MLA_NOPE + MLA_ROPE)), 256)
    return w.reshape(lead + (4 * 256,)).astype(BF16)


def _nsa_importance_map(s):
    n = s // NSA_CMP_STRIDE
    nblk = s // NSA_SEL_BLOCK
    ni = np.arange(n)[:, None]
    bi = np.arange(nblk)[None, :]
    m = ((ni >= 4 * bi - 1) & (ni <= 4 * bi + 3)).astype(np.float32)
    return jnp.asarray(np.pad(m, ((0, 0), (0, 128 - nblk))))


def _mixer_heads(x, tabs, imp_map, w_groups, g_cq, g_ckv, w_uq, w_ukv, cmp_pe, cmp_w1, cmp_w2):
    cos128, sin128, cos64, sup64, sdn64 = tabs
    s = x.shape[0]
    w_a, w_b, w_c, w_d = w_groups

    aq, ak, av, akm = _proj_a(x, w_a, cos128, sin128)
    o_a = _moba(aq, ak, av, akm)

    bq, bk, bv = _proj_b(x, w_b, g_cq, g_ckv, w_uq, w_ukv, cos64, sup64, sdn64)
    o_b = _mla(bq, bk, bv)

    cq, cqr, kvcmp, kslc, vslc, kwin, vwin, gates = _proj_c(x, w_c, cos128, sin128)
    kv16 = kvcmp.reshape(2, s // NSA_CMP_STRIDE, NSA_CMP_STRIDE * HEAD_DIM)
    kvc = _nsa_compress(kv16, cmp_pe, cmp_w1, cmp_w2)
    o_cmp, sel = _nsa_cmp(cq, kvc, imp_map)
    o_c = _nsa_slc_win(cqr, sel, kslc, vslc, kwin, vwin, gates, o_cmp)

    dq, dk, dv, diq, dik, diw = _proj_d(x, w_d, cos128, sin128, cos64, sup64, sdn64)
    o_d = _dsa(dq, dk, dv, diq, dik, diw)
    return o_a, o_b, o_c, o_d


def kernel(x, mem, positions, ln_g, ln_b, ffn_w_gu, ffn_w_down, w_in, w_out, mla_g_cq, mla_g_ckv, mla_w_uq,
           mla_w_ukv, nsa_cmp_pe, nsa_cmp_w1, nsa_cmp_w2, mem_wq, mem_wkv, mem_wo):
    batch, s, d = x.shape
    w_gu_b, w_dn_b = ffn_w_gu, ffn_w_down
    w_groups = _split_w_in(w_in)
    w_out_b = w_out.astype(BF16)
    w_uq_b, w_ukv_b = _pad_w_uq(mla_w_uq), mla_w_ukv.astype(BF16)
    g_cq, g_ckv = mla_g_cq[:, None, :], mla_g_ckv[:, None, :]
    pe = nsa_cmp_pe.reshape(DEPTH, 2, 1, -1)
    w1_b, w2_b = nsa_cmp_w1.astype(BF16), nsa_cmp_w2.astype(BF16)
    wq_b, wkv_b, wo_b = mem_wq.astype(BF16), mem_wkv.astype(BF16), mem_wo.astype(BF16)
    g4, b4 = ln_g[:, :, None, :], ln_b[:, :, None, :]
    imp_map = _nsa_importance_map(s)

    outs = []
    for bi in range(batch):
        xb = x.reshape(s, d) if batch == 1 else x[bi]
        tabs = _rope_tables(positions[bi])
        for l in range(DEPTH):
            at = lambda a, *lead: (a, (l,) + lead)
            xb = _ffn_ln(xb, at(w_gu_b, 0), at(w_dn_b, 0), at(g4, 0), at(b4, 0))
            o_heads = _mixer_heads(xb, tabs, imp_map, tuple(at(w) for w in w_groups), at(g_cq), at(g_ckv),
                                   at(w_uq_b), at(w_ukv_b), at(pe), at(w1_b), at(w2_b))
            kv = _mem_kv(mem[bi], at(wkv_b))
            xb = _mix_mem_ln(*o_heads, at(w_out_b), xb, at(g4, 1), at(b4, 1), at(wq_b), kv, at(wo_b),
                             at(g4, 2), at(b4, 2))
            xb = _ffn_ln(xb, at(w_gu_b, 1), at(w_dn_b, 1), at(g4, 3), at(b4, 3))
        outs.append(xb)
    return outs[0].reshape(1, s, d) if batch == 1 else jnp.stack(outs)
```

```python
import functools

import numpy as np
import jax
import jax.numpy as jnp
from jax import lax
from jax.experimental import pallas as pl
from jax.experimental.pallas import tpu as pltpu

F32 = jnp.float32
BF16 = jnp.bfloat16
I32 = jnp.int32

D_MODEL = 2048
DEPTH = 4
HEAD_DIM = 128
ROPE_THETA = 10000.0
LN_EPS = 1e-5
RMS_EPS = 1e-6

MOBA_BLOCK = 256
MOBA_TOPK = 3
MLA_NOPE = 128
MLA_ROPE = 64
NSA_CMP_STRIDE = 16
NSA_SEL_BLOCK = 64
NSA_SEL_TOPK = 16
NSA_WINDOW = 512
DSA_IDX_HEADS = 16
DSA_TOPK = 256
D_FF = 5632
DN_ALPHA = (2 * DEPTH) ** 0.25

NEG = -(2.0 ** 100)
LOG2_E = 1.4426950408889634
INT_MIN = -(2 ** 31)
VMEM_LIMIT = 56 * 1024 * 1024

ROW_TILE = 512
FFN_ROWS = 1024
FFN_COLS = 256
ATT_TQ = 512
ATT_TK = 1024
CAUSAL_TK = 2048
DSA_TQ = 256
DSA_COUNT_ROWS = 128


def _cp(*sem):
    return pltpu.CompilerParams(dimension_semantics=sem, vmem_limit_bytes=VMEM_LIMIT)


def _wspec(w, block, index_map, resident=True):
    _, lead = w
    shape = (None,) * len(lead) + tuple(block)
    imap = lambda *g: tuple(lead) + tuple(index_map(*g))
    if resident:
        return pl.BlockSpec(shape, imap, pipeline_mode=pl.Buffered(1))
    return pl.BlockSpec(shape, imap)


def _wshape(w):
    arr, lead = w
    return arr.shape[len(lead):]


def _resident(shape, index_map):
    return pl.BlockSpec(shape, index_map, pipeline_mode=pl.Buffered(1))


def _dot(a, b):
    return jnp.dot(a, b, preferred_element_type=F32)


def _dot_nt(a, b):
    return lax.dot_general(a, b, (((1,), (1,)), ((), ())), preferred_element_type=F32)


def _layer_norm(y, g, b):
    mu = jnp.mean(y, axis=-1, keepdims=True)
    d = y - mu
    var = jnp.mean(d * d, axis=-1, keepdims=True)
    return d * lax.rsqrt(var + LN_EPS) * g + b


def _rope128(x, cos, sin_signed):
    return x * cos + pltpu.roll(x, 64, 1) * sin_signed


def _rope64(x, cos, s_up, s_dn):
    return x * cos + pltpu.roll(x, 32, 1) * s_up + pltpu.roll(x, 96, 1) * s_dn


def _log2_scale(d):
    return float(d) ** -0.5 * LOG2_E


def _with_ones(v):
    return jnp.concatenate([v, jnp.ones_like(v)], axis=1)


def _flash_update(s, v1, bias, keep, m_ref, acc_ref, slot):
    if bias is not None:
        s = s + bias
    if keep is not None:
        s = jnp.where(keep, s, NEG)
    m_old = m_ref[slot]
    m_new = jnp.maximum(m_old, jnp.max(s, axis=1, keepdims=True))
    alpha = jnp.exp2(m_old - m_new)
    p = jnp.exp2(s - m_new)
    acc_ref[slot] = alpha * acc_ref[slot] + _dot(p.astype(BF16), v1)
    m_ref[slot] = m_new


FLASH_ROWS = 256


def _flash_rows(q, k, v1, bias, keep, m_ref, acc_ref, hd):
    n = max(1, q.shape[0] // FLASH_ROWS)
    r = q.shape[0] // n
    rows = [slice(sp * r, (sp + 1) * r) for sp in range(n)]
    logits = [_dot_nt(q[rw], k) for rw in rows]
    for sp, rw in enumerate(rows):
        _flash_update(logits[sp], v1, None if bias is None else bias[rw], None if keep is None else keep[rw],
                      m_ref, acc_ref, hd * n + sp)


def _flash_heads(qs, ks, v1s, bias, keep, m_ref, acc_ref):
    def logits(hd):
        q = qs[hd]
        n = max(1, q.shape[0] // FLASH_ROWS)
        r = q.shape[0] // n
        rows = [slice(sp * r, (sp + 1) * r) for sp in range(n)]
        return rows, [_dot_nt(q[rw], ks[hd]) for rw in rows]

    ahead = logits(0)
    for hd in range(len(qs)):
        rows, s = ahead
        if hd + 1 < len(qs):
            ahead = logits(hd + 1)
        for sp, rw in enumerate(rows):
            _flash_update(s[sp], v1s[hd], None if bias is None else bias[rw], None if keep is None else keep[rw],
                          m_ref, acc_ref, hd * len(rows) + sp)


def _causal_flash(q, k_ref, v_ref, m_ref, acc_ref, i, tk):
    tq = q.shape[0]
    n_past = (i * tq) // tk
    n_rest = (i * tq - n_past * tk) // tq

    def tile(off, width, keep):
        _flash_rows(q, k_ref[pl.ds(off, width), :], _with_ones(v_ref[pl.ds(off, width), :]), None, keep,
                    m_ref, acc_ref, 0)

    def body(j, carry):
        tile(pl.multiple_of(j * tk, tk), tk, None)
        return carry

    lax.fori_loop(0, n_past, body, 0)
    for r in range(tk // tq):
        @pl.when(n_rest == r)
        def _():
            width = (r + 1) * tq
            row = lax.broadcasted_iota(I32, (tq, width), 0) + r * tq
            col = lax.broadcasted_iota(I32, (tq, width), 1)
            tile(pl.multiple_of(n_past * tk, tk), width, col <= row)


def _flash_result(acc_ref, hd, heads):
    n = acc_ref.shape[0] // heads
    parts = []
    for sp in range(n):
        a = acc_ref[hd * n + sp]
        parts.append(a[:, 0:HEAD_DIM] / a[:, HEAD_DIM:2 * HEAD_DIM])
    return jnp.concatenate(parts, axis=0) if n > 1 else parts[0]


def _flash_scratch(heads, tq):
    n = max(1, tq // FLASH_ROWS)
    return [pltpu.VMEM((heads * n, tq // n, 1), F32), pltpu.VMEM((heads * n, tq // n, 2 * HEAD_DIM), F32)]


def _flash_init(m_ref, acc_ref):
    m_ref[...] = jnp.full(m_ref.shape, NEG, F32)
    acc_ref[...] = jnp.zeros(acc_ref.shape, F32)


def _top_k_mask(work, index, k, axis=1):
    n = work.shape[axis]
    sel = jnp.zeros(work.shape, F32)
    for _ in range(k):
        m = jnp.max(work, axis=axis, keepdims=True)
        idx = jnp.min(jnp.where(work == m, index, n), axis=axis, keepdims=True)
        pick = index == jnp.where(m > -jnp.inf, idx, n)
        sel = jnp.where(pick, 1.0, sel)
        work = jnp.where(pick, -jnp.inf, work)
    return sel


def _key_block_table(s, block):
    return jnp.asarray((np.arange(128)[None, :] == (np.arange(s) // block)[:, None]) * -NEG, dtype=BF16)


def _rope_table_kernel(pos_ref, c_ref, cos128_ref, sin128_ref, cos64_ref, sup64_ref, sdn64_ref):
    pos = pos_ref[...]
    a128 = pos * c_ref[0:1, :]
    cos128_ref[...] = jnp.cos(a128)
    sin128_ref[...] = jnp.sin(a128) * c_ref[1:2, :]
    a64 = pos * c_ref[2:3, :]
    s64 = jnp.sin(a64)
    cos64_ref[...] = jnp.cos(a64)
    sup64_ref[...] = s64 * c_ref[3:4, :]
    sdn64_ref[...] = s64 * c_ref[4:5, :]


def _rope_consts():
    lane = np.arange(128)
    inv128 = (ROPE_THETA ** (-np.arange(0, 128, 2, dtype=np.float32) / 128)).astype(np.float32)
    inv64 = (ROPE_THETA ** (-np.arange(0, 64, 2, dtype=np.float32) / 64)).astype(np.float32)
    c = np.zeros((8, 128), np.float32)
    c[0] = inv128[lane % 64]
    c[1] = np.where(lane < 64, -1.0, 1.0)
    c[2] = inv64[(lane % 64) % 32]
    c[3] = np.where(lane % 64 >= 32, 1.0, 0.0)
    c[4] = np.where(lane % 64 < 32, -1.0, 0.0)
    return jnp.asarray(c)


def _rope_tables(positions):
    s = positions.shape[0]
    tm = min(ROW_TILE, s)
    pos = positions.astype(F32).reshape(s, 1)
    tab = jax.ShapeDtypeStruct((s, 128), F32)
    row = pl.BlockSpec((tm, 128), lambda i: (i, 0))
    return pl.pallas_call(
        _rope_table_kernel,
        grid=(s // tm,),
        in_specs=[pl.BlockSpec((tm, 1), lambda i: (i, 0)), pl.BlockSpec((8, 128), lambda i: (0, 0))],
        out_specs=[row] * 5,
        out_shape=[tab] * 5,
        compiler_params=_cp("parallel"),
        name="rope_tables",
    )(pos, _rope_consts())


def _ffn_kernel(x_ref, wg_ref, wu_ref, wd_ref, g_ref, b_ref, o_ref, xb_ref):
    j = pl.program_id(1)

    @pl.when(j == 0)
    def _():
        xb_ref[...] = x_ref[...].astype(BF16)
        o_ref[...] = jnp.zeros_like(o_ref)

    xb = xb_ref[...]
    g = _dot(xb, wg_ref[...].astype(BF16))
    u = _dot(xb, wu_ref[...].astype(BF16))
    h = (g * jax.nn.sigmoid(g) * u).astype(BF16)
    o_ref[...] += _dot(h, wd_ref[...].astype(BF16))

    @pl.when(j == pl.num_programs(1) - 1)
    def _():
        y = DN_ALPHA * x_ref[...] + 0.5 * o_ref[...]
        o_ref[...] = _layer_norm(y, g_ref[...], b_ref[...])


def _ffn_ln(x, w_gu, w_down, g, b):
    s, d = x.shape
    f = _wshape(w_down)[0]
    tm = min(FFN_ROWS, s)
    tf = FFN_COLS
    nf = f // tf
    return pl.pallas_call(
        _ffn_kernel,
        grid=(s // tm, nf),
        in_specs=[
            pl.BlockSpec((tm, d), lambda i, j: (i, 0), pipeline_mode=pl.Buffered(1)),
            _wspec(w_gu, (d, tf), lambda i, j: (0, j), resident=False),
            _wspec(w_gu, (d, tf), lambda i, j: (0, j + nf), resident=False),
            _wspec(w_down, (tf, d), lambda i, j: (j, 0), resident=False),
            _wspec(g, (1, d), lambda i, j: (0, 0)),
            _wspec(b, (1, d), lambda i, j: (0, 0)),
        ],
        out_specs=pl.BlockSpec((tm, d), lambda i, j: (i, 0)),
        out_shape=jax.ShapeDtypeStruct((s, d), F32),
        scratch_shapes=[pltpu.VMEM((tm, d), BF16)],
        compiler_params=_cp("parallel", "arbitrary"),
        name="ffn_ln",
    )(x, w_gu[0], w_gu[0], w_down[0], g[0], b[0])


def _proj_a_kernel(x_ref, w_ref, cos_ref, sin_ref, kb_ref, q_ref, k_ref, v_ref, km_ref):
    h = _dot(x_ref[...].astype(BF16), w_ref[...])
    cos, sin = cos_ref[...], sin_ref[...]
    nblk = km_ref.shape[0]
    for hd in range(4):
        lo = hd * HEAD_DIM
        q_ref[:, lo:lo + HEAD_DIM] = _rope128(h[:, lo:lo + HEAD_DIM], cos, sin)
        k = _rope128(h[:, 512 + lo:512 + lo + HEAD_DIM], cos, sin)
        k_ref[:, 2 * lo:2 * lo + HEAD_DIM] = k.astype(BF16)
        k_ref[:, 2 * lo + HEAD_DIM:2 * lo + 2 * HEAD_DIM] = kb_ref[...]
        for bi in range(nblk):
            kb = k[bi * MOBA_BLOCK:(bi + 1) * MOBA_BLOCK]
            km_ref[bi, :, lo:lo + HEAD_DIM] = jnp.mean(kb, axis=0, keepdims=True)
    v_ref[...] = h[:, 1024:1536].astype(BF16)


def _proj_a(x, w, cos128, sin128):
    s, d = x.shape
    tm = min(ROW_TILE, s)
    nblk = tm // MOBA_BLOCK
    row512 = pl.BlockSpec((tm, 512), lambda i: (i, 0))
    tab = pl.BlockSpec((tm, 128), lambda i: (i, 0))
    return pl.pallas_call(
        _proj_a_kernel,
        grid=(s // tm,),
        in_specs=[pl.BlockSpec((tm, d), lambda i: (i, 0)), _wspec(w, _wshape(w), lambda i: (0, 0)), tab, tab, tab],
        out_specs=[row512, pl.BlockSpec((tm, 1024), lambda i: (i, 0)), row512,
                   pl.BlockSpec((nblk, 1, 512), lambda i: (i, 0, 0))],
        out_shape=[
            jax.ShapeDtypeStruct((s, 512), F32),
            jax.ShapeDtypeStruct((s, 1024), BF16),
            jax.ShapeDtypeStruct((s, 512), BF16),
            jax.ShapeDtypeStruct((s // MOBA_BLOCK, 1, 512), F32),
        ],
        compiler_params=_cp("parallel"),
        name="proj_moba",
    )(x, w[0], cos128, sin128, _key_block_table(s, MOBA_BLOCK))


def _rms_norm(x, g):
    return x * lax.rsqrt(jnp.mean(x * x, axis=-1, keepdims=True) + RMS_EPS) * g


def _proj_b_kernel(x_ref, w_ref, gq_ref, gkv_ref, wuq_ref, wukv_ref, cos_ref, sup_ref, sdn_ref,
                   q_ref, k_ref, v_ref):
    h = _dot(x_ref[...].astype(BF16), w_ref[...])
    cos, sup, sdn = cos_ref[...], sup_ref[...], sdn_ref[...]
    cq = _rms_norm(h[:, 0:512], gq_ref[...]).astype(BF16)
    ckv = _rms_norm(h[:, 512:1024], gkv_ref[...]).astype(BF16)
    kr = _rope64(h[:, 1024:1152], cos, sup, sdn).astype(BF16)
    q = _dot(cq, wuq_ref[...])
    kv = _dot(ckv, wukv_ref[...])
    qs = _log2_scale(MLA_NOPE + MLA_ROPE)
    for hd in range(4):
        lo = hd * 256
        q_ref[:, lo:lo + 128] = (q[:, lo:lo + 128] * qs).astype(BF16)
        q_ref[:, lo + 128:lo + 256] = (_rope64(q[:, lo + 128:lo + 256], cos, sup, sdn) * qs).astype(BF16)
        k_ref[:, lo:lo + 128] = kv[:, lo:lo + 128].astype(BF16)
        k_ref[:, lo + 128:lo + 256] = kr
        v_ref[:, hd * 128:(hd + 1) * 128] = kv[:, lo + 128:lo + 256].astype(BF16)


def _proj_b(x, w, g_cq, g_ckv, w_uq, w_ukv, cos64, sup64, sdn64):
    s, d = x.shape
    tm = min(ROW_TILE, s)
    tab = pl.BlockSpec((tm, 128), lambda i: (i, 0))
    full = lambda a: _wspec(a, _wshape(a), lambda i: (0, 0))
    return pl.pallas_call(
        _proj_b_kernel,
        grid=(s // tm,),
        in_specs=[pl.BlockSpec((tm, d), lambda i: (i, 0)), full(w), full(g_cq), full(g_ckv), full(w_uq),
                  full(w_ukv), tab, tab, tab],
        out_specs=[pl.BlockSpec((tm, 1024), lambda i: (i, 0)), pl.BlockSpec((tm, 1024), lambda i: (i, 0)),
                   pl.BlockSpec((tm, 512), lambda i: (i, 0))],
        out_shape=[jax.ShapeDtypeStruct((s, 1024), BF16), jax.ShapeDtypeStruct((s, 1024), BF16),
                   jax.ShapeDtypeStruct((s, 512), BF16)],
        compiler_params=_cp("parallel"),
        name="proj_mla",
    )(x, w[0], g_cq[0], g_ckv[0], w_uq[0], w_ukv[0], cos64, sup64, sdn64)


def _proj_c_kernel(x_ref, w_ref, cos_ref, sin_ref, kb_ref, q_ref, qr_ref, kvcmp_ref, kslc_ref, vslc_ref,
                   kwin_ref, vwin_ref, gate_ref):
    h = _dot(x_ref[...].astype(BF16), w_ref[...])
    cos, sin = cos_ref[...], sin_ref[...]
    for hd in range(4):
        lo = hd * HEAD_DIM
        qh = h[:, lo:lo + HEAD_DIM]
        q_ref[:, lo:lo + HEAD_DIM] = qh.astype(BF16)
        qr_ref[:, lo:lo + HEAD_DIM] = (_rope128(qh, cos, sin) * _log2_scale(HEAD_DIM)).astype(BF16)
    kv = lambda i: h[:, 512 + i * 128:512 + (i + 1) * 128]
    kvcmp_ref[0] = kv(0).astype(BF16)
    kvcmp_ref[1] = kv(1).astype(BF16)
    kslc_ref[:, 0:HEAD_DIM] = _rope128(kv(2), cos, sin).astype(BF16)
    kslc_ref[:, HEAD_DIM:2 * HEAD_DIM] = kb_ref[...]
    vslc_ref[...] = kv(3).astype(BF16)
    kwin_ref[...] = _rope128(kv(4), cos, sin).astype(BF16)
    vwin_ref[...] = kv(5).astype(BF16)
    gate_ref[...] = jax.nn.sigmoid(h[:, 1280:1408])


def _proj_c(x, w, cos128, sin128):
    s, d = x.shape
    tm = min(ROW_TILE, s)
    r512 = pl.BlockSpec((tm, 512), lambda i: (i, 0))
    r128 = pl.BlockSpec((tm, 128), lambda i: (i, 0))
    b512 = jax.ShapeDtypeStruct((s, 512), BF16)
    b128 = jax.ShapeDtypeStruct((s, 128), BF16)
    return pl.pallas_call(
        _proj_c_kernel,
        grid=(s // tm,),
        in_specs=[pl.BlockSpec((tm, d), lambda i: (i, 0)), _wspec(w, _wshape(w), lambda i: (0, 0)), r128, r128, r128],
        out_specs=[r512, r512, pl.BlockSpec((2, tm, 128), lambda i: (0, i, 0)),
                   pl.BlockSpec((tm, 256), lambda i: (i, 0))] + [r128] * 4,
        out_shape=[b512, b512, jax.ShapeDtypeStruct((2, s, 128), BF16), jax.ShapeDtypeStruct((s, 256), BF16)]
                  + [b128] * 3 + [jax.ShapeDtypeStruct((s, 128), F32)],
        compiler_params=_cp("parallel"),
        name="proj_nsa",
    )(x, w[0], cos128, sin128, _key_block_table(s, NSA_SEL_BLOCK))


def _proj_d_kernel(x_ref, w_ref, cos_ref, sin_ref, c64_ref, sup_ref, sdn_ref,
                   q_ref, k_ref, v_ref, iq_ref, ik_ref, iw_ref):
    h = _dot(x_ref[...].astype(BF16), w_ref[...])
    cos, sin = cos_ref[...], sin_ref[...]
    c64, sup, sdn = c64_ref[...], sup_ref[...], sdn_ref[...]
    for hd in range(4):
        lo = hd * HEAD_DIM
        q_ref[:, lo:lo + HEAD_DIM] = (_rope128(h[:, lo:lo + HEAD_DIM], cos, sin) * _log2_scale(HEAD_DIM)).astype(BF16)
        k_ref[:, lo:lo + HEAD_DIM] = _rope128(h[:, 512 + lo:512 + lo + HEAD_DIM], cos, sin).astype(BF16)
    v_ref[...] = h[:, 1024:1536].astype(BF16)
    for p in range(8):
        lo = 1536 + p * 128
        iq_ref[:, p * 128:(p + 1) * 128] = _rope64(h[:, lo:lo + 128], c64, sup, sdn).astype(BF16)
    ik = _rope64(h[:, 2560:2688], c64, sup, sdn)
    ik_ref[:, 0:128] = ik.astype(BF16)
    ik_ref[:, 128:256] = pltpu.roll(ik, 64, 1).astype(BF16)
    iw_ref[...] = h[:, 2688:2816] * (1.0 / 32.0)


def _proj_d(x, w, cos128, sin128, cos64, sup64, sdn64):
    s, d = x.shape
    tm = min(ROW_TILE, s)
    r = lambda n: pl.BlockSpec((tm, n), lambda i: (i, 0))
    return pl.pallas_call(
        _proj_d_kernel,
        grid=(s // tm,),
        in_specs=[pl.BlockSpec((tm, d), lambda i: (i, 0)), _wspec(w, _wshape(w), lambda i: (0, 0))] + [r(128)] * 5,
        out_specs=[r(512), r(512), r(512), r(1024), r(256), r(128)],
        out_shape=[jax.ShapeDtypeStruct((s, 512), BF16)] * 3 + [
            jax.ShapeDtypeStruct((s, 1024), BF16), jax.ShapeDtypeStruct((s, 256), BF16),
            jax.ShapeDtypeStruct((s, 128), F32)],
        compiler_params=_cp("parallel"),
        name="proj_dsa",
    )(x, w[0], cos128, sin128, cos64, sup64, sdn64)


def _moba_kernel(q_ref, k_ref, v_ref, km_ref, o_ref, m_ref, acc_ref, *, tk, nb):
    i = pl.program_id(1)
    tq = q_ref.shape[0]
    q32 = q_ref[...]

    gate = lax.dot_general(q32, km_ref[...], (((1,), (1,)), ((), ())),
                           precision=lax.Precision.HIGHEST, preferred_element_type=F32)
    blk = lax.broadcasted_iota(I32, (128, tq), 0)
    cur = (i * tq + lax.broadcasted_iota(I32, (128, tq), 1)) // MOBA_BLOCK
    sel = _top_k_mask(jnp.where(blk < cur, gate.T, -jnp.inf), blk, min(MOBA_TOPK, nb - 1), axis=0)
    unchosen = (jnp.where(blk == cur, 1.0, sel) - 1.0).T
    qa = jnp.concatenate([q32 * _log2_scale(HEAD_DIM), unchosen], axis=1).astype(BF16)

    _flash_init(m_ref, acc_ref)
    _causal_flash(qa, k_ref, v_ref, m_ref, acc_ref, i, tk)
    o_ref[...] = _flash_result(acc_ref, 0, 1).astype(o_ref.dtype)


def _moba(q32, k, v, kmean):
    s = q32.shape[0]
    nb = s // MOBA_BLOCK
    tq = min(ATT_TQ, s)
    tk = min(CAUSAL_TK, s)
    km = jnp.pad(kmean.reshape(nb, 512), ((0, 128 - nb), (0, 0)))
    return pl.pallas_call(
        functools.partial(_moba_kernel, tk=tk, nb=nb),
        grid=(4, s // tq),
        in_specs=[
            pl.BlockSpec((tq, HEAD_DIM), lambda h, i: (i, h)),
            pl.BlockSpec((s, 2 * HEAD_DIM), lambda h, i: (0, h)),
            pl.BlockSpec((s, HEAD_DIM), lambda h, i: (0, h)),
            pl.BlockSpec((128, HEAD_DIM), lambda h, i: (0, h)),
        ],
        out_specs=pl.BlockSpec((tq, HEAD_DIM), lambda h, i: (i, h)),
        out_shape=jax.ShapeDtypeStruct((s, 512), BF16),
        scratch_shapes=_flash_scratch(1, tq),
        compiler_params=_cp("parallel", "arbitrary"),
        name="moba_attn",
    )(q32, k, v, km)


def _mla_kernel(q_ref, k_ref, v_ref, o_ref, m_ref, acc_ref, *, tk):
    _flash_init(m_ref, acc_ref)
    _causal_flash(q_ref[...], k_ref, v_ref, m_ref, acc_ref, pl.program_id(1), tk)
    o_ref[...] = _flash_result(acc_ref, 0, 1).astype(o_ref.dtype)


def _mla(qcat, kcat, v):
    s = qcat.shape[0]
    tq = min(ATT_TQ, s)
    tk = min(CAUSAL_TK, s)
    return pl.pallas_call(
        functools.partial(_mla_kernel, tk=tk),
        grid=(4, s // tq),
        in_specs=[
            pl.BlockSpec((tq, 256), lambda h, i: (i, h)),
            pl.BlockSpec((s, 256), lambda h, i: (0, h)),
            pl.BlockSpec((s, HEAD_DIM), lambda h, i: (0, h)),
        ],
        out_specs=pl.BlockSpec((tq, HEAD_DIM), lambda h, i: (i, h)),
        out_shape=jax.ShapeDtypeStruct((s, 512), BF16),
        scratch_shapes=_flash_scratch(1, tq),
        compiler_params=_cp("parallel", "arbitrary"),
        name="mla_attn",
    )(qcat, kcat, v)


def _gelu_tanh(x):
    return 0.5 * x * (1.0 + jnp.tanh(0.7978845608028654 * (x + 0.044715 * x * x * x)))


def _nsa_compress_kernel(t_ref, pe_ref, w1_ref, w2_ref, o_ref):
    t = t_ref[0]
    w1 = w1_ref[...]
    half = t.shape[1]
    n = t.shape[0]
    lo = _dot(t, w1[0:half])
    hi = _dot(t, w1[half:2 * half])
    pe = _dot(pe_ref[...].astype(BF16), w1)
    pre = lo + pltpu.roll(hi, n - 1, 0) + pe
    o_ref[0] = _dot(_gelu_tanh(pre).astype(BF16), w2_ref[...]).astype(o_ref.dtype)


def _nsa_compress(kv16, pe, w1, w2):
    n = kv16.shape[1]
    sub = lambda a: _wspec(a, (None,) + _wshape(a)[1:], lambda i: (i, 0, 0), resident=False)
    return pl.pallas_call(
        _nsa_compress_kernel,
        grid=(2,),
        in_specs=[pl.BlockSpec((1,) + kv16.shape[1:], lambda i: (i, 0, 0)), sub(pe), sub(w1), sub(w2)],
        out_specs=pl.BlockSpec((1, n, HEAD_DIM), lambda i: (i, 0, 0)),
        out_shape=jax.ShapeDtypeStruct((2, n, HEAD_DIM), BF16),
        compiler_params=_cp("parallel"),
        name="nsa_compress",
    )(kv16, pe[0], w1[0], w2[0])


def _nsa_cmp_kernel(q_ref, kvc_ref, m_ref, o_ref, sel_ref, *, n_sel):
    i = pl.program_id(0)
    tq = q_ref.shape[0]
    n = kvc_ref.shape[1]
    nblk = m_ref.shape[1]
    scale = HEAD_DIM ** -0.5
    kc, vc = kvc_ref[0], kvc_ref[1]
    t_pos = i * tq + lax.broadcasted_iota(I32, (tq, n), 0)
    cmp_end = lax.broadcasted_iota(I32, (tq, n), 1) * NSA_CMP_STRIDE + (2 * NSA_CMP_STRIDE - 1)
    ok = cmp_end <= t_pos
    p_sum = jnp.zeros((tq, n), F32)
    for hd in range(4):
        lo = hd * HEAD_DIM
        s = jnp.where(ok, _dot_nt(q_ref[:, lo:lo + HEAD_DIM], kc) * scale, NEG)
        m = jnp.max(s, axis=1, keepdims=True)
        e = jnp.where(ok, jnp.exp(s - m), 0.0)
        l = jnp.sum(e, axis=1, keepdims=True)
        p = e / jnp.where(l > 0, l, 1.0)
        p_sum = p_sum + p
        o_ref[:, lo:lo + HEAD_DIM] = _dot(p.astype(BF16), vc)

    imp = jnp.dot(p_sum, m_ref[...], precision=lax.Precision.HIGHEST, preferred_element_type=F32)
    imp_t = imp.T
    blk = lax.broadcasted_iota(I32, (nblk, tq), 0)
    cur = (i * tq + lax.broadcasted_iota(I32, (nblk, tq), 1)) // NSA_SEL_BLOCK
    forced = (blk == 0) | (blk == cur) | (blk == cur - 1)
    imp_t = jnp.where(blk > cur, -jnp.inf, jnp.where(forced, jnp.inf, imp_t))
    sel_ref[...] = (_top_k_mask(imp_t, blk, n_sel, axis=0).T - 1.0).astype(sel_ref.dtype)


def _nsa_cmp(q, kvc, imp_map):
    s = q.shape[0]
    n = kvc.shape[1]
    nblk = imp_map.shape[1]
    tq = min(256, s)
    return pl.pallas_call(
        functools.partial(_nsa_cmp_kernel, n_sel=min(NSA_SEL_TOPK, s // NSA_SEL_BLOCK)),
        grid=(s // tq,),
        in_specs=[pl.BlockSpec((tq, 512), lambda i: (i, 0)),
                  _resident((2, n, HEAD_DIM), lambda i: (0, 0, 0)),
                  _resident((n, nblk), lambda i: (0, 0))],
        out_specs=[pl.BlockSpec((tq, 512), lambda i: (i, 0)), pl.BlockSpec((tq, nblk), lambda i: (i, 0))],
        out_shape=[jax.ShapeDtypeStruct((s, 512), F32), jax.ShapeDtypeStruct((s, nblk), BF16)],
        compiler_params=_cp("parallel"),
        name="nsa_cmp_select",
    )(q, kvc, imp_map)


def _nsa_slc_win_kernel(q_ref, sel_ref, ks_ref, vs_ref, kw_ref, vw_ref, g_ref, oc_ref, o_ref,
                        m_ref, acc_ref, *, tk):
    i = pl.program_id(0)
    tq = q_ref.shape[0]
    _flash_init(m_ref, acc_ref)

    def tile(off, width, keep):
        k = ks_ref[pl.ds(off, width), 0:HEAD_DIM]
        v1 = _with_ones(vs_ref[pl.ds(off, width), :])
        bias = _dot_nt(sel_ref[...], ks_ref[pl.ds(off, width), HEAD_DIM:2 * HEAD_DIM])
        _flash_heads([q_ref[:, hd * HEAD_DIM:(hd + 1) * HEAD_DIM] for hd in range(4)], [k] * 4, [v1] * 4,
                     bias, keep, m_ref, acc_ref)

    n_past = (i * tq) // tk
    n_rest = (i * tq - n_past * tk) // tq

    def body(j, carry):
        tile(pl.multiple_of(j * tk, tk), tk, None)
        return carry

    lax.fori_loop(0, n_past, body, 0)
    for r in range(tk // tq):
        @pl.when(n_rest == r)
        def _():
            width = (r + 1) * tq
            row = lax.broadcasted_iota(I32, (tq, width), 0) + r * tq
            col = lax.broadcasted_iota(I32, (tq, width), 1)
            tile(pl.multiple_of(n_past * tk, tk), width, col <= row)

    wlen = NSA_WINDOW + tq
    start = pl.multiple_of(jnp.maximum(i * tq - NSA_WINDOW, 0), 256)
    kw = kw_ref[pl.ds(start, wlen), :]
    vw1 = _with_ones(vw_ref[pl.ds(start, wlen), :])
    diff = (i * tq + lax.broadcasted_iota(I32, (tq, wlen), 0)) - (start + lax.broadcasted_iota(I32, (tq, wlen), 1))
    in_win = (diff >= 0) & (diff < NSA_WINDOW)
    g = g_ref[...]
    for hd in range(4):
        lo = hd * HEAD_DIM
        s = jnp.where(in_win, _dot_nt(q_ref[:, lo:lo + HEAD_DIM], kw), NEG)
        p = jnp.exp2(s - jnp.max(s, axis=1, keepdims=True))
        pv = _dot(p.astype(BF16), vw1)
        o_win = pv[:, 0:HEAD_DIM] / pv[:, HEAD_DIM:2 * HEAD_DIM]
        o_slc = _flash_result(acc_ref, hd, 4)
        out = (g[:, 3 * hd:3 * hd + 1] * oc_ref[:, lo:lo + HEAD_DIM] + g[:, 3 * hd + 1:3 * hd + 2] * o_slc
               + g[:, 3 * hd + 2:3 * hd + 3] * o_win)
        o_ref[:, lo:lo + HEAD_DIM] = out.astype(o_ref.dtype)


def _nsa_slc_win(q_r, sel, kslc, vslc, kwin, vwin, gates, o_cmp):
    s = q_r.shape[0]
    tq = min(ATT_TQ, s)
    tk = min(ATT_TK, s)
    full = lambda a: _resident(a.shape, lambda i: (0,) * a.ndim)
    return pl.pallas_call(
        functools.partial(_nsa_slc_win_kernel, tk=tk),
        grid=(s // tq,),
        in_specs=[pl.BlockSpec((tq, 512), lambda i: (i, 0)), pl.BlockSpec((tq, 128), lambda i: (i, 0)),
                  full(kslc), full(vslc), full(kwin), full(vwin),
                  pl.BlockSpec((tq, 128), lambda i: (i, 0)), pl.BlockSpec((tq, 512), lambda i: (i, 0))],
        out_specs=pl.BlockSpec((tq, 512), lambda i: (i, 0)),
        out_shape=jax.ShapeDtypeStruct((s, 512), BF16),
        scratch_shapes=_flash_scratch(4, tq),
        compiler_params=_cp("parallel"),
        name="nsa_slc_win",
    )(q_r, sel, kslc, vslc, kwin, vwin, gates, o_cmp)


def _dsa_kernel(q_ref, iq_ref, iw_ref, k_ref, v_ref, ik_ref, o_ref, key_ref, wb_ref, m_ref, acc_ref, *, tk):
    i = pl.program_id(0)
    tq = q_ref.shape[0]
    half = tk // 2
    n_tiles = (i * tq + tq + tk - 1) // tk
    row_h = lax.broadcasted_iota(I32, (tq, half), 0) + i * tq
    col_h = lax.broadcasted_iota(I32, (tq, half), 1)

    w = iw_ref[...]
    for hd in range(DSA_IDX_HEADS):
        wb_ref[hd] = jnp.broadcast_to(w[:, hd:hd + 1], (tq, 128))

    def score_body(j, carry):
        for hf in range(2):
            off = pl.multiple_of(j * tk + hf * half, half)
            ik_even = ik_ref[pl.ds(off, half), 0:128]
            ik_odd = ik_ref[pl.ds(off, half), 128:256]
            sc = jnp.zeros((tq, half), F32)
            for p in range(DSA_IDX_HEADS // 2):
                x = iq_ref[:, p * 128:(p + 1) * 128]
                we = jnp.concatenate([wb_ref[2 * p]] * (half // 128), axis=1)
                wo = jnp.concatenate([wb_ref[2 * p + 1]] * (half // 128), axis=1)
                sc = sc + jnp.maximum(_dot_nt(x, ik_even), 0.0) * we
                sc = sc + jnp.maximum(_dot_nt(x, ik_odd), 0.0) * wo
            bits = pltpu.bitcast(sc, I32)
            key = jnp.where(bits >= 0, bits, bits ^ 0x7FFFFFFF)
            key_ref[j, :, hf * half:(hf + 1) * half] = jnp.where(col_h + off <= row_h, key, INT_MIN)
        return carry

    lax.fori_loop(0, n_tiles, score_body, 0)

    rb = min(DSA_COUNT_ROWS, tq)

    ones = jnp.ones((128, 128), BF16)
    k_f = float(DSA_TOPK)

    def count_ge(cand128):
        parts = []
        for r0 in range(0, tq, rb):
            cand_b = cand128[r0:r0 + rb]

            def body(j, acc):
                for c in range(tk // 128):
                    acc = acc + jnp.where(key_ref[j, r0:r0 + rb, c * 128:(c + 1) * 128] >= cand_b, 1, 0)
                return acc

            parts.append(lax.fori_loop(0, n_tiles, body, jnp.zeros((rb, 128), I32)))
        per_lane = jnp.concatenate(parts, axis=0).astype(F32).astype(BF16)
        return _dot(per_lane, ones)

    zero = jnp.zeros((tq, 128), I32)
    c0 = count_ge(zero)
    lo0 = jnp.where(c0 >= k_f, zero, INT_MIN)
    c_lo0 = jnp.where(c0 >= k_f, c0, -1.0)

    def bit_cond(state):
        b, _, _, more = state
        return (b < 31) & (more > 0.0)

    def bit_body(state):
        b, lo, c_lo, _ = state
        more = jnp.max(jnp.where(c_lo != k_f, 1.0, 0.0))
        cand = lo | lax.shift_left(jnp.int32(1), 30 - b)
        c = count_ge(cand)
        take = c >= k_f
        return b + 1, jnp.where(take, cand, lo), jnp.where(take, c, c_lo), more

    passes, lo, c_lo, _ = lax.while_loop(bit_cond, bit_body, (jnp.int32(0), lo0, c_lo0, jnp.float32(1.0)))
    thr128 = jnp.maximum(lo, INT_MIN + 1)

    def retire_ties():
        need = k_f - count_ge(lo + 1)
        lane = lax.broadcasted_iota(I32, (rb, 128), 1)

        def tied_pos(j, r0, c):
            tied = key_ref[j, r0:r0 + rb, c * 128:(c + 1) * 128] == lo[r0:r0 + rb]
            return jnp.where(tied, j * tk + c * 128 + lane, -1)

        def count_tied_before(limit):
            parts = []
            for r0 in range(0, tq, rb):
                lim_b = limit[r0:r0 + rb]

                def body(j, acc, r0=r0, lim_b=lim_b):
                    for c in range(tk // 128):
                        pos = tied_pos(j, r0, c)
                        acc = acc + jnp.where(jnp.where(pos >= 0, pos, lim_b) < lim_b, 1, 0)
                    return acc

                parts.append(lax.fori_loop(0, n_tiles, body, jnp.zeros((rb, 128), I32)))
            return _dot(jnp.concatenate(parts, axis=0).astype(F32).astype(BF16), ones)

        n_bits = (key_ref.shape[0] * tk - 1).bit_length()

        def pos_body(b, last):
            cand = last | lax.shift_left(jnp.int32(1), n_bits - 1 - b)
            return jnp.where(count_tied_before(cand) < need, cand, last)

        last = lax.fori_loop(0, n_bits, pos_body, jnp.zeros((tq, 128), I32))
        last = jnp.where(c_lo > k_f, last, 2 ** 31 - 1)

        for r0 in range(0, tq, rb):
            def drop_body(j, carry, r0=r0):
                for c in range(tk // 128):
                    cols = slice(c * 128, (c + 1) * 128)
                    beyond = tied_pos(j, r0, c) > last[r0:r0 + rb]
                    key_ref[j, r0:r0 + rb, cols] = jnp.where(beyond, INT_MIN, key_ref[j, r0:r0 + rb, cols])
                return carry

            lax.fori_loop(0, n_tiles, drop_body, 0)

    @pl.when(passes >= 31)
    def _():
        @pl.when(jnp.max(jnp.where(c_lo > k_f, 1.0, 0.0)) > 0.0)
        def _():
            retire_ties()

    _flash_init(m_ref, acc_ref)

    def attn_body(j, carry):
        off = pl.multiple_of(j * tk, tk)
        keep = key_ref[j] >= jnp.concatenate([thr128] * (tk // 128), axis=1)
        cols = [slice(hd * HEAD_DIM, (hd + 1) * HEAD_DIM) for hd in range(4)]
        _flash_heads([q_ref[:, c] for c in cols], [k_ref[pl.ds(off, tk), c] for c in cols],
                     [_with_ones(v_ref[pl.ds(off, tk), c]) for c in cols], None, keep, m_ref, acc_ref)
        return carry

    lax.fori_loop(0, n_tiles, attn_body, 0)
    for hd in range(4):
        o_ref[:, hd * HEAD_DIM:(hd + 1) * HEAD_DIM] = _flash_result(acc_ref, hd, 4).astype(o_ref.dtype)


def _dsa(q, k, v, iq, ik2, iw):
    s = q.shape[0]
    tq = min(DSA_TQ, s)
    tk = min(ATT_TK, s)
    full = lambda a: _resident(a.shape, lambda i: (0, 0))
    return pl.pallas_call(
        functools.partial(_dsa_kernel, tk=tk),
        grid=(s // tq,),
        in_specs=[pl.BlockSpec((tq, 512), lambda i: (i, 0)), pl.BlockSpec((tq, 1024), lambda i: (i, 0)),
                  pl.BlockSpec((tq, 128), lambda i: (i, 0)), full(k), full(v), full(ik2)],
        out_specs=pl.BlockSpec((tq, 512), lambda i: (i, 0)),
        out_shape=jax.ShapeDtypeStruct((s, 512), BF16),
        scratch_shapes=[pltpu.VMEM((s // tk, tq, tk), I32), pltpu.VMEM((DSA_IDX_HEADS, tq, 128), F32)]
                       + _flash_scratch(4, tq),
        compiler_params=_cp("parallel"),
        name="dsa_attn",
    )(q, iq, iw, k, v, ik2)


def _mix_mem_kernel(oa_ref, ob_ref, oc_ref, od_ref, w_ref, x_ref, g1_ref, b1_ref, wq_ref, kv_ref, wo_ref,
                    g2_ref, b2_ref, o_ref):
    mix = _dot(oa_ref[...], w_ref[0:512, :])
    mix = mix + _dot(ob_ref[...], w_ref[512:1024, :])
    mix = mix + _dot(oc_ref[...], w_ref[1024:1536, :])
    mix = mix + _dot(od_ref[...], w_ref[1536:2048, :])
    x = _layer_norm(DN_ALPHA * x_ref[...] + mix, g1_ref[...], b1_ref[...])

    q = _dot(x.astype(BF16), wq_ref[...])
    heads = []
    for hd in range(4):
        lo = hd * HEAD_DIM
        qh = (q[:, lo:lo + HEAD_DIM] * _log2_scale(HEAD_DIM)).astype(BF16)
        s = _dot_nt(qh, kv_ref[:, lo:lo + HEAD_DIM])
        p = jnp.exp2(s - jnp.max(s, axis=1, keepdims=True))
        o = _dot(p.astype(BF16), kv_ref[:, 512 + lo:512 + lo + HEAD_DIM]) / jnp.sum(p, axis=1, keepdims=True)
        heads.append(o.astype(BF16))
    out = _dot(jnp.concatenate(heads, axis=1), wo_ref[...])
    o_ref[...] = _layer_norm(DN_ALPHA * x + out, g2_ref[...], b2_ref[...])


def _mix_mem_ln(o_a, o_b, o_c, o_d, w_out, x, g1, b1, wq, kv, wo, g2, b2):
    s, d = x.shape
    tm = min(ROW_TILE, s)
    r512 = pl.BlockSpec((tm, 512), lambda i: (i, 0))
    full = lambda a: _wspec(a, _wshape(a), lambda i: (0, 0))
    return pl.pallas_call(
        _mix_mem_kernel,
        grid=(s // tm,),
        in_specs=[r512, r512, r512, r512, full(w_out), pl.BlockSpec((tm, d), lambda i: (i, 0)), full(g1), full(b1),
                  full(wq), _resident(kv.shape, lambda i: (0, 0)), full(wo), full(g2), full(b2)],
        out_specs=pl.BlockSpec((tm, d), lambda i: (i, 0)),
        out_shape=jax.ShapeDtypeStruct((s, d), F32),
        compiler_params=_cp("parallel"),
        name="mix_mem_ln",
    )(o_a, o_b, o_c, o_d, w_out[0], x, g1[0], b1[0], wq[0], kv, wo[0], g2[0], b2[0])


def _mem_kv_kernel(mem_ref, w_ref, o_ref):
    o_ref[...] = _dot(mem_ref[...].astype(BF16), w_ref[...]).astype(o_ref.dtype)


def _mem_kv(mem, wkv):
    m, d = mem.shape
    n = _wshape(wkv)[1]
    return pl.pallas_call(
        _mem_kv_kernel,
        grid=(1,),
        in_specs=[pl.BlockSpec((m, d), lambda i: (0, 0)), _wspec(wkv, (d, n), lambda i: (0, 0))],
        out_specs=pl.BlockSpec((m, n), lambda i: (0, 0)),
        out_shape=jax.ShapeDtypeStruct((m, n), BF16),
        compiler_params=_cp("arbitrary"),
        name="mem_kv",
    )(mem, wkv[0])


def _pad_last(w, n):
    return jnp.pad(w, [(0, 0)] * (w.ndim - 1) + [(0, n - w.shape[-1])])


def _split_w_in(w_in):
    a = w_in[..., 0:1536]
    b = _pad_last(w_in[..., 1536:2624], 1152)
    c = _pad_last(w_in[..., 2624:3916], 1408)
    d = jnp.concatenate([w_in[..., 3916:6476], _pad_last(w_in[..., 6476:6540], 128),
                         _pad_last(w_in[..., 6540:6556], 128)], axis=-1)
    return tuple(t.astype(BF16) for t in (a, b, c, d))


def _pad_w_uq(w_uq):
    lead = w_uq.shape[:-1]
    w = _pad_last(w_uq.reshape(lead + (4, MLA_NOPE + MLA_ROPE)), 256)
    return w.reshape(lead + (4 * 256,)).astype(BF16)


def _nsa_importance_map(s):
    n = s // NSA_CMP_STRIDE
    nblk = s // NSA_SEL_BLOCK
    ni = np.arange(n)[:, None]
    bi = np.arange(nblk)[None, :]
    m = ((ni >= 4 * bi - 1) & (ni <= 4 * bi + 3)).astype(np.float32)
    return jnp.asarray(np.pad(m, ((0, 0), (0, 128 - nblk))))


def _mixer_heads(x, tabs, imp_map, w_groups, g_cq, g_ckv, w_uq, w_ukv, cmp_pe, cmp_w1, cmp_w2):
    cos128, sin128, cos64, sup64, sdn64 = tabs
    s = x.shape[0]
    w_a, w_b, w_c, w_d = w_groups

    aq, ak, av, akm = _proj_a(x, w_a, cos128, sin128)
    o_a = _moba(aq, ak, av, akm)

    bq, bk, bv = _proj_b(x, w_b, g_cq, g_ckv, w_uq, w_ukv, cos64, sup64, sdn64)
    o_b = _mla(bq, bk, bv)

    cq, cqr, kvcmp, kslc, vslc, kwin, vwin, gates = _proj_c(x, w_c, cos128, sin128)
    kv16 = kvcmp.reshape(2, s // NSA_CMP_STRIDE, NSA_CMP_STRIDE * HEAD_DIM)
    kvc = _nsa_compress(kv16, cmp_pe, cmp_w1, cmp_w2)
    o_cmp, sel = _nsa_cmp(cq, kvc, imp_map)
    o_c = _nsa_slc_win(cqr, sel, kslc, vslc, kwin, vwin, gates, o_cmp)

    dq, dk, dv, diq, dik, diw = _proj_d(x, w_d, cos128, sin128, cos64, sup64, sdn64)
    o_d = _dsa(dq, dk, dv, diq, dik, diw)
    return o_a, o_b, o_c, o_d


def kernel(x, mem, positions, ln_g, ln_b, ffn_w_gu, ffn_w_down, w_in, w_out, mla_g_cq, mla_g_ckv, mla_w_uq,
           mla_w_ukv, nsa_cmp_pe, nsa_cmp_w1, nsa_cmp_w2, mem_wq, mem_wkv, mem_wo):
    batch, s, d = x.shape
    w_gu_b, w_dn_b = ffn_w_gu, ffn_w_down
    w_groups = _split_w_in(w_in)
    w_out_b = w_out.astype(BF16)
    w_uq_b, w_ukv_b = _pad_w_uq(mla_w_uq), mla_w_ukv.astype(BF16)
    g_cq, g_ckv = mla_g_cq[:, None, :], mla_g_ckv[:, None, :]
    pe = nsa_cmp_pe.reshape(DEPTH, 2, 1, -1)
    w1_b, w2_b = nsa_cmp_w1.astype(BF16), nsa_cmp_w2.astype(BF16)
    wq_b, wkv_b, wo_b = mem_wq.astype(BF16), mem_wkv.astype(BF16), mem_wo.astype(BF16)
    g4, b4 = ln_g[:, :, None, :], ln_b[:, :, None, :]
    imp_map = _nsa_importance_map(s)

    outs = []
    for bi in range(batch):
        xb = x.reshape(s, d) if batch == 1 else x[bi]
        tabs = _rope_tables(positions[bi])
        for l in range(DEPTH):
            at = lambda a, *lead: (a, (l,) + lead)
            xb = _ffn_ln(xb, at(w_gu_b, 0), at(w_dn_b, 0), at(g4, 0), at(b4, 0))
            o_heads = _mixer_heads(xb, tabs, imp_map, tuple(at(w) for w in w_groups), at(g_cq), at(g_ckv),
                                   at(w_uq_b), at(w_ukv_b), at(pe), at(w1_b), at(w2_b))
            kv = _mem_kv(mem[bi], at(wkv_b))
            xb = _mix_mem_ln(*o_heads, at(w_out_b), xb, at(g4, 1), at(b4, 1), at(wq_b), kv, at(wo_b),
                             at(g4, 2), at(b4, 2))
            xb = _ffn_ln(xb, at(w_gu_b, 1), at(w_dn_b, 1), at(g4, 3), at(b4, 3))
        outs.append(xb)
    return outs[0].reshape(1, s, d) if batch == 1 else jnp.stack(outs)
```

```python
import functools

import numpy as np
import jax
import jax.numpy as jnp
from jax import lax
from jax.experimental import pallas as pl
from jax.experimental.pallas import tpu as pltpu

F32 = jnp.float32
BF16 = jnp.bfloat16
I32 = jnp.int32

D_MODEL = 2048
DEPTH = 4
HEAD_DIM = 128
ROPE_THETA = 10000.0
LN_EPS = 1e-5
RMS_EPS = 1e-6

MOBA_BLOCK = 256
MOBA_TOPK = 3
MLA_NOPE = 128
MLA_ROPE = 64
NSA_CMP_STRIDE = 16
NSA_SEL_BLOCK = 64
NSA_SEL_TOPK = 16
NSA_WINDOW = 512
DSA_IDX_HEADS = 16
DSA_TOPK = 256
D_FF = 5632
DN_ALPHA = (2 * DEPTH) ** 0.25

NEG = -(2.0 ** 100)
LOG2_E = 1.4426950408889634
INT_MIN = -(2 ** 31)
VMEM_LIMIT = 56 * 1024 * 1024

ROW_TILE = 512
FFN_ROWS = 1024
FFN_COLS = 256
ATT_TQ = 512
ATT_TK = 1024
CAUSAL_TK = 2048
DSA_TQ = 256
DSA_COUNT_ROWS = 128


def _cp(*sem):
    return pltpu.CompilerParams(dimension_semantics=sem, vmem_limit_bytes=VMEM_LIMIT)


def _wspec(w, block, index_map, resident=True):
    _, lead = w
    shape = (None,) * len(lead) + tuple(block)
    imap = lambda *g: tuple(lead) + tuple(index_map(*g))
    if resident:
        return pl.BlockSpec(shape, imap, pipeline_mode=pl.Buffered(1))
    return pl.BlockSpec(shape, imap)


def _wshape(w):
    arr, lead = w
    return arr.shape[len(lead):]


def _resident(shape, index_map):
    return pl.BlockSpec(shape, index_map, pipeline_mode=pl.Buffered(1))


def _dot(a, b):
    return jnp.dot(a, b, preferred_element_type=F32)


def _dot_nt(a, b):
    return lax.dot_general(a, b, (((1,), (1,)), ((), ())), preferred_element_type=F32)


def _layer_norm(y, g, b):
    mu = jnp.mean(y, axis=-1, keepdims=True)
    d = y - mu
    var = jnp.mean(d * d, axis=-1, keepdims=True)
    return d * lax.rsqrt(var + LN_EPS) * g + b


def _rope128(x, cos, sin_signed):
    return x * cos + pltpu.roll(x, 64, 1) * sin_signed


def _rope64(x, cos, s_up, s_dn):
    return x * cos + pltpu.roll(x, 32, 1) * s_up + pltpu.roll(x, 96, 1) * s_dn


def _log2_scale(d):
    return float(d) ** -0.5 * LOG2_E


def _with_ones(v):
    return jnp.concatenate([v, jnp.ones_like(v)], axis=1)


def _flash_update(s, v1, bias, keep, m_ref, acc_ref, slot):
    if bias is not None:
        s = s + bias
    if keep is not None:
        s = jnp.where(keep, s, NEG)
    m_old = m_ref[slot]
    m_new = jnp.maximum(m_old, jnp.max(s, axis=1, keepdims=True))
    alpha = jnp.exp2(m_old - m_new)
    p = jnp.exp2(s - m_new)
    acc_ref[slot] = alpha * acc_ref[slot] + _dot(p.astype(BF16), v1)
    m_ref[slot] = m_new


FLASH_ROWS = 256


def _flash_rows(q, k, v1, bias, keep, m_ref, acc_ref, hd):
    n = max(1, q.shape[0] // FLASH_ROWS)
    r = q.shape[0] // n
    rows = [slice(sp * r, (sp + 1) * r) for sp in range(n)]
    logits = [_dot_nt(q[rw], k) for rw in rows]
    for sp, rw in enumerate(rows):
        _flash_update(logits[sp], v1, None if bias is None else bias[rw], None if keep is None else keep[rw],
                      m_ref, acc_ref, hd * n + sp)


def _flash_heads(qs, ks, v1s, bias, keep, m_ref, acc_ref):
    def logits(hd):
        q = qs[hd]
        n = max(1, q.shape[0] // FLASH_ROWS)
        r = q.shape[0] // n
        rows = [slice(sp * r, (sp + 1) * r) for sp in range(n)]
        return rows, [_dot_nt(q[rw], ks[hd]) for rw in rows]

    ahead = logits(0)
    for hd in range(len(qs)):
        rows, s = ahead
        if hd + 1 < len(qs):
            ahead = logits(hd + 1)
        for sp, rw in enumerate(rows):
            _flash_update(s[sp], v1s[hd], None if bias is None else bias[rw], None if keep is None else keep[rw],
                          m_ref, acc_ref, hd * len(rows) + sp)


def _causal_flash(q, k_ref, v_ref, m_ref, acc_ref, i, tk):
    tq = q.shape[0]
    n_past = (i * tq) // tk
    n_rest = (i * tq - n_past * tk) // tq

    def tile(off, width, keep):
        _flash_rows(q, k_ref[pl.ds(off, width), :], _with_ones(v_ref[pl.ds(off, width), :]), None, keep,
                    m_ref, acc_ref, 0)

    def body(j, carry):
        tile(pl.multiple_of(j * tk, tk), tk, None)
        return carry

    lax.fori_loop(0, n_past, body, 0)
    for r in range(tk // tq):
        @pl.when(n_rest == r)
        def _():
            width = (r + 1) * tq
            row = lax.broadcasted_iota(I32, (tq, width), 0) + r * tq
            col = lax.broadcasted_iota(I32, (tq, width), 1)
            tile(pl.multiple_of(n_past * tk, tk), width, col <= row)


def _flash_result(acc_ref, hd, heads):
    n = acc_ref.shape[0] // heads
    parts = []
    for sp in range(n):
        a = acc_ref[hd * n + sp]
        parts.append(a[:, 0:HEAD_DIM] / a[:, HEAD_DIM:2 * HEAD_DIM])
    return jnp.concatenate(parts, axis=0) if n > 1 else parts[0]


def _flash_scratch(heads, tq):
    n = max(1, tq // FLASH_ROWS)
    return [pltpu.VMEM((heads * n, tq // n, 1), F32), pltpu.VMEM((heads * n, tq // n, 2 * HEAD_DIM), F32)]


def _flash_init(m_ref, acc_ref):
    m_ref[...] = jnp.full(m_ref.shape, NEG, F32)
    acc_ref[...] = jnp.zeros(acc_ref.shape, F32)


def _top_k_mask(work, index, k, axis=1):
    n = work.shape[axis]
    sel = jnp.zeros(work.shape, F32)
    for _ in range(k):
        m = jnp.max(work, axis=axis, keepdims=True)
        idx = jnp.min(jnp.where(work == m, index, n), axis=axis, keepdims=True)
        pick = index == jnp.where(m > -jnp.inf, idx, n)
        sel = jnp.where(pick, 1.0, sel)
        work = jnp.where(pick, -jnp.inf, work)
    return sel


def _key_block_table(s, block):
    return jnp.asarray((np.arange(128)[None, :] == (np.arange(s) // block)[:, None]) * -NEG, dtype=BF16)


def _rope_table_kernel(pos_ref, c_ref, cos128_ref, sin128_ref, cos64_ref, sup64_ref, sdn64_ref):
    pos = pos_ref[...]
    a128 = pos * c_ref[0:1, :]
    cos128_ref[...] = jnp.cos(a128)
    sin128_ref[...] = jnp.sin(a128) * c_ref[1:2, :]
    a64 = pos * c_ref[2:3, :]
    s64 = jnp.sin(a64)
    cos64_ref[...] = jnp.cos(a64)
    sup64_ref[...] = s64 * c_ref[3:4, :]
    sdn64_ref[...] = s64 * c_ref[4:5, :]


def _rope_consts():
    lane = np.arange(128)
    inv128 = (ROPE_THETA ** (-np.arange(0, 128, 2, dtype=np.float32) / 128)).astype(np.float32)
    inv64 = (ROPE_THETA ** (-np.arange(0, 64, 2, dtype=np.float32) / 64)).astype(np.float32)
    c = np.zeros((8, 128), np.float32)
    c[0] = inv128[lane % 64]
    c[1] = np.where(lane < 64, -1.0, 1.0)
    c[2] = inv64[(lane % 64) % 32]
    c[3] = np.where(lane % 64 >= 32, 1.0, 0.0)
    c[4] = np.where(lane % 64 < 32, -1.0, 0.0)
    return jnp.asarray(c)


def _rope_tables(positions):
    s = positions.shape[0]
    tm = min(ROW_TILE, s)
    pos = positions.astype(F32).reshape(s, 1)
    tab = jax.ShapeDtypeStruct((s, 128), F32)
    row = pl.BlockSpec((tm, 128), lambda i: (i, 0))
    return pl.pallas_call(
        _rope_table_kernel,
        grid=(s // tm,),
        in_specs=[pl.BlockSpec((tm, 1), lambda i: (i, 0)), pl.BlockSpec((8, 128), lambda i: (0, 0))],
        out_specs=[row] * 5,
        out_shape=[tab] * 5,
        compiler_params=_cp("parallel"),
        name="rope_tables",
    )(pos, _rope_consts())


def _ffn_kernel(x_ref, wg_ref, wu_ref, wd_ref, g_ref, b_ref, o_ref, xb_ref):
    j = pl.program_id(1)

    @pl.when(j == 0)
    def _():
        xb_ref[...] = x_ref[...].astype(BF16)
        o_ref[...] = jnp.zeros_like(o_ref)

    xb = xb_ref[...]
    g = _dot(xb, wg_ref[...].astype(BF16))
    u = _dot(xb, wu_ref[...].astype(BF16))
    h = (g * jax.nn.sigmoid(g) * u).astype(BF16)
    o_ref[...] += _dot(h, wd_ref[...].astype(BF16))

    @pl.when(j == pl.num_programs(1) - 1)
    def _():
        y = DN_ALPHA * x_ref[...] + 0.5 * o_ref[...]
        o_ref[...] = _layer_norm(y, g_ref[...], b_ref[...])


def _ffn_ln(x, w_gu, w_down, g, b):
    s, d = x.shape
    f = _wshape(w_down)[0]
    tm = min(FFN_ROWS, s)
    tf = FFN_COLS
    nf = f // tf
    return pl.pallas_call(
        _ffn_kernel,
        grid=(s // tm, nf),
        in_specs=[
            pl.BlockSpec((tm, d), lambda i, j: (i, 0), pipeline_mode=pl.Buffered(1)),
            _wspec(w_gu, (d, tf), lambda i, j: (0, j), resident=False),
            _wspec(w_gu, (d, tf), lambda i, j: (0, j + nf), resident=False),
            _wspec(w_down, (tf, d), lambda i, j: (j, 0), resident=False),
            _wspec(g, (1, d), lambda i, j: (0, 0)),
            _wspec(b, (1, d), lambda i, j: (0, 0)),
        ],
        out_specs=pl.BlockSpec((tm, d), lambda i, j: (i, 0)),
        out_shape=jax.ShapeDtypeStruct((s, d), F32),
        scratch_shapes=[pltpu.VMEM((tm, d), BF16)],
        compiler_params=_cp("parallel", "arbitrary"),
        name="ffn_ln",
    )(x, w_gu[0], w_gu[0], w_down[0], g[0], b[0])


def _proj_a_kernel(x_ref, w_ref, cos_ref, sin_ref, kb_ref, q_ref, k_ref, v_ref, km_ref):
    h = _dot(x_ref[...].astype(BF16), w_ref[...])
    cos, sin = cos_ref[...], sin_ref[...]
    nblk = km_ref.shape[0]
    for hd in range(4):
        lo = hd * HEAD_DIM
        q_ref[:, lo:lo + HEAD_DIM] = _rope128(h[:, lo:lo + HEAD_DIM], cos, sin)
        k = _rope128(h[:, 512 + lo:512 + lo + HEAD_DIM], cos, sin)
        k_ref[:, 2 * lo:2 * lo + HEAD_DIM] = k.astype(BF16)
        k_ref[:, 2 * lo + HEAD_DIM:2 * lo + 2 * HEAD_DIM] = kb_ref[...]
        for bi in range(nblk):
            kb = k[bi * MOBA_BLOCK:(bi + 1) * MOBA_BLOCK]
            km_ref[bi, :, lo:lo + HEAD_DIM] = jnp.mean(kb, axis=0, keepdims=True)
    v_ref[...] = h[:, 1024:1536].astype(BF16)


def _proj_a(x, w, cos128, sin128):
    s, d = x.shape
    tm = min(ROW_TILE, s)
    nblk = tm // MOBA_BLOCK
    row512 = pl.BlockSpec((tm, 512), lambda i: (i, 0))
    tab = pl.BlockSpec((tm, 128), lambda i: (i, 0))
    return pl.pallas_call(
        _proj_a_kernel,
        grid=(s // tm,),
        in_specs=[pl.BlockSpec((tm, d), lambda i: (i, 0)), _wspec(w, _wshape(w), lambda i: (0, 0)), tab, tab, tab],
        out_specs=[row512, pl.BlockSpec((tm, 1024), lambda i: (i, 0)), row512,
                   pl.BlockSpec((nblk, 1, 512), lambda i: (i, 0, 0))],
        out_shape=[
            jax.ShapeDtypeStruct((s, 512), F32),
            jax.ShapeDtypeStruct((s, 1024), BF16),
            jax.ShapeDtypeStruct((s, 512), BF16),
            jax.ShapeDtypeStruct((s // MOBA_BLOCK, 1, 512), F32),
        ],
        compiler_params=_cp("parallel"),
        name="proj_moba",
    )(x, w[0], cos128, sin128, _key_block_table(s, MOBA_BLOCK))


def _rms_norm(x, g):
    return x * lax.rsqrt(jnp.mean(x * x, axis=-1, keepdims=True) + RMS_EPS) * g


def _proj_b_kernel(x_ref, w_ref, gq_ref, gkv_ref, wuq_ref, wukv_ref, cos_ref, sup_ref, sdn_ref,
                   q_ref, k_ref, v_ref):
    h = _dot(x_ref[...].astype(BF16), w_ref[...])
    cos, sup, sdn = cos_ref[...], sup_ref[...], sdn_ref[...]
    cq = _rms_norm(h[:, 0:512], gq_ref[...]).astype(BF16)
    ckv = _rms_norm(h[:, 512:1024], gkv_ref[...]).astype(BF16)
    kr = _rope64(h[:, 1024:1152], cos, sup, sdn).astype(BF16)
    q = _dot(cq, wuq_ref[...])
    kv = _dot(ckv, wukv_ref[...])
    qs = _log2_scale(MLA_NOPE + MLA_ROPE)
    for hd in range(4):
        lo = hd * 256
        q_ref[:, lo:lo + 128] = (q[:, lo:lo + 128] * qs).astype(BF16)
        q_ref[:, lo + 128:lo + 256] = (_rope64(q[:, lo + 128:lo + 256], cos, sup, sdn) * qs).astype(BF16)
        k_ref[:, lo:lo + 128] = kv[:, lo:lo + 128].astype(BF16)
        k_ref[:, lo + 128:lo + 256] = kr
        v_ref[:, hd * 128:(hd + 1) * 128] = kv[:, lo + 128:lo + 256].astype(BF16)


def _proj_b(x, w, g_cq, g_ckv, w_uq, w_ukv, cos64, sup64, sdn64):
    s, d = x.shape
    tm = min(ROW_TILE, s)
    tab = pl.BlockSpec((tm, 128), lambda i: (i, 0))
    full = lambda a: _wspec(a, _wshape(a), lambda i: (0, 0))
    return pl.pallas_call(
        _proj_b_kernel,
        grid=(s // tm,),
        in_specs=[pl.BlockSpec((tm, d), lambda i: (i, 0)), full(w), full(g_cq), full(g_ckv), full(w_uq),
                  full(w_ukv), tab, tab, tab],
        out_specs=[pl.BlockSpec((tm, 1024), lambda i: (i, 0)), pl.BlockSpec((tm, 1024), lambda i: (i, 0)),
                   pl.BlockSpec((tm, 512), lambda i: (i, 0))],
        out_shape=[jax.ShapeDtypeStruct((s, 1024), BF16), jax.ShapeDtypeStruct((s, 1024), BF16),
                   jax.ShapeDtypeStruct((s, 512), BF16)],
        compiler_params=_cp("parallel"),
        name="proj_mla",
    )(x, w[0], g_cq[0], g_ckv[0], w_uq[0], w_ukv[0], cos64, sup64, sdn64)


def _proj_c_kernel(x_ref, w_ref, cos_ref, sin_ref, kb_ref, q_ref, qr_ref, kvcmp_ref, kslc_ref, vslc_ref,
                   kwin_ref, vwin_ref, gate_ref):
    h = _dot(x_ref[...].astype(BF16), w_ref[...])
    cos, sin = cos_ref[...], sin_ref[...]
    for hd in range(4):
        lo = hd * HEAD_DIM
        qh = h[:, lo:lo + HEAD_DIM]
        q_ref[:, lo:lo + HEAD_DIM] = qh.astype(BF16)
        qr_ref[:, lo:lo + HEAD_DIM] = (_rope128(qh, cos, sin) * _log2_scale(HEAD_DIM)).astype(BF16)
    kv = lambda i: h[:, 512 + i * 128:512 + (i + 1) * 128]
    kvcmp_ref[0] = kv(0).astype(BF16)
    kvcmp_ref[1] = kv(1).astype(BF16)
    kslc_ref[:, 0:HEAD_DIM] = _rope128(kv(2), cos, sin).astype(BF16)
    kslc_ref[:, HEAD_DIM:2 * HEAD_DIM] = kb_ref[...]
    vslc_ref[...] = kv(3).astype(BF16)
    kwin_ref[...] = _rope128(kv(4), cos, sin).astype(BF16)
    vwin_ref[...] = kv(5).astype(BF16)
    gate_ref[...] = jax.nn.sigmoid(h[:, 1280:1408])


def _proj_c(x, w, cos128, sin128):
    s, d = x.shape
    tm = min(ROW_TILE, s)
    r512 = pl.BlockSpec((tm, 512), lambda i: (i, 0))
    r128 = pl.BlockSpec((tm, 128), lambda i: (i, 0))
    b512 = jax.ShapeDtypeStruct((s, 512), BF16)
    b128 = jax.ShapeDtypeStruct((s, 128), BF16)
    return pl.pallas_call(
        _proj_c_kernel,
        grid=(s // tm,),
        in_specs=[pl.BlockSpec((tm, d), lambda i: (i, 0)), _wspec(w, _wshape(w), lambda i: (0, 0)), r128, r128, r128],
        out_specs=[r512, r512, pl.BlockSpec((2, tm, 128), lambda i: (0, i, 0)),
                   pl.BlockSpec((tm, 256), lambda i: (i, 0))] + [r128] * 4,
        out_shape=[b512, b512, jax.ShapeDtypeStruct((2, s, 128), BF16), jax.ShapeDtypeStruct((s, 256), BF16)]
                  + [b128] * 3 + [jax.ShapeDtypeStruct((s, 128), F32)],
        compiler_params=_cp("parallel"),
        name="proj_nsa",
    )(x, w[0], cos128, sin128, _key_block_table(s, NSA_SEL_BLOCK))


def _proj_d_kernel(x_ref, w_ref, cos_ref, sin_ref, c64_ref, sup_ref, sdn_ref,
                   q_ref, k_ref, v_ref, iq_ref, ik_ref, iw_ref):
    h = _dot(x_ref[...].astype(BF16), w_ref[...])
    cos, sin = cos_ref[...], sin_ref[...]
    c64, sup, sdn = c64_ref[...], sup_ref[...], sdn_ref[...]
    for hd in range(4):
        lo = hd * HEAD_DIM
        q_ref[:, lo:lo + HEAD_DIM] = (_rope128(h[:, lo:lo + HEAD_DIM], cos, sin) * _log2_scale(HEAD_DIM)).astype(BF16)
        k_ref[:, lo:lo + HEAD_DIM] = _rope128(h[:, 512 + lo:512 + lo + HEAD_DIM], cos, sin).astype(BF16)
    v_ref[...] = h[:, 1024:1536].astype(BF16)
    for p in range(8):
        lo = 1536 + p * 128
        iq_ref[:, p * 128:(p + 1) * 128] = _rope64(h[:, lo:lo + 128], c64, sup, sdn).astype(BF16)
    ik = _rope64(h[:, 2560:2688], c64, sup, sdn)
    ik_ref[:, 0:128] = ik.astype(BF16)
    ik_ref[:, 128:256] = pltpu.roll(ik, 64, 1).astype(BF16)
    iw_ref[...] = h[:, 2688:2816] * (1.0 / 32.0)


def _proj_d(x, w, cos128, sin128, cos64, sup64, sdn64):
    s, d = x.shape
    tm = min(ROW_TILE, s)
    r = lambda n: pl.BlockSpec((tm, n), lambda i: (i, 0))
    return pl.pallas_call(
        _proj_d_kernel,
        grid=(s // tm,),
        in_specs=[pl.BlockSpec((tm, d), lambda i: (i, 0)), _wspec(w, _wshape(w), lambda i: (0, 0))] + [r(128)] * 5,
        out_specs=[r(512), r(512), r(512), r(1024), r(256), r(128)],
        out_shape=[jax.ShapeDtypeStruct((s, 512), BF16)] * 3 + [
            jax.ShapeDtypeStruct((s, 1024), BF16), jax.ShapeDtypeStruct((s, 256), BF16),
            jax.ShapeDtypeStruct((s, 128), F32)],
        compiler_params=_cp("parallel"),
        name="proj_dsa",
    )(x, w[0], cos128, sin128, cos64, sup64, sdn64)


def _moba_kernel(q_ref, k_ref, v_ref, km_ref, o_ref, m_ref, acc_ref, *, tk, nb):
    i = pl.program_id(1)
    tq = q_ref.shape[0]
    q32 = q_ref[...]

    nbr = km_ref.shape[0]
    gate = lax.dot_general(km_ref[...], q32, (((1,), (1,)), ((), ())),
                           precision=lax.Precision.HIGHEST, preferred_element_type=F32)
    blk = lax.broadcasted_iota(I32, (nbr, tq), 0)
    cur = (i * tq + lax.broadcasted_iota(I32, (nbr, tq), 1)) // MOBA_BLOCK
    sel = _top_k_mask(jnp.where(blk < cur, gate, -jnp.inf), blk, min(MOBA_TOPK, nb - 1), axis=0)
    unchosen = jnp.where(blk == cur, 1.0, sel) - 1.0
    unchosen = jnp.concatenate([unchosen, jnp.full((128 - nbr, tq), -1.0, F32)], axis=0).T
    qa = jnp.concatenate([q32 * _log2_scale(HEAD_DIM), unchosen], axis=1).astype(BF16)

    _flash_init(m_ref, acc_ref)
    _causal_flash(qa, k_ref, v_ref, m_ref, acc_ref, i, tk)
    o_ref[...] = _flash_result(acc_ref, 0, 1).astype(o_ref.dtype)


def _moba(q32, k, v, kmean):
    s = q32.shape[0]
    nb = s // MOBA_BLOCK
    tq = min(ATT_TQ, s)
    tk = min(CAUSAL_TK, s)
    nbr = -(-nb // 8) * 8
    km = jnp.pad(kmean.reshape(nb, 512), ((0, nbr - nb), (0, 0)))
    return pl.pallas_call(
        functools.partial(_moba_kernel, tk=tk, nb=nb),
        grid=(4, s // tq),
        in_specs=[
            pl.BlockSpec((tq, HEAD_DIM), lambda h, i: (i, h)),
            pl.BlockSpec((s, 2 * HEAD_DIM), lambda h, i: (0, h)),
            pl.BlockSpec((s, HEAD_DIM), lambda h, i: (0, h)),
            pl.BlockSpec((nbr, HEAD_DIM), lambda h, i: (0, h)),
        ],
        out_specs=pl.BlockSpec((tq, HEAD_DIM), lambda h, i: (i, h)),
        out_shape=jax.ShapeDtypeStruct((s, 512), BF16),
        scratch_shapes=_flash_scratch(1, tq),
        compiler_params=_cp("parallel", "arbitrary"),
        name="moba_attn",
    )(q32, k, v, km)


def _mla_kernel(q_ref, k_ref, v_ref, o_ref, m_ref, acc_ref, *, tk):
    _flash_init(m_ref, acc_ref)
    _causal_flash(q_ref[...], k_ref, v_ref, m_ref, acc_ref, pl.program_id(1), tk)
    o_ref[...] = _flash_result(acc_ref, 0, 1).astype(o_ref.dtype)


def _mla(qcat, kcat, v):
    s = qcat.shape[0]
    tq = min(ATT_TQ, s)
    tk = min(CAUSAL_TK, s)
    return pl.pallas_call(
        functools.partial(_mla_kernel, tk=tk),
        grid=(4, s // tq),
        in_specs=[
            pl.BlockSpec((tq, 256), lambda h, i: (i, h)),
            pl.BlockSpec((s, 256), lambda h, i: (0, h)),
            pl.BlockSpec((s, HEAD_DIM), lambda h, i: (0, h)),
        ],
        out_specs=pl.BlockSpec((tq, HEAD_DIM), lambda h, i: (i, h)),
        out_shape=jax.ShapeDtypeStruct((s, 512), BF16),
        scratch_shapes=_flash_scratch(1, tq),
        compiler_params=_cp("parallel", "arbitrary"),
        name="mla_attn",
    )(qcat, kcat, v)


def _gelu_tanh(x):
    return 0.5 * x * (1.0 + jnp.tanh(0.7978845608028654 * (x + 0.044715 * x * x * x)))


def _nsa_compress_kernel(t_ref, pe_ref, w1_ref, w2_ref, o_ref):
    t = t_ref[0]
    w1 = w1_ref[...]
    half = t.shape[1]
    n = t.shape[0]
    lo = _dot(t, w1[0:half])
    hi = _dot(t, w1[half:2 * half])
    pe = _dot(pe_ref[...].astype(BF16), w1)
    pre = lo + pltpu.roll(hi, n - 1, 0) + pe
    o_ref[0] = _dot(_gelu_tanh(pre).astype(BF16), w2_ref[...]).astype(o_ref.dtype)


def _nsa_compress(kv16, pe, w1, w2):
    n = kv16.shape[1]
    sub = lambda a: _wspec(a, (None,) + _wshape(a)[1:], lambda i: (i, 0, 0), resident=False)
    return pl.pallas_call(
        _nsa_compress_kernel,
        grid=(2,),
        in_specs=[pl.BlockSpec((1,) + kv16.shape[1:], lambda i: (i, 0, 0)), sub(pe), sub(w1), sub(w2)],
        out_specs=pl.BlockSpec((1, n, HEAD_DIM), lambda i: (i, 0, 0)),
        out_shape=jax.ShapeDtypeStruct((2, n, HEAD_DIM), BF16),
        compiler_params=_cp("parallel"),
        name="nsa_compress",
    )(kv16, pe[0], w1[0], w2[0])


def _nsa_cmp_kernel(q_ref, kvc_ref, m_ref, o_ref, sel_ref, *, n_sel):
    i = pl.program_id(0)
    tq = q_ref.shape[0]
    n = kvc_ref.shape[1]
    nblk = m_ref.shape[1]
    scale = HEAD_DIM ** -0.5
    kc, vc = kvc_ref[0], kvc_ref[1]
    t_pos = i * tq + lax.broadcasted_iota(I32, (tq, n), 0)
    cmp_end = lax.broadcasted_iota(I32, (tq, n), 1) * NSA_CMP_STRIDE + (2 * NSA_CMP_STRIDE - 1)
    ok = cmp_end <= t_pos
    p_sum = jnp.zeros((tq, n), F32)
    for hd in range(4):
        lo = hd * HEAD_DIM
        s = jnp.where(ok, _dot_nt(q_ref[:, lo:lo + HEAD_DIM], kc) * scale, NEG)
        m = jnp.max(s, axis=1, keepdims=True)
        e = jnp.where(ok, jnp.exp(s - m), 0.0)
        l = jnp.sum(e, axis=1, keepdims=True)
        p = e / jnp.where(l > 0, l, 1.0)
        p_sum = p_sum + p
        o_ref[:, lo:lo + HEAD_DIM] = _dot(p.astype(BF16), vc)

    imp = jnp.dot(p_sum, m_ref[...], precision=lax.Precision.HIGHEST, preferred_element_type=F32)
    imp_t = imp.T
    blk = lax.broadcasted_iota(I32, (nblk, tq), 0)
    cur = (i * tq + lax.broadcasted_iota(I32, (nblk, tq), 1)) // NSA_SEL_BLOCK
    forced = (blk == 0) | (blk == cur) | (blk == cur - 1)
    imp_t = jnp.where(blk > cur, -jnp.inf, jnp.where(forced, jnp.inf, imp_t))
    sel_ref[...] = (_top_k_mask(imp_t, blk, n_sel, axis=0).T - 1.0).astype(sel_ref.dtype)


def _nsa_cmp(q, kvc, imp_map):
    s = q.shape[0]
    n = kvc.shape[1]
    nblk = imp_map.shape[1]
    tq = min(256, s)
    return pl.pallas_call(
        functools.partial(_nsa_cmp_kernel, n_sel=min(NSA_SEL_TOPK, s // NSA_SEL_BLOCK)),
        grid=(s // tq,),
        in_specs=[pl.BlockSpec((tq, 512), lambda i: (i, 0)),
                  _resident((2, n, HEAD_DIM), lambda i: (0, 0, 0)),
                  _resident((n, nblk), lambda i: (0, 0))],
        out_specs=[pl.BlockSpec((tq, 512), lambda i: (i, 0)), pl.BlockSpec((tq, nblk), lambda i: (i, 0))],
        out_shape=[jax.ShapeDtypeStruct((s, 512), F32), jax.ShapeDtypeStruct((s, nblk), BF16)],
        compiler_params=_cp("parallel"),
        name="nsa_cmp_select",
    )(q, kvc, imp_map)


def _nsa_slc_win_kernel(q_ref, sel_ref, ks_ref, vs_ref, kw_ref, vw_ref, g_ref, oc_ref, o_ref,
                        m_ref, acc_ref, *, tk):
    i = pl.program_id(0)
    tq = q_ref.shape[0]
    _flash_init(m_ref, acc_ref)

    def tile(off, width, keep):
        k = ks_ref[pl.ds(off, width), 0:HEAD_DIM]
        v1 = _with_ones(vs_ref[pl.ds(off, width), :])
        bias = _dot_nt(sel_ref[...], ks_ref[pl.ds(off, width), HEAD_DIM:2 * HEAD_DIM])
        _flash_heads([q_ref[:, hd * HEAD_DIM:(hd + 1) * HEAD_DIM] for hd in range(4)], [k] * 4, [v1] * 4,
                     bias, keep, m_ref, acc_ref)

    n_past = (i * tq) // tk
    n_rest = (i * tq - n_past * tk) // tq

    def body(j, carry):
        tile(pl.multiple_of(j * tk, tk), tk, None)
        return carry

    lax.fori_loop(0, n_past, body, 0)
    for r in range(tk // tq):
        @pl.when(n_rest == r)
        def _():
            width = (r + 1) * tq
            row = lax.broadcasted_iota(I32, (tq, width), 0) + r * tq
            col = lax.broadcasted_iota(I32, (tq, width), 1)
            tile(pl.multiple_of(n_past * tk, tk), width, col <= row)

    wlen = NSA_WINDOW + tq
    start = pl.multiple_of(jnp.maximum(i * tq - NSA_WINDOW, 0), 256)
    kw = kw_ref[pl.ds(start, wlen), :]
    vw1 = _with_ones(vw_ref[pl.ds(start, wlen), :])
    diff = (i * tq + lax.broadcasted_iota(I32, (tq, wlen), 0)) - (start + lax.broadcasted_iota(I32, (tq, wlen), 1))
    in_win = (diff >= 0) & (diff < NSA_WINDOW)
    g = g_ref[...]
    for hd in range(4):
        lo = hd * HEAD_DIM
        s = jnp.where(in_win, _dot_nt(q_ref[:, lo:lo + HEAD_DIM], kw), NEG)
        p = jnp.exp2(s - jnp.max(s, axis=1, keepdims=True))
        pv = _dot(p.astype(BF16), vw1)
        o_win = pv[:, 0:HEAD_DIM] / pv[:, HEAD_DIM:2 * HEAD_DIM]
        o_slc = _flash_result(acc_ref, hd, 4)
        out = (g[:, 3 * hd:3 * hd + 1] * oc_ref[:, lo:lo + HEAD_DIM] + g[:, 3 * hd + 1:3 * hd + 2] * o_slc
               + g[:, 3 * hd + 2:3 * hd + 3] * o_win)
        o_ref[:, lo:lo + HEAD_DIM] = out.astype(o_ref.dtype)


def _nsa_slc_win(q_r, sel, kslc, vslc, kwin, vwin, gates, o_cmp):
    s = q_r.shape[0]
    tq = min(ATT_TQ, s)
    tk = min(ATT_TK, s)
    full = lambda a: _resident(a.shape, lambda i: (0,) * a.ndim)
    return pl.pallas_call(
        functools.partial(_nsa_slc_win_kernel, tk=tk),
        grid=(s // tq,),
        in_specs=[pl.BlockSpec((tq, 512), lambda i: (i, 0)), pl.BlockSpec((tq, 128), lambda i: (i, 0)),
                  full(kslc), full(vslc), full(kwin), full(vwin),
                  pl.BlockSpec((tq, 128), lambda i: (i, 0)), pl.BlockSpec((tq, 512), lambda i: (i, 0))],
        out_specs=pl.BlockSpec((tq, 512), lambda i: (i, 0)),
        out_shape=jax.ShapeDtypeStruct((s, 512), BF16),
        scratch_shapes=_flash_scratch(4, tq),
        compiler_params=_cp("parallel"),
        name="nsa_slc_win",
    )(q_r, sel, kslc, vslc, kwin, vwin, gates, o_cmp)


def _dsa_kernel(q_ref, iq_ref, iw_ref, k_ref, v_ref, ik_ref, o_ref, key_ref, wb_ref, m_ref, acc_ref, *, tk):
    i = pl.program_id(0)
    tq = q_ref.shape[0]
    half = tk // 2
    n_tiles = (i * tq + tq + tk - 1) // tk
    row_h = lax.broadcasted_iota(I32, (tq, half), 0) + i * tq
    col_h = lax.broadcasted_iota(I32, (tq, half), 1)

    w = iw_ref[...]
    for hd in range(DSA_IDX_HEADS):
        wb_ref[hd] = jnp.broadcast_to(w[:, hd:hd + 1], (tq, 128))

    def score_body(j, carry):
        for hf in range(2):
            off = pl.multiple_of(j * tk + hf * half, half)
            ik_even = ik_ref[pl.ds(off, half), 0:128]
            ik_odd = ik_ref[pl.ds(off, half), 128:256]
            sc = jnp.zeros((tq, half), F32)
            for p in range(DSA_IDX_HEADS // 2):
                x = iq_ref[:, p * 128:(p + 1) * 128]
                we = jnp.concatenate([wb_ref[2 * p]] * (half // 128), axis=1)
                wo = jnp.concatenate([wb_ref[2 * p + 1]] * (half // 128), axis=1)
                sc = sc + jnp.maximum(_dot_nt(x, ik_even), 0.0) * we
                sc = sc + jnp.maximum(_dot_nt(x, ik_odd), 0.0) * wo
            bits = pltpu.bitcast(sc, I32)
            key = jnp.where(bits >= 0, bits, bits ^ 0x7FFFFFFF)
            key_ref[j, :, hf * half:(hf + 1) * half] = jnp.where(col_h + off <= row_h, key, INT_MIN)
        return carry

    lax.fori_loop(0, n_tiles, score_body, 0)

    rb = min(DSA_COUNT_ROWS, tq)

    ones = jnp.ones((128, 128), BF16)
    k_f = float(DSA_TOPK)

    def count_ge(cand128):
        parts = []
        for r0 in range(0, tq, rb):
            cand_b = cand128[r0:r0 + rb]

            def body(j, acc):
                for c in range(tk // 128):
                    acc = acc + jnp.where(key_ref[j, r0:r0 + rb, c * 128:(c + 1) * 128] >= cand_b, 1, 0)
                return acc

            parts.append(lax.fori_loop(0, n_tiles, body, jnp.zeros((rb, 128), I32)))
        per_lane = jnp.concatenate(parts, axis=0).astype(F32).astype(BF16)
        return _dot(per_lane, ones)

    zero = jnp.zeros((tq, 128), I32)
    c0 = count_ge(zero)
    lo0 = jnp.where(c0 >= k_f, zero, INT_MIN)
    c_lo0 = jnp.where(c0 >= k_f, c0, -1.0)

    def bit_cond(state):
        b, _, _, more = state
        return (b < 31) & (more > 0.0)

    def bit_body(state):
        b, lo, c_lo, _ = state
        more = jnp.max(jnp.where(c_lo != k_f, 1.0, 0.0))
        cand = lo | lax.shift_left(jnp.int32(1), 30 - b)
        c = count_ge(cand)
        take = c >= k_f
        return b + 1, jnp.where(take, cand, lo), jnp.where(take, c, c_lo), more

    passes, lo, c_lo, _ = lax.while_loop(bit_cond, bit_body, (jnp.int32(0), lo0, c_lo0, jnp.float32(1.0)))
    thr128 = jnp.maximum(lo, INT_MIN + 1)

    def retire_ties():
        need = k_f - count_ge(lo + 1)
        lane = lax.broadcasted_iota(I32, (rb, 128), 1)

        def tied_pos(j, r0, c):
            tied = key_ref[j, r0:r0 + rb, c * 128:(c + 1) * 128] == lo[r0:r0 + rb]
            return jnp.where(tied, j * tk + c * 128 + lane, -1)

        def count_tied_before(limit):
            parts = []
            for r0 in range(0, tq, rb):
                lim_b = limit[r0:r0 + rb]

                def body(j, acc, r0=r0, lim_b=lim_b):
                    for c in range(tk // 128):
                        pos = tied_pos(j, r0, c)
                        acc = acc + jnp.where(jnp.where(pos >= 0, pos, lim_b) < lim_b, 1, 0)
                    return acc

                parts.append(lax.fori_loop(0, n_tiles, body, jnp.zeros((rb, 128), I32)))
            return _dot(jnp.concatenate(parts, axis=0).astype(F32).astype(BF16), ones)

        n_bits = (key_ref.shape[0] * tk - 1).bit_length()

        def pos_body(b, last):
            cand = last | lax.shift_left(jnp.int32(1), n_bits - 1 - b)
            return jnp.where(count_tied_before(cand) < need, cand, last)

        last = lax.fori_loop(0, n_bits, pos_body, jnp.zeros((tq, 128), I32))
        last = jnp.where(c_lo > k_f, last, 2 ** 31 - 1)

        for r0 in range(0, tq, rb):
            def drop_body(j, carry, r0=r0):
                for c in range(tk // 128):
                    cols = slice(c * 128, (c + 1) * 128)
                    beyond = tied_pos(j, r0, c) > last[r0:r0 + rb]
                    key_ref[j, r0:r0 + rb, cols] = jnp.where(beyond, INT_MIN, key_ref[j, r0:r0 + rb, cols])
                return carry

            lax.fori_loop(0, n_tiles, drop_body, 0)

    @pl.when(passes >= 31)
    def _():
        @pl.when(jnp.max(jnp.where(c_lo > k_f, 1.0, 0.0)) > 0.0)
        def _():
            retire_ties()

    _flash_init(m_ref, acc_ref)

    def attn_body(j, carry):
        off = pl.multiple_of(j * tk, tk)
        keep = key_ref[j] >= jnp.concatenate([thr128] * (tk // 128), axis=1)
        cols = [slice(hd * HEAD_DIM, (hd + 1) * HEAD_DIM) for hd in range(4)]
        _flash_heads([q_ref[:, c] for c in cols], [k_ref[pl.ds(off, tk), c] for c in cols],
                     [_with_ones(v_ref[pl.ds(off, tk), c]) for c in cols], None, keep, m_ref, acc_ref)
        return carry

    lax.fori_loop(0, n_tiles, attn_body, 0)
    for hd in range(4):
        o_ref[:, hd * HEAD_DIM:(hd + 1) * HEAD_DIM] = _flash_result(acc_ref, hd, 4).astype(o_ref.dtype)


def _dsa(q, k, v, iq, ik2, iw):
    s = q.shape[0]
    tq = min(DSA_TQ, s)
    tk = min(ATT_TK, s)
    full = lambda a: _resident(a.shape, lambda i: (0, 0))
    return pl.pallas_call(
        functools.partial(_dsa_kernel, tk=tk),
        grid=(s // tq,),
        in_specs=[pl.BlockSpec((tq, 512), lambda i: (i, 0)), pl.BlockSpec((tq, 1024), lambda i: (i, 0)),
                  pl.BlockSpec((tq, 128), lambda i: (i, 0)), full(k), full(v), full(ik2)],
        out_specs=pl.BlockSpec((tq, 512), lambda i: (i, 0)),
        out_shape=jax.ShapeDtypeStruct((s, 512), BF16),
        scratch_shapes=[pltpu.VMEM((s // tk, tq, tk), I32), pltpu.VMEM((DSA_IDX_HEADS, tq, 128), F32)]
                       + _flash_scratch(4, tq),
        compiler_params=_cp("parallel"),
        name="dsa_attn",
    )(q, iq, iw, k, v, ik2)


def _mix_mem_kernel(oa_ref, ob_ref, oc_ref, od_ref, w_ref, x_ref, g1_ref, b1_ref, wq_ref, kv_ref, wo_ref,
                    g2_ref, b2_ref, o_ref):
    mix = _dot(oa_ref[...], w_ref[0:512, :])
    mix = mix + _dot(ob_ref[...], w_ref[512:1024, :])
    mix = mix + _dot(oc_ref[...], w_ref[1024:1536, :])
    mix = mix + _dot(od_ref[...], w_ref[1536:2048, :])
    x = _layer_norm(DN_ALPHA * x_ref[...] + mix, g1_ref[...], b1_ref[...])

    q = _dot(x.astype(BF16), wq_ref[...])
    heads = []
    for hd in range(4):
        lo = hd * HEAD_DIM
        qh = (q[:, lo:lo + HEAD_DIM] * _log2_scale(HEAD_DIM)).astype(BF16)
        s = _dot_nt(qh, kv_ref[:, lo:lo + HEAD_DIM])
        p = jnp.exp2(s - jnp.max(s, axis=1, keepdims=True))
        o = _dot(p.astype(BF16), kv_ref[:, 512 + lo:512 + lo + HEAD_DIM]) / jnp.sum(p, axis=1, keepdims=True)
        heads.append(o.astype(BF16))
    out = _dot(jnp.concatenate(heads, axis=1), wo_ref[...])
    o_ref[...] = _layer_norm(DN_ALPHA * x + out, g2_ref[...], b2_ref[...])


def _mix_mem_ln(o_a, o_b, o_c, o_d, w_out, x, g1, b1, wq, kv, wo, g2, b2):
    s, d = x.shape
    tm = min(ROW_TILE, s)
    r512 = pl.BlockSpec((tm, 512), lambda i: (i, 0))
    full = lambda a: _wspec(a, _wshape(a), lambda i: (0, 0))
    return pl.pallas_call(
        _mix_mem_kernel,
        grid=(s // tm,),
        in_specs=[r512, r512, r512, r512, full(w_out), pl.BlockSpec((tm, d), lambda i: (i, 0)), full(g1), full(b1),
                  full(wq), _resident(kv.shape, lambda i: (0, 0)), full(wo), full(g2), full(b2)],
        out_specs=pl.BlockSpec((tm, d), lambda i: (i, 0)),
        out_shape=jax.ShapeDtypeStruct((s, d), F32),
        compiler_params=_cp("parallel"),
        name="mix_mem_ln",
    )(o_a, o_b, o_c, o_d, w_out[0], x, g1[0], b1[0], wq[0], kv, wo[0], g2[0], b2[0])


def _mem_kv_kernel(mem_ref, w_ref, o_ref):
    o_ref[...] = _dot(mem_ref[...].astype(BF16), w_ref[...]).astype(o_ref.dtype)


def _mem_kv(mem, wkv):
    m, d = mem.shape
    n = _wshape(wkv)[1]
    return pl.pallas_call(
        _mem_kv_kernel,
        grid=(1,),
        in_specs=[pl.BlockSpec((m, d), lambda i: (0, 0)), _wspec(wkv, (d, n), lambda i: (0, 0))],
        out_specs=pl.BlockSpec((m, n), lambda i: (0, 0)),
        out_shape=jax.ShapeDtypeStruct((m, n), BF16),
        compiler_params=_cp("arbitrary"),
        name="mem_kv",
    )(mem, wkv[0])


def _pad_last(w, n):
    return jnp.pad(w, [(0, 0)] * (w.ndim - 1) + [(0, n - w.shape[-1])])


def _split_w_in(w_in):
    a = w_in[..., 0:1536]
    b = _pad_last(w_in[..., 1536:2624], 1152)
    c = _pad_last(w_in[..., 2624:3916], 1408)
    d = jnp.concatenate([w_in[..., 3916:6476], _pad_last(w_in[..., 6476:6540], 128),
                         _pad_last(w_in[..., 6540:6556], 128)], axis=-1)
    return tuple(t.astype(BF16) for t in (a, b, c, d))


def _pad_w_uq(w_uq):
    lead = w_uq.shape[:-1]
    w = _pad_last(w_uq.reshape(lead + (4, MLA_NOPE + MLA_ROPE)), 256)
    return w.reshape(lead + (4 * 256,)).astype(BF16)


def _nsa_importance_map(s):
    n = s // NSA_CMP_STRIDE
    nblk = s // NSA_SEL_BLOCK
    ni = np.arange(n)[:, None]
    bi = np.arange(nblk)[None, :]
    m = ((ni >= 4 * bi - 1) & (ni <= 4 * bi + 3)).astype(np.float32)
    return jnp.asarray(np.pad(m, ((0, 0), (0, 128 - nblk))))


def _mixer_heads(x, tabs, imp_map, w_groups, g_cq, g_ckv, w_uq, w_ukv, cmp_pe, cmp_w1, cmp_w2):
    cos128, sin128, cos64, sup64, sdn64 = tabs
    s = x.shape[0]
    w_a, w_b, w_c, w_d = w_groups

    aq, ak, av, akm = _proj_a(x, w_a, cos128, sin128)
    o_a = _moba(aq, ak, av, akm)

    bq, bk, bv = _proj_b(x, w_b, g_cq, g_ckv, w_uq, w_ukv, cos64, sup64, sdn64)
    o_b = _mla(bq, bk, bv)

    cq, cqr, kvcmp, kslc, vslc, kwin, vwin, gates = _proj_c(x, w_c, cos128, sin128)
    kv16 = kvcmp.reshape(2, s // NSA_CMP_STRIDE, NSA_CMP_STRIDE * HEAD_DIM)
    kvc = _nsa_compress(kv16, cmp_pe, cmp_w1, cmp_w2)
    o_cmp, sel = _nsa_cmp(cq, kvc, imp_map)
    o_c = _nsa_slc_win(cqr, sel, kslc, vslc, kwin, vwin, gates, o_cmp)

    dq, dk, dv, diq, dik, diw = _proj_d(x, w_d, cos128, sin128, cos64, sup64, sdn64)
    o_d = _dsa(dq, dk, dv, diq, dik, diw)
    return o_a, o_b, o_c, o_d


def kernel(x, mem, positions, ln_g, ln_b, ffn_w_gu, ffn_w_down, w_in, w_out, mla_g_cq, mla_g_ckv, mla_w_uq,
           mla_w_ukv, nsa_cmp_pe, nsa_cmp_w1, nsa_cmp_w2, mem_wq, mem_wkv, mem_wo):
    batch, s, d = x.shape
    w_gu_b, w_dn_b = ffn_w_gu, ffn_w_down
    w_groups = _split_w_in(w_in)
    w_out_b = w_out.astype(BF16)
    w_uq_b, w_ukv_b = _pad_w_uq(mla_w_uq), mla_w_ukv.astype(BF16)
    g_cq, g_ckv = mla_g_cq[:, None, :], mla_g_ckv[:, None, :]
    pe = nsa_cmp_pe.reshape(DEPTH, 2, 1, -1)
    w1_b, w2_b = nsa_cmp_w1.astype(BF16), nsa_cmp_w2.astype(BF16)
    wq_b, wkv_b, wo_b = mem_wq.astype(BF16), mem_wkv.astype(BF16), mem_wo.astype(BF16)
    g4, b4 = ln_g[:, :, None, :], ln_b[:, :, None, :]
    imp_map = _nsa_importance_map(s)

    outs = []
    for bi in range(batch):
        xb = x.reshape(s, d) if batch == 1 else x[bi]
        tabs = _rope_tables(positions[bi])
        for l in range(DEPTH):
            at = lambda a, *lead: (a, (l,) + lead)
            xb = _ffn_ln(xb, at(w_gu_b, 0), at(w_dn_b, 0), at(g4, 0), at(b4, 0))
            o_heads = _mixer_heads(xb, tabs, imp_map, tuple(at(w) for w in w_groups), at(g_cq), at(g_ckv),
                                   at(w_uq_b), at(w_ukv_b), at(pe), at(w1_b), at(w2_b))
            kv = _mem_kv(mem[bi], at(wkv_b))
            xb = _mix_mem_ln(*o_heads, at(w_out_b), xb, at(g4, 1), at(b4, 1), at(wq_b), kv, at(wo_b),
                             at(g4, 2), at(b4, 2))
            xb = _ffn_ln(xb, at(w_gu_b, 1), at(w_dn_b, 1), at(g4, 3), at(b4, 3))
        outs.append(xb)
    return outs[0].reshape(1, s, d) if batch == 1 else jnp.stack(outs)
```

```python
import functools

import numpy as np
import jax
import jax.numpy as jnp
from jax import lax
from jax.experimental import pallas as pl
from jax.experimental.pallas import tpu as pltpu

F32 = jnp.float32
BF16 = jnp.bfloat16
I32 = jnp.int32

D_MODEL = 2048
DEPTH = 4
HEAD_DIM = 128
ROPE_THETA = 10000.0
LN_EPS = 1e-5
RMS_EPS = 1e-6

MOBA_BLOCK = 256
MOBA_TOPK = 3
MLA_NOPE = 128
MLA_ROPE = 64
NSA_CMP_STRIDE = 16
NSA_SEL_BLOCK = 64
NSA_SEL_TOPK = 16
NSA_WINDOW = 512
DSA_IDX_HEADS = 16
DSA_TOPK = 256
D_FF = 5632
DN_ALPHA = (2 * DEPTH) ** 0.25

NEG = -(2.0 ** 100)
LOG2_E = 1.4426950408889634
INT_MIN = -(2 ** 31)
VMEM_LIMIT = 56 * 1024 * 1024

ROW_TILE = 512
FFN_ROWS = 1024
FFN_COLS = 256
ATT_TQ = 512
ATT_TK = 1024
CAUSAL_TK = 2048
DSA_TQ = 256
DSA_COUNT_ROWS = 128


def _cp(*sem):
    return pltpu.CompilerParams(dimension_semantics=sem, vmem_limit_bytes=VMEM_LIMIT)


def _wspec(w, block, index_map, resident=True):
    _, lead = w
    shape = (None,) * len(lead) + tuple(block)
    imap = lambda *g: tuple(lead) + tuple(index_map(*g))
    if resident:
        return pl.BlockSpec(shape, imap, pipeline_mode=pl.Buffered(1))
    return pl.BlockSpec(shape, imap)


def _wshape(w):
    arr, lead = w
    return arr.shape[len(lead):]


def _resident(shape, index_map):
    return pl.BlockSpec(shape, index_map, pipeline_mode=pl.Buffered(1))


def _dot(a, b):
    return jnp.dot(a, b, preferred_element_type=F32)


def _dot_nt(a, b):
    return lax.dot_general(a, b, (((1,), (1,)), ((), ())), preferred_element_type=F32)


def _layer_norm(y, g, b):
    mu = jnp.mean(y, axis=-1, keepdims=True)
    d = y - mu
    var = jnp.mean(d * d, axis=-1, keepdims=True)
    return d * lax.rsqrt(var + LN_EPS) * g + b


def _rope128(x, cos, sin_signed):
    return x * cos + pltpu.roll(x, 64, 1) * sin_signed


def _rope64(x, cos, s_up, s_dn):
    return x * cos + pltpu.roll(x, 32, 1) * s_up + pltpu.roll(x, 96, 1) * s_dn


def _log2_scale(d):
    return float(d) ** -0.5 * LOG2_E


def _with_ones(v):
    return jnp.concatenate([v, jnp.ones_like(v)], axis=1)


def _flash_update(s, v1, bias, keep, m_ref, acc_ref, slot):
    if bias is not None:
        s = s + bias
    if keep is not None:
        s = jnp.where(keep, s, NEG)
    m_old = m_ref[slot]
    m_new = jnp.maximum(m_old, jnp.max(s, axis=1, keepdims=True))
    alpha = jnp.exp2(m_old - m_new)
    p = jnp.exp2(s - m_new)
    acc_ref[slot] = alpha * acc_ref[slot] + _dot(p.astype(BF16), v1)
    m_ref[slot] = m_new


FLASH_ROWS = 256


def _flash_rows(q, k, v1, bias, keep, m_ref, acc_ref, hd):
    n = max(1, q.shape[0] // FLASH_ROWS)
    r = q.shape[0] // n
    rows = [slice(sp * r, (sp + 1) * r) for sp in range(n)]
    logits = [_dot_nt(q[rw], k) for rw in rows]
    for sp, rw in enumerate(rows):
        _flash_update(logits[sp], v1, None if bias is None else bias[rw], None if keep is None else keep[rw],
                      m_ref, acc_ref, hd * n + sp)


def _flash_heads(qs, ks, v1s, bias, keep, m_ref, acc_ref):
    def logits(hd):
        q = qs[hd]
        n = max(1, q.shape[0] // FLASH_ROWS)
        r = q.shape[0] // n
        rows = [slice(sp * r, (sp + 1) * r) for sp in range(n)]
        return rows, [_dot_nt(q[rw], ks[hd]) for rw in rows]

    ahead = logits(0)
    for hd in range(len(qs)):
        rows, s = ahead
        if hd + 1 < len(qs):
            ahead = logits(hd + 1)
        for sp, rw in enumerate(rows):
            _flash_update(s[sp], v1s[hd], None if bias is None else bias[rw], None if keep is None else keep[rw],
                          m_ref, acc_ref, hd * len(rows) + sp)


def _causal_flash(q, k_ref, v_ref, m_ref, acc_ref, i, tk):
    tq = q.shape[0]
    n_past = (i * tq) // tk
    n_rest = (i * tq - n_past * tk) // tq

    def tile(off, width, keep):
        _flash_rows(q, k_ref[pl.ds(off, width), :], _with_ones(v_ref[pl.ds(off, width), :]), None, keep,
                    m_ref, acc_ref, 0)

    def body(j, carry):
        tile(pl.multiple_of(j * tk, tk), tk, None)
        return carry

    lax.fori_loop(0, n_past, body, 0)
    for r in range(tk // tq):
        @pl.when(n_rest == r)
        def _():
            width = (r + 1) * tq
            row = lax.broadcasted_iota(I32, (tq, width), 0) + r * tq
            col = lax.broadcasted_iota(I32, (tq, width), 1)
            tile(pl.multiple_of(n_past * tk, tk), width, col <= row)


def _flash_result(acc_ref, hd, heads):
    n = acc_ref.shape[0] // heads
    parts = []
    for sp in range(n):
        a = acc_ref[hd * n + sp]
        parts.append(a[:, 0:HEAD_DIM] / a[:, HEAD_DIM:2 * HEAD_DIM])
    return jnp.concatenate(parts, axis=0) if n > 1 else parts[0]


def _flash_scratch(heads, tq):
    n = max(1, tq // FLASH_ROWS)
    return [pltpu.VMEM((heads * n, tq // n, 1), F32), pltpu.VMEM((heads * n, tq // n, 2 * HEAD_DIM), F32)]


def _flash_init(m_ref, acc_ref):
    m_ref[...] = jnp.full(m_ref.shape, NEG, F32)
    acc_ref[...] = jnp.zeros(acc_ref.shape, F32)


def _top_k_mask(work, index, k, axis=1):
    n = work.shape[axis]
    sel = jnp.zeros(work.shape, F32)
    for _ in range(k):
        m = jnp.max(work, axis=axis, keepdims=True)
        idx = jnp.min(jnp.where(work == m, index, n), axis=axis, keepdims=True)
        pick = index == jnp.where(m > -jnp.inf, idx, n)
        sel = jnp.where(pick, 1.0, sel)
        work = jnp.where(pick, -jnp.inf, work)
    return sel


def _key_block_table(s, block):
    return jnp.asarray((np.arange(128)[None, :] == (np.arange(s) // block)[:, None]) * -NEG, dtype=BF16)


def _rope_table_kernel(pos_ref, c_ref, cos128_ref, sin128_ref, cos64_ref, sup64_ref, sdn64_ref):
    pos = pos_ref[...]
    a128 = pos * c_ref[0:1, :]
    cos128_ref[...] = jnp.cos(a128)
    sin128_ref[...] = jnp.sin(a128) * c_ref[1:2, :]
    a64 = pos * c_ref[2:3, :]
    s64 = jnp.sin(a64)
    cos64_ref[...] = jnp.cos(a64)
    sup64_ref[...] = s64 * c_ref[3:4, :]
    sdn64_ref[...] = s64 * c_ref[4:5, :]


def _rope_consts():
    lane = np.arange(128)
    inv128 = (ROPE_THETA ** (-np.arange(0, 128, 2, dtype=np.float32) / 128)).astype(np.float32)
    inv64 = (ROPE_THETA ** (-np.arange(0, 64, 2, dtype=np.float32) / 64)).astype(np.float32)
    c = np.zeros((8, 128), np.float32)
    c[0] = inv128[lane % 64]
    c[1] = np.where(lane < 64, -1.0, 1.0)
    c[2] = inv64[(lane % 64) % 32]
    c[3] = np.where(lane % 64 >= 32, 1.0, 0.0)
    c[4] = np.where(lane % 64 < 32, -1.0, 0.0)
    return jnp.asarray(c)


def _rope_tables(positions):
    s = positions.shape[0]
    tm = min(ROW_TILE, s)
    pos = positions.astype(F32).reshape(s, 1)
    tab = jax.ShapeDtypeStruct((s, 128), F32)
    row = pl.BlockSpec((tm, 128), lambda i: (i, 0))
    return pl.pallas_call(
        _rope_table_kernel,
        grid=(s // tm,),
        in_specs=[pl.BlockSpec((tm, 1), lambda i: (i, 0)), pl.BlockSpec((8, 128), lambda i: (0, 0))],
        out_specs=[row] * 5,
        out_shape=[tab] * 5,
        compiler_params=_cp("parallel"),
        name="rope_tables",
    )(pos, _rope_consts())


def _ffn_kernel(x_ref, wg_ref, wu_ref, wd_ref, g_ref, b_ref, o_ref, xb_ref):
    j = pl.program_id(1)

    @pl.when(j == 0)
    def _():
        xb_ref[...] = x_ref[...].astype(BF16)
        o_ref[...] = jnp.zeros_like(o_ref)

    xb = xb_ref[...]
    g = _dot(xb, wg_ref[...].astype(BF16))
    u = _dot(xb, wu_ref[...].astype(BF16))
    h = (g * jax.nn.sigmoid(g) * u).astype(BF16)
    o_ref[...] += _dot(h, wd_ref[...].astype(BF16))

    @pl.when(j == pl.num_programs(1) - 1)
    def _():
        y = DN_ALPHA * x_ref[...] + 0.5 * o_ref[...]
        o_ref[...] = _layer_norm(y, g_ref[...], b_ref[...])


def _ffn_ln(x, w_gu, w_down, g, b):
    s, d = x.shape
    f = _wshape(w_down)[0]
    tm = min(FFN_ROWS, s)
    tf = FFN_COLS
    nf = f // tf
    return pl.pallas_call(
        _ffn_kernel,
        grid=(s // tm, nf),
        in_specs=[
            pl.BlockSpec((tm, d), lambda i, j: (i, 0), pipeline_mode=pl.Buffered(1)),
            _wspec(w_gu, (d, tf), lambda i, j: (0, j), resident=False),
            _wspec(w_gu, (d, tf), lambda i, j: (0, j + nf), resident=False),
            _wspec(w_down, (tf, d), lambda i, j: (j, 0), resident=False),
            _wspec(g, (1, d), lambda i, j: (0, 0)),
            _wspec(b, (1, d), lambda i, j: (0, 0)),
        ],
        out_specs=pl.BlockSpec((tm, d), lambda i, j: (i, 0)),
        out_shape=jax.ShapeDtypeStruct((s, d), F32),
        scratch_shapes=[pltpu.VMEM((tm, d), BF16)],
        compiler_params=_cp("parallel", "arbitrary"),
        name="ffn_ln",
    )(x, w_gu[0], w_gu[0], w_down[0], g[0], b[0])


def _proj_a_kernel(x_ref, w_ref, cos_ref, sin_ref, kb_ref, q_ref, k_ref, v_ref, km_ref):
    h = _dot(x_ref[...].astype(BF16), w_ref[...])
    cos, sin = cos_ref[...], sin_ref[...]
    nblk = km_ref.shape[0]
    for hd in range(4):
        lo = hd * HEAD_DIM
        q_ref[:, lo:lo + HEAD_DIM] = _rope128(h[:, lo:lo + HEAD_DIM], cos, sin)
        k = _rope128(h[:, 512 + lo:512 + lo + HEAD_DIM], cos, sin)
        k_ref[:, 2 * lo:2 * lo + HEAD_DIM] = k.astype(BF16)
        k_ref[:, 2 * lo + HEAD_DIM:2 * lo + 2 * HEAD_DIM] = kb_ref[...]
        for bi in range(nblk):
            kb = k[bi * MOBA_BLOCK:(bi + 1) * MOBA_BLOCK]
            km_ref[bi, :, lo:lo + HEAD_DIM] = jnp.mean(kb, axis=0, keepdims=True)
    v_ref[...] = h[:, 1024:1536].astype(BF16)


def _proj_a(x, w, cos128, sin128):
    s, d = x.shape
    tm = min(ROW_TILE, s)
    nblk = tm // MOBA_BLOCK
    row512 = pl.BlockSpec((tm, 512), lambda i: (i, 0))
    tab = pl.BlockSpec((tm, 128), lambda i: (i, 0))
    return pl.pallas_call(
        _proj_a_kernel,
        grid=(s // tm,),
        in_specs=[pl.BlockSpec((tm, d), lambda i: (i, 0)), _wspec(w, _wshape(w), lambda i: (0, 0)), tab, tab, tab],
        out_specs=[row512, pl.BlockSpec((tm, 1024), lambda i: (i, 0)), row512,
                   pl.BlockSpec((nblk, 1, 512), lambda i: (i, 0, 0))],
        out_shape=[
            jax.ShapeDtypeStruct((s, 512), F32),
            jax.ShapeDtypeStruct((s, 1024), BF16),
            jax.ShapeDtypeStruct((s, 512), BF16),
            jax.ShapeDtypeStruct((s // MOBA_BLOCK, 1, 512), F32),
        ],
        compiler_params=_cp("parallel"),
        name="proj_moba",
    )(x, w[0], cos128, sin128, _key_block_table(s, MOBA_BLOCK))


def _rms_norm(x, g):
    return x * lax.rsqrt(jnp.mean(x * x, axis=-1, keepdims=True) + RMS_EPS) * g


def _proj_b_kernel(x_ref, w_ref, gq_ref, gkv_ref, wuq_ref, wukv_ref, cos_ref, sup_ref, sdn_ref,
                   q_ref, k_ref, v_ref):
    h = _dot(x_ref[...].astype(BF16), w_ref[...])
    cos, sup, sdn = cos_ref[...], sup_ref[...], sdn_ref[...]
    cq = _rms_norm(h[:, 0:512], gq_ref[...]).astype(BF16)
    ckv = _rms_norm(h[:, 512:1024], gkv_ref[...]).astype(BF16)
    kr = _rope64(h[:, 1024:1152], cos, sup, sdn).astype(BF16)
    q = _dot(cq, wuq_ref[...])
    kv = _dot(ckv, wukv_ref[...])
    qs = _log2_scale(MLA_NOPE + MLA_ROPE)
    for hd in range(4):
        lo = hd * 256
        q_ref[:, lo:lo + 128] = (q[:, lo:lo + 128] * qs).astype(BF16)
        q_ref[:, lo + 128:lo + 256] = (_rope64(q[:, lo + 128:lo + 256], cos, sup, sdn) * qs).astype(BF16)
        k_ref[:, lo:lo + 128] = kv[:, lo:lo + 128].astype(BF16)
        k_ref[:, lo + 128:lo + 256] = kr
        v_ref[:, hd * 128:(hd + 1) * 128] = kv[:, lo + 128:lo + 256].astype(BF16)


def _proj_b(x, w, g_cq, g_ckv, w_uq, w_ukv, cos64, sup64, sdn64):
    s, d = x.shape
    tm = min(ROW_TILE, s)
    tab = pl.BlockSpec((tm, 128), lambda i: (i, 0))
    full = lambda a: _wspec(a, _wshape(a), lambda i: (0, 0))
    return pl.pallas_call(
        _proj_b_kernel,
        grid=(s // tm,),
        in_specs=[pl.BlockSpec((tm, d), lambda i: (i, 0)), full(w), full(g_cq), full(g_ckv), full(w_uq),
                  full(w_ukv), tab, tab, tab],
        out_specs=[pl.BlockSpec((tm, 1024), lambda i: (i, 0)), pl.BlockSpec((tm, 1024), lambda i: (i, 0)),
                   pl.BlockSpec((tm, 512), lambda i: (i, 0))],
        out_shape=[jax.ShapeDtypeStruct((s, 1024), BF16), jax.ShapeDtypeStruct((s, 1024), BF16),
                   jax.ShapeDtypeStruct((s, 512), BF16)],
        compiler_params=_cp("parallel"),
        name="proj_mla",
    )(x, w[0], g_cq[0], g_ckv[0], w_uq[0], w_ukv[0], cos64, sup64, sdn64)


def _proj_c_kernel(x_ref, w_ref, cos_ref, sin_ref, kb_ref, q_ref, qr_ref, kvcmp_ref, kslc_ref, vslc_ref,
                   kwin_ref, vwin_ref, gate_ref):
    h = _dot(x_ref[...].astype(BF16), w_ref[...])
    cos, sin = cos_ref[...], sin_ref[...]
    for hd in range(4):
        lo = hd * HEAD_DIM
        qh = h[:, lo:lo + HEAD_DIM]
        q_ref[:, lo:lo + HEAD_DIM] = qh.astype(BF16)
        qr_ref[:, lo:lo + HEAD_DIM] = (_rope128(qh, cos, sin) * _log2_scale(HEAD_DIM)).astype(BF16)
    kv = lambda i: h[:, 512 + i * 128:512 + (i + 1) * 128]
    kvcmp_ref[0] = kv(0).astype(BF16)
    kvcmp_ref[1] = kv(1).astype(BF16)
    kslc_ref[:, 0:HEAD_DIM] = _rope128(kv(2), cos, sin).astype(BF16)
    kslc_ref[:, HEAD_DIM:2 * HEAD_DIM] = kb_ref[...]
    vslc_ref[...] = kv(3).astype(BF16)
    kwin_ref[...] = _rope128(kv(4), cos, sin).astype(BF16)
    vwin_ref[...] = kv(5).astype(BF16)
    gate_ref[...] = jax.nn.sigmoid(h[:, 1280:1408])


def _proj_c(x, w, cos128, sin128):
    s, d = x.shape
    tm = min(ROW_TILE, s)
    r512 = pl.BlockSpec((tm, 512), lambda i: (i, 0))
    r128 = pl.BlockSpec((tm, 128), lambda i: (i, 0))
    b512 = jax.ShapeDtypeStruct((s, 512), BF16)
    b128 = jax.ShapeDtypeStruct((s, 128), BF16)
    return pl.pallas_call(
        _proj_c_kernel,
        grid=(s // tm,),
        in_specs=[pl.BlockSpec((tm, d), lambda i: (i, 0)), _wspec(w, _wshape(w), lambda i: (0, 0)), r128, r128, r128],
        out_specs=[r512, r512, pl.BlockSpec((2, tm, 128), lambda i: (0, i, 0)),
                   pl.BlockSpec((tm, 256), lambda i: (i, 0))] + [r128] * 4,
        out_shape=[b512, b512, jax.ShapeDtypeStruct((2, s, 128), BF16), jax.ShapeDtypeStruct((s, 256), BF16)]
                  + [b128] * 3 + [jax.ShapeDtypeStruct((s, 128), F32)],
        compiler_params=_cp("parallel"),
        name="proj_nsa",
    )(x, w[0], cos128, sin128, _key_block_table(s, NSA_SEL_BLOCK))


def _proj_d_kernel(x_ref, w_ref, cos_ref, sin_ref, c64_ref, sup_ref, sdn_ref,
                   q_ref, k_ref, v_ref, iq_ref, ik_ref, iw_ref):
    h = _dot(x_ref[...].astype(BF16), w_ref[...])
    cos, sin = cos_ref[...], sin_ref[...]
    c64, sup, sdn = c64_ref[...], sup_ref[...], sdn_ref[...]
    for hd in range(4):
        lo = hd * HEAD_DIM
        q_ref[:, lo:lo + HEAD_DIM] = (_rope128(h[:, lo:lo + HEAD_DIM], cos, sin) * _log2_scale(HEAD_DIM)).astype(BF16)
        k_ref[:, lo:lo + HEAD_DIM] = _rope128(h[:, 512 + lo:512 + lo + HEAD_DIM], cos, sin).astype(BF16)
    v_ref[...] = h[:, 1024:1536].astype(BF16)
    for p in range(8):
        lo = 1536 + p * 128
        iq_ref[:, p * 128:(p + 1) * 128] = _rope64(h[:, lo:lo + 128], c64, sup, sdn).astype(BF16)
    ik = _rope64(h[:, 2560:2688], c64, sup, sdn)
    ik_ref[:, 0:128] = ik.astype(BF16)
    ik_ref[:, 128:256] = pltpu.roll(ik, 64, 1).astype(BF16)
    iw_ref[...] = h[:, 2688:2816] * (1.0 / 32.0)


def _proj_d(x, w, cos128, sin128, cos64, sup64, sdn64):
    s, d = x.shape
    tm = min(ROW_TILE, s)
    r = lambda n: pl.BlockSpec((tm, n), lambda i: (i, 0))
    return pl.pallas_call(
        _proj_d_kernel,
        grid=(s // tm,),
        in_specs=[pl.BlockSpec((tm, d), lambda i: (i, 0)), _wspec(w, _wshape(w), lambda i: (0, 0))] + [r(128)] * 5,
        out_specs=[r(512), r(512), r(512), r(1024), r(256), r(128)],
        out_shape=[jax.ShapeDtypeStruct((s, 512), BF16)] * 3 + [
            jax.ShapeDtypeStruct((s, 1024), BF16), jax.ShapeDtypeStruct((s, 256), BF16),
            jax.ShapeDtypeStruct((s, 128), F32)],
        compiler_params=_cp("parallel"),
        name="proj_dsa",
    )(x, w[0], cos128, sin128, cos64, sup64, sdn64)


def _moba_kernel(q_ref, k_ref, v_ref, km_ref, o_ref, m_ref, acc_ref, *, tk, nb):
    i = pl.program_id(1)
    tq = q_ref.shape[0]
    q32 = q_ref[...]

    nbr = km_ref.shape[0]
    gate = lax.dot_general(km_ref[...], q32, (((1,), (1,)), ((), ())),
                           precision=lax.Precision.HIGHEST, preferred_element_type=F32)
    blk = lax.broadcasted_iota(I32, (nbr, tq), 0)
    cur = (i * tq + lax.broadcasted_iota(I32, (nbr, tq), 1)) // MOBA_BLOCK
    sel = _top_k_mask(jnp.where(blk < cur, gate, -jnp.inf), blk, min(MOBA_TOPK, nb - 1), axis=0)
    unchosen = jnp.where(blk == cur, 1.0, sel) - 1.0
    unchosen = jnp.concatenate([unchosen, jnp.full((128 - nbr, tq), -1.0, F32)], axis=0).T
    qa = jnp.concatenate([q32 * _log2_scale(HEAD_DIM), unchosen], axis=1).astype(BF16)

    _flash_init(m_ref, acc_ref)
    _causal_flash(qa, k_ref, v_ref, m_ref, acc_ref, i, tk)
    o_ref[...] = _flash_result(acc_ref, 0, 1).astype(o_ref.dtype)


def _moba(q32, k, v, kmean):
    s = q32.shape[0]
    nb = s // MOBA_BLOCK
    tq = min(ATT_TQ, s)
    tk = min(CAUSAL_TK, s)
    nbr = -(-nb // 8) * 8
    km = jnp.pad(kmean.reshape(nb, 512), ((0, nbr - nb), (0, 0)))
    return pl.pallas_call(
        functools.partial(_moba_kernel, tk=tk, nb=nb),
        grid=(4, s // tq),
        in_specs=[
            pl.BlockSpec((tq, HEAD_DIM), lambda h, i: (i, h)),
            pl.BlockSpec((s, 2 * HEAD_DIM), lambda h, i: (0, h)),
            pl.BlockSpec((s, HEAD_DIM), lambda h, i: (0, h)),
            pl.BlockSpec((nbr, HEAD_DIM), lambda h, i: (0, h)),
        ],
        out_specs=pl.BlockSpec((tq, HEAD_DIM), lambda h, i: (i, h)),
        out_shape=jax.ShapeDtypeStruct((s, 512), BF16),
        scratch_shapes=_flash_scratch(1, tq),
        compiler_params=_cp("parallel", "arbitrary"),
        name="moba_attn",
    )(q32, k, v, km)


def _mla_kernel(q_ref, k_ref, v_ref, o_ref, m_ref, acc_ref, *, tk):
    _flash_init(m_ref, acc_ref)
    _causal_flash(q_ref[...], k_ref, v_ref, m_ref, acc_ref, pl.program_id(1), tk)
    o_ref[...] = _flash_result(acc_ref, 0, 1).astype(o_ref.dtype)


def _mla(qcat, kcat, v):
    s = qcat.shape[0]
    tq = min(ATT_TQ, s)
    tk = min(CAUSAL_TK, s)
    return pl.pallas_call(
        functools.partial(_mla_kernel, tk=tk),
        grid=(4, s // tq),
        in_specs=[
            pl.BlockSpec((tq, 256), lambda h, i: (i, h)),
            pl.BlockSpec((s, 256), lambda h, i: (0, h)),
            pl.BlockSpec((s, HEAD_DIM), lambda h, i: (0, h)),
        ],
        out_specs=pl.BlockSpec((tq, HEAD_DIM), lambda h, i: (i, h)),
        out_shape=jax.ShapeDtypeStruct((s, 512), BF16),
        scratch_shapes=_flash_scratch(1, tq),
        compiler_params=_cp("parallel", "arbitrary"),
        name="mla_attn",
    )(qcat, kcat, v)


def _gelu_tanh(x):
    return 0.5 * x * (1.0 + jnp.tanh(0.7978845608028654 * (x + 0.044715 * x * x * x)))


def _nsa_compress_kernel(t_ref, pe_ref, w1_ref, w2_ref, o_ref):
    t = t_ref[0]
    w1 = w1_ref[...]
    half = t.shape[1]
    n = t.shape[0]
    lo = _dot(t, w1[0:half])
    hi = _dot(t, w1[half:2 * half])
    pe = _dot(pe_ref[...].astype(BF16), w1)
    pre = lo + pltpu.roll(hi, n - 1, 0) + pe
    o_ref[0] = _dot(_gelu_tanh(pre).astype(BF16), w2_ref[...]).astype(o_ref.dtype)


def _nsa_compress(kv16, pe, w1, w2):
    n = kv16.shape[1]
    sub = lambda a: _wspec(a, (None,) + _wshape(a)[1:], lambda i: (i, 0, 0), resident=False)
    return pl.pallas_call(
        _nsa_compress_kernel,
        grid=(2,),
        in_specs=[pl.BlockSpec((1,) + kv16.shape[1:], lambda i: (i, 0, 0)), sub(pe), sub(w1), sub(w2)],
        out_specs=pl.BlockSpec((1, n, HEAD_DIM), lambda i: (i, 0, 0)),
        out_shape=jax.ShapeDtypeStruct((2, n, HEAD_DIM), BF16),
        compiler_params=_cp("parallel"),
        name="nsa_compress",
    )(kv16, pe[0], w1[0], w2[0])


def _nsa_cmp_kernel(q_ref, kvc_ref, m_ref, o_ref, sel_ref, *, n_sel):
    i = pl.program_id(0)
    tq = q_ref.shape[0]
    n = kvc_ref.shape[1]
    nblk = m_ref.shape[1]
    scale = HEAD_DIM ** -0.5
    kc, vc = kvc_ref[0], kvc_ref[1]
    t_pos = i * tq + lax.broadcasted_iota(I32, (tq, n), 0)
    cmp_end = lax.broadcasted_iota(I32, (tq, n), 1) * NSA_CMP_STRIDE + (2 * NSA_CMP_STRIDE - 1)
    ok = cmp_end <= t_pos
    p_sum = jnp.zeros((tq, n), F32)
    for hd in range(4):
        lo = hd * HEAD_DIM
        s = jnp.where(ok, _dot_nt(q_ref[:, lo:lo + HEAD_DIM], kc) * scale, NEG)
        m = jnp.max(s, axis=1, keepdims=True)
        e = jnp.where(ok, jnp.exp(s - m), 0.0)
        l = jnp.sum(e, axis=1, keepdims=True)
        p = e / jnp.where(l > 0, l, 1.0)
        p_sum = p_sum + p
        o_ref[:, lo:lo + HEAD_DIM] = _dot(p.astype(BF16), vc)

    imp = jnp.dot(p_sum, m_ref[...], precision=lax.Precision.HIGHEST, preferred_element_type=F32)
    imp_t = imp.T
    blk = lax.broadcasted_iota(I32, (nblk, tq), 0)
    cur = (i * tq + lax.broadcasted_iota(I32, (nblk, tq), 1)) // NSA_SEL_BLOCK
    forced = (blk == 0) | (blk == cur) | (blk == cur - 1)
    imp_t = jnp.where(blk > cur, -jnp.inf, jnp.where(forced, jnp.inf, imp_t))
    sel_ref[...] = (_top_k_mask(imp_t, blk, n_sel, axis=0).T - 1.0).astype(sel_ref.dtype)


def _nsa_cmp(q, kvc, imp_map):
    s = q.shape[0]
    n = kvc.shape[1]
    nblk = imp_map.shape[1]
    tq = min(256, s)
    return pl.pallas_call(
        functools.partial(_nsa_cmp_kernel, n_sel=min(NSA_SEL_TOPK, s // NSA_SEL_BLOCK)),
        grid=(s // tq,),
        in_specs=[pl.BlockSpec((tq, 512), lambda i: (i, 0)),
                  _resident((2, n, HEAD_DIM), lambda i: (0, 0, 0)),
                  _resident((n, nblk), lambda i: (0, 0))],
        out_specs=[pl.BlockSpec((tq, 512), lambda i: (i, 0)), pl.BlockSpec((tq, nblk), lambda i: (i, 0))],
        out_shape=[jax.ShapeDtypeStruct((s, 512), F32), jax.ShapeDtypeStruct((s, nblk), BF16)],
        compiler_params=_cp("parallel"),
        name="nsa_cmp_select",
    )(q, kvc, imp_map)


def _nsa_slc_win_kernel(q_ref, sel_ref, ks_ref, vs_ref, kw_ref, vw_ref, g_ref, oc_ref, o_ref,
                        m_ref, acc_ref, *, tk):
    i = pl.program_id(0)
    tq = q_ref.shape[0]
    _flash_init(m_ref, acc_ref)

    def tile(off, width, keep):
        k = ks_ref[pl.ds(off, width), 0:HEAD_DIM]
        v1 = _with_ones(vs_ref[pl.ds(off, width), :])
        bias = _dot_nt(sel_ref[...], ks_ref[pl.ds(off, width), HEAD_DIM:2 * HEAD_DIM])
        _flash_heads([q_ref[:, hd * HEAD_DIM:(hd + 1) * HEAD_DIM] for hd in range(4)], [k] * 4, [v1] * 4,
                     bias, keep, m_ref, acc_ref)

    n_past = (i * tq) // tk
    n_rest = (i * tq - n_past * tk) // tq

    def body(j, carry):
        tile(pl.multiple_of(j * tk, tk), tk, None)
        return carry

    lax.fori_loop(0, n_past, body, 0)
    for r in range(tk // tq):
        @pl.when(n_rest == r)
        def _():
            width = (r + 1) * tq
            row = lax.broadcasted_iota(I32, (tq, width), 0) + r * tq
            col = lax.broadcasted_iota(I32, (tq, width), 1)
            tile(pl.multiple_of(n_past * tk, tk), width, col <= row)

    wlen = NSA_WINDOW + tq
    start = pl.multiple_of(jnp.maximum(i * tq - NSA_WINDOW, 0), 256)
    kw = kw_ref[pl.ds(start, wlen), :]
    vw1 = _with_ones(vw_ref[pl.ds(start, wlen), :])
    diff = (i * tq + lax.broadcasted_iota(I32, (tq, wlen), 0)) - (start + lax.broadcasted_iota(I32, (tq, wlen), 1))
    in_win = (diff >= 0) & (diff < NSA_WINDOW)
    g = g_ref[...]
    for hd in range(4):
        lo = hd * HEAD_DIM
        s = jnp.where(in_win, _dot_nt(q_ref[:, lo:lo + HEAD_DIM], kw), NEG)
        p = jnp.exp2(s - jnp.max(s, axis=1, keepdims=True))
        pv = _dot(p.astype(BF16), vw1)
        o_win = pv[:, 0:HEAD_DIM] / pv[:, HEAD_DIM:2 * HEAD_DIM]
        o_slc = _flash_result(acc_ref, hd, 4)
        out = (g[:, 3 * hd:3 * hd + 1] * oc_ref[:, lo:lo + HEAD_DIM] + g[:, 3 * hd + 1:3 * hd + 2] * o_slc
               + g[:, 3 * hd + 2:3 * hd + 3] * o_win)
        o_ref[:, lo:lo + HEAD_DIM] = out.astype(o_ref.dtype)


def _nsa_slc_win(q_r, sel, kslc, vslc, kwin, vwin, gates, o_cmp):
    s = q_r.shape[0]
    tq = min(ATT_TQ, s)
    tk = min(ATT_TK, s)
    full = lambda a: _resident(a.shape, lambda i: (0,) * a.ndim)
    return pl.pallas_call(
        functools.partial(_nsa_slc_win_kernel, tk=tk),
        grid=(s // tq,),
        in_specs=[pl.BlockSpec((tq, 512), lambda i: (i, 0)), pl.BlockSpec((tq, 128), lambda i: (i, 0)),
                  full(kslc), full(vslc), full(kwin), full(vwin),
                  pl.BlockSpec((tq, 128), lambda i: (i, 0)), pl.BlockSpec((tq, 512), lambda i: (i, 0))],
        out_specs=pl.BlockSpec((tq, 512), lambda i: (i, 0)),
        out_shape=jax.ShapeDtypeStruct((s, 512), BF16),
        scratch_shapes=_flash_scratch(4, tq),
        compiler_params=_cp("parallel"),
        name="nsa_slc_win",
    )(q_r, sel, kslc, vslc, kwin, vwin, gates, o_cmp)


def _dsa_kernel(q_ref, iq_ref, iw_ref, k_ref, v_ref, ik_ref, o_ref, key_ref, wb_ref, m_ref, acc_ref, *, tk):
    i = pl.program_id(0)
    tq = q_ref.shape[0]
    half = tk // 2
    n_tiles = (i * tq + tq + tk - 1) // tk
    row_h = lax.broadcasted_iota(I32, (tq, half), 0) + i * tq
    col_h = lax.broadcasted_iota(I32, (tq, half), 1)

    w = iw_ref[...]
    for hd in range(DSA_IDX_HEADS):
        wb_ref[hd] = jnp.broadcast_to(w[:, hd:hd + 1], (tq, 128))

    def score_body(j, carry):
        for hf in range(2):
            off = pl.multiple_of(j * tk + hf * half, half)
            ik_even = ik_ref[pl.ds(off, half), 0:128]
            ik_odd = ik_ref[pl.ds(off, half), 128:256]
            sc = jnp.zeros((tq, half), F32)
            for p in range(DSA_IDX_HEADS // 2):
                x = iq_ref[:, p * 128:(p + 1) * 128]
                we = jnp.concatenate([wb_ref[2 * p]] * (half // 128), axis=1)
                wo = jnp.concatenate([wb_ref[2 * p + 1]] * (half // 128), axis=1)
                sc = sc + jnp.maximum(_dot_nt(x, ik_even), 0.0) * we
                sc = sc + jnp.maximum(_dot_nt(x, ik_odd), 0.0) * wo
            bits = pltpu.bitcast(sc, I32)
            key = jnp.where(bits >= 0, bits, bits ^ 0x7FFFFFFF)
            key_ref[j, :, hf * half:(hf + 1) * half] = jnp.where(col_h + off <= row_h, key, INT_MIN)
        return carry

    lax.fori_loop(0, n_tiles, score_body, 0)

    rb = min(DSA_COUNT_ROWS, tq)

    ones = jnp.ones((128, 128), BF16)
    k_f = float(DSA_TOPK)

    def count_ge(cand128):
        parts = []
        for r0 in range(0, tq, rb):
            cand_b = cand128[r0:r0 + rb]

            def body(j, acc):
                for c in range(tk // 128):
                    acc = acc + jnp.where(key_ref[j, r0:r0 + rb, c * 128:(c + 1) * 128] >= cand_b, 1, 0)
                return acc

            parts.append(lax.fori_loop(0, n_tiles, body, jnp.zeros((rb, 128), I32)))
        per_lane = jnp.concatenate(parts, axis=0).astype(F32).astype(BF16)
        return _dot(per_lane, ones)

    zero = jnp.zeros((tq, 128), I32)
    c0 = count_ge(zero)
    lo0 = jnp.where(c0 >= k_f, zero, INT_MIN)
    c_lo0 = jnp.where(c0 >= k_f, c0, -1.0)

    def bit_cond(state):
        b, _, _, more = state
        return (b < 31) & (more > 0.0)

    def bit_body(state):
        b, lo, c_lo, _ = state
        more = jnp.max(jnp.where(c_lo != k_f, 1.0, 0.0))
        cand = lo | lax.shift_left(jnp.int32(1), 30 - b)
        c = count_ge(cand)
        take = c >= k_f
        return b + 1, jnp.where(take, cand, lo), jnp.where(take, c, c_lo), more

    passes, lo, c_lo, _ = lax.while_loop(bit_cond, bit_body, (jnp.int32(0), lo0, c_lo0, jnp.float32(1.0)))
    thr128 = jnp.maximum(lo, INT_MIN + 1)

    def retire_ties():
        need = k_f - count_ge(lo + 1)
        lane = lax.broadcasted_iota(I32, (rb, 128), 1)

        def tied_pos(j, r0, c):
            tied = key_ref[j, r0:r0 + rb, c * 128:(c + 1) * 128] == lo[r0:r0 + rb]
            return jnp.where(tied, j * tk + c * 128 + lane, -1)

        def count_tied_before(limit):
            parts = []
            for r0 in range(0, tq, rb):
                lim_b = limit[r0:r0 + rb]

                def body(j, acc, r0=r0, lim_b=lim_b):
                    for c in range(tk // 128):
                        pos = tied_pos(j, r0, c)
                        acc = acc + jnp.where(jnp.where(pos >= 0, pos, lim_b) < lim_b, 1, 0)
                    return acc

                parts.append(lax.fori_loop(0, n_tiles, body, jnp.zeros((rb, 128), I32)))
            return _dot(jnp.concatenate(parts, axis=0).astype(F32).astype(BF16), ones)

        n_bits = (key_ref.shape[0] * tk - 1).bit_length()

        def pos_body(b, last):
            cand = last | lax.shift_left(jnp.int32(1), n_bits - 1 - b)
            return jnp.where(count_tied_before(cand) < need, cand, last)

        last = lax.fori_loop(0, n_bits, pos_body, jnp.zeros((tq, 128), I32))
        last = jnp.where(c_lo > k_f, last, 2 ** 31 - 1)

        for r0 in range(0, tq, rb):
            def drop_body(j, carry, r0=r0):
                for c in range(tk // 128):
                    cols = slice(c * 128, (c + 1) * 128)
                    beyond = tied_pos(j, r0, c) > last[r0:r0 + rb]
                    key_ref[j, r0:r0 + rb, cols] = jnp.where(beyond, INT_MIN, key_ref[j, r0:r0 + rb, cols])
                return carry

            lax.fori_loop(0, n_tiles, drop_body, 0)

    @pl.when(passes >= 31)
    def _():
        @pl.when(jnp.max(jnp.where(c_lo > k_f, 1.0, 0.0)) > 0.0)
        def _():
            retire_ties()

    _flash_init(m_ref, acc_ref)

    def attn_body(j, carry):
        off = pl.multiple_of(j * tk, tk)
        keep = key_ref[j] >= jnp.concatenate([thr128] * (tk // 128), axis=1)
        cols = [slice(hd * HEAD_DIM, (hd + 1) * HEAD_DIM) for hd in range(4)]
        _flash_heads([q_ref[:, c] for c in cols], [k_ref[pl.ds(off, tk), c] for c in cols],
                     [_with_ones(v_ref[pl.ds(off, tk), c]) for c in cols], None, keep, m_ref, acc_ref)
        return carry

    lax.fori_loop(0, n_tiles, attn_body, 0)
    for hd in range(4):
        o_ref[:, hd * HEAD_DIM:(hd + 1) * HEAD_DIM] = _flash_result(acc_ref, hd, 4).astype(o_ref.dtype)


def _dsa(q, k, v, iq, ik2, iw):
    s = q.shape[0]
    tq = min(DSA_TQ, s)
    tk = min(ATT_TK, s)
    full = lambda a: _resident(a.shape, lambda i: (0, 0))
    return pl.pallas_call(
        functools.partial(_dsa_kernel, tk=tk),
        grid=(s // tq,),
        in_specs=[pl.BlockSpec((tq, 512), lambda i: (i, 0)), pl.BlockSpec((tq, 1024), lambda i: (i, 0)),
                  pl.BlockSpec((tq, 128), lambda i: (i, 0)), full(k), full(v), full(ik2)],
        out_specs=pl.BlockSpec((tq, 512), lambda i: (i, 0)),
        out_shape=jax.ShapeDtypeStruct((s, 512), BF16),
        scratch_shapes=[pltpu.VMEM((s // tk, tq, tk), I32), pltpu.VMEM((DSA_IDX_HEADS, tq, 128), F32)]
                       + _flash_scratch(4, tq),
        compiler_params=_cp("parallel"),
        name="dsa_attn",
    )(q, iq, iw, k, v, ik2)


def _mix_mem_kernel(oa_ref, ob_ref, oc_ref, od_ref, w_ref, x_ref, g1_ref, b1_ref, wq_ref, kv_ref, wo_ref,
                    g2_ref, b2_ref, o_ref):
    mix = _dot(oa_ref[...], w_ref[0:512, :])
    mix = mix + _dot(ob_ref[...], w_ref[512:1024, :])
    mix = mix + _dot(oc_ref[...], w_ref[1024:1536, :])
    mix = mix + _dot(od_ref[...], w_ref[1536:2048, :])
    x = _layer_norm(DN_ALPHA * x_ref[...] + mix, g1_ref[...], b1_ref[...])

    q = _dot(x.astype(BF16), wq_ref[...])
    heads = []
    for hd in range(4):
        lo = hd * HEAD_DIM
        qh = (q[:, lo:lo + HEAD_DIM] * _log2_scale(HEAD_DIM)).astype(BF16)
        s = _dot_nt(qh, kv_ref[:, lo:lo + HEAD_DIM])
        p = jnp.exp2(s - jnp.max(s, axis=1, keepdims=True))
        o = _dot(p.astype(BF16), kv_ref[:, 512 + lo:512 + lo + HEAD_DIM]) / jnp.sum(p, axis=1, keepdims=True)
        heads.append(o.astype(BF16))
    out = _dot(jnp.concatenate(heads, axis=1), wo_ref[...])
    o_ref[...] = _layer_norm(DN_ALPHA * x + out, g2_ref[...], b2_ref[...])


def _mix_mem_ln(o_a, o_b, o_c, o_d, w_out, x, g1, b1, wq, kv, wo, g2, b2):
    s, d = x.shape
    tm = min(ROW_TILE, s)
    r512 = pl.BlockSpec((tm, 512), lambda i: (i, 0))
    full = lambda a: _wspec(a, _wshape(a), lambda i: (0, 0))
    return pl.pallas_call(
        _mix_mem_kernel,
        grid=(s // tm,),
        in_specs=[r512, r512, r512, r512, full(w_out), pl.BlockSpec((tm, d), lambda i: (i, 0)), full(g1), full(b1),
                  full(wq), _resident(kv.shape, lambda i: (0, 0)), full(wo), full(g2), full(b2)],
        out_specs=pl.BlockSpec((tm, d), lambda i: (i, 0)),
        out_shape=jax.ShapeDtypeStruct((s, d), F32),
        compiler_params=_cp("parallel"),
        name="mix_mem_ln",
    )(o_a, o_b, o_c, o_d, w_out[0], x, g1[0], b1[0], wq[0], kv, wo[0], g2[0], b2[0])


def _mem_kv_kernel(mem_ref, w_ref, o_ref):
    o_ref[...] = _dot(mem_ref[...].astype(BF16), w_ref[...]).astype(o_ref.dtype)


def _mem_kv(mem, wkv):
    m, d = mem.shape
    n = _wshape(wkv)[1]
    return pl.pallas_call(
        _mem_kv_kernel,
        grid=(1,),
        in_specs=[pl.BlockSpec((m, d), lambda i: (0, 0)), _wspec(wkv, (d, n), lambda i: (0, 0))],
        out_specs=pl.BlockSpec((m, n), lambda i: (0, 0)),
        out_shape=jax.ShapeDtypeStruct((m, n), BF16),
        compiler_params=_cp("arbitrary"),
        name="mem_kv",
    )(mem, wkv[0])


def _pad_last(w, n):
    return jnp.pad(w, [(0, 0)] * (w.ndim - 1) + [(0, n - w.shape[-1])])


def _split_w_in(w_in):
    w_in = w_in.astype(BF16)
    a = w_in[..., 0:1536]
    b = _pad_last(w_in[..., 1536:2624], 1152)
    c = _pad_last(w_in[..., 2624:3916], 1408)
    d = jnp.concatenate([w_in[..., 3916:6476], _pad_last(w_in[..., 6476:6540], 128),
                         _pad_last(w_in[..., 6540:6556], 128)], axis=-1)
    return a, b, c, d


def _pad_w_uq(w_uq):
    lead = w_uq.shape[:-1]
    w = _pad_last(w_uq.reshape(lead + (4, MLA_NOPE + MLA_ROPE)), 256)
    return w.reshape(lead + (4 * 256,)).astype(BF16)


def _nsa_importance_map(s):
    n = s // NSA_CMP_STRIDE
    nblk = s // NSA_SEL_BLOCK
    ni = np.arange(n)[:, None]
    bi = np.arange(nblk)[None, :]
    m = ((ni >= 4 * bi - 1) & (ni <= 4 * bi + 3)).astype(np.float32)
    return jnp.asarray(np.pad(m, ((0, 0), (0, 128 - nblk))))


def _mixer_heads(x, tabs, imp_map, w_groups, g_cq, g_ckv, w_uq, w_ukv, cmp_pe, cmp_w1, cmp_w2):
    cos128, sin128, cos64, sup64, sdn64 = tabs
    s = x.shape[0]
    w_a, w_b, w_c, w_d = w_groups

    aq, ak, av, akm = _proj_a(x, w_a, cos128, sin128)
    o_a = _moba(aq, ak, av, akm)

    bq, bk, bv = _proj_b(x, w_b, g_cq, g_ckv, w_uq, w_ukv, cos64, sup64, sdn64)
    o_b = _mla(bq, bk, bv)

    cq, cqr, kvcmp, kslc, vslc, kwin, vwin, gates = _proj_c(x, w_c, cos128, sin128)
    kv16 = kvcmp.reshape(2, s // NSA_CMP_STRIDE, NSA_CMP_STRIDE * HEAD_DIM)
    kvc = _nsa_compress(kv16, cmp_pe, cmp_w1, cmp_w2)
    o_cmp, sel = _nsa_cmp(cq, kvc, imp_map)
    o_c = _nsa_slc_win(cqr, sel, kslc, vslc, kwin, vwin, gates, o_cmp)

    dq, dk, dv, diq, dik, diw = _proj_d(x, w_d, cos128, sin128, cos64, sup64, sdn64)
    o_d = _dsa(dq, dk, dv, diq, dik, diw)
    return o_a, o_b, o_c, o_d


def kernel(x, mem, positions, ln_g, ln_b, ffn_w_gu, ffn_w_down, w_in, w_out, mla_g_cq, mla_g_ckv, mla_w_uq,
           mla_w_ukv, nsa_cmp_pe, nsa_cmp_w1, nsa_cmp_w2, mem_wq, mem_wkv, mem_wo):
    batch, s, d = x.shape
    w_gu_b, w_dn_b = ffn_w_gu, ffn_w_down
    w_groups = _split_w_in(w_in)
    w_out_b = w_out.astype(BF16)
    w_uq_b, w_ukv_b = _pad_w_uq(mla_w_uq), mla_w_ukv.astype(BF16)
    g_cq, g_ckv = mla_g_cq[:, None, :], mla_g_ckv[:, None, :]
    pe = nsa_cmp_pe.reshape(DEPTH, 2, 1, -1)
    w1_b, w2_b = nsa_cmp_w1.astype(BF16), nsa_cmp_w2.astype(BF16)
    wq_b, wkv_b, wo_b = mem_wq.astype(BF16), mem_wkv.astype(BF16), mem_wo.astype(BF16)
    g4, b4 = ln_g[:, :, None, :], ln_b[:, :, None, :]
    imp_map = _nsa_importance_map(s)

    outs = []
    for bi in range(batch):
        xb = x.reshape(s, d) if batch == 1 else x[bi]
        tabs = _rope_tables(positions[bi])
        for l in range(DEPTH):
            at = lambda a, *lead: (a, (l,) + lead)
            xb = _ffn_ln(xb, at(w_gu_b, 0), at(w_dn_b, 0), at(g4, 0), at(b4, 0))
            o_heads = _mixer_heads(xb, tabs, imp_map, tuple(at(w) for w in w_groups), at(g_cq), at(g_ckv),
                                   at(w_uq_b), at(w_ukv_b), at(pe), at(w1_b), at(w2_b))
            kv = _mem_kv(mem[bi], at(wkv_b))
            xb = _mix_mem_ln(*o_heads, at(w_out_b), xb, at(g4, 1), at(b4, 1), at(wq_b), kv, at(wo_b),
                             at(g4, 2), at(b4, 2))
            xb = _ffn_ln(xb, at(w_gu_b, 1), at(w_dn_b, 1), at(g4, 3), at(b4, 3))
        outs.append(xb)
    return outs[0].reshape(1, s, d) if batch == 1 else jnp.stack(outs)
```

```python
import functools

import numpy as np
import jax
import jax.numpy as jnp
from jax import lax
from jax.experimental import pallas as pl
from jax.experimental.pallas import tpu as pltpu

F32 = jnp.float32
BF16 = jnp.bfloat16
I32 = jnp.int32

D_MODEL = 2048
DEPTH = 4
HEAD_DIM = 128
ROPE_THETA = 10000.0
LN_EPS = 1e-5
RMS_EPS = 1e-6

MOBA_BLOCK = 256
MOBA_TOPK = 3
MLA_NOPE = 128
MLA_ROPE = 64
NSA_CMP_STRIDE = 16
NSA_SEL_BLOCK = 64
NSA_SEL_TOPK = 16
NSA_WINDOW = 512
DSA_IDX_HEADS = 16
DSA_TOPK = 256
D_FF = 5632
DN_ALPHA = (2 * DEPTH) ** 0.25

NEG = -(2.0 ** 100)
LOG2_E = 1.4426950408889634
INT_MIN = -(2 ** 31)
VMEM_LIMIT = 56 * 1024 * 1024

ROW_TILE = 512
FFN_ROWS = 1024
FFN_COLS = 256
ATT_TQ = 512
ATT_TK = 1024
CAUSAL_TK = 4096
DSA_TQ = 256
DSA_COUNT_ROWS = 128


def _cp(*sem):
    return pltpu.CompilerParams(dimension_semantics=sem, vmem_limit_bytes=VMEM_LIMIT)


def _wspec(w, block, index_map, resident=True):
    _, lead = w
    shape = (None,) * len(lead) + tuple(block)
    imap = lambda *g: tuple(lead) + tuple(index_map(*g))
    if resident:
        return pl.BlockSpec(shape, imap, pipeline_mode=pl.Buffered(1))
    return pl.BlockSpec(shape, imap)


def _wshape(w):
    arr, lead = w
    return arr.shape[len(lead):]


def _resident(shape, index_map):
    return pl.BlockSpec(shape, index_map, pipeline_mode=pl.Buffered(1))


def _dot(a, b):
    return jnp.dot(a, b, preferred_element_type=F32)


def _dot_nt(a, b):
    return lax.dot_general(a, b, (((1,), (1,)), ((), ())), preferred_element_type=F32)


def _layer_norm(y, g, b):
    mu = jnp.mean(y, axis=-1, keepdims=True)
    d = y - mu
    var = jnp.mean(d * d, axis=-1, keepdims=True)
    return d * lax.rsqrt(var + LN_EPS) * g + b


def _rope128(x, cos, sin_signed):
    return x * cos + pltpu.roll(x, 64, 1) * sin_signed


def _rope64(x, cos, s_up, s_dn):
    return x * cos + pltpu.roll(x, 32, 1) * s_up + pltpu.roll(x, 96, 1) * s_dn


def _log2_scale(d):
    return float(d) ** -0.5 * LOG2_E


def _with_ones(v):
    return jnp.concatenate([v, jnp.ones_like(v)], axis=1)


def _flash_update(s, v1, bias, keep, m_ref, acc_ref, slot):
    if bias is not None:
        s = s + bias
    if keep is not None:
        s = jnp.where(keep, s, NEG)
    m_old = m_ref[slot]
    m_new = jnp.maximum(m_old, jnp.max(s, axis=1, keepdims=True))
    alpha = jnp.exp2(m_old - m_new)
    p = jnp.exp2(s - m_new)
    acc_ref[slot] = alpha * acc_ref[slot] + _dot(p.astype(BF16), v1)
    m_ref[slot] = m_new


FLASH_ROWS = 256


def _flash_rows(q, k, v1, bias, keep, m_ref, acc_ref, hd):
    n = max(1, q.shape[0] // FLASH_ROWS)
    r = q.shape[0] // n
    rows = [slice(sp * r, (sp + 1) * r) for sp in range(n)]
    logits = [_dot_nt(q[rw], k) for rw in rows]
    for sp, rw in enumerate(rows):
        _flash_update(logits[sp], v1, None if bias is None else bias[rw], None if keep is None else keep[rw],
                      m_ref, acc_ref, hd * n + sp)


def _flash_heads(qs, ks, v1s, bias, keep, m_ref, acc_ref):
    def logits(hd):
        q = qs[hd]
        n = max(1, q.shape[0] // FLASH_ROWS)
        r = q.shape[0] // n
        rows = [slice(sp * r, (sp + 1) * r) for sp in range(n)]
        return rows, [_dot_nt(q[rw], ks[hd]) for rw in rows]

    ahead = logits(0)
    for hd in range(len(qs)):
        rows, s = ahead
        if hd + 1 < len(qs):
            ahead = logits(hd + 1)
        for sp, rw in enumerate(rows):
            _flash_update(s[sp], v1s[hd], None if bias is None else bias[rw], None if keep is None else keep[rw],
                          m_ref, acc_ref, hd * len(rows) + sp)


def _causal_flash(q, k_ref, v_ref, m_ref, acc_ref, i, tk):
    tq = q.shape[0]
    n_past = (i * tq) // tk
    n_rest = (i * tq - n_past * tk) // tq

    def tile(off, width, keep):
        _flash_rows(q, k_ref[pl.ds(off, width), :], _with_ones(v_ref[pl.ds(off, width), :]), None, keep,
                    m_ref, acc_ref, 0)

    def body(j, carry):
        tile(pl.multiple_of(j * tk, tk), tk, None)
        return carry

    lax.fori_loop(0, n_past, body, 0)
    for r in range(tk // tq):
        @pl.when(n_rest == r)
        def _():
            width = (r + 1) * tq
            row = lax.broadcasted_iota(I32, (tq, width), 0) + r * tq
            col = lax.broadcasted_iota(I32, (tq, width), 1)
            tile(pl.multiple_of(n_past * tk, tk), width, col <= row)


def _flash_result(acc_ref, hd, heads):
    n = acc_ref.shape[0] // heads
    parts = []
    for sp in range(n):
        a = acc_ref[hd * n + sp]
        parts.append(a[:, 0:HEAD_DIM] / a[:, HEAD_DIM:2 * HEAD_DIM])
    return jnp.concatenate(parts, axis=0) if n > 1 else parts[0]


def _flash_scratch(heads, tq):
    n = max(1, tq // FLASH_ROWS)
    return [pltpu.VMEM((heads * n, tq // n, 1), F32), pltpu.VMEM((heads * n, tq // n, 2 * HEAD_DIM), F32)]


def _flash_init(m_ref, acc_ref):
    m_ref[...] = jnp.full(m_ref.shape, NEG, F32)
    acc_ref[...] = jnp.zeros(acc_ref.shape, F32)


def _top_k_mask(work, index, k, axis=1):
    n = work.shape[axis]
    sel = jnp.zeros(work.shape, F32)
    for _ in range(k):
        m = jnp.max(work, axis=axis, keepdims=True)
        idx = jnp.min(jnp.where(work == m, index, n), axis=axis, keepdims=True)
        pick = index == jnp.where(m > -jnp.inf, idx, n)
        sel = jnp.where(pick, 1.0, sel)
        work = jnp.where(pick, -jnp.inf, work)
    return sel


def _key_block_table(s, block):
    return jnp.asarray((np.arange(128)[None, :] == (np.arange(s) // block)[:, None]) * -NEG, dtype=BF16)


def _rope_table_kernel(pos_ref, c_ref, cos128_ref, sin128_ref, cos64_ref, sup64_ref, sdn64_ref):
    pos = pos_ref[...]
    a128 = pos * c_ref[0:1, :]
    cos128_ref[...] = jnp.cos(a128)
    sin128_ref[...] = jnp.sin(a128) * c_ref[1:2, :]
    a64 = pos * c_ref[2:3, :]
    s64 = jnp.sin(a64)
    cos64_ref[...] = jnp.cos(a64)
    sup64_ref[...] = s64 * c_ref[3:4, :]
    sdn64_ref[...] = s64 * c_ref[4:5, :]


def _rope_consts():
    lane = np.arange(128)
    inv128 = (ROPE_THETA ** (-np.arange(0, 128, 2, dtype=np.float32) / 128)).astype(np.float32)
    inv64 = (ROPE_THETA ** (-np.arange(0, 64, 2, dtype=np.float32) / 64)).astype(np.float32)
    c = np.zeros((8, 128), np.float32)
    c[0] = inv128[lane % 64]
    c[1] = np.where(lane < 64, -1.0, 1.0)
    c[2] = inv64[(lane % 64) % 32]
    c[3] = np.where(lane % 64 >= 32, 1.0, 0.0)
    c[4] = np.where(lane % 64 < 32, -1.0, 0.0)
    return jnp.asarray(c)


def _rope_tables(positions):
    s = positions.shape[0]
    tm = min(ROW_TILE, s)
    pos = positions.astype(F32).reshape(s, 1)
    tab = jax.ShapeDtypeStruct((s, 128), F32)
    row = pl.BlockSpec((tm, 128), lambda i: (i, 0))
    return pl.pallas_call(
        _rope_table_kernel,
        grid=(s // tm,),
        in_specs=[pl.BlockSpec((tm, 1), lambda i: (i, 0)), pl.BlockSpec((8, 128), lambda i: (0, 0))],
        out_specs=[row] * 5,
        out_shape=[tab] * 5,
        compiler_params=_cp("parallel"),
        name="rope_tables",
    )(pos, _rope_consts())


def _ffn_kernel(x_ref, wg_ref, wu_ref, wd_ref, g_ref, b_ref, o_ref, xb_ref):
    j = pl.program_id(1)

    @pl.when(j == 0)
    def _():
        xb_ref[...] = x_ref[...].astype(BF16)
        o_ref[...] = jnp.zeros_like(o_ref)

    xb = xb_ref[...]
    g = _dot(xb, wg_ref[...].astype(BF16))
    u = _dot(xb, wu_ref[...].astype(BF16))
    h = (g * jax.nn.sigmoid(g) * u).astype(BF16)
    o_ref[...] += _dot(h, wd_ref[...].astype(BF16))

    @pl.when(j == pl.num_programs(1) - 1)
    def _():
        y = DN_ALPHA * x_ref[...] + 0.5 * o_ref[...]
        o_ref[...] = _layer_norm(y, g_ref[...], b_ref[...])


def _ffn_ln(x, w_gu, w_down, g, b):
    s, d = x.shape
    f = _wshape(w_down)[0]
    tm = min(FFN_ROWS, s)
    tf = FFN_COLS
    nf = f // tf
    return pl.pallas_call(
        _ffn_kernel,
        grid=(s // tm, nf),
        in_specs=[
            pl.BlockSpec((tm, d), lambda i, j: (i, 0), pipeline_mode=pl.Buffered(1)),
            _wspec(w_gu, (d, tf), lambda i, j: (0, j), resident=False),
            _wspec(w_gu, (d, tf), lambda i, j: (0, j + nf), resident=False),
            _wspec(w_down, (tf, d), lambda i, j: (j, 0), resident=False),
            _wspec(g, (1, d), lambda i, j: (0, 0)),
            _wspec(b, (1, d), lambda i, j: (0, 0)),
        ],
        out_specs=pl.BlockSpec((tm, d), lambda i, j: (i, 0)),
        out_shape=jax.ShapeDtypeStruct((s, d), F32),
        scratch_shapes=[pltpu.VMEM((tm, d), BF16)],
        compiler_params=_cp("parallel", "arbitrary"),
        name="ffn_ln",
    )(x, w_gu[0], w_gu[0], w_down[0], g[0], b[0])


def _proj_a_kernel(x_ref, w_ref, cos_ref, sin_ref, kb_ref, q_ref, k_ref, v_ref, km_ref):
    h = _dot(x_ref[...].astype(BF16), w_ref[...])
    cos, sin = cos_ref[...], sin_ref[...]
    nblk = km_ref.shape[0]
    for hd in range(4):
        lo = hd * HEAD_DIM
        q_ref[:, lo:lo + HEAD_DIM] = _rope128(h[:, lo:lo + HEAD_DIM], cos, sin)
        k = _rope128(h[:, 512 + lo:512 + lo + HEAD_DIM], cos, sin)
        k_ref[:, 2 * lo:2 * lo + HEAD_DIM] = k.astype(BF16)
        k_ref[:, 2 * lo + HEAD_DIM:2 * lo + 2 * HEAD_DIM] = kb_ref[...]
        for bi in range(nblk):
            kb = k[bi * MOBA_BLOCK:(bi + 1) * MOBA_BLOCK]
            km_ref[bi, :, lo:lo + HEAD_DIM] = jnp.mean(kb, axis=0, keepdims=True)
    v_ref[...] = h[:, 1024:1536].astype(BF16)


def _proj_a(x, w, cos128, sin128):
    s, d = x.shape
    tm = min(ROW_TILE, s)
    nblk = tm // MOBA_BLOCK
    row512 = pl.BlockSpec((tm, 512), lambda i: (i, 0))
    tab = pl.BlockSpec((tm, 128), lambda i: (i, 0))
    return pl.pallas_call(
        _proj_a_kernel,
        grid=(s // tm,),
        in_specs=[pl.BlockSpec((tm, d), lambda i: (i, 0)), _wspec(w, _wshape(w), lambda i: (0, 0)), tab, tab, tab],
        out_specs=[row512, pl.BlockSpec((tm, 1024), lambda i: (i, 0)), row512,
                   pl.BlockSpec((nblk, 1, 512), lambda i: (i, 0, 0))],
        out_shape=[
            jax.ShapeDtypeStruct((s, 512), F32),
            jax.ShapeDtypeStruct((s, 1024), BF16),
            jax.ShapeDtypeStruct((s, 512), BF16),
            jax.ShapeDtypeStruct((s // MOBA_BLOCK, 1, 512), F32),
        ],
        compiler_params=_cp("parallel"),
        name="proj_moba",
    )(x, w[0], cos128, sin128, _key_block_table(s, MOBA_BLOCK))


def _rms_norm(x, g):
    return x * lax.rsqrt(jnp.mean(x * x, axis=-1, keepdims=True) + RMS_EPS) * g


def _proj_b_kernel(x_ref, w_ref, gq_ref, gkv_ref, wuq_ref, wukv_ref, cos_ref, sup_ref, sdn_ref,
                   q_ref, k_ref, v_ref):
    h = _dot(x_ref[...].astype(BF16), w_ref[...])
    cos, sup, sdn = cos_ref[...], sup_ref[...], sdn_ref[...]
    cq = _rms_norm(h[:, 0:512], gq_ref[...]).astype(BF16)
    ckv = _rms_norm(h[:, 512:1024], gkv_ref[...]).astype(BF16)
    kr = _rope64(h[:, 1024:1152], cos, sup, sdn).astype(BF16)
    q = _dot(cq, wuq_ref[...])
    kv = _dot(ckv, wukv_ref[...])
    qs = _log2_scale(MLA_NOPE + MLA_ROPE)
    for hd in range(4):
        lo = hd * 256
        q_ref[:, lo:lo + 128] = (q[:, lo:lo + 128] * qs).astype(BF16)
        q_ref[:, lo + 128:lo + 256] = (_rope64(q[:, lo + 128:lo + 256], cos, sup, sdn) * qs).astype(BF16)
        k_ref[:, lo:lo + 128] = kv[:, lo:lo + 128].astype(BF16)
        k_ref[:, lo + 128:lo + 256] = kr
        v_ref[:, hd * 128:(hd + 1) * 128] = kv[:, lo + 128:lo + 256].astype(BF16)


def _proj_b(x, w, g_cq, g_ckv, w_uq, w_ukv, cos64, sup64, sdn64):
    s, d = x.shape
    tm = min(ROW_TILE, s)
    tab = pl.BlockSpec((tm, 128), lambda i: (i, 0))
    full = lambda a: _wspec(a, _wshape(a), lambda i: (0, 0))
    return pl.pallas_call(
        _proj_b_kernel,
        grid=(s // tm,),
        in_specs=[pl.BlockSpec((tm, d), lambda i: (i, 0)), full(w), full(g_cq), full(g_ckv), full(w_uq),
                  full(w_ukv), tab, tab, tab],
        out_specs=[pl.BlockSpec((tm, 1024), lambda i: (i, 0)), pl.BlockSpec((tm, 1024), lambda i: (i, 0)),
                   pl.BlockSpec((tm, 512), lambda i: (i, 0))],
        out_shape=[jax.ShapeDtypeStruct((s, 1024), BF16), jax.ShapeDtypeStruct((s, 1024), BF16),
                   jax.ShapeDtypeStruct((s, 512), BF16)],
        compiler_params=_cp("parallel"),
        name="proj_mla",
    )(x, w[0], g_cq[0], g_ckv[0], w_uq[0], w_ukv[0], cos64, sup64, sdn64)


def _proj_c_kernel(x_ref, w_ref, cos_ref, sin_ref, kb_ref, q_ref, qr_ref, kvcmp_ref, kslc_ref, vslc_ref,
                   kwin_ref, vwin_ref, gate_ref):
    h = _dot(x_ref[...].astype(BF16), w_ref[...])
    cos, sin = cos_ref[...], sin_ref[...]
    for hd in range(4):
        lo = hd * HEAD_DIM
        qh = h[:, lo:lo + HEAD_DIM]
        q_ref[:, lo:lo + HEAD_DIM] = qh.astype(BF16)
        qr_ref[:, lo:lo + HEAD_DIM] = (_rope128(qh, cos, sin) * _log2_scale(HEAD_DIM)).astype(BF16)
    kv = lambda i: h[:, 512 + i * 128:512 + (i + 1) * 128]
    kvcmp_ref[0] = kv(0).astype(BF16)
    kvcmp_ref[1] = kv(1).astype(BF16)
    kslc_ref[:, 0:HEAD_DIM] = _rope128(kv(2), cos, sin).astype(BF16)
    kslc_ref[:, HEAD_DIM:2 * HEAD_DIM] = kb_ref[...]
    vslc_ref[...] = kv(3).astype(BF16)
    kwin_ref[...] = _rope128(kv(4), cos, sin).astype(BF16)
    vwin_ref[...] = kv(5).astype(BF16)
    gate_ref[...] = jax.nn.sigmoid(h[:, 1280:1408])


def _proj_c(x, w, cos128, sin128):
    s, d = x.shape
    tm = min(ROW_TILE, s)
    r512 = pl.BlockSpec((tm, 512), lambda i: (i, 0))
    r128 = pl.BlockSpec((tm, 128), lambda i: (i, 0))
    b512 = jax.ShapeDtypeStruct((s, 512), BF16)
    b128 = jax.ShapeDtypeStruct((s, 128), BF16)
    return pl.pallas_call(
        _proj_c_kernel,
        grid=(s // tm,),
        in_specs=[pl.BlockSpec((tm, d), lambda i: (i, 0)), _wspec(w, _wshape(w), lambda i: (0, 0)), r128, r128, r128],
        out_specs=[r512, r512, pl.BlockSpec((2, tm, 128), lambda i: (0, i, 0)),
                   pl.BlockSpec((tm, 256), lambda i: (i, 0))] + [r128] * 4,
        out_shape=[b512, b512, jax.ShapeDtypeStruct((2, s, 128), BF16), jax.ShapeDtypeStruct((s, 256), BF16)]
                  + [b128] * 3 + [jax.ShapeDtypeStruct((s, 128), F32)],
        compiler_params=_cp("parallel"),
        name="proj_nsa",
    )(x, w[0], cos128, sin128, _key_block_table(s, NSA_SEL_BLOCK))


def _proj_d_kernel(x_ref, w_ref, cos_ref, sin_ref, c64_ref, sup_ref, sdn_ref,
                   q_ref, k_ref, v_ref, iq_ref, ik_ref, iw_ref):
    h = _dot(x_ref[...].astype(BF16), w_ref[...])
    cos, sin = cos_ref[...], sin_ref[...]
    c64, sup, sdn = c64_ref[...], sup_ref[...], sdn_ref[...]
    for hd in range(4):
        lo = hd * HEAD_DIM
        q_ref[:, lo:lo + HEAD_DIM] = (_rope128(h[:, lo:lo + HEAD_DIM], cos, sin) * _log2_scale(HEAD_DIM)).astype(BF16)
        k_ref[:, lo:lo + HEAD_DIM] = _rope128(h[:, 512 + lo:512 + lo + HEAD_DIM], cos, sin).astype(BF16)
    v_ref[...] = h[:, 1024:1536].astype(BF16)
    for p in range(8):
        lo = 1536 + p * 128
        iq_ref[:, p * 128:(p + 1) * 128] = _rope64(h[:, lo:lo + 128], c64, sup, sdn).astype(BF16)
    ik = _rope64(h[:, 2560:2688], c64, sup, sdn)
    ik_ref[:, 0:128] = ik.astype(BF16)
    ik_ref[:, 128:256] = pltpu.roll(ik, 64, 1).astype(BF16)
    iw_ref[...] = h[:, 2688:2816] * (1.0 / 32.0)


def _proj_d(x, w, cos128, sin128, cos64, sup64, sdn64):
    s, d = x.shape
    tm = min(ROW_TILE, s)
    r = lambda n: pl.BlockSpec((tm, n), lambda i: (i, 0))
    return pl.pallas_call(
        _proj_d_kernel,
        grid=(s // tm,),
        in_specs=[pl.BlockSpec((tm, d), lambda i: (i, 0)), _wspec(w, _wshape(w), lambda i: (0, 0))] + [r(128)] * 5,
        out_specs=[r(512), r(512), r(512), r(1024), r(256), r(128)],
        out_shape=[jax.ShapeDtypeStruct((s, 512), BF16)] * 3 + [
            jax.ShapeDtypeStruct((s, 1024), BF16), jax.ShapeDtypeStruct((s, 256), BF16),
            jax.ShapeDtypeStruct((s, 128), F32)],
        compiler_params=_cp("parallel"),
        name="proj_dsa",
    )(x, w[0], cos128, sin128, cos64, sup64, sdn64)


def _moba_kernel(q_ref, k_ref, v_ref, km_ref, o_ref, m_ref, acc_ref, *, tk, nb):
    i = pl.program_id(1)
    tq = q_ref.shape[0]
    q32 = q_ref[...]

    nbr = km_ref.shape[0]
    gate = lax.dot_general(km_ref[...], q32, (((1,), (1,)), ((), ())),
                           precision=lax.Precision.HIGHEST, preferred_element_type=F32)
    blk = lax.broadcasted_iota(I32, (nbr, tq), 0)
    cur = (i * tq + lax.broadcasted_iota(I32, (nbr, tq), 1)) // MOBA_BLOCK
    sel = _top_k_mask(jnp.where(blk < cur, gate, -jnp.inf), blk, min(MOBA_TOPK, nb - 1), axis=0)
    unchosen = jnp.where(blk == cur, 1.0, sel) - 1.0
    unchosen = jnp.concatenate([unchosen, jnp.full((128 - nbr, tq), -1.0, F32)], axis=0).T
    qa = jnp.concatenate([q32 * _log2_scale(HEAD_DIM), unchosen], axis=1).astype(BF16)

    _flash_init(m_ref, acc_ref)
    _causal_flash(qa, k_ref, v_ref, m_ref, acc_ref, i, tk)
    o_ref[...] = _flash_result(acc_ref, 0, 1).astype(o_ref.dtype)


def _moba(q32, k, v, kmean):
    s = q32.shape[0]
    nb = s // MOBA_BLOCK
    tq = min(ATT_TQ, s)
    tk = min(CAUSAL_TK, s)
    nbr = -(-nb // 8) * 8
    km = jnp.pad(kmean.reshape(nb, 512), ((0, nbr - nb), (0, 0)))
    return pl.pallas_call(
        functools.partial(_moba_kernel, tk=tk, nb=nb),
        grid=(4, s // tq),
        in_specs=[
            pl.BlockSpec((tq, HEAD_DIM), lambda h, i: (i, h)),
            pl.BlockSpec((s, 2 * HEAD_DIM), lambda h, i: (0, h)),
            pl.BlockSpec((s, HEAD_DIM), lambda h, i: (0, h)),
            pl.BlockSpec((nbr, HEAD_DIM), lambda h, i: (0, h)),
        ],
        out_specs=pl.BlockSpec((tq, HEAD_DIM), lambda h, i: (i, h)),
        out_shape=jax.ShapeDtypeStruct((s, 512), BF16),
        scratch_shapes=_flash_scratch(1, tq),
        compiler_params=_cp("parallel", "arbitrary"),
        name="moba_attn",
    )(q32, k, v, km)


def _mla_kernel(q_ref, k_ref, v_ref, o_ref, m_ref, acc_ref, *, tk):
    _flash_init(m_ref, acc_ref)
    _causal_flash(q_ref[...], k_ref, v_ref, m_ref, acc_ref, pl.program_id(1), tk)
    o_ref[...] = _flash_result(acc_ref, 0, 1).astype(o_ref.dtype)


def _mla(qcat, kcat, v):
    s = qcat.shape[0]
    tq = min(ATT_TQ, s)
    tk = min(CAUSAL_TK, s)
    return pl.pallas_call(
        functools.partial(_mla_kernel, tk=tk),
        grid=(4, s // tq),
        in_specs=[
            pl.BlockSpec((tq, 256), lambda h, i: (i, h)),
            pl.BlockSpec((s, 256), lambda h, i: (0, h)),
            pl.BlockSpec((s, HEAD_DIM), lambda h, i: (0, h)),
        ],
        out_specs=pl.BlockSpec((tq, HEAD_DIM), lambda h, i: (i, h)),
        out_shape=jax.ShapeDtypeStruct((s, 512), BF16),
        scratch_shapes=_flash_scratch(1, tq),
        compiler_params=_cp("parallel", "arbitrary"),
        name="mla_attn",
    )(qcat, kcat, v)


def _gelu_tanh(x):
    return 0.5 * x * (1.0 + jnp.tanh(0.7978845608028654 * (x + 0.044715 * x * x * x)))


def _nsa_compress_kernel(t_ref, pe_ref, w1_ref, w2_ref, o_ref):
    t = t_ref[0]
    w1 = w1_ref[...]
    half = t.shape[1]
    n = t.shape[0]
    lo = _dot(t, w1[0:half])
    hi = _dot(t, w1[half:2 * half])
    pe = _dot(pe_ref[...].astype(BF16), w1)
    pre = lo + pltpu.roll(hi, n - 1, 0) + pe
    o_ref[0] = _dot(_gelu_tanh(pre).astype(BF16), w2_ref[...]).astype(o_ref.dtype)


def _nsa_compress(kv16, pe, w1, w2):
    n = kv16.shape[1]
    sub = lambda a: _wspec(a, (None,) + _wshape(a)[1:], lambda i: (i, 0, 0), resident=False)
    return pl.pallas_call(
        _nsa_compress_kernel,
        grid=(2,),
        in_specs=[pl.BlockSpec((1,) + kv16.shape[1:], lambda i: (i, 0, 0)), sub(pe), sub(w1), sub(w2)],
        out_specs=pl.BlockSpec((1, n, HEAD_DIM), lambda i: (i, 0, 0)),
        out_shape=jax.ShapeDtypeStruct((2, n, HEAD_DIM), BF16),
        compiler_params=_cp("parallel"),
        name="nsa_compress",
    )(kv16, pe[0], w1[0], w2[0])


def _nsa_cmp_kernel(q_ref, kvc_ref, m_ref, o_ref, sel_ref, *, n_sel):
    i = pl.program_id(0)
    tq = q_ref.shape[0]
    n = kvc_ref.shape[1]
    nblk = m_ref.shape[1]
    scale = HEAD_DIM ** -0.5
    kc, vc = kvc_ref[0], kvc_ref[1]
    t_pos = i * tq + lax.broadcasted_iota(I32, (tq, n), 0)
    cmp_end = lax.broadcasted_iota(I32, (tq, n), 1) * NSA_CMP_STRIDE + (2 * NSA_CMP_STRIDE - 1)
    ok = cmp_end <= t_pos
    p_sum = jnp.zeros((tq, n), F32)
    for hd in range(4):
        lo = hd * HEAD_DIM
        s = jnp.where(ok, _dot_nt(q_ref[:, lo:lo + HEAD_DIM], kc) * scale, NEG)
        m = jnp.max(s, axis=1, keepdims=True)
        e = jnp.where(ok, jnp.exp(s - m), 0.0)
        l = jnp.sum(e, axis=1, keepdims=True)
        p = e / jnp.where(l > 0, l, 1.0)
        p_sum = p_sum + p
        o_ref[:, lo:lo + HEAD_DIM] = _dot(p.astype(BF16), vc)

    imp = jnp.dot(p_sum, m_ref[...], precision=lax.Precision.HIGHEST, preferred_element_type=F32)
    imp_t = imp.T
    blk = lax.broadcasted_iota(I32, (nblk, tq), 0)
    cur = (i * tq + lax.broadcasted_iota(I32, (nblk, tq), 1)) // NSA_SEL_BLOCK
    forced = (blk == 0) | (blk == cur) | (blk == cur - 1)
    imp_t = jnp.where(blk > cur, -jnp.inf, jnp.where(forced, jnp.inf, imp_t))
    sel_ref[...] = (_top_k_mask(imp_t, blk, n_sel, axis=0).T - 1.0).astype(sel_ref.dtype)


def _nsa_cmp(q, kvc, imp_map):
    s = q.shape[0]
    n = kvc.shape[1]
    nblk = imp_map.shape[1]
    tq = min(256, s)
    return pl.pallas_call(
        functools.partial(_nsa_cmp_kernel, n_sel=min(NSA_SEL_TOPK, s // NSA_SEL_BLOCK)),
        grid=(s // tq,),
        in_specs=[pl.BlockSpec((tq, 512), lambda i: (i, 0)),
                  _resident((2, n, HEAD_DIM), lambda i: (0, 0, 0)),
                  _resident((n, nblk), lambda i: (0, 0))],
        out_specs=[pl.BlockSpec((tq, 512), lambda i: (i, 0)), pl.BlockSpec((tq, nblk), lambda i: (i, 0))],
        out_shape=[jax.ShapeDtypeStruct((s, 512), F32), jax.ShapeDtypeStruct((s, nblk), BF16)],
        compiler_params=_cp("parallel"),
        name="nsa_cmp_select",
    )(q, kvc, imp_map)


def _nsa_slc_win_kernel(q_ref, sel_ref, ks_ref, vs_ref, kw_ref, vw_ref, g_ref, oc_ref, o_ref,
                        m_ref, acc_ref, *, tk):
    i = pl.program_id(0)
    tq = q_ref.shape[0]
    _flash_init(m_ref, acc_ref)

    def tile(off, width, keep):
        k = ks_ref[pl.ds(off, width), 0:HEAD_DIM]
        v1 = _with_ones(vs_ref[pl.ds(off, width), :])
        bias = _dot_nt(sel_ref[...], ks_ref[pl.ds(off, width), HEAD_DIM:2 * HEAD_DIM])
        _flash_heads([q_ref[:, hd * HEAD_DIM:(hd + 1) * HEAD_DIM] for hd in range(4)], [k] * 4, [v1] * 4,
                     bias, keep, m_ref, acc_ref)

    n_past = (i * tq) // tk
    n_rest = (i * tq - n_past * tk) // tq

    def body(j, carry):
        tile(pl.multiple_of(j * tk, tk), tk, None)
        return carry

    lax.fori_loop(0, n_past, body, 0)
    for r in range(tk // tq):
        @pl.when(n_rest == r)
        def _():
            width = (r + 1) * tq
            row = lax.broadcasted_iota(I32, (tq, width), 0) + r * tq
            col = lax.broadcasted_iota(I32, (tq, width), 1)
            tile(pl.multiple_of(n_past * tk, tk), width, col <= row)

    wlen = NSA_WINDOW + tq
    start = pl.multiple_of(jnp.maximum(i * tq - NSA_WINDOW, 0), 256)
    kw = kw_ref[pl.ds(start, wlen), :]
    vw1 = _with_ones(vw_ref[pl.ds(start, wlen), :])
    diff = (i * tq + lax.broadcasted_iota(I32, (tq, wlen), 0)) - (start + lax.broadcasted_iota(I32, (tq, wlen), 1))
    in_win = (diff >= 0) & (diff < NSA_WINDOW)
    g = g_ref[...]
    for hd in range(4):
        lo = hd * HEAD_DIM
        s = jnp.where(in_win, _dot_nt(q_ref[:, lo:lo + HEAD_DIM], kw), NEG)
        p = jnp.exp2(s - jnp.max(s, axis=1, keepdims=True))
        pv = _dot(p.astype(BF16), vw1)
        o_win = pv[:, 0:HEAD_DIM] / pv[:, HEAD_DIM:2 * HEAD_DIM]
        o_slc = _flash_result(acc_ref, hd, 4)
        out = (g[:, 3 * hd:3 * hd + 1] * oc_ref[:, lo:lo + HEAD_DIM] + g[:, 3 * hd + 1:3 * hd + 2] * o_slc
               + g[:, 3 * hd + 2:3 * hd + 3] * o_win)
        o_ref[:, lo:lo + HEAD_DIM] = out.astype(o_ref.dtype)


def _nsa_slc_win(q_r, sel, kslc, vslc, kwin, vwin, gates, o_cmp):
    s = q_r.shape[0]
    tq = min(ATT_TQ, s)
    tk = min(ATT_TK, s)
    full = lambda a: _resident(a.shape, lambda i: (0,) * a.ndim)
    return pl.pallas_call(
        functools.partial(_nsa_slc_win_kernel, tk=tk),
        grid=(s // tq,),
        in_specs=[pl.BlockSpec((tq, 512), lambda i: (i, 0)), pl.BlockSpec((tq, 128), lambda i: (i, 0)),
                  full(kslc), full(vslc), full(kwin), full(vwin),
                  pl.BlockSpec((tq, 128), lambda i: (i, 0)), pl.BlockSpec((tq, 512), lambda i: (i, 0))],
        out_specs=pl.BlockSpec((tq, 512), lambda i: (i, 0)),
        out_shape=jax.ShapeDtypeStruct((s, 512), BF16),
        scratch_shapes=_flash_scratch(4, tq),
        compiler_params=_cp("parallel"),
        name="nsa_slc_win",
    )(q_r, sel, kslc, vslc, kwin, vwin, gates, o_cmp)


def _dsa_kernel(q_ref, iq_ref, iw_ref, k_ref, v_ref, ik_ref, o_ref, key_ref, wb_ref, m_ref, acc_ref, *, tk):
    i = pl.program_id(0)
    tq = q_ref.shape[0]
    half = tk // 2
    n_tiles = (i * tq + tq + tk - 1) // tk
    row_h = lax.broadcasted_iota(I32, (tq, half), 0) + i * tq
    col_h = lax.broadcasted_iota(I32, (tq, half), 1)

    w = iw_ref[...]
    for hd in range(DSA_IDX_HEADS):
        wb_ref[hd] = jnp.broadcast_to(w[:, hd:hd + 1], (tq, 128))

    def score_body(j, carry):
        for hf in range(2):
            off = pl.multiple_of(j * tk + hf * half, half)
            ik_even = ik_ref[pl.ds(off, half), 0:128]
            ik_odd = ik_ref[pl.ds(off, half), 128:256]
            sc = jnp.zeros((tq, half), F32)
            for p in range(DSA_IDX_HEADS // 2):
                x = iq_ref[:, p * 128:(p + 1) * 128]
                we = jnp.concatenate([wb_ref[2 * p]] * (half // 128), axis=1)
                wo = jnp.concatenate([wb_ref[2 * p + 1]] * (half // 128), axis=1)
                sc = sc + jnp.maximum(_dot_nt(x, ik_even), 0.0) * we
                sc = sc + jnp.maximum(_dot_nt(x, ik_odd), 0.0) * wo
            bits = pltpu.bitcast(sc, I32)
            key = jnp.where(bits >= 0, bits, bits ^ 0x7FFFFFFF)
            key_ref[j, :, hf * half:(hf + 1) * half] = jnp.where(col_h + off <= row_h, key, INT_MIN)
        return carry

    lax.fori_loop(0, n_tiles, score_body, 0)

    rb = min(DSA_COUNT_ROWS, tq)

    ones = jnp.ones((128, 128), BF16)
    k_f = float(DSA_TOPK)

    def count_ge(cand128):
        parts = []
        for r0 in range(0, tq, rb):
            cand_b = cand128[r0:r0 + rb]

            def body(j, acc):
                for c in range(tk // 128):
                    acc = acc + jnp.where(key_ref[j, r0:r0 + rb, c * 128:(c + 1) * 128] >= cand_b, 1, 0)
                return acc

            parts.append(lax.fori_loop(0, n_tiles, body, jnp.zeros((rb, 128), I32)))
        per_lane = jnp.concatenate(parts, axis=0).astype(F32).astype(BF16)
        return _dot(per_lane, ones)

    zero = jnp.zeros((tq, 128), I32)
    c0 = count_ge(zero)
    lo0 = jnp.where(c0 >= k_f, zero, INT_MIN)
    c_lo0 = jnp.where(c0 >= k_f, c0, -1.0)

    def bit_cond(state):
        b, _, _, more = state
        return (b < 31) & (more > 0.0)

    def bit_body(state):
        b, lo, c_lo, _ = state
        more = jnp.max(jnp.where(c_lo != k_f, 1.0, 0.0))
        cand = lo | lax.shift_left(jnp.int32(1), 30 - b)
        c = count_ge(cand)
        take = c >= k_f
        return b + 1, jnp.where(take, cand, lo), jnp.where(take, c, c_lo), more

    passes, lo, c_lo, _ = lax.while_loop(bit_cond, bit_body, (jnp.int32(0), lo0, c_lo0, jnp.float32(1.0)))
    thr128 = jnp.maximum(lo, INT_MIN + 1)

    def retire_ties():
        need = k_f - count_ge(lo + 1)
        lane = lax.broadcasted_iota(I32, (rb, 128), 1)

        def tied_pos(j, r0, c):
            tied = key_ref[j, r0:r0 + rb, c * 128:(c + 1) * 128] == lo[r0:r0 + rb]
            return jnp.where(tied, j * tk + c * 128 + lane, -1)

        def count_tied_before(limit):
            parts = []
            for r0 in range(0, tq, rb):
                lim_b = limit[r0:r0 + rb]

                def body(j, acc, r0=r0, lim_b=lim_b):
                    for c in range(tk // 128):
                        pos = tied_pos(j, r0, c)
                        acc = acc + jnp.where(jnp.where(pos >= 0, pos, lim_b) < lim_b, 1, 0)
                    return acc

                parts.append(lax.fori_loop(0, n_tiles, body, jnp.zeros((rb, 128), I32)))
            return _dot(jnp.concatenate(parts, axis=0).astype(F32).astype(BF16), ones)

        n_bits = (key_ref.shape[0] * tk - 1).bit_length()

        def pos_body(b, last):
            cand = last | lax.shift_left(jnp.int32(1), n_bits - 1 - b)
            return jnp.where(count_tied_before(cand) < need, cand, last)

        last = lax.fori_loop(0, n_bits, pos_body, jnp.zeros((tq, 128), I32))
        last = jnp.where(c_lo > k_f, last, 2 ** 31 - 1)

        for r0 in range(0, tq, rb):
            def drop_body(j, carry, r0=r0):
                for c in range(tk // 128):
                    cols = slice(c * 128, (c + 1) * 128)
                    beyond = tied_pos(j, r0, c) > last[r0:r0 + rb]
                    key_ref[j, r0:r0 + rb, cols] = jnp.where(beyond, INT_MIN, key_ref[j, r0:r0 + rb, cols])
                return carry

            lax.fori_loop(0, n_tiles, drop_body, 0)

    @pl.when(passes >= 31)
    def _():
        @pl.when(jnp.max(jnp.where(c_lo > k_f, 1.0, 0.0)) > 0.0)
        def _():
            retire_ties()

    _flash_init(m_ref, acc_ref)

    def attn_body(j, carry):
        off = pl.multiple_of(j * tk, tk)
        keep = key_ref[j] >= jnp.concatenate([thr128] * (tk // 128), axis=1)
        cols = [slice(hd * HEAD_DIM, (hd + 1) * HEAD_DIM) for hd in range(4)]
        _flash_heads([q_ref[:, c] for c in cols], [k_ref[pl.ds(off, tk), c] for c in cols],
                     [_with_ones(v_ref[pl.ds(off, tk), c]) for c in cols], None, keep, m_ref, acc_ref)
        return carry

    lax.fori_loop(0, n_tiles, attn_body, 0)
    for hd in range(4):
        o_ref[:, hd * HEAD_DIM:(hd + 1) * HEAD_DIM] = _flash_result(acc_ref, hd, 4).astype(o_ref.dtype)


def _dsa(q, k, v, iq, ik2, iw):
    s = q.shape[0]
    tq = min(DSA_TQ, s)
    tk = min(ATT_TK, s)
    full = lambda a: _resident(a.shape, lambda i: (0, 0))
    return pl.pallas_call(
        functools.partial(_dsa_kernel, tk=tk),
        grid=(s // tq,),
        in_specs=[pl.BlockSpec((tq, 512), lambda i: (i, 0)), pl.BlockSpec((tq, 1024), lambda i: (i, 0)),
                  pl.BlockSpec((tq, 128), lambda i: (i, 0)), full(k), full(v), full(ik2)],
        out_specs=pl.BlockSpec((tq, 512), lambda i: (i, 0)),
        out_shape=jax.ShapeDtypeStruct((s, 512), BF16),
        scratch_shapes=[pltpu.VMEM((s // tk, tq, tk), I32), pltpu.VMEM((DSA_IDX_HEADS, tq, 128), F32)]
                       + _flash_scratch(4, tq),
        compiler_params=_cp("parallel"),
        name="dsa_attn",
    )(q, iq, iw, k, v, ik2)


def _mix_mem_kernel(oa_ref, ob_ref, oc_ref, od_ref, w_ref, x_ref, g1_ref, b1_ref, wq_ref, kv_ref, wo_ref,
                    g2_ref, b2_ref, o_ref):
    mix = _dot(oa_ref[...], w_ref[0:512, :])
    mix = mix + _dot(ob_ref[...], w_ref[512:1024, :])
    mix = mix + _dot(oc_ref[...], w_ref[1024:1536, :])
    mix = mix + _dot(od_ref[...], w_ref[1536:2048, :])
    x = _layer_norm(DN_ALPHA * x_ref[...] + mix, g1_ref[...], b1_ref[...])

    q = _dot(x.astype(BF16), wq_ref[...])
    heads = []
    for hd in range(4):
        lo = hd * HEAD_DIM
        qh = (q[:, lo:lo + HEAD_DIM] * _log2_scale(HEAD_DIM)).astype(BF16)
        s = _dot_nt(qh, kv_ref[:, lo:lo + HEAD_DIM])
        p = jnp.exp2(s - jnp.max(s, axis=1, keepdims=True))
        o = _dot(p.astype(BF16), kv_ref[:, 512 + lo:512 + lo + HEAD_DIM]) / jnp.sum(p, axis=1, keepdims=True)
        heads.append(o.astype(BF16))
    out = _dot(jnp.concatenate(heads, axis=1), wo_ref[...])
    o_ref[...] = _layer_norm(DN_ALPHA * x + out, g2_ref[...], b2_ref[...])


def _mix_mem_ln(o_a, o_b, o_c, o_d, w_out, x, g1, b1, wq, kv, wo, g2, b2):
    s, d = x.shape
    tm = min(ROW_TILE, s)
    r512 = pl.BlockSpec((tm, 512), lambda i: (i, 0))
    full = lambda a: _wspec(a, _wshape(a), lambda i: (0, 0))
    return pl.pallas_call(
        _mix_mem_kernel,
        grid=(s // tm,),
        in_specs=[r512, r512, r512, r512, full(w_out), pl.BlockSpec((tm, d), lambda i: (i, 0)), full(g1), full(b1),
                  full(wq), _resident(kv.shape, lambda i: (0, 0)), full(wo), full(g2), full(b2)],
        out_specs=pl.BlockSpec((tm, d), lambda i: (i, 0)),
        out_shape=jax.ShapeDtypeStruct((s, d), F32),
        compiler_params=_cp("parallel"),
        name="mix_mem_ln",
    )(o_a, o_b, o_c, o_d, w_out[0], x, g1[0], b1[0], wq[0], kv, wo[0], g2[0], b2[0])


def _mem_kv_kernel(mem_ref, w_ref, o_ref):
    o_ref[...] = _dot(mem_ref[...].astype(BF16), w_ref[...]).astype(o_ref.dtype)


def _mem_kv(mem, wkv):
    m, d = mem.shape
    n = _wshape(wkv)[1]
    return pl.pallas_call(
        _mem_kv_kernel,
        grid=(1,),
        in_specs=[pl.BlockSpec((m, d), lambda i: (0, 0)), _wspec(wkv, (d, n), lambda i: (0, 0))],
        out_specs=pl.BlockSpec((m, n), lambda i: (0, 0)),
        out_shape=jax.ShapeDtypeStruct((m, n), BF16),
        compiler_params=_cp("arbitrary"),
        name="mem_kv",
    )(mem, wkv[0])


def _pad_last(w, n):
    return jnp.pad(w, [(0, 0)] * (w.ndim - 1) + [(0, n - w.shape[-1])])


def _split_w_in(w_in):
    a = w_in[..., 0:1536]
    b = _pad_last(w_in[..., 1536:2624], 1152)
    c = _pad_last(w_in[..., 2624:3916], 1408)
    d = jnp.concatenate([w_in[..., 3916:6476], _pad_last(w_in[..., 6476:6540], 128),
                         _pad_last(w_in[..., 6540:6556], 128)], axis=-1)
    return tuple(t.astype(BF16) for t in (a, b, c, d))


def _pad_w_uq(w_uq):
    lead = w_uq.shape[:-1]
    w = _pad_last(w_uq.reshape(lead + (4, MLA_NOPE + MLA_ROPE)), 256)
    return w.reshape(lead + (4 * 256,)).astype(BF16)


def _nsa_importance_map(s):
    n = s // NSA_CMP_STRIDE
    nblk = s // NSA_SEL_BLOCK
    ni = np.arange(n)[:, None]
    bi = np.arange(nblk)[None, :]
    m = ((ni >= 4 * bi - 1) & (ni <= 4 * bi + 3)).astype(np.float32)
    return jnp.asarray(np.pad(m, ((0, 0), (0, 128 - nblk))))


def _mixer_heads(x, tabs, imp_map, w_groups, g_cq, g_ckv, w_uq, w_ukv, cmp_pe, cmp_w1, cmp_w2):
    cos128, sin128, cos64, sup64, sdn64 = tabs
    s = x.shape[0]
    w_a, w_b, w_c, w_d = w_groups

    aq, ak, av, akm = _proj_a(x, w_a, cos128, sin128)
    o_a = _moba(aq, ak, av, akm)

    bq, bk, bv = _proj_b(x, w_b, g_cq, g_ckv, w_uq, w_ukv, cos64, sup64, sdn64)
    o_b = _mla(bq, bk, bv)

    cq, cqr, kvcmp, kslc, vslc, kwin, vwin, gates = _proj_c(x, w_c, cos128, sin128)
    kv16 = kvcmp.reshape(2, s // NSA_CMP_STRIDE, NSA_CMP_STRIDE * HEAD_DIM)
    kvc = _nsa_compress(kv16, cmp_pe, cmp_w1, cmp_w2)
    o_cmp, sel = _nsa_cmp(cq, kvc, imp_map)
    o_c = _nsa_slc_win(cqr, sel, kslc, vslc, kwin, vwin, gates, o_cmp)

    dq, dk, dv, diq, dik, diw = _proj_d(x, w_d, cos128, sin128, cos64, sup64, sdn64)
    o_d = _dsa(dq, dk, dv, diq, dik, diw)
    return o_a, o_b, o_c, o_d


def kernel(x, mem, positions, ln_g, ln_b, ffn_w_gu, ffn_w_down, w_in, w_out, mla_g_cq, mla_g_ckv, mla_w_uq,
           mla_w_ukv, nsa_cmp_pe, nsa_cmp_w1, nsa_cmp_w2, mem_wq, mem_wkv, mem_wo):
    batch, s, d = x.shape
    w_gu_b, w_dn_b = ffn_w_gu, ffn_w_down
    w_groups = _split_w_in(w_in)
    w_out_b = w_out.astype(BF16)
    w_uq_b, w_ukv_b = _pad_w_uq(mla_w_uq), mla_w_ukv.astype(BF16)
    g_cq, g_ckv = mla_g_cq[:, None, :], mla_g_ckv[:, None, :]
    pe = nsa_cmp_pe.reshape(DEPTH, 2, 1, -1)
    w1_b, w2_b = nsa_cmp_w1.astype(BF16), nsa_cmp_w2.astype(BF16)
    wq_b, wkv_b, wo_b = mem_wq.astype(BF16), mem_wkv.astype(BF16), mem_wo.astype(BF16)
    g4, b4 = ln_g[:, :, None, :], ln_b[:, :, None, :]
    imp_map = _nsa_importance_map(s)

    outs = []
    for bi in range(batch):
        xb = x.reshape(s, d) if batch == 1 else x[bi]
        tabs = _rope_tables(positions[bi])
        for l in range(DEPTH):
            at = lambda a, *lead: (a, (l,) + lead)
            xb = _ffn_ln(xb, at(w_gu_b, 0), at(w_dn_b, 0), at(g4, 0), at(b4, 0))
            o_heads = _mixer_heads(xb, tabs, imp_map, tuple(at(w) for w in w_groups), at(g_cq), at(g_ckv),
                                   at(w_uq_b), at(w_ukv_b), at(pe), at(w1_b), at(w2_b))
            kv = _mem_kv(mem[bi], at(wkv_b))
            xb = _mix_mem_ln(*o_heads, at(w_out_b), xb, at(g4, 1), at(b4, 1), at(wq_b), kv, at(wo_b),
                             at(g4, 2), at(b4, 2))
            xb = _ffn_ln(xb, at(w_gu_b, 1), at(w_dn_b, 1), at(g4, 3), at(b4, 3))
        outs.append(xb)
    return outs[0].reshape(1, s, d) if batch == 1 else jnp.stack(outs)
```

```python
import functools

import numpy as np
import jax
import jax.numpy as jnp
from jax import lax
from jax.experimental import pallas as pl
from jax.experimental.pallas import tpu as pltpu

F32 = jnp.float32
BF16 = jnp.bfloat16
I32 = jnp.int32

D_MODEL = 2048
DEPTH = 4
HEAD_DIM = 128
ROPE_THETA = 10000.0
LN_EPS = 1e-5
RMS_EPS = 1e-6

MOBA_BLOCK = 256
MOBA_TOPK = 3
MLA_NOPE = 128
MLA_ROPE = 64
NSA_CMP_STRIDE = 16
NSA_SEL_BLOCK = 64
NSA_SEL_TOPK = 16
NSA_WINDOW = 512
DSA_IDX_HEADS = 16
DSA_TOPK = 256
D_FF = 5632
DN_ALPHA = (2 * DEPTH) ** 0.25

NEG = -(2.0 ** 100)
LOG2_E = 1.4426950408889634
INT_MIN = -(2 ** 31)
VMEM_LIMIT = 56 * 1024 * 1024

ROW_TILE = 512
FFN_ROWS = 1024
FFN_COLS = 256
ATT_TQ = 512
ATT_TK = 1024
NSA_TK = 2048
CAUSAL_TK = 4096
DSA_TQ = 256
DSA_COUNT_ROWS = 128


def _cp(*sem):
    return pltpu.CompilerParams(dimension_semantics=sem, vmem_limit_bytes=VMEM_LIMIT)


def _wspec(w, block, index_map, resident=True):
    _, lead = w
    shape = (None,) * len(lead) + tuple(block)
    imap = lambda *g: tuple(lead) + tuple(index_map(*g))
    if resident:
        return pl.BlockSpec(shape, imap, pipeline_mode=pl.Buffered(1))
    return pl.BlockSpec(shape, imap)


def _wshape(w):
    arr, lead = w
    return arr.shape[len(lead):]


def _resident(shape, index_map):
    return pl.BlockSpec(shape, index_map, pipeline_mode=pl.Buffered(1))


def _dot(a, b):
    return jnp.dot(a, b, preferred_element_type=F32)


def _dot_nt(a, b):
    return lax.dot_general(a, b, (((1,), (1,)), ((), ())), preferred_element_type=F32)


def _layer_norm(y, g, b):
    mu = jnp.mean(y, axis=-1, keepdims=True)
    d = y - mu
    var = jnp.mean(d * d, axis=-1, keepdims=True)
    return d * lax.rsqrt(var + LN_EPS) * g + b


def _rope128(x, cos, sin_signed):
    return x * cos + pltpu.roll(x, 64, 1) * sin_signed


def _rope64(x, cos, s_up, s_dn):
    return x * cos + pltpu.roll(x, 32, 1) * s_up + pltpu.roll(x, 96, 1) * s_dn


def _log2_scale(d):
    return float(d) ** -0.5 * LOG2_E


def _with_ones(v):
    return jnp.concatenate([v, jnp.ones_like(v)], axis=1)


def _flash_update(s, v1, bias, keep, m_ref, acc_ref, slot):
    if bias is not None:
        s = s + bias
    if keep is not None:
        s = jnp.where(keep, s, NEG)
    m_old = m_ref[slot]
    m_new = jnp.maximum(m_old, jnp.max(s, axis=1, keepdims=True))
    alpha = jnp.exp2(m_old - m_new)
    p = jnp.exp2(s - m_new)
    acc_ref[slot] = alpha * acc_ref[slot] + _dot(p.astype(BF16), v1)
    m_ref[slot] = m_new


FLASH_ROWS = 256


def _flash_rows(q, k, v1, bias, keep, m_ref, acc_ref, hd):
    n = max(1, q.shape[0] // FLASH_ROWS)
    r = q.shape[0] // n
    rows = [slice(sp * r, (sp + 1) * r) for sp in range(n)]
    logits = [_dot_nt(q[rw], k) for rw in rows]
    for sp, rw in enumerate(rows):
        _flash_update(logits[sp], v1, None if bias is None else bias[rw], None if keep is None else keep[rw],
                      m_ref, acc_ref, hd * n + sp)


def _flash_heads(qs, ks, v1s, bias, keep, m_ref, acc_ref):
    def logits(hd):
        q = qs[hd]
        n = max(1, q.shape[0] // FLASH_ROWS)
        r = q.shape[0] // n
        rows = [slice(sp * r, (sp + 1) * r) for sp in range(n)]
        return rows, [_dot_nt(q[rw], ks[hd]) for rw in rows]

    ahead = logits(0)
    for hd in range(len(qs)):
        rows, s = ahead
        if hd + 1 < len(qs):
            ahead = logits(hd + 1)
        for sp, rw in enumerate(rows):
            _flash_update(s[sp], v1s[hd], None if bias is None else bias[rw], None if keep is None else keep[rw],
                          m_ref, acc_ref, hd * len(rows) + sp)


def _causal_flash(q, k_ref, v_ref, m_ref, acc_ref, i, tk):
    tq = q.shape[0]
    n_past = (i * tq) // tk
    n_rest = (i * tq - n_past * tk) // tq

    def tile(off, width, keep):
        _flash_rows(q, k_ref[pl.ds(off, width), :], _with_ones(v_ref[pl.ds(off, width), :]), None, keep,
                    m_ref, acc_ref, 0)

    def body(j, carry):
        tile(pl.multiple_of(j * tk, tk), tk, None)
        return carry

    lax.fori_loop(0, n_past, body, 0)
    for r in range(tk // tq):
        @pl.when(n_rest == r)
        def _():
            width = (r + 1) * tq
            row = lax.broadcasted_iota(I32, (tq, width), 0) + r * tq
            col = lax.broadcasted_iota(I32, (tq, width), 1)
            tile(pl.multiple_of(n_past * tk, tk), width, col <= row)


def _flash_result(acc_ref, hd, heads):
    n = acc_ref.shape[0] // heads
    parts = []
    for sp in range(n):
        a = acc_ref[hd * n + sp]
        parts.append(a[:, 0:HEAD_DIM] / a[:, HEAD_DIM:2 * HEAD_DIM])
    return jnp.concatenate(parts, axis=0) if n > 1 else parts[0]


def _flash_scratch(heads, tq):
    n = max(1, tq // FLASH_ROWS)
    return [pltpu.VMEM((heads * n, tq // n, 1), F32), pltpu.VMEM((heads * n, tq // n, 2 * HEAD_DIM), F32)]


def _flash_init(m_ref, acc_ref):
    m_ref[...] = jnp.full(m_ref.shape, NEG, F32)
    acc_ref[...] = jnp.zeros(acc_ref.shape, F32)


def _top_k_mask(work, index, k, axis=1):
    n = work.shape[axis]
    sel = jnp.zeros(work.shape, F32)
    for _ in range(k):
        m = jnp.max(work, axis=axis, keepdims=True)
        idx = jnp.min(jnp.where(work == m, index, n), axis=axis, keepdims=True)
        pick = index == jnp.where(m > -jnp.inf, idx, n)
        sel = jnp.where(pick, 1.0, sel)
        work = jnp.where(pick, -jnp.inf, work)
    return sel


def _key_block_table(s, block):
    return jnp.asarray((np.arange(128)[None, :] == (np.arange(s) // block)[:, None]) * -NEG, dtype=BF16)


def _rope_table_kernel(pos_ref, c_ref, cos128_ref, sin128_ref, cos64_ref, sup64_ref, sdn64_ref):
    pos = pos_ref[...]
    a128 = pos * c_ref[0:1, :]
    cos128_ref[...] = jnp.cos(a128)
    sin128_ref[...] = jnp.sin(a128) * c_ref[1:2, :]
    a64 = pos * c_ref[2:3, :]
    s64 = jnp.sin(a64)
    cos64_ref[...] = jnp.cos(a64)
    sup64_ref[...] = s64 * c_ref[3:4, :]
    sdn64_ref[...] = s64 * c_ref[4:5, :]


def _rope_consts():
    lane = np.arange(128)
    inv128 = (ROPE_THETA ** (-np.arange(0, 128, 2, dtype=np.float32) / 128)).astype(np.float32)
    inv64 = (ROPE_THETA ** (-np.arange(0, 64, 2, dtype=np.float32) / 64)).astype(np.float32)
    c = np.zeros((8, 128), np.float32)
    c[0] = inv128[lane % 64]
    c[1] = np.where(lane < 64, -1.0, 1.0)
    c[2] = inv64[(lane % 64) % 32]
    c[3] = np.where(lane % 64 >= 32, 1.0, 0.0)
    c[4] = np.where(lane % 64 < 32, -1.0, 0.0)
    return jnp.asarray(c)


def _rope_tables(positions):
    s = positions.shape[0]
    tm = min(ROW_TILE, s)
    pos = positions.astype(F32).reshape(s, 1)
    tab = jax.ShapeDtypeStruct((s, 128), F32)
    row = pl.BlockSpec((tm, 128), lambda i: (i, 0))
    return pl.pallas_call(
        _rope_table_kernel,
        grid=(s // tm,),
        in_specs=[pl.BlockSpec((tm, 1), lambda i: (i, 0)), pl.BlockSpec((8, 128), lambda i: (0, 0))],
        out_specs=[row] * 5,
        out_shape=[tab] * 5,
        compiler_params=_cp("parallel"),
        name="rope_tables",
    )(pos, _rope_consts())


def _ffn_kernel(x_ref, wg_ref, wu_ref, wd_ref, g_ref, b_ref, o_ref, xb_ref):
    j = pl.program_id(1)

    @pl.when(j == 0)
    def _():
        xb_ref[...] = x_ref[...].astype(BF16)
        o_ref[...] = jnp.zeros_like(o_ref)

    xb = xb_ref[...]
    g = _dot(xb, wg_ref[...].astype(BF16))
    u = _dot(xb, wu_ref[...].astype(BF16))
    h = (g * jax.nn.sigmoid(g) * u).astype(BF16)
    o_ref[...] += _dot(h, wd_ref[...].astype(BF16))

    @pl.when(j == pl.num_programs(1) - 1)
    def _():
        y = DN_ALPHA * x_ref[...] + 0.5 * o_ref[...]
        o_ref[...] = _layer_norm(y, g_ref[...], b_ref[...])


def _ffn_ln(x, w_gu, w_down, g, b):
    s, d = x.shape
    f = _wshape(w_down)[0]
    tm = min(FFN_ROWS, s)
    tf = FFN_COLS
    nf = f // tf
    return pl.pallas_call(
        _ffn_kernel,
        grid=(s // tm, nf),
        in_specs=[
            pl.BlockSpec((tm, d), lambda i, j: (i, 0), pipeline_mode=pl.Buffered(1)),
            _wspec(w_gu, (d, tf), lambda i, j: (0, j), resident=False),
            _wspec(w_gu, (d, tf), lambda i, j: (0, j + nf), resident=False),
            _wspec(w_down, (tf, d), lambda i, j: (j, 0), resident=False),
            _wspec(g, (1, d), lambda i, j: (0, 0)),
            _wspec(b, (1, d), lambda i, j: (0, 0)),
        ],
        out_specs=pl.BlockSpec((tm, d), lambda i, j: (i, 0)),
        out_shape=jax.ShapeDtypeStruct((s, d), F32),
        scratch_shapes=[pltpu.VMEM((tm, d), BF16)],
        compiler_params=_cp("parallel", "arbitrary"),
        name="ffn_ln",
    )(x, w_gu[0], w_gu[0], w_down[0], g[0], b[0])


def _proj_a_kernel(x_ref, w_ref, cos_ref, sin_ref, kb_ref, q_ref, k_ref, v_ref, km_ref):
    h = _dot(x_ref[...].astype(BF16), w_ref[...])
    cos, sin = cos_ref[...], sin_ref[...]
    nblk = km_ref.shape[0]
    for hd in range(4):
        lo = hd * HEAD_DIM
        q_ref[:, lo:lo + HEAD_DIM] = _rope128(h[:, lo:lo + HEAD_DIM], cos, sin)
        k = _rope128(h[:, 512 + lo:512 + lo + HEAD_DIM], cos, sin)
        k_ref[:, 2 * lo:2 * lo + HEAD_DIM] = k.astype(BF16)
        k_ref[:, 2 * lo + HEAD_DIM:2 * lo + 2 * HEAD_DIM] = kb_ref[...]
        for bi in range(nblk):
            kb = k[bi * MOBA_BLOCK:(bi + 1) * MOBA_BLOCK]
            km_ref[bi, :, lo:lo + HEAD_DIM] = jnp.mean(kb, axis=0, keepdims=True)
    v_ref[...] = h[:, 1024:1536].astype(BF16)


def _proj_a(x, w, cos128, sin128):
    s, d = x.shape
    tm = min(ROW_TILE, s)
    nblk = tm // MOBA_BLOCK
    row512 = pl.BlockSpec((tm, 512), lambda i: (i, 0))
    tab = pl.BlockSpec((tm, 128), lambda i: (i, 0))
    return pl.pallas_call(
        _proj_a_kernel,
        grid=(s // tm,),
        in_specs=[pl.BlockSpec((tm, d), lambda i: (i, 0)), _wspec(w, _wshape(w), lambda i: (0, 0)), tab, tab, tab],
        out_specs=[row512, pl.BlockSpec((tm, 1024), lambda i: (i, 0)), row512,
                   pl.BlockSpec((nblk, 1, 512), lambda i: (i, 0, 0))],
        out_shape=[
            jax.ShapeDtypeStruct((s, 512), F32),
            jax.ShapeDtypeStruct((s, 1024), BF16),
            jax.ShapeDtypeStruct((s, 512), BF16),
            jax.ShapeDtypeStruct((s // MOBA_BLOCK, 1, 512), F32),
        ],
        compiler_params=_cp("parallel"),
        name="proj_moba",
    )(x, w[0], cos128, sin128, _key_block_table(s, MOBA_BLOCK))


def _rms_norm(x, g):
    return x * lax.rsqrt(jnp.mean(x * x, axis=-1, keepdims=True) + RMS_EPS) * g


def _proj_b_kernel(x_ref, w_ref, gq_ref, gkv_ref, wuq_ref, wukv_ref, cos_ref, sup_ref, sdn_ref,
                   q_ref, k_ref, v_ref):
    h = _dot(x_ref[...].astype(BF16), w_ref[...])
    cos, sup, sdn = cos_ref[...], sup_ref[...], sdn_ref[...]
    cq = _rms_norm(h[:, 0:512], gq_ref[...]).astype(BF16)
    ckv = _rms_norm(h[:, 512:1024], gkv_ref[...]).astype(BF16)
    kr = _rope64(h[:, 1024:1152], cos, sup, sdn).astype(BF16)
    q = _dot(cq, wuq_ref[...])
    kv = _dot(ckv, wukv_ref[...])
    qs = _log2_scale(MLA_NOPE + MLA_ROPE)
    for hd in range(4):
        lo = hd * 256
        q_ref[:, lo:lo + 128] = (q[:, lo:lo + 128] * qs).astype(BF16)
        q_ref[:, lo + 128:lo + 256] = (_rope64(q[:, lo + 128:lo + 256], cos, sup, sdn) * qs).astype(BF16)
        k_ref[:, lo:lo + 128] = kv[:, lo:lo + 128].astype(BF16)
        k_ref[:, lo + 128:lo + 256] = kr
        v_ref[:, hd * 128:(hd + 1) * 128] = kv[:, lo + 128:lo + 256].astype(BF16)


def _proj_b(x, w, g_cq, g_ckv, w_uq, w_ukv, cos64, sup64, sdn64):
    s, d = x.shape
    tm = min(ROW_TILE, s)
    tab = pl.BlockSpec((tm, 128), lambda i: (i, 0))
    full = lambda a: _wspec(a, _wshape(a), lambda i: (0, 0))
    return pl.pallas_call(
        _proj_b_kernel,
        grid=(s // tm,),
        in_specs=[pl.BlockSpec((tm, d), lambda i: (i, 0)), full(w), full(g_cq), full(g_ckv), full(w_uq),
                  full(w_ukv), tab, tab, tab],
        out_specs=[pl.BlockSpec((tm, 1024), lambda i: (i, 0)), pl.BlockSpec((tm, 1024), lambda i: (i, 0)),
                   pl.BlockSpec((tm, 512), lambda i: (i, 0))],
        out_shape=[jax.ShapeDtypeStruct((s, 1024), BF16), jax.ShapeDtypeStruct((s, 1024), BF16),
                   jax.ShapeDtypeStruct((s, 512), BF16)],
        compiler_params=_cp("parallel"),
        name="proj_mla",
    )(x, w[0], g_cq[0], g_ckv[0], w_uq[0], w_ukv[0], cos64, sup64, sdn64)


def _proj_c_kernel(x_ref, w_ref, cos_ref, sin_ref, kb_ref, q_ref, qr_ref, kvcmp_ref, kslc_ref, vslc_ref,
                   kwin_ref, vwin_ref, gate_ref):
    h = _dot(x_ref[...].astype(BF16), w_ref[...])
    cos, sin = cos_ref[...], sin_ref[...]
    for hd in range(4):
        lo = hd * HEAD_DIM
        qh = h[:, lo:lo + HEAD_DIM]
        q_ref[:, lo:lo + HEAD_DIM] = qh.astype(BF16)
        qr_ref[:, lo:lo + HEAD_DIM] = (_rope128(qh, cos, sin) * _log2_scale(HEAD_DIM)).astype(BF16)
    kv = lambda i: h[:, 512 + i * 128:512 + (i + 1) * 128]
    kvcmp_ref[0] = kv(0).astype(BF16)
    kvcmp_ref[1] = kv(1).astype(BF16)
    kslc_ref[:, 0:HEAD_DIM] = _rope128(kv(2), cos, sin).astype(BF16)
    kslc_ref[:, HEAD_DIM:2 * HEAD_DIM] = kb_ref[...]
    vslc_ref[...] = kv(3).astype(BF16)
    kwin_ref[...] = _rope128(kv(4), cos, sin).astype(BF16)
    vwin_ref[...] = kv(5).astype(BF16)
    gate_ref[...] = jax.nn.sigmoid(h[:, 1280:1408])


def _proj_c(x, w, cos128, sin128):
    s, d = x.shape
    tm = min(ROW_TILE, s)
    r512 = pl.BlockSpec((tm, 512), lambda i: (i, 0))
    r128 = pl.BlockSpec((tm, 128), lambda i: (i, 0))
    b512 = jax.ShapeDtypeStruct((s, 512), BF16)
    b128 = jax.ShapeDtypeStruct((s, 128), BF16)
    return pl.pallas_call(
        _proj_c_kernel,
        grid=(s // tm,),
        in_specs=[pl.BlockSpec((tm, d), lambda i: (i, 0)), _wspec(w, _wshape(w), lambda i: (0, 0)), r128, r128, r128],
        out_specs=[r512, r512, pl.BlockSpec((2, tm, 128), lambda i: (0, i, 0)),
                   pl.BlockSpec((tm, 256), lambda i: (i, 0))] + [r128] * 4,
        out_shape=[b512, b512, jax.ShapeDtypeStruct((2, s, 128), BF16), jax.ShapeDtypeStruct((s, 256), BF16)]
                  + [b128] * 3 + [jax.ShapeDtypeStruct((s, 128), F32)],
        compiler_params=_cp("parallel"),
        name="proj_nsa",
    )(x, w[0], cos128, sin128, _key_block_table(s, NSA_SEL_BLOCK))


def _proj_d_kernel(x_ref, w_ref, cos_ref, sin_ref, c64_ref, sup_ref, sdn_ref,
                   q_ref, k_ref, v_ref, iq_ref, ik_ref, iw_ref):
    h = _dot(x_ref[...].astype(BF16), w_ref[...])
    cos, sin = cos_ref[...], sin_ref[...]
    c64, sup, sdn = c64_ref[...], sup_ref[...], sdn_ref[...]
    for hd in range(4):
        lo = hd * HEAD_DIM
        q_ref[:, lo:lo + HEAD_DIM] = (_rope128(h[:, lo:lo + HEAD_DIM], cos, sin) * _log2_scale(HEAD_DIM)).astype(BF16)
        k_ref[:, lo:lo + HEAD_DIM] = _rope128(h[:, 512 + lo:512 + lo + HEAD_DIM], cos, sin).astype(BF16)
    v_ref[...] = h[:, 1024:1536].astype(BF16)
    for p in range(8):
        lo = 1536 + p * 128
        iq_ref[:, p * 128:(p + 1) * 128] = _rope64(h[:, lo:lo + 128], c64, sup, sdn).astype(BF16)
    ik = _rope64(h[:, 2560:2688], c64, sup, sdn)
    ik_ref[:, 0:128] = ik.astype(BF16)
    ik_ref[:, 128:256] = pltpu.roll(ik, 64, 1).astype(BF16)
    iw_ref[...] = h[:, 2688:2816] * (1.0 / 32.0)


def _proj_d(x, w, cos128, sin128, cos64, sup64, sdn64):
    s, d = x.shape
    tm = min(ROW_TILE, s)
    r = lambda n: pl.BlockSpec((tm, n), lambda i: (i, 0))
    return pl.pallas_call(
        _proj_d_kernel,
        grid=(s // tm,),
        in_specs=[pl.BlockSpec((tm, d), lambda i: (i, 0)), _wspec(w, _wshape(w), lambda i: (0, 0))] + [r(128)] * 5,
        out_specs=[r(512), r(512), r(512), r(1024), r(256), r(128)],
        out_shape=[jax.ShapeDtypeStruct((s, 512), BF16)] * 3 + [
            jax.ShapeDtypeStruct((s, 1024), BF16), jax.ShapeDtypeStruct((s, 256), BF16),
            jax.ShapeDtypeStruct((s, 128), F32)],
        compiler_params=_cp("parallel"),
        name="proj_dsa",
    )(x, w[0], cos128, sin128, cos64, sup64, sdn64)


def _moba_kernel(q_ref, k_ref, v_ref, km_ref, o_ref, m_ref, acc_ref, *, tk, nb):
    i = pl.program_id(1)
    tq = q_ref.shape[0]
    q32 = q_ref[...]

    nbr = km_ref.shape[0]
    gate = lax.dot_general(km_ref[...], q32, (((1,), (1,)), ((), ())),
                           precision=lax.Precision.HIGHEST, preferred_element_type=F32)
    blk = lax.broadcasted_iota(I32, (nbr, tq), 0)
    cur = (i * tq + lax.broadcasted_iota(I32, (nbr, tq), 1)) // MOBA_BLOCK
    sel = _top_k_mask(jnp.where(blk < cur, gate, -jnp.inf), blk, min(MOBA_TOPK, nb - 1), axis=0)
    unchosen = jnp.where(blk == cur, 1.0, sel) - 1.0
    unchosen = jnp.concatenate([unchosen, jnp.full((128 - nbr, tq), -1.0, F32)], axis=0).T
    qa = jnp.concatenate([q32 * _log2_scale(HEAD_DIM), unchosen], axis=1).astype(BF16)

    _flash_init(m_ref, acc_ref)
    _causal_flash(qa, k_ref, v_ref, m_ref, acc_ref, i, tk)
    o_ref[...] = _flash_result(acc_ref, 0, 1).astype(o_ref.dtype)


def _moba(q32, k, v, kmean):
    s = q32.shape[0]
    nb = s // MOBA_BLOCK
    tq = min(ATT_TQ, s)
    tk = min(CAUSAL_TK, s)
    nbr = -(-nb // 8) * 8
    km = jnp.pad(kmean.reshape(nb, 512), ((0, nbr - nb), (0, 0)))
    return pl.pallas_call(
        functools.partial(_moba_kernel, tk=tk, nb=nb),
        grid=(4, s // tq),
        in_specs=[
            pl.BlockSpec((tq, HEAD_DIM), lambda h, i: (i, h)),
            pl.BlockSpec((s, 2 * HEAD_DIM), lambda h, i: (0, h)),
            pl.BlockSpec((s, HEAD_DIM), lambda h, i: (0, h)),
            pl.BlockSpec((nbr, HEAD_DIM), lambda h, i: (0, h)),
        ],
        out_specs=pl.BlockSpec((tq, HEAD_DIM), lambda h, i: (i, h)),
        out_shape=jax.ShapeDtypeStruct((s, 512), BF16),
        scratch_shapes=_flash_scratch(1, tq),
        compiler_params=_cp("parallel", "arbitrary"),
        name="moba_attn",
    )(q32, k, v, km)


def _mla_kernel(q_ref, k_ref, v_ref, o_ref, m_ref, acc_ref, *, tk):
    _flash_init(m_ref, acc_ref)
    _causal_flash(q_ref[...], k_ref, v_ref, m_ref, acc_ref, pl.program_id(1), tk)
    o_ref[...] = _flash_result(acc_ref, 0, 1).astype(o_ref.dtype)


def _mla(qcat, kcat, v):
    s = qcat.shape[0]
    tq = min(ATT_TQ, s)
    tk = min(CAUSAL_TK, s)
    return pl.pallas_call(
        functools.partial(_mla_kernel, tk=tk),
        grid=(4, s // tq),
        in_specs=[
            pl.BlockSpec((tq, 256), lambda h, i: (i, h)),
            pl.BlockSpec((s, 256), lambda h, i: (0, h)),
            pl.BlockSpec((s, HEAD_DIM), lambda h, i: (0, h)),
        ],
        out_specs=pl.BlockSpec((tq, HEAD_DIM), lambda h, i: (i, h)),
        out_shape=jax.ShapeDtypeStruct((s, 512), BF16),
        scratch_shapes=_flash_scratch(1, tq),
        compiler_params=_cp("parallel", "arbitrary"),
        name="mla_attn",
    )(qcat, kcat, v)


def _gelu_tanh(x):
    return 0.5 * x * (1.0 + jnp.tanh(0.7978845608028654 * (x + 0.044715 * x * x * x)))


def _nsa_compress_kernel(t_ref, pe_ref, w1_ref, w2_ref, o_ref):
    t = t_ref[0]
    w1 = w1_ref[...]
    half = t.shape[1]
    n = t.shape[0]
    lo = _dot(t, w1[0:half])
    hi = _dot(t, w1[half:2 * half])
    pe = _dot(pe_ref[...].astype(BF16), w1)
    pre = lo + pltpu.roll(hi, n - 1, 0) + pe
    o_ref[0] = _dot(_gelu_tanh(pre).astype(BF16), w2_ref[...]).astype(o_ref.dtype)


def _nsa_compress(kv16, pe, w1, w2):
    n = kv16.shape[1]
    sub = lambda a: _wspec(a, (None,) + _wshape(a)[1:], lambda i: (i, 0, 0), resident=False)
    return pl.pallas_call(
        _nsa_compress_kernel,
        grid=(2,),
        in_specs=[pl.BlockSpec((1,) + kv16.shape[1:], lambda i: (i, 0, 0)), sub(pe), sub(w1), sub(w2)],
        out_specs=pl.BlockSpec((1, n, HEAD_DIM), lambda i: (i, 0, 0)),
        out_shape=jax.ShapeDtypeStruct((2, n, HEAD_DIM), BF16),
        compiler_params=_cp("parallel"),
        name="nsa_compress",
    )(kv16, pe[0], w1[0], w2[0])


def _nsa_cmp_kernel(q_ref, kvc_ref, m_ref, o_ref, sel_ref, *, n_sel):
    i = pl.program_id(0)
    tq = q_ref.shape[0]
    n = kvc_ref.shape[1]
    nblk = m_ref.shape[1]
    scale = HEAD_DIM ** -0.5
    kc, vc = kvc_ref[0], kvc_ref[1]
    t_pos = i * tq + lax.broadcasted_iota(I32, (tq, n), 0)
    cmp_end = lax.broadcasted_iota(I32, (tq, n), 1) * NSA_CMP_STRIDE + (2 * NSA_CMP_STRIDE - 1)
    ok = cmp_end <= t_pos
    p_sum = jnp.zeros((tq, n), F32)
    for hd in range(4):
        lo = hd * HEAD_DIM
        s = jnp.where(ok, _dot_nt(q_ref[:, lo:lo + HEAD_DIM], kc) * scale, NEG)
        m = jnp.max(s, axis=1, keepdims=True)
        e = jnp.where(ok, jnp.exp(s - m), 0.0)
        l = jnp.sum(e, axis=1, keepdims=True)
        p = e / jnp.where(l > 0, l, 1.0)
        p_sum = p_sum + p
        o_ref[:, lo:lo + HEAD_DIM] = _dot(p.astype(BF16), vc)

    imp = jnp.dot(p_sum, m_ref[...], precision=lax.Precision.HIGHEST, preferred_element_type=F32)
    imp_t = imp.T
    blk = lax.broadcasted_iota(I32, (nblk, tq), 0)
    cur = (i * tq + lax.broadcasted_iota(I32, (nblk, tq), 1)) // NSA_SEL_BLOCK
    forced = (blk == 0) | (blk == cur) | (blk == cur - 1)
    imp_t = jnp.where(blk > cur, -jnp.inf, jnp.where(forced, jnp.inf, imp_t))
    sel_ref[...] = (_top_k_mask(imp_t, blk, n_sel, axis=0).T - 1.0).astype(sel_ref.dtype)


def _nsa_cmp(q, kvc, imp_map):
    s = q.shape[0]
    n = kvc.shape[1]
    nblk = imp_map.shape[1]
    tq = min(256, s)
    return pl.pallas_call(
        functools.partial(_nsa_cmp_kernel, n_sel=min(NSA_SEL_TOPK, s // NSA_SEL_BLOCK)),
        grid=(s // tq,),
        in_specs=[pl.BlockSpec((tq, 512), lambda i: (i, 0)),
                  _resident((2, n, HEAD_DIM), lambda i: (0, 0, 0)),
                  _resident((n, nblk), lambda i: (0, 0))],
        out_specs=[pl.BlockSpec((tq, 512), lambda i: (i, 0)), pl.BlockSpec((tq, nblk), lambda i: (i, 0))],
        out_shape=[jax.ShapeDtypeStruct((s, 512), F32), jax.ShapeDtypeStruct((s, nblk), BF16)],
        compiler_params=_cp("parallel"),
        name="nsa_cmp_select",
    )(q, kvc, imp_map)


def _nsa_slc_win_kernel(q_ref, sel_ref, ks_ref, vs_ref, kw_ref, vw_ref, g_ref, oc_ref, o_ref,
                        m_ref, acc_ref, *, tk):
    i = pl.program_id(0)
    tq = q_ref.shape[0]
    _flash_init(m_ref, acc_ref)

    def tile(off, width, keep):
        k = ks_ref[pl.ds(off, width), 0:HEAD_DIM]
        v1 = _with_ones(vs_ref[pl.ds(off, width), :])
        bias = _dot_nt(sel_ref[...], ks_ref[pl.ds(off, width), HEAD_DIM:2 * HEAD_DIM])
        _flash_heads([q_ref[:, hd * HEAD_DIM:(hd + 1) * HEAD_DIM] for hd in range(4)], [k] * 4, [v1] * 4,
                     bias, keep, m_ref, acc_ref)

    n_past = (i * tq) // tk
    n_rest = (i * tq - n_past * tk) // tq

    def body(j, carry):
        tile(pl.multiple_of(j * tk, tk), tk, None)
        return carry

    lax.fori_loop(0, n_past, body, 0)
    for r in range(tk // tq):
        @pl.when(n_rest == r)
        def _():
            width = (r + 1) * tq
            row = lax.broadcasted_iota(I32, (tq, width), 0) + r * tq
            col = lax.broadcasted_iota(I32, (tq, width), 1)
            tile(pl.multiple_of(n_past * tk, tk), width, col <= row)

    wlen = NSA_WINDOW + tq
    start = pl.multiple_of(jnp.maximum(i * tq - NSA_WINDOW, 0), 256)
    kw = kw_ref[pl.ds(start, wlen), :]
    vw1 = _with_ones(vw_ref[pl.ds(start, wlen), :])
    diff = (i * tq + lax.broadcasted_iota(I32, (tq, wlen), 0)) - (start + lax.broadcasted_iota(I32, (tq, wlen), 1))
    in_win = (diff >= 0) & (diff < NSA_WINDOW)
    g = g_ref[...]
    for hd in range(4):
        lo = hd * HEAD_DIM
        s = jnp.where(in_win, _dot_nt(q_ref[:, lo:lo + HEAD_DIM], kw), NEG)
        p = jnp.exp2(s - jnp.max(s, axis=1, keepdims=True))
        pv = _dot(p.astype(BF16), vw1)
        o_win = pv[:, 0:HEAD_DIM] / pv[:, HEAD_DIM:2 * HEAD_DIM]
        o_slc = _flash_result(acc_ref, hd, 4)
        out = (g[:, 3 * hd:3 * hd + 1] * oc_ref[:, lo:lo + HEAD_DIM] + g[:, 3 * hd + 1:3 * hd + 2] * o_slc
               + g[:, 3 * hd + 2:3 * hd + 3] * o_win)
        o_ref[:, lo:lo + HEAD_DIM] = out.astype(o_ref.dtype)


def _nsa_slc_win(q_r, sel, kslc, vslc, kwin, vwin, gates, o_cmp):
    s = q_r.shape[0]
    tq = min(ATT_TQ, s)
    tk = min(NSA_TK, s)
    full = lambda a: _resident(a.shape, lambda i: (0,) * a.ndim)
    return pl.pallas_call(
        functools.partial(_nsa_slc_win_kernel, tk=tk),
        grid=(s // tq,),
        in_specs=[pl.BlockSpec((tq, 512), lambda i: (i, 0)), pl.BlockSpec((tq, 128), lambda i: (i, 0)),
                  full(kslc), full(vslc), full(kwin), full(vwin),
                  pl.BlockSpec((tq, 128), lambda i: (i, 0)), pl.BlockSpec((tq, 512), lambda i: (i, 0))],
        out_specs=pl.BlockSpec((tq, 512), lambda i: (i, 0)),
        out_shape=jax.ShapeDtypeStruct((s, 512), BF16),
        scratch_shapes=_flash_scratch(4, tq),
        compiler_params=_cp("parallel"),
        name="nsa_slc_win",
    )(q_r, sel, kslc, vslc, kwin, vwin, gates, o_cmp)


def _dsa_kernel(q_ref, iq_ref, iw_ref, k_ref, v_ref, ik_ref, o_ref, key_ref, wb_ref, m_ref, acc_ref, *, tk):
    i = pl.program_id(0)
    tq = q_ref.shape[0]
    half = tk // 2
    n_tiles = (i * tq + tq + tk - 1) // tk
    row_h = lax.broadcasted_iota(I32, (tq, half), 0) + i * tq
    col_h = lax.broadcasted_iota(I32, (tq, half), 1)

    w = iw_ref[...]
    for hd in range(DSA_IDX_HEADS):
        wb_ref[hd] = jnp.broadcast_to(w[:, hd:hd + 1], (tq, 128))

    def score_body(j, carry):
        for hf in range(2):
            off = pl.multiple_of(j * tk + hf * half, half)
            ik_even = ik_ref[pl.ds(off, half), 0:128]
            ik_odd = ik_ref[pl.ds(off, half), 128:256]
            sc = jnp.zeros((tq, half), F32)
            for p in range(DSA_IDX_HEADS // 2):
                x = iq_ref[:, p * 128:(p + 1) * 128]
                we = jnp.concatenate([wb_ref[2 * p]] * (half // 128), axis=1)
                wo = jnp.concatenate([wb_ref[2 * p + 1]] * (half // 128), axis=1)
                sc = sc + jnp.maximum(_dot_nt(x, ik_even), 0.0) * we
                sc = sc + jnp.maximum(_dot_nt(x, ik_odd), 0.0) * wo
            bits = pltpu.bitcast(sc, I32)
            key = jnp.where(bits >= 0, bits, bits ^ 0x7FFFFFFF)
            key_ref[j, :, hf * half:(hf + 1) * half] = jnp.where(col_h + off <= row_h, key, INT_MIN)
        return carry

    lax.fori_loop(0, n_tiles, score_body, 0)

    rb = min(DSA_COUNT_ROWS, tq)

    ones = jnp.ones((128, 128), BF16)
    k_f = float(DSA_TOPK)

    def count_ge(cand128):
        parts = []
        for r0 in range(0, tq, rb):
            cand_b = cand128[r0:r0 + rb]

            def body(j, acc):
                for c in range(tk // 128):
                    acc = acc + jnp.where(key_ref[j, r0:r0 + rb, c * 128:(c + 1) * 128] >= cand_b, 1, 0)
                return acc

            parts.append(lax.fori_loop(0, n_tiles, body, jnp.zeros((rb, 128), I32)))
        per_lane = jnp.concatenate(parts, axis=0).astype(F32).astype(BF16)
        return _dot(per_lane, ones)

    zero = jnp.zeros((tq, 128), I32)
    c0 = count_ge(zero)
    lo0 = jnp.where(c0 >= k_f, zero, INT_MIN)
    c_lo0 = jnp.where(c0 >= k_f, c0, -1.0)

    def bit_cond(state):
        b, _, _, more = state
        return (b < 31) & (more > 0.0)

    def bit_body(state):
        b, lo, c_lo, _ = state
        more = jnp.max(jnp.where(c_lo != k_f, 1.0, 0.0))
        cand = lo | lax.shift_left(jnp.int32(1), 30 - b)
        c = count_ge(cand)
        take = c >= k_f
        return b + 1, jnp.where(take, cand, lo), jnp.where(take, c, c_lo), more

    passes, lo, c_lo, _ = lax.while_loop(bit_cond, bit_body, (jnp.int32(0), lo0, c_lo0, jnp.float32(1.0)))
    thr128 = jnp.maximum(lo, INT_MIN + 1)

    def retire_ties():
        need = k_f - count_ge(lo + 1)
        lane = lax.broadcasted_iota(I32, (rb, 128), 1)

        def tied_pos(j, r0, c):
            tied = key_ref[j, r0:r0 + rb, c * 128:(c + 1) * 128] == lo[r0:r0 + rb]
            return jnp.where(tied, j * tk + c * 128 + lane, -1)

        def count_tied_before(limit):
            parts = []
            for r0 in range(0, tq, rb):
                lim_b = limit[r0:r0 + rb]

                def body(j, acc, r0=r0, lim_b=lim_b):
                    for c in range(tk // 128):
                        pos = tied_pos(j, r0, c)
                        acc = acc + jnp.where(jnp.where(pos >= 0, pos, lim_b) < lim_b, 1, 0)
                    return acc

                parts.append(lax.fori_loop(0, n_tiles, body, jnp.zeros((rb, 128), I32)))
            return _dot(jnp.concatenate(parts, axis=0).astype(F32).astype(BF16), ones)

        n_bits = (key_ref.shape[0] * tk - 1).bit_length()

        def pos_body(b, last):
            cand = last | lax.shift_left(jnp.int32(1), n_bits - 1 - b)
            return jnp.where(count_tied_before(cand) < need, cand, last)

        last = lax.fori_loop(0, n_bits, pos_body, jnp.zeros((tq, 128), I32))
        last = jnp.where(c_lo > k_f, last, 2 ** 31 - 1)

        for r0 in range(0, tq, rb):
            def drop_body(j, carry, r0=r0):
                for c in range(tk // 128):
                    cols = slice(c * 128, (c + 1) * 128)
                    beyond = tied_pos(j, r0, c) > last[r0:r0 + rb]
                    key_ref[j, r0:r0 + rb, cols] = jnp.where(beyond, INT_MIN, key_ref[j, r0:r0 + rb, cols])
                return carry

            lax.fori_loop(0, n_tiles, drop_body, 0)

    @pl.when(passes >= 31)
    def _():
        @pl.when(jnp.max(jnp.where(c_lo > k_f, 1.0, 0.0)) > 0.0)
        def _():
            retire_ties()

    _flash_init(m_ref, acc_ref)

    def attn_body(j, carry):
        off = pl.multiple_of(j * tk, tk)
        keep = key_ref[j] >= jnp.concatenate([thr128] * (tk // 128), axis=1)
        cols = [slice(hd * HEAD_DIM, (hd + 1) * HEAD_DIM) for hd in range(4)]
        _flash_heads([q_ref[:, c] for c in cols], [k_ref[pl.ds(off, tk), c] for c in cols],
                     [_with_ones(v_ref[pl.ds(off, tk), c]) for c in cols], None, keep, m_ref, acc_ref)
        return carry

    lax.fori_loop(0, n_tiles, attn_body, 0)
    for hd in range(4):
        o_ref[:, hd * HEAD_DIM:(hd + 1) * HEAD_DIM] = _flash_result(acc_ref, hd, 4).astype(o_ref.dtype)


def _dsa(q, k, v, iq, ik2, iw):
    s = q.shape[0]
    tq = min(DSA_TQ, s)
    tk = min(ATT_TK, s)
    full = lambda a: _resident(a.shape, lambda i: (0, 0))
    return pl.pallas_call(
        functools.partial(_dsa_kernel, tk=tk),
        grid=(s // tq,),
        in_specs=[pl.BlockSpec((tq, 512), lambda i: (i, 0)), pl.BlockSpec((tq, 1024), lambda i: (i, 0)),
                  pl.BlockSpec((tq, 128), lambda i: (i, 0)), full(k), full(v), full(ik2)],
        out_specs=pl.BlockSpec((tq, 512), lambda i: (i, 0)),
        out_shape=jax.ShapeDtypeStruct((s, 512), BF16),
        scratch_shapes=[pltpu.VMEM((s // tk, tq, tk), I32), pltpu.VMEM((DSA_IDX_HEADS, tq, 128), F32)]
                       + _flash_scratch(4, tq),
        compiler_params=_cp("parallel"),
        name="dsa_attn",
    )(q, iq, iw, k, v, ik2)


def _mix_mem_kernel(oa_ref, ob_ref, oc_ref, od_ref, w_ref, x_ref, g1_ref, b1_ref, wq_ref, kv_ref, wo_ref,
                    g2_ref, b2_ref, o_ref):
    mix = _dot(oa_ref[...], w_ref[0:512, :])
    mix = mix + _dot(ob_ref[...], w_ref[512:1024, :])
    mix = mix + _dot(oc_ref[...], w_ref[1024:1536, :])
    mix = mix + _dot(od_ref[...], w_ref[1536:2048, :])
    x = _layer_norm(DN_ALPHA * x_ref[...] + mix, g1_ref[...], b1_ref[...])

    q = _dot(x.astype(BF16), wq_ref[...])
    heads = []
    for hd in range(4):
        lo = hd * HEAD_DIM
        qh = (q[:, lo:lo + HEAD_DIM] * _log2_scale(HEAD_DIM)).astype(BF16)
        s = _dot_nt(qh, kv_ref[:, lo:lo + HEAD_DIM])
        p = jnp.exp2(s - jnp.max(s, axis=1, keepdims=True))
        o = _dot(p.astype(BF16), kv_ref[:, 512 + lo:512 + lo + HEAD_DIM]) / jnp.sum(p, axis=1, keepdims=True)
        heads.append(o.astype(BF16))
    out = _dot(jnp.concatenate(heads, axis=1), wo_ref[...])
    o_ref[...] = _layer_norm(DN_ALPHA * x + out, g2_ref[...], b2_ref[...])


def _mix_mem_ln(o_a, o_b, o_c, o_d, w_out, x, g1, b1, wq, kv, wo, g2, b2):
    s, d = x.shape
    tm = min(ROW_TILE, s)
    r512 = pl.BlockSpec((tm, 512), lambda i: (i, 0))
    full = lambda a: _wspec(a, _wshape(a), lambda i: (0, 0))
    return pl.pallas_call(
        _mix_mem_kernel,
        grid=(s // tm,),
        in_specs=[r512, r512, r512, r512, full(w_out), pl.BlockSpec((tm, d), lambda i: (i, 0)), full(g1), full(b1),
                  full(wq), _resident(kv.shape, lambda i: (0, 0)), full(wo), full(g2), full(b2)],
        out_specs=pl.BlockSpec((tm, d), lambda i: (i, 0)),
        out_shape=jax.ShapeDtypeStruct((s, d), F32),
        compiler_params=_cp("parallel"),
        name="mix_mem_ln",
    )(o_a, o_b, o_c, o_d, w_out[0], x, g1[0], b1[0], wq[0], kv, wo[0], g2[0], b2[0])


def _mem_kv_kernel(mem_ref, w_ref, o_ref):
    o_ref[...] = _dot(mem_ref[...].astype(BF16), w_ref[...]).astype(o_ref.dtype)


def _mem_kv(mem, wkv):
    m, d = mem.shape
    n = _wshape(wkv)[1]
    return pl.pallas_call(
        _mem_kv_kernel,
        grid=(1,),
        in_specs=[pl.BlockSpec((m, d), lambda i: (0, 0)), _wspec(wkv, (d, n), lambda i: (0, 0))],
        out_specs=pl.BlockSpec((m, n), lambda i: (0, 0)),
        out_shape=jax.ShapeDtypeStruct((m, n), BF16),
        compiler_params=_cp("arbitrary"),
        name="mem_kv",
    )(mem, wkv[0])


def _pad_last(w, n):
    return jnp.pad(w, [(0, 0)] * (w.ndim - 1) + [(0, n - w.shape[-1])])


def _split_w_in(w_in):
    a = w_in[..., 0:1536]
    b = _pad_last(w_in[..., 1536:2624], 1152)
    c = _pad_last(w_in[..., 2624:3916], 1408)
    d = jnp.concatenate([w_in[..., 3916:6476], _pad_last(w_in[..., 6476:6540], 128),
                         _pad_last(w_in[..., 6540:6556], 128)], axis=-1)
    return tuple(t.astype(BF16) for t in (a, b, c, d))


def _pad_w_uq(w_uq):
    lead = w_uq.shape[:-1]
    w = _pad_last(w_uq.reshape(lead + (4, MLA_NOPE + MLA_ROPE)), 256)
    return w.reshape(lead + (4 * 256,)).astype(BF16)


def _nsa_importance_map(s):
    n = s // NSA_CMP_STRIDE
    nblk = s // NSA_SEL_BLOCK
    ni = np.arange(n)[:, None]
    bi = np.arange(nblk)[None, :]
    m = ((ni >= 4 * bi - 1) & (ni <= 4 * bi + 3)).astype(np.float32)
    return jnp.asarray(np.pad(m, ((0, 0), (0, 128 - nblk))))


def _mixer_heads(x, tabs, imp_map, w_groups, g_cq, g_ckv, w_uq, w_ukv, cmp_pe, cmp_w1, cmp_w2):
    cos128, sin128, cos64, sup64, sdn64 = tabs
    s = x.shape[0]
    w_a, w_b, w_c, w_d = w_groups

    aq, ak, av, akm = _proj_a(x, w_a, cos128, sin128)
    o_a = _moba(aq, ak, av, akm)

    bq, bk, bv = _proj_b(x, w_b, g_cq, g_ckv, w_uq, w_ukv, cos64, sup64, sdn64)
    o_b = _mla(bq, bk, bv)

    cq, cqr, kvcmp, kslc, vslc, kwin, vwin, gates = _proj_c(x, w_c, cos128, sin128)
    kv16 = kvcmp.reshape(2, s // NSA_CMP_STRIDE, NSA_CMP_STRIDE * HEAD_DIM)
    kvc = _nsa_compress(kv16, cmp_pe, cmp_w1, cmp_w2)
    o_cmp, sel = _nsa_cmp(cq, kvc, imp_map)
    o_c = _nsa_slc_win(cqr, sel, kslc, vslc, kwin, vwin, gates, o_cmp)

    dq, dk, dv, diq, dik, diw = _proj_d(x, w_d, cos128, sin128, cos64, sup64, sdn64)
    o_d = _dsa(dq, dk, dv, diq, dik, diw)
    return o_a, o_b, o_c, o_d


def kernel(x, mem, positions, ln_g, ln_b, ffn_w_gu, ffn_w_down, w_in, w_out, mla_g_cq, mla_g_ckv, mla_w_uq,
           mla_w_ukv, nsa_cmp_pe, nsa_cmp_w1, nsa_cmp_w2, mem_wq, mem_wkv, mem_wo):
    batch, s, d = x.shape
    w_gu_b, w_dn_b = ffn_w_gu, ffn_w_down
    w_groups = _split_w_in(w_in)
    w_out_b = w_out.astype(BF16)
    w_uq_b, w_ukv_b = _pad_w_uq(mla_w_uq), mla_w_ukv.astype(BF16)
    g_cq, g_ckv = mla_g_cq[:, None, :], mla_g_ckv[:, None, :]
    pe = nsa_cmp_pe.reshape(DEPTH, 2, 1, -1)
    w1_b, w2_b = nsa_cmp_w1.astype(BF16), nsa_cmp_w2.astype(BF16)
    wq_b, wkv_b, wo_b = mem_wq.astype(BF16), mem_wkv.astype(BF16), mem_wo.astype(BF16)
    g4, b4 = ln_g[:, :, None, :], ln_b[:, :, None, :]
    imp_map = _nsa_importance_map(s)

    outs = []
    for bi in range(batch):
        xb = x.reshape(s, d) if batch == 1 else x[bi]
        tabs = _rope_tables(positions[bi])
        for l in range(DEPTH):
            at = lambda a, *lead: (a, (l,) + lead)
            xb = _ffn_ln(xb, at(w_gu_b, 0), at(w_dn_b, 0), at(g4, 0), at(b4, 0))
            o_heads = _mixer_heads(xb, tabs, imp_map, tuple(at(w) for w in w_groups), at(g_cq), at(g_ckv),
                                   at(w_uq_b), at(w_ukv_b), at(pe), at(w1_b), at(w2_b))
            kv = _mem_kv(mem[bi], at(wkv_b))
            xb = _mix_mem_ln(*o_heads, at(w_out_b), xb, at(g4, 1), at(b4, 1), at(wq_b), kv, at(wo_b),
                             at(g4, 2), at(b4, 2))
            xb = _ffn_ln(xb, at(w_gu_b, 1), at(w_dn_b, 1), at(g4, 3), at(b4, 3))
        outs.append(xb)
    return outs[0].reshape(1, s, d) if batch == 1 else jnp.stack(outs)
```
